```python
import jax, jax.numpy as jnp
from jax import lax
import numpy as np

D_MODEL = 2048
BATCH = 4
SEQ = 2048
DEPTH = 1
DEC_BATCH = 128
DEC_SEQ = 8
PAST_LEN = 16384
PAGE_SIZE = 128

META_TOKENS = 16
D_INNER = 2 * D_MODEL
SSD_HEAD_DIM = 64
SSD_HEADS = D_INNER // SSD_HEAD_DIM
SSD_STATE = 128
SSD_GROUPS = 8
SSD_CONV_W = 4
SSD_CHUNK = 128
SSD_CONV_DIM = D_INNER + 2 * SSD_GROUPS * SSD_STATE
SC_WIDTH = D_MODEL
SC_CONV_W = 3
PROJ_COLS = D_INNER + SSD_CONV_DIM + SSD_HEADS + 4 * SC_WIDTH + 2 * D_MODEL
EPS = 1e-6

kernel_name = "hybrid_ssd_shortconv_gated_decode_step"


def rms_norm(x, w):
    xf = x.astype(jnp.float32)
    xf = xf * lax.rsqrt(jnp.mean(xf * xf, axis=-1, keepdims=True) + EPS)
    return (xf * w.astype(jnp.float32)).astype(x.dtype)


def gated_group_rms_norm(y, z, w):
    g = y.astype(jnp.float32) * jax.nn.silu(z.astype(jnp.float32))
    shp = g.shape
    g = g.reshape(shp[:-1] + (SSD_GROUPS, shp[-1] // SSD_GROUPS))
    g = g * lax.rsqrt(jnp.mean(g * g, axis=-1, keepdims=True) + EPS)
    return (g.reshape(shp) * w.astype(jnp.float32)).astype(y.dtype)


def causal_dwconv(x, prev, w):
    width = w.shape[0]
    length = x.shape[1]
    xp = jnp.concatenate([prev.astype(x.dtype), x], axis=1)
    out = xp[:, 0:length] * w[0]
    for k in range(1, width):
        out = out + xp[:, k:k + length] * w[k]
    return out, xp[:, length:]


def ssd_chunked(xs, dt, log_a, bm, cm, h0, chunk):
    f32 = jnp.float32
    bsz, length, nh, hp = xs.shape
    ng, ns = bm.shape[2], bm.shape[3]
    hr = nh // ng
    nc = length // chunk
    xc = xs.astype(f32).reshape(bsz, nc, chunk, ng, hr, hp)
    dtc = dt.astype(f32).reshape(bsz, nc, chunk, ng, hr)
    acum = jnp.cumsum(log_a.astype(f32).reshape(bsz, nc, chunk, ng, hr), axis=2)
    bc = bm.astype(f32).reshape(bsz, nc, chunk, ng, ns)
    cc = cm.astype(f32).reshape(bsz, nc, chunk, ng, ns)
    causal = jnp.tril(jnp.ones((chunk, chunk), dtype=bool))[:, :, None, None]
    seg = acum[:, :, :, None] - acum[:, :, None, :]
    decay = jnp.exp(jnp.where(causal, seg, -jnp.inf))
    cb = jnp.einsum('bcign,bcjgn->bcijg', cc, bc)
    wts = cb[..., None] * decay * dtc[:, :, None]
    y_diag = jnp.einsum('bcijgr,bcjgrp->bcigrp', wts, xc)
    to_end = jnp.exp(acum[:, :, -1:] - acum) * dtc
    chunk_states = jnp.einsum('bcjgr,bcjgrp,bcjgn->bcgrpn', to_end, xc, bc)
    chunk_decay = jnp.exp(acum[:, :, -1])

    def step(h, inp):
        dec, st = inp
        return dec[..., None, None] * h + st, h

    h_last, h_in = lax.scan(step, h0.astype(f32).reshape(bsz, ng, hr, hp, ns),
                            (jnp.moveaxis(chunk_decay, 1, 0), jnp.moveaxis(chunk_states, 1, 0)))
    h_in = jnp.moveaxis(h_in, 0, 1)
    y_off = jnp.einsum('bcign,bcgrpn->bcigrp', cc, h_in) * jnp.exp(acum)[..., None]
    y = (y_diag + y_off).reshape(bsz, length, nh, hp).astype(xs.dtype)
    return y, h_last.reshape(bsz, nh, hp, ns).astype(h0.dtype)


def split_proj(proj):
    sizes = [D_INNER, SSD_CONV_DIM, SSD_HEADS, SC_WIDTH, SC_WIDTH, SC_WIDTH, SC_WIDTH, D_MODEL]
    idx = []
    acc = 0
    for s in sizes:
        acc += s
        idx.append(acc)
    return jnp.split(proj, idx, axis=-1)


def mixer_layer(x, conv_prev, ssm_prev, sconv_prev, segments, norm_w, w_in, ssd_conv_w,
                ssd_conv_b, dt_bias, a_log, d_skip, ssd_norm_w, w_ssd_out, sconv_w,
                w_sconv_out, w_o):
    bsz, length = x.shape[0], x.shape[1]
    h = rms_norm(x, norm_w)
    proj = h @ w_in
    z, xbc, dt_raw, sc_b, sc_c, sc_h, sc_z, g_a, g_b = split_proj(proj)

    xbc_c, conv_new = causal_dwconv(xbc, conv_prev, ssd_conv_w)
    xbc = jax.nn.silu(xbc_c + ssd_conv_b)
    xs, bm, cm = jnp.split(xbc, [D_INNER, D_INNER + SSD_GROUPS * SSD_STATE], axis=-1)
    xs = xs.reshape(bsz, length, SSD_HEADS, SSD_HEAD_DIM)
    bm = bm.reshape(bsz, length, SSD_GROUPS, SSD_STATE)
    cm = cm.reshape(bsz, length, SSD_GROUPS, SSD_STATE)
    dt = jax.nn.softplus(dt_raw.astype(jnp.float32) + dt_bias.astype(jnp.float32))
    log_a = dt * (-jnp.exp(a_log.astype(jnp.float32)))
    state = ssm_prev
    ys = []
    start = 0
    for seg_len, chunk in segments:
        sl = slice(start, start + seg_len)
        y_seg, state = ssd_chunked(xs[:, sl], dt[:, sl], log_a[:, sl], bm[:, sl],
                                   cm[:, sl], state, chunk)
        ys.append(y_seg)
        start += seg_len
    y = jnp.concatenate(ys, axis=1) + d_skip[:, None] * xs
    y = gated_group_rms_norm(y.reshape(bsz, length, D_INNER), z, ssd_norm_w)
    y_a = y @ w_ssd_out

    u = sc_c * sc_h
    u_c, sconv_new = causal_dwconv(u, sconv_prev, sconv_w)
    y_b = (sc_b * u_c * jax.nn.silu(sc_z)) @ w_sconv_out

    merged = jax.nn.sigmoid(g_a) * y_a + jax.nn.sigmoid(g_b) * y_b
    return x + merged @ w_o, conv_new, state, sconv_new


def setup_inputs(seed: int = 0) -> dict:
    key = jax.random.key(seed)
    ks = jax.random.split(key, 24)
    f32 = jnp.float32
    nrm = lambda k, shp, s: jax.random.normal(k, shp, f32) * s
    dt0 = jnp.exp(jax.random.uniform(ks[10], (DEPTH, SSD_HEADS), f32,
                                     np.log(1e-3).astype(np.float32), np.log(1e-1).astype(np.float32)))
    dt_bias = dt0 + jnp.log(-jnp.expm1(-dt0))
    return {
        "x_prompt": nrm(ks[0], (BATCH, SEQ, D_MODEL), 1.0),
        "x_sample": nrm(ks[1], (DEC_BATCH, DEC_SEQ, D_MODEL), 1.0),
        "state_ssd_conv": nrm(ks[2], (DEPTH, DEC_BATCH, SSD_CONV_W - 1, SSD_CONV_DIM), 1.0),
        "state_ssm": nrm(ks[3], (DEPTH, DEC_BATCH, SSD_HEADS, SSD_HEAD_DIM, SSD_STATE), 0.5),
        "state_sconv": nrm(ks[4], (DEPTH, DEC_BATCH, SC_CONV_W - 1, SC_WIDTH), 1.0),
        "meta_tokens": nrm(ks[5], (META_TOKENS, D_MODEL), 1.0),
        "norm_w": 1.0 + nrm(ks[6], (DEPTH, D_MODEL), 0.02),
        "w_in": nrm(ks[7], (DEPTH, D_MODEL, PROJ_COLS), D_MODEL ** -0.5),
        "ssd_conv_w": nrm(ks[8], (DEPTH, SSD_CONV_W, SSD_CONV_DIM), SSD_CONV_W ** -0.5),
        "ssd_conv_b": nrm(ks[9], (DEPTH, SSD_CONV_DIM), 0.02),
        "dt_bias": dt_bias,
        "a_log": jnp.log(jax.random.uniform(ks[11], (DEPTH, SSD_HEADS), f32, 1.0, 16.0)),
        "d_skip": 1.0 + nrm(ks[12], (DEPTH, SSD_HEADS), 0.1),
        "ssd_norm_w": 1.0 + nrm(ks[13], (DEPTH, D_INNER), 0.02),
        "w_ssd_out": nrm(ks[14], (DEPTH, D_INNER, D_MODEL), D_INNER ** -0.5),
        "sconv_w": nrm(ks[15], (DEPTH, SC_CONV_W, SC_WIDTH), SC_CONV_W ** -0.5),
        "w_sconv_out": nrm(ks[16], (DEPTH, SC_WIDTH, D_MODEL), SC_WIDTH ** -0.5),
        "w_o": nrm(ks[17], (DEPTH, D_MODEL, D_MODEL), D_MODEL ** -0.5),
        "final_norm_w": 1.0 + nrm(ks[18], (D_MODEL,), 0.02),
    }


def reference(x_prompt, x_sample, state_ssd_conv, state_ssm, state_sconv, meta_tokens, norm_w,
              w_in, ssd_conv_w, ssd_conv_b, dt_bias, a_log, d_skip, ssd_norm_w, w_ssd_out,
              sconv_w, w_sconv_out, w_o, final_norm_w):
    bp, seq = x_prompt.shape[0], x_prompt.shape[1]
    dec_seq = x_sample.shape[1]
    dt_ = x_prompt.dtype
    meta = jnp.broadcast_to(meta_tokens[None].astype(dt_), (bp, META_TOKENS, D_MODEL))
    xp = jnp.concatenate([meta, x_prompt], axis=1)
    xs = x_sample
    prompt_segments = ((META_TOKENS, META_TOKENS), (seq, min(SSD_CHUNK, seq)))
    sample_segments = ((dec_seq, dec_seq),)
    conv0 = jnp.zeros((bp, SSD_CONV_W - 1, SSD_CONV_DIM), dt_)
    ssm0 = jnp.zeros((bp, SSD_HEADS, SSD_HEAD_DIM, SSD_STATE), dt_)
    sconv0 = jnp.zeros((bp, SC_CONV_W - 1, SC_WIDTH), dt_)
    cp_l, sp_l, scp_l, cs_l, ss_l, scs_l = [], [], [], [], [], []
    for layer in range(DEPTH):
        lw = (norm_w[layer], w_in[layer], ssd_conv_w[layer], ssd_conv_b[layer], dt_bias[layer],
              a_log[layer], d_skip[layer], ssd_norm_w[layer], w_ssd_out[layer], sconv_w[layer],
              w_sconv_out[layer], w_o[layer])
        xp, cp, sp, scp = mixer_layer(xp, conv0, ssm0, sconv0, prompt_segments, *lw)
        xs, cs, ss, scs = mixer_layer(xs, state_ssd_conv[layer], state_ssm[layer],
                                      state_sconv[layer], sample_segments, *lw)
        cp_l.append(cp); sp_l.append(sp); scp_l.append(scp)
        cs_l.append(cs); ss_l.append(ss); scs_l.append(scs)
    y_prompt = rms_norm(xp, final_norm_w)[:, META_TOKENS:]
    y_sample = rms_norm(xs, final_norm_w)
    return (y_prompt, y_sample, jnp.stack(cp_l), jnp.stack(sp_l), jnp.stack(scp_l),
            jnp.stack(cs_l), jnp.stack(ss_l), jnp.stack(scs_l))
```

```python
import functools

import jax
import jax.numpy as jnp
from jax import lax
from jax.experimental import pallas as pl
from jax.experimental.pallas import tpu as pltpu

F32 = jnp.float32
BF16 = jnp.bfloat16

D_MODEL = 2048
D_INNER = 4096
N_HEADS = 64
HEAD_DIM = 64
N_STATE = 128
N_GROUPS = 8
GROUP_W = D_INNER // N_GROUPS
HEADS_PER_GROUP = N_HEADS // N_GROUPS
CONV_DIM = D_INNER + 2 * N_GROUPS * N_STATE
SSD_CONV_W = 4
SC_CONV_W = 3
META = 16
EPS = 1e-6

LANES = 128
SUBLANES = 8
ROW_TILE = 128
PROJ_MAIN = 2 * D_INNER + 2 * N_GROUPS * N_STATE + 6 * D_MODEL
OFF_Z = 0
OFF_X = D_INNER
OFF_B = 2 * D_INNER
OFF_C = 2 * D_INNER + N_GROUPS * N_STATE
OFF_SCB = CONV_DIM + D_INNER
OFF_SCC = OFF_SCB + D_MODEL
OFF_SCH = OFF_SCC + D_MODEL
OFF_SCZ = OFF_SCH + D_MODEL
OFF_GA = OFF_SCZ + D_MODEL
OFF_GB = OFF_GA + D_MODEL

VMEM_LIMIT = 52 * 1024 * 1024


def _silu(x):
    return x * jax.nn.sigmoid(x)


def _inproj_kernel(x_ref, nw_ref, w_ref, wdt_ref, o_ref, dt_ref, hs_ref):
    @pl.when(pl.program_id(1) == 0)
    def _():
        xf = x_ref[...]
        ms = jnp.mean(xf * xf, axis=-1, keepdims=True)
        hb = (xf * lax.rsqrt(ms + EPS) * nw_ref[...]).astype(BF16)
        hs_ref[...] = hb
        dt_ref[...] = jnp.dot(hb, wdt_ref[...], preferred_element_type=F32)

    o_ref[...] = jnp.dot(hs_ref[...], w_ref[...], preferred_element_type=F32)


def _inproj(x, norm_w, w_main, w_dt, *, tm, tn):
    rows = x.shape[0]
    return pl.pallas_call(
        _inproj_kernel,
        grid=(rows // tm, PROJ_MAIN // tn),
        in_specs=[
            pl.BlockSpec((tm, D_MODEL), lambda i, j: (i, 0)),
            pl.BlockSpec((1, D_MODEL), lambda i, j: (0, 0)),
            pl.BlockSpec((D_MODEL, tn), lambda i, j: (0, j)),
            pl.BlockSpec((D_MODEL, LANES), lambda i, j: (0, 0)),
        ],
        out_specs=[
            pl.BlockSpec((tm, tn), lambda i, j: (i, j)),
            pl.BlockSpec((tm, LANES), lambda i, j: (i, 0)),
        ],
        out_shape=[
            jax.ShapeDtypeStruct((rows, PROJ_MAIN), F32),
            jax.ShapeDtypeStruct((rows, LANES), F32),
        ],
        scratch_shapes=[pltpu.VMEM((tm, D_MODEL), BF16)],
        compiler_params=pltpu.CompilerParams(
            dimension_semantics=("arbitrary", "arbitrary"),
            vmem_limit_bytes=VMEM_LIMIT),
        name="inproj",
    )(x, norm_w, w_main, w_dt)


def _causal_conv(pad_ref, new_ref, prev_ref, w_ref, bias_ref, *, first, bs, q, width):
    taps = w_ref.shape[0]
    lo = SUBLANES - (taps - 1)

    @pl.when(first)
    def _():
        pad_ref[:, lo:SUBLANES, :] = prev_ref[...]

    pad_ref[:, SUBLANES:SUBLANES + q, :] = new_ref[...].reshape(bs, q, width)
    acc = pad_ref[:, lo:lo + q, :] * w_ref[0:1, :]
    for k in range(1, taps):
        acc = acc + pad_ref[:, lo + k:lo + k + q, :] * w_ref[k:k + 1, :]
    pad_ref[:, lo:SUBLANES, :] = pad_ref[:, SUBLANES + q - (taps - 1):SUBLANES + q, :]
    if bias_ref is not None:
        acc = acc + bias_ref[...]
    return acc.reshape(bs * q, width)


def _seg_cumsum(a, q):
    pos = lax.broadcasted_iota(jnp.int32, a.shape, 0) & (q - 1)
    s = 1
    while s < q:
        shifted = pltpu.roll(a, s, 0)
        a = a + jnp.where(pos >= s, shifted, 0.0)
        s *= 2
    return a


def _ssd_kernel(z_ref, x_ref, b_ref, c_ref, dtr_ref, cpx_ref, cpb_ref, cpc_ref,
                cwx_ref, cwb_ref, cwc_ref, cbx_ref, cbb_ref, cbc_ref,
                dtb_ref, alog_ref, dsk_ref, nw_ref, s0_ref,
                yn_ref, sout_ref,
                padx, padb, padc, st_ref, at_ref, dtt_ref, *, bs, q, nc, valid):
    rt = bs * q
    g = pl.program_id(1)
    c = pl.program_id(2)
    first = c == 0

    @pl.when(first)
    def _():
        st_ref[...] = s0_ref[:, 0]

    conv = functools.partial(_causal_conv, first=first, bs=bs, q=q)
    xc = _silu(conv(padx, x_ref, cpx_ref, cwx_ref, cbx_ref, width=GROUP_W))
    bc = _silu(conv(padb, b_ref, cpb_ref, cwb_ref, cbb_ref, width=N_STATE))
    cc = _silu(conv(padc, c_ref, cpc_ref, cwc_ref, cbc_ref, width=N_STATE))

    dtv = jax.nn.softplus(dtr_ref[...] + dtb_ref[...])
    if valid < rt:
        rows = lax.broadcasted_iota(jnp.int32, dtv.shape, 0)
        dtv = jnp.where(rows < valid, dtv, 0.0)
    acum = _seg_cumsum(dtv * (-jnp.exp(alog_ref[...])), q)
    at_ref[...] = acum.T
    dtt_ref[...] = dtv.T
    r0 = pl.multiple_of(g * HEADS_PER_GROUP, HEADS_PER_GROUP)
    a_t = at_ref[pl.ds(r0, HEADS_PER_GROUP), :]
    d_t = dtt_ref[pl.ds(r0, HEADS_PER_GROUP), :]
    cols = jnp.concatenate(
        [a_t, d_t, jnp.zeros((LANES - 2 * HEADS_PER_GROUP, rt), F32)], axis=0).T

    bcb = bc.astype(BF16)
    ccb = cc.astype(BF16)
    cb = lax.dot_general(ccb, bcb, (((1,), (1,)), ((), ())), preferred_element_type=F32)
    ri = lax.broadcasted_iota(jnp.int32, (rt, rt), 0)
    ci = lax.broadcasted_iota(jnp.int32, (rt, rt), 1)
    mask = (ri >= ci) & ((ri // q) == (ci // q))
    lane = lax.broadcasted_iota(jnp.int32, (rt, LANES), 1)
    low = lane < HEAD_DIM

    ydiag, ea, xw, dec_rows = [], [], [], []
    for pr in range(HEADS_PER_GROUP // 2):
        wts, ab, db = [], [], []
        for hh in range(2):
            h = 2 * pr + hh
            a_col = jnp.broadcast_to(cols[:, h:h + 1], (rt, LANES))
            a_row = jnp.broadcast_to(a_t[h:h + 1, :], (rt, rt))
            seg = jnp.where(mask, a_col - a_row, -jnp.inf)
            wts.append((cb * jnp.exp(seg)).astype(BF16))
            ab.append(a_col)
            db.append(jnp.broadcast_to(
                cols[:, HEADS_PER_GROUP + h:HEADS_PER_GROUP + h + 1], (rt, LANES)))
            dec_rows.append(a_col)
        a_pair = jnp.where(low, ab[0], ab[1])
        d_pair = jnp.where(low, db[0], db[1])
        a3 = a_pair.reshape(bs, q, LANES)
        a_last = jnp.broadcast_to(a3[:, q - 1:q, :], (bs, q, LANES)).reshape(rt, LANES)
        xdt = xc[:, pr * LANES:(pr + 1) * LANES] * d_pair
        xw.append(xdt * jnp.exp(a_last - a_pair))
        ea.append(jnp.exp(a_pair))
        xb = xdt.astype(BF16)
        zero = jnp.zeros_like(xb)
        rhs = jnp.concatenate([jnp.where(low, xb, zero), jnp.where(low, zero, xb)], axis=0)
        lhs = jnp.concatenate(wts, axis=1)
        ydiag.append(jnp.dot(lhs, rhs, preferred_element_type=F32))
    ydiag = jnp.concatenate(ydiag, axis=1)
    ea = jnp.concatenate(ea, axis=1)
    xwt = jnp.concatenate(xw, axis=1).T.astype(BF16)

    seq_of_row = lax.broadcasted_iota(jnp.int32, (rt, N_STATE), 0) // q
    yoff = []
    for s in range(bs):
        st = st_ref[s]
        cs = ccb[s * q:(s + 1) * q, :]
        yoff.append(lax.dot_general(cs, st.astype(BF16), (((1,), (1,)), ((), ())),
                                    preferred_element_type=F32))
        bsel = bcb if bs == 1 else jnp.where(seq_of_row == s, bcb, jnp.zeros_like(bcb))
        upd = jnp.dot(xwt, bsel, preferred_element_type=F32)
        last = (s + 1) * q - 1
        dec = jnp.concatenate(
            [jnp.broadcast_to(jnp.exp(dec_rows[h][last:last + 1, :]), (HEAD_DIM, N_STATE))
             for h in range(HEADS_PER_GROUP)], axis=0)
        new = st * dec + upd
        st_ref[s] = new

        @pl.when(c == nc - 1)
        def _():
            sout_ref[s, 0] = new
    yoff = yoff[0] if bs == 1 else jnp.concatenate(yoff, axis=0)

    y = ydiag + yoff * ea + dsk_ref[...] * xc
    gz = y * _silu(z_ref[...])
    ms = jnp.mean(gz * gz, axis=-1, keepdims=True)
    yn_ref[...] = (gz * lax.rsqrt(ms + EPS) * nw_ref[...]).astype(BF16)


def _ssd(proj, dt_raw, conv_prev, conv_w, conv_b, dt_bias, a_log, d_skip_x, norm_w, state0,
         *, n_seq, bs, q, nc, valid, shared_init):
    rt = bs * q
    assert rt == ROW_TILE
    rows = proj.shape[0]
    nsb = n_seq // bs
    kern = functools.partial(_ssd_kernel, bs=bs, q=q, nc=nc, valid=valid)
    bx, bb, bc_ = OFF_X // GROUP_W, OFF_B // N_STATE, OFF_C // N_STATE
    cvb, cvc = D_INNER // N_STATE, (D_INNER + N_GROUPS * N_STATE) // N_STATE
    sidx = (lambda s: 0) if shared_init else (lambda s: s)

    def tile(s, c):
        return s * nc + c

    in_specs = [
        pl.BlockSpec((rt, GROUP_W), lambda s, g, c: (tile(s, c), g)),
        pl.BlockSpec((rt, GROUP_W), lambda s, g, c: (tile(s, c), bx + g)),
        pl.BlockSpec((rt, N_STATE), lambda s, g, c: (tile(s, c), bb + g)),
        pl.BlockSpec((rt, N_STATE), lambda s, g, c: (tile(s, c), bc_ + g)),
        pl.BlockSpec((rt, LANES), lambda s, g, c: (tile(s, c), 0)),
        pl.BlockSpec((bs, SSD_CONV_W - 1, GROUP_W), lambda s, g, c: (sidx(s), 0, g)),
        pl.BlockSpec((bs, SSD_CONV_W - 1, N_STATE), lambda s, g, c: (sidx(s), 0, cvb + g)),
        pl.BlockSpec((bs, SSD_CONV_W - 1, N_STATE), lambda s, g, c: (sidx(s), 0, cvc + g)),
        pl.BlockSpec((SSD_CONV_W, GROUP_W), lambda s, g, c: (0, g)),
        pl.BlockSpec((SSD_CONV_W, N_STATE), lambda s, g, c: (0, cvb + g)),
        pl.BlockSpec((SSD_CONV_W, N_STATE), lambda s, g, c: (0, cvc + g)),
        pl.BlockSpec((1, GROUP_W), lambda s, g, c: (0, g)),
        pl.BlockSpec((1, N_STATE), lambda s, g, c: (0, cvb + g)),
        pl.BlockSpec((1, N_STATE), lambda s, g, c: (0, cvc + g)),
        pl.BlockSpec((1, LANES), lambda s, g, c: (0, 0)),
        pl.BlockSpec((1, LANES), lambda s, g, c: (0, 0)),
        pl.BlockSpec((1, GROUP_W), lambda s, g, c: (0, g)),
        pl.BlockSpec((1, GROUP_W), lambda s, g, c: (0, g)),
        pl.BlockSpec((bs, 1, GROUP_W, N_STATE), lambda s, g, c: (sidx(s), g, 0, 0)),
    ]
    out_specs = [
        pl.BlockSpec((rt, GROUP_W), lambda s, g, c: (tile(s, c), g)),
        pl.BlockSpec((bs, 1, GROUP_W, N_STATE), lambda s, g, c: (s, g, 0, 0)),
    ]
    return pl.pallas_call(
        kern,
        grid=(nsb, N_GROUPS, nc),
        in_specs=in_specs,
        out_specs=out_specs,
        out_shape=[
            jax.ShapeDtypeStruct((rows, D_INNER), BF16),
            jax.ShapeDtypeStruct((n_seq, N_GROUPS, GROUP_W, N_STATE), F32),
        ],
        scratch_shapes=[
            pltpu.VMEM((bs, SUBLANES + q, GROUP_W), F32),
            pltpu.VMEM((bs, SUBLANES + q, N_STATE), F32),
            pltpu.VMEM((bs, SUBLANES + q, N_STATE), F32),
            pltpu.VMEM((bs, GROUP_W, N_STATE), F32),
            pltpu.VMEM((LANES, rt), F32),
            pltpu.VMEM((LANES, rt), F32),
        ],
        compiler_params=pltpu.CompilerParams(
            dimension_semantics=("arbitrary", "arbitrary", "arbitrary"),
            vmem_limit_bytes=VMEM_LIMIT),
        name="ssd",
    )(proj, proj, proj, proj, dt_raw, conv_prev, conv_prev, conv_prev,
      conv_w, conv_w, conv_w, conv_b, conv_b, conv_b,
      dt_bias, a_log, d_skip_x, norm_w, state0)


def _sconv_kernel(scb_ref, scc_ref, sch_ref, scz_ref, prev_ref, w_ref, v_ref, new_ref,
                  pad, u_ref, *, bs, q, nrt, valid, width):
    r = pl.program_id(2)
    u_ref[...] = scc_ref[...] * sch_ref[...]
    uc = _causal_conv(pad, u_ref, prev_ref, w_ref, None, first=r == 0, bs=bs, q=q, width=width)
    v_ref[...] = (scb_ref[...] * uc * _silu(scz_ref[...])).astype(BF16)

    @pl.when(r == nrt - 1)
    def _():
        end = SUBLANES + valid
        new_ref[...] = pad[:, end - (SC_CONV_W - 1):end, :]


def _sconv(proj, prev, w, *, n_seq, bs, q, nrt, valid, shared_init, width=512):
    rt = bs * q
    rows = proj.shape[0]
    nsb = n_seq // bs
    ncb = D_MODEL // width
    kern = functools.partial(_sconv_kernel, bs=bs, q=q, nrt=nrt, valid=valid, width=width)
    sidx = (lambda s: 0) if shared_init else (lambda s: s)

    def col(off):
        base = off // width
        return lambda s, cbk, r: (s * nrt + r, base + cbk)

    return pl.pallas_call(
        kern,
        grid=(nsb, ncb, nrt),
        in_specs=[
            pl.BlockSpec((rt, width), col(OFF_SCB)),
            pl.BlockSpec((rt, width), col(OFF_SCC)),
            pl.BlockSpec((rt, width), col(OFF_SCH)),
            pl.BlockSpec((rt, width), col(OFF_SCZ)),
            pl.BlockSpec((bs, SC_CONV_W - 1, width), lambda s, cbk, r: (sidx(s), 0, cbk)),
            pl.BlockSpec((SC_CONV_W, width), lambda s, cbk, r: (0, cbk)),
        ],
        out_specs=[
            pl.BlockSpec((rt, width), lambda s, cbk, r: (s * nrt + r, cbk)),
            pl.BlockSpec((bs, SC_CONV_W - 1, width), lambda s, cbk, r: (s, 0, cbk)),
        ],
        out_shape=[
            jax.ShapeDtypeStruct((rows, D_MODEL), BF16),
            jax.ShapeDtypeStruct((n_seq, SC_CONV_W - 1, D_MODEL), F32),
        ],
        scratch_shapes=[
            pltpu.VMEM((bs, SUBLANES + q, width), F32),
            pltpu.VMEM((rt, width), F32),
        ],
        compiler_params=pltpu.CompilerParams(
            dimension_semantics=("arbitrary", "arbitrary", "arbitrary"),
            vmem_limit_bytes=VMEM_LIMIT),
        name="sconv",
    )(proj, proj, proj, proj, prev, w)


def _merge_kernel(yn_ref, v_ref, ga_ref, gb_ref, wa_ref, wb_ref, o_ref):
    ya = jnp.dot(yn_ref[...], wa_ref[...], preferred_element_type=F32)
    yb = jnp.dot(v_ref[...], wb_ref[...], preferred_element_type=F32)
    o_ref[...] = (jax.nn.sigmoid(ga_ref[...]) * ya + jax.nn.sigmoid(gb_ref[...]) * yb).astype(BF16)


def _merge(yn, v, proj, wa, wb, *, tm, tn):
    rows = yn.shape[0]
    ga0, gb0 = OFF_GA // tn, OFF_GB // tn
    return pl.pallas_call(
        _merge_kernel,
        grid=(D_MODEL // tn, rows // tm),
        in_specs=[
            pl.BlockSpec((tm, D_INNER), lambda j, i: (i, 0)),
            pl.BlockSpec((tm, D_MODEL), lambda j, i: (i, 0)),
            pl.BlockSpec((tm, tn), lambda j, i: (i, ga0 + j)),
            pl.BlockSpec((tm, tn), lambda j, i: (i, gb0 + j)),
            pl.BlockSpec((D_INNER, tn), lambda j, i: (0, j)),
            pl.BlockSpec((D_MODEL, tn), lambda j, i: (0, j)),
        ],
        out_specs=pl.BlockSpec((tm, tn), lambda j, i: (i, j)),
        out_shape=jax.ShapeDtypeStruct((rows, D_MODEL), BF16),
        compiler_params=pltpu.CompilerParams(
            dimension_semantics=("arbitrary", "arbitrary"),
            vmem_limit_bytes=VMEM_LIMIT),
        name="merge",
    )(yn, v, proj, proj, wa, wb)


def _outproj_kernel(m_ref, x_ref, wo_ref, fw_ref, o_ref):
    y = x_ref[...] + jnp.dot(m_ref[...], wo_ref[...], preferred_element_type=F32)
    ms = jnp.mean(y * y, axis=-1, keepdims=True)
    o_ref[...] = y * lax.rsqrt(ms + EPS) * fw_ref[...]


def _outproj(m, x, wo, fw, *, tm):
    rows = m.shape[0]
    return pl.pallas_call(
        _outproj_kernel,
        grid=(rows // tm,),
        in_specs=[
            pl.BlockSpec((tm, D_MODEL), lambda i: (i, 0)),
            pl.BlockSpec((tm, D_MODEL), lambda i: (i, 0)),
            pl.BlockSpec((D_MODEL, D_MODEL), lambda i: (0, 0)),
            pl.BlockSpec((1, D_MODEL), lambda i: (0, 0)),
        ],
        out_specs=pl.BlockSpec((tm, D_MODEL), lambda i: (i, 0)),
        out_shape=jax.ShapeDtypeStruct((rows, D_MODEL), F32),
        compiler_params=pltpu.CompilerParams(
            dimension_semantics=("arbitrary",),
            vmem_limit_bytes=VMEM_LIMIT),
        name="outproj",
    )(m, x, wo, fw)


def kernel(x_prompt, x_sample, state_ssd_conv, state_ssm, state_sconv, meta_tokens, norm_w,
           w_in, ssd_conv_w, ssd_conv_b, dt_bias, a_log, d_skip, ssd_norm_w, w_ssd_out,
           sconv_w, w_sconv_out, w_o, final_norm_w):
    bp, seq = x_prompt.shape[0], x_prompt.shape[1]
    bd, dec_seq = x_sample.shape[0], x_sample.shape[1]
    dt_lo = D_INNER + CONV_DIM

    w_in0 = w_in[0]
    w_main = jnp.concatenate([w_in0[:, :dt_lo], w_in0[:, dt_lo + N_HEADS:]], axis=1).astype(BF16)
    w_dt = jnp.pad(w_in0[:, dt_lo:dt_lo + N_HEADS], ((0, 0), (0, LANES - N_HEADS))).astype(BF16)
    wa = w_ssd_out[0].astype(BF16)
    wb = w_sconv_out[0].astype(BF16)
    wo = w_o[0].astype(BF16)
    nw = norm_w[0].reshape(1, D_MODEL)
    fw = final_norm_w.reshape(1, D_MODEL)
    conv_w = ssd_conv_w[0]
    conv_b = ssd_conv_b[0].reshape(1, CONV_DIM)
    dtb = jnp.pad(dt_bias[0], (0, LANES - N_HEADS)).reshape(1, LANES)
    alog = jnp.pad(a_log[0], (0, LANES - N_HEADS)).reshape(1, LANES)
    dsk = jnp.repeat(d_skip[0], HEAD_DIM).reshape(1, D_INNER)
    gnw = ssd_norm_w[0].reshape(1, D_INNER)
    scw = sconv_w[0]

    ssd = functools.partial(_ssd, conv_w=conv_w, conv_b=conv_b, dt_bias=dtb, a_log=alog,
                            d_skip_x=dsk, norm_w=gnw)

    xm = jnp.concatenate([meta_tokens, jnp.zeros((ROW_TILE - META, D_MODEL), F32)], axis=0)
    proj_m, dt_m = _inproj(xm, nw, w_main, w_dt, tm=ROW_TILE, tn=2048)
    _, ssm_m = ssd(proj_m, dt_m, jnp.zeros((1, SSD_CONV_W - 1, CONV_DIM), F32),
                   state0=jnp.zeros((1, N_GROUPS, GROUP_W, N_STATE), F32),
                   n_seq=1, bs=1, q=ROW_TILE, nc=1, valid=META, shared_init=False)
    _, sc_m = _sconv(proj_m, jnp.zeros((1, SC_CONV_W - 1, D_MODEL), F32), scw,
                     n_seq=1, bs=1, q=ROW_TILE, nrt=1, valid=META, shared_init=False)
    conv_m = proj_m[META - (SSD_CONV_W - 1):META, OFF_X:OFF_X + CONV_DIM][None]

    xp = x_prompt.reshape(bp * seq, D_MODEL)
    proj_p, dt_p = _inproj(xp, nw, w_main, w_dt, tm=1024, tn=1024)
    yn_p, ssm_p = ssd(proj_p, dt_p, conv_m, state0=ssm_m, n_seq=bp, bs=1, q=ROW_TILE,
                      nc=seq // ROW_TILE, valid=ROW_TILE, shared_init=True)
    v_p, sc_p = _sconv(proj_p, sc_m, scw, n_seq=bp, bs=1, q=256, nrt=seq // 256, valid=256,
                       shared_init=True)
    m_p = _merge(yn_p, v_p, proj_p, wa, wb, tm=256, tn=1024)
    y_p = _outproj(m_p, xp, wo, fw, tm=512)
    conv_p = proj_p.reshape(bp, seq, PROJ_MAIN)[:, seq - (SSD_CONV_W - 1):, OFF_X:OFF_X + CONV_DIM]

    xs = x_sample.reshape(bd * dec_seq, D_MODEL)
    sbs = ROW_TILE // dec_seq
    proj_s, dt_s = _inproj(xs, nw, w_main, w_dt, tm=1024, tn=1024)
    yn_s, ssm_s = ssd(proj_s, dt_s, state_ssd_conv[0],
                      state0=state_ssm[0].reshape(bd, N_GROUPS, GROUP_W, N_STATE),
                      n_seq=bd, bs=sbs, q=dec_seq, nc=1, valid=ROW_TILE, shared_init=False)
    v_s, sc_s = _sconv(proj_s, state_sconv[0], scw, n_seq=bd, bs=sbs, q=dec_seq, nrt=1,
                       valid=dec_seq, shared_init=False)
    m_s = _merge(yn_s, v_s, proj_s, wa, wb, tm=256, tn=1024)
    y_s = _outproj(m_s, xs, wo, fw, tm=512)
    conv_s = proj_s.reshape(bd, dec_seq, PROJ_MAIN)[:, dec_seq - (SSD_CONV_W - 1):,
                                                    OFF_X:OFF_X + CONV_DIM]

    return (y_p.reshape(bp, seq, D_MODEL),
            y_s.reshape(bd, dec_seq, D_MODEL),
            conv_p[None],
            ssm_p.reshape(1, bp, N_HEADS, HEAD_DIM, N_STATE),
            sc_p[None],
            conv_s[None],
            ssm_s.reshape(1, bd, N_HEADS, HEAD_DIM, N_STATE),
            sc_s[None])
```

```python
import functools

import jax
import jax.numpy as jnp
from jax import lax
from jax.experimental import pallas as pl
from jax.experimental.pallas import tpu as pltpu

F32 = jnp.float32
BF16 = jnp.bfloat16

D_MODEL = 2048
D_INNER = 4096
N_HEADS = 64
HEAD_DIM = 64
N_STATE = 128
N_GROUPS = 8
GROUP_W = D_INNER // N_GROUPS
HEADS_PER_GROUP = N_HEADS // N_GROUPS
CONV_DIM = D_INNER + 2 * N_GROUPS * N_STATE
SSD_CONV_W = 4
SC_CONV_W = 3
META = 16
EPS = 1e-6

LANES = 128
SUBLANES = 8
ROW_TILE = 128
PROJ_MAIN = 2 * D_INNER + 2 * N_GROUPS * N_STATE + 6 * D_MODEL
OFF_Z = 0
OFF_X = D_INNER
OFF_B = 2 * D_INNER
OFF_C = 2 * D_INNER + N_GROUPS * N_STATE
OFF_SCB = CONV_DIM + D_INNER
OFF_SCC = OFF_SCB + D_MODEL
OFF_SCH = OFF_SCC + D_MODEL
OFF_SCZ = OFF_SCH + D_MODEL
OFF_GA = OFF_SCZ + D_MODEL
OFF_GB = OFF_GA + D_MODEL

VMEM_LIMIT = 52 * 1024 * 1024


def _silu(x):
    return x * jax.nn.sigmoid(x)


def _wprep_kernel(a_ref, nxt_ref, dtc_ref, o_ref, odt_ref, *, first_shifted):
    j = pl.program_id(1)

    @pl.when(j == 0)
    def _():
        odt_ref[...] = dtc_ref[...].astype(BF16)

    @pl.when(j < first_shifted)
    def _():
        o_ref[...] = a_ref[...].astype(BF16)

    @pl.when(j >= first_shifted)
    def _():
        both = jnp.concatenate([a_ref[...], nxt_ref[...]], axis=1)
        o_ref[...] = both[:, N_HEADS:N_HEADS + o_ref.shape[1]].astype(BF16)


def _wprep(w, *, tn=1024, tr=512):
    dt_lo = D_INNER + CONV_DIM
    assert dt_lo % tn == 0 and PROJ_MAIN % tn == 0
    kern = functools.partial(_wprep_kernel, first_shifted=dt_lo // tn)
    return pl.pallas_call(
        kern,
        grid=(D_MODEL // tr, PROJ_MAIN // tn),
        in_specs=[
            pl.BlockSpec((tr, tn), lambda r, j: (r, j)),
            pl.BlockSpec((tr, LANES), lambda r, j: (r, (j + 1) * (tn // LANES))),
            pl.BlockSpec((tr, LANES), lambda r, j: (r, dt_lo // LANES)),
        ],
        out_specs=[
            pl.BlockSpec((tr, tn), lambda r, j: (r, j)),
            pl.BlockSpec((tr, LANES), lambda r, j: (r, 0)),
        ],
        out_shape=[
            jax.ShapeDtypeStruct((D_MODEL, PROJ_MAIN), BF16),
            jax.ShapeDtypeStruct((D_MODEL, LANES), BF16),
        ],
        compiler_params=pltpu.CompilerParams(
            dimension_semantics=("arbitrary", "arbitrary"),
            vmem_limit_bytes=VMEM_LIMIT),
        name="wprep",
    )(w, w, w)


def _inproj_kernel(x_ref, nw_ref, w_ref, wdt_ref, o_ref, dt_ref, hs_ref):
    @pl.when(pl.program_id(1) == 0)
    def _():
        xf = x_ref[...]
        ms = jnp.mean(xf * xf, axis=-1, keepdims=True)
        hb = (xf * lax.rsqrt(ms + EPS) * nw_ref[...]).astype(BF16)
        hs_ref[...] = hb
        dt_ref[...] = jnp.dot(hb, wdt_ref[...], preferred_element_type=F32)

    o_ref[...] = jnp.dot(hs_ref[...], w_ref[...], preferred_element_type=F32)


def _inproj(x, norm_w, w_main, w_dt, *, tm, tn):
    rows = x.shape[0]
    return pl.pallas_call(
        _inproj_kernel,
        grid=(rows // tm, PROJ_MAIN // tn),
        in_specs=[
            pl.BlockSpec((tm, D_MODEL), lambda i, j: (i, 0)),
            pl.BlockSpec((1, D_MODEL), lambda i, j: (0, 0)),
            pl.BlockSpec((D_MODEL, tn), lambda i, j: (0, j)),
            pl.BlockSpec((D_MODEL, LANES), lambda i, j: (0, 0)),
        ],
        out_specs=[
            pl.BlockSpec((tm, tn), lambda i, j: (i, j)),
            pl.BlockSpec((tm, LANES), lambda i, j: (i, 0)),
        ],
        out_shape=[
            jax.ShapeDtypeStruct((rows, PROJ_MAIN), F32),
            jax.ShapeDtypeStruct((rows, LANES), F32),
        ],
        scratch_shapes=[pltpu.VMEM((tm, D_MODEL), BF16)],
        compiler_params=pltpu.CompilerParams(
            dimension_semantics=("arbitrary", "arbitrary"),
            vmem_limit_bytes=VMEM_LIMIT),
        name="inproj",
    )(x, norm_w, w_main, w_dt)


def _conv_rows(x, halo_ref, prev_ref, w_ref, *, first, bs, q, carry):
    taps = w_ref.shape[0]
    rt, width = x.shape

    @pl.when(first)
    def _():
        halo_ref[:, SUBLANES - (taps - 1):, :] = prev_ref[...]

    prev = halo_ref[...]
    acc = None
    if bs == 1:
        row = lax.broadcasted_iota(jnp.int32, (SUBLANES, width), 0)
        for s in range(taps - 1, 0, -1):
            rolled = pltpu.roll(x, s, 0)
            head = jnp.where(row < s, pltpu.roll(prev[0], s, 0), rolled[:SUBLANES])
            term = jnp.concatenate([head, rolled[SUBLANES:]], axis=0) * w_ref[taps - 1 - s:taps - s, :]
            acc = term if acc is None else acc + term
        acc = acc + x * w_ref[taps - 1:taps, :]
        if carry:
            halo_ref[0] = x[rt - SUBLANES:, :]
        return acc
    assert q == SUBLANES and not carry
    x3 = x.reshape(bs, q, width)
    row = lax.broadcasted_iota(jnp.int32, x3.shape, 1)
    for s in range(taps - 1, 0, -1):
        shifted = jnp.where(row < s, pltpu.roll(prev, s, 1), pltpu.roll(x3, s, 1))
        term = shifted * w_ref[taps - 1 - s:taps - s, :]
        acc = term if acc is None else acc + term
    acc = acc + x3 * w_ref[taps - 1:taps, :]
    return acc.reshape(rt, width)


def _seg_cumsum(a, q):
    pos = lax.broadcasted_iota(jnp.int32, a.shape, 0) & (q - 1)
    s = 1
    while s < q:
        shifted = pltpu.roll(a, s, 0)
        a = a + jnp.where(pos >= s, shifted, 0.0)
        s *= 2
    return a


def _headscal_kernel(dtr_ref, dtb_ref, alog_ref, at_ref, dtt_ref, *, q, valid):
    dtv = jax.nn.softplus(dtr_ref[...] + dtb_ref[...])
    if valid < dtv.shape[0]:
        rows = lax.broadcasted_iota(jnp.int32, dtv.shape, 0)
        dtv = jnp.where(rows < valid, dtv, 0.0)
    acum = _seg_cumsum(dtv * (-jnp.exp(alog_ref[...])), q)
    at_ref[0] = acum.T
    dtt_ref[0] = dtv.T


def _headscal(dt_raw, dt_bias, a_log, *, q, valid):
    ntiles = dt_raw.shape[0] // ROW_TILE
    kern = functools.partial(_headscal_kernel, q=q, valid=valid)
    shape = jax.ShapeDtypeStruct((ntiles, LANES, ROW_TILE), F32)
    return pl.pallas_call(
        kern,
        grid=(ntiles,),
        in_specs=[
            pl.BlockSpec((ROW_TILE, LANES), lambda t: (t, 0)),
            pl.BlockSpec((1, LANES), lambda t: (0, 0)),
            pl.BlockSpec((1, LANES), lambda t: (0, 0)),
        ],
        out_specs=[
            pl.BlockSpec((1, LANES, ROW_TILE), lambda t: (t, 0, 0)),
            pl.BlockSpec((1, LANES, ROW_TILE), lambda t: (t, 0, 0)),
        ],
        out_shape=[shape, shape],
        compiler_params=pltpu.CompilerParams(dimension_semantics=("arbitrary",)),
        name="headscal",
    )(dt_raw, dt_bias, a_log)


def _ssd_kernel(z_ref, x_ref, b_ref, c_ref, at_ref, dtt_ref, cpx_ref, cpb_ref, cpc_ref,
                cwx_ref, cwb_ref, cwc_ref, cbx_ref, cbb_ref, cbc_ref, dsk_ref, nw_ref, s0_ref,
                yn_ref, sout_ref, hx, hb, hc, st_ref, *, bs, q, nc, gps):
    rt = bs * q
    nh = gps * HEADS_PER_GROUP
    c = pl.program_id(2)
    first = c == 0
    carry = nc > 1

    if carry:
        @pl.when(first)
        def _():
            st_ref[...] = s0_ref[...]

    conv = functools.partial(_conv_rows, first=first, bs=bs, q=q, carry=carry)
    xc = _silu(conv(x_ref[...], hx, cpx_ref, cwx_ref) + cbx_ref[...])
    bcb = _silu(conv(b_ref[...], hb, cpb_ref, cwb_ref) + cbb_ref[...]).astype(BF16)
    ccb = _silu(conv(c_ref[...], hc, cpc_ref, cwc_ref) + cbc_ref[...]).astype(BF16)

    a_t = at_ref[0]
    cols = jnp.concatenate(
        [a_t, dtt_ref[0], jnp.zeros((LANES - 2 * nh, rt), F32)], axis=0).T

    ri = lax.broadcasted_iota(jnp.int32, (rt, rt), 0)
    ci = lax.broadcasted_iota(jnp.int32, (rt, rt), 1)
    mask = (ri >= ci) & ((ri // q) == (ci // q))
    low = lax.broadcasted_iota(jnp.int32, (rt, LANES), 1) < HEAD_DIM
    seq_of_row = lax.broadcasted_iota(jnp.int32, (rt, N_STATE), 0) // q

    for k in range(gps):
        xg = xc[:, k * GROUP_W:(k + 1) * GROUP_W]
        bg = bcb[:, k * N_STATE:(k + 1) * N_STATE]
        cg = ccb[:, k * N_STATE:(k + 1) * N_STATE]
        cb = lax.dot_general(cg, bg, (((1,), (1,)), ((), ())), preferred_element_type=F32)

        ydiag, ea, xw, a_cols = [], [], [], []
        for pr in range(HEADS_PER_GROUP // 2):
            wts, ab, db = [], [], []
            for hh in range(2):
                h = k * HEADS_PER_GROUP + 2 * pr + hh
                a_col = jnp.broadcast_to(cols[:, h:h + 1], (rt, LANES))
                a_row = jnp.broadcast_to(a_t[h:h + 1, :], (rt, rt))
                seg = jnp.where(mask, a_col - a_row, -jnp.inf)
                wts.append((cb * jnp.exp(seg)).astype(BF16))
                ab.append(a_col)
                db.append(jnp.broadcast_to(cols[:, nh + h:nh + h + 1], (rt, LANES)))
            a_cols += ab
            a_pair = jnp.where(low, ab[0], ab[1])
            d_pair = jnp.where(low, db[0], db[1])
            a3 = a_pair.reshape(bs, q, LANES)
            a_last = jnp.broadcast_to(a3[:, q - 1:q, :], (bs, q, LANES)).reshape(rt, LANES)
            xdt = xg[:, pr * LANES:(pr + 1) * LANES] * d_pair
            xw.append(xdt * jnp.exp(a_last - a_pair))
            ea.append(jnp.exp(a_pair))
            xb = xdt.astype(BF16)
            zero = jnp.zeros_like(xb)
            rhs = jnp.concatenate([jnp.where(low, xb, zero), jnp.where(low, zero, xb)], axis=0)
            ydiag.append(jnp.dot(jnp.concatenate(wts, axis=1), rhs, preferred_element_type=F32))
        ydiag = jnp.concatenate(ydiag, axis=1)
        ea = jnp.concatenate(ea, axis=1)
        xwt = jnp.concatenate(xw, axis=1).T.astype(BF16)

        yoff = []
        for s in range(bs):
            st = st_ref[s, k] if carry else s0_ref[s, k]
            yoff.append(lax.dot_general(cg[s * q:(s + 1) * q, :], st.astype(BF16),
                                        (((1,), (1,)), ((), ())), preferred_element_type=F32))
            bsel = bg if bs == 1 else jnp.where(seq_of_row == s, bg, jnp.zeros_like(bg))
            upd = jnp.dot(xwt, bsel, preferred_element_type=F32)
            last = (s + 1) * q - 1
            dec = jnp.concatenate(
                [jnp.broadcast_to(jnp.exp(a_cols[h][last:last + 1, :]), (HEAD_DIM, N_STATE))
                 for h in range(HEADS_PER_GROUP)], axis=0)
            new = st * dec + upd
            if carry:
                st_ref[s, k] = new

                @pl.when(c == nc - 1)
                def _():
                    sout_ref[s, k] = new
            else:
                sout_ref[s, k] = new
        yoff = yoff[0] if bs == 1 else jnp.concatenate(yoff, axis=0)

        lanes = slice(k * GROUP_W, (k + 1) * GROUP_W)
        y = ydiag + yoff * ea + dsk_ref[:, lanes] * xg
        gz = y * _silu(z_ref[:, lanes])
        ms = jnp.mean(gz * gz, axis=-1, keepdims=True)
        yn_ref[:, lanes] = (gz * lax.rsqrt(ms + EPS) * nw_ref[:, lanes]).astype(BF16)


def _ssd(proj, a_t, dt_t, conv_prev, conv_w, conv_b, d_skip_x, norm_w, state0,
         *, n_seq, bs, q, nc, gps, shared_init):
    rt = bs * q
    assert rt == ROW_TILE and N_GROUPS % gps == 0
    rows = proj.shape[0]
    nsb = n_seq // bs
    gw, gn, nh = gps * GROUP_W, gps * N_STATE, gps * HEADS_PER_GROUP
    kern = functools.partial(_ssd_kernel, bs=bs, q=q, nc=nc, gps=gps)
    bx, bb, bc_ = OFF_X // gw, OFF_B // gn, OFF_C // gn
    cvb, cvc = D_INNER // gn, (D_INNER + N_GROUPS * N_STATE) // gn
    sidx = (lambda s: 0) if shared_init else (lambda s: s)

    def tile(s, c):
        return s * nc + c

    in_specs = [
        pl.BlockSpec((rt, gw), lambda s, g, c: (tile(s, c), g)),
        pl.BlockSpec((rt, gw), lambda s, g, c: (tile(s, c), bx + g)),
        pl.BlockSpec((rt, gn), lambda s, g, c: (tile(s, c), bb + g)),
        pl.BlockSpec((rt, gn), lambda s, g, c: (tile(s, c), bc_ + g)),
        pl.BlockSpec((1, nh, rt), lambda s, g, c: (tile(s, c), g, 0)),
        pl.BlockSpec((1, nh, rt), lambda s, g, c: (tile(s, c), g, 0)),
        pl.BlockSpec((bs, SSD_CONV_W - 1, gw), lambda s, g, c: (sidx(s), 0, g)),
        pl.BlockSpec((bs, SSD_CONV_W - 1, gn), lambda s, g, c: (sidx(s), 0, cvb + g)),
        pl.BlockSpec((bs, SSD_CONV_W - 1, gn), lambda s, g, c: (sidx(s), 0, cvc + g)),
        pl.BlockSpec((SSD_CONV_W, gw), lambda s, g, c: (0, g)),
        pl.BlockSpec((SSD_CONV_W, gn), lambda s, g, c: (0, cvb + g)),
        pl.BlockSpec((SSD_CONV_W, gn), lambda s, g, c: (0, cvc + g)),
        pl.BlockSpec((1, gw), lambda s, g, c: (0, g)),
        pl.BlockSpec((1, gn), lambda s, g, c: (0, cvb + g)),
        pl.BlockSpec((1, gn), lambda s, g, c: (0, cvc + g)),
        pl.BlockSpec((1, gw), lambda s, g, c: (0, g)),
        pl.BlockSpec((1, gw), lambda s, g, c: (0, g)),
        pl.BlockSpec((bs, gps, GROUP_W, N_STATE), lambda s, g, c: (sidx(s), g, 0, 0)),
    ]
    out_specs = [
        pl.BlockSpec((rt, gw), lambda s, g, c: (tile(s, c), g)),
        pl.BlockSpec((bs, gps, GROUP_W, N_STATE), lambda s, g, c: (s, g, 0, 0)),
    ]
    st_shape = (bs, gps, GROUP_W, N_STATE) if nc > 1 else (1, 1, SUBLANES, N_STATE)
    return pl.pallas_call(
        kern,
        grid=(nsb, N_GROUPS // gps, nc),
        in_specs=in_specs,
        out_specs=out_specs,
        out_shape=[
            jax.ShapeDtypeStruct((rows, D_INNER), BF16),
            jax.ShapeDtypeStruct((n_seq, N_GROUPS, GROUP_W, N_STATE), F32),
        ],
        scratch_shapes=[
            pltpu.VMEM((bs, SUBLANES, gw), F32),
            pltpu.VMEM((bs, SUBLANES, gn), F32),
            pltpu.VMEM((bs, SUBLANES, gn), F32),
            pltpu.VMEM(st_shape, F32),
        ],
        compiler_params=pltpu.CompilerParams(
            dimension_semantics=("arbitrary", "arbitrary", "arbitrary"),
            vmem_limit_bytes=VMEM_LIMIT),
        name="ssd",
    )(proj, proj, proj, proj, a_t, dt_t, conv_prev, conv_prev, conv_prev,
      conv_w, conv_w, conv_w, conv_b, conv_b, conv_b, d_skip_x, norm_w, state0)


def _sconv_kernel(scb_ref, scc_ref, sch_ref, scz_ref, prev_ref, w_ref, v_ref, new_ref,
                  halo, *, bs, q, nrt, valid):
    r = pl.program_id(2)
    u = scc_ref[...] * sch_ref[...]
    uc = _conv_rows(u, halo, prev_ref, w_ref, first=r == 0, bs=bs, q=q, carry=nrt > 1)
    v_ref[...] = (scb_ref[...] * uc * _silu(scz_ref[...])).astype(BF16)

    @pl.when(r == nrt - 1)
    def _():
        keep = SC_CONV_W - 1
        if bs == 1:
            new_ref[0] = u[valid - keep:valid, :]
        else:
            new_ref[...] = u.reshape(bs, q, u.shape[1])[:, valid - keep:valid, :]


def _sconv(proj, prev, w, *, n_seq, bs, q, nrt, valid, shared_init, width=512):
    rt = bs * q
    rows = proj.shape[0]
    nsb = n_seq // bs
    ncb = D_MODEL // width
    kern = functools.partial(_sconv_kernel, bs=bs, q=q, nrt=nrt, valid=valid)
    sidx = (lambda s: 0) if shared_init else (lambda s: s)

    def col(off):
        base = off // width
        return lambda s, cbk, r: (s * nrt + r, base + cbk)

    return pl.pallas_call(
        kern,
        grid=(nsb, ncb, nrt),
        in_specs=[
            pl.BlockSpec((rt, width), col(OFF_SCB)),
            pl.BlockSpec((rt, width), col(OFF_SCC)),
            pl.BlockSpec((rt, width), col(OFF_SCH)),
            pl.BlockSpec((rt, width), col(OFF_SCZ)),
            pl.BlockSpec((bs, SC_CONV_W - 1, width), lambda s, cbk, r: (sidx(s), 0, cbk)),
            pl.BlockSpec((SC_CONV_W, width), lambda s, cbk, r: (0, cbk)),
        ],
        out_specs=[
            pl.BlockSpec((rt, width), lambda s, cbk, r: (s * nrt + r, cbk)),
            pl.BlockSpec((bs, SC_CONV_W - 1, width), lambda s, cbk, r: (s, 0, cbk)),
        ],
        out_shape=[
            jax.ShapeDtypeStruct((rows, D_MODEL), BF16),
            jax.ShapeDtypeStruct((n_seq, SC_CONV_W - 1, D_MODEL), F32),
        ],
        scratch_shapes=[pltpu.VMEM((bs, SUBLANES, width), F32)],
        compiler_params=pltpu.CompilerParams(
            dimension_semantics=("arbitrary", "arbitrary", "arbitrary"),
            vmem_limit_bytes=VMEM_LIMIT),
        name="sconv",
    )(proj, proj, proj, proj, prev, w)


def _merge_kernel(yn_ref, v_ref, ga_ref, gb_ref, wa_ref, wb_ref, o_ref):
    ya = jnp.dot(yn_ref[...], wa_ref[...], preferred_element_type=F32)
    yb = jnp.dot(v_ref[...], wb_ref[...], preferred_element_type=F32)
    o_ref[...] = (jax.nn.sigmoid(ga_ref[...]) * ya + jax.nn.sigmoid(gb_ref[...]) * yb).astype(BF16)


def _merge(yn, v, proj, wa, wb, *, tm, tn):
    rows = yn.shape[0]
    ga0, gb0 = OFF_GA // tn, OFF_GB // tn
    return pl.pallas_call(
        _merge_kernel,
        grid=(D_MODEL // tn, rows // tm),
        in_specs=[
            pl.BlockSpec((tm, D_INNER), lambda j, i: (i, 0)),
            pl.BlockSpec((tm, D_MODEL), lambda j, i: (i, 0)),
            pl.BlockSpec((tm, tn), lambda j, i: (i, ga0 + j)),
            pl.BlockSpec((tm, tn), lambda j, i: (i, gb0 + j)),
            pl.BlockSpec((D_INNER, tn), lambda j, i: (0, j)),
            pl.BlockSpec((D_MODEL, tn), lambda j, i: (0, j)),
        ],
        out_specs=pl.BlockSpec((tm, tn), lambda j, i: (i, j)),
        out_shape=jax.ShapeDtypeStruct((rows, D_MODEL), BF16),
        compiler_params=pltpu.CompilerParams(
            dimension_semantics=("arbitrary", "arbitrary"),
            vmem_limit_bytes=VMEM_LIMIT),
        name="merge",
    )(yn, v, proj, proj, wa, wb)


def _outproj_kernel(m_ref, x_ref, wo_ref, fw_ref, o_ref):
    y = x_ref[...] + jnp.dot(m_ref[...], wo_ref[...], preferred_element_type=F32)
    ms = jnp.mean(y * y, axis=-1, keepdims=True)
    o_ref[...] = y * lax.rsqrt(ms + EPS) * fw_ref[...]


def _outproj(m, x, wo, fw, *, tm):
    rows = m.shape[0]
    return pl.pallas_call(
        _outproj_kernel,
        grid=(rows // tm,),
        in_specs=[
            pl.BlockSpec((tm, D_MODEL), lambda i: (i, 0)),
            pl.BlockSpec((tm, D_MODEL), lambda i: (i, 0)),
            pl.BlockSpec((D_MODEL, D_MODEL), lambda i: (0, 0)),
            pl.BlockSpec((1, D_MODEL), lambda i: (0, 0)),
        ],
        out_specs=pl.BlockSpec((tm, D_MODEL), lambda i: (i, 0)),
        out_shape=jax.ShapeDtypeStruct((rows, D_MODEL), F32),
        compiler_params=pltpu.CompilerParams(
            dimension_semantics=("arbitrary",),
            vmem_limit_bytes=VMEM_LIMIT),
        name="outproj",
    )(m, x, wo, fw)


def kernel(x_prompt, x_sample, state_ssd_conv, state_ssm, state_sconv, meta_tokens, norm_w,
           w_in, ssd_conv_w, ssd_conv_b, dt_bias, a_log, d_skip, ssd_norm_w, w_ssd_out,
           sconv_w, w_sconv_out, w_o, final_norm_w):
    bp, seq = x_prompt.shape[0], x_prompt.shape[1]
    bd, dec_seq = x_sample.shape[0], x_sample.shape[1]

    w_main, w_dt = _wprep(w_in[0])
    wa = w_ssd_out[0].astype(BF16)
    wb = w_sconv_out[0].astype(BF16)
    wo = w_o[0].astype(BF16)
    nw = norm_w[0].reshape(1, D_MODEL)
    fw = final_norm_w.reshape(1, D_MODEL)
    conv_w = ssd_conv_w[0]
    conv_b = ssd_conv_b[0].reshape(1, CONV_DIM)
    dtb = jnp.pad(dt_bias[0], (0, LANES - N_HEADS)).reshape(1, LANES)
    alog = jnp.pad(a_log[0], (0, LANES - N_HEADS)).reshape(1, LANES)
    dsk = jnp.repeat(d_skip[0], HEAD_DIM).reshape(1, D_INNER)
    gnw = ssd_norm_w[0].reshape(1, D_INNER)
    scw = sconv_w[0]

    ssd = functools.partial(_ssd, conv_w=conv_w, conv_b=conv_b, d_skip_x=dsk, norm_w=gnw)

    xm = jnp.concatenate([meta_tokens, jnp.zeros((ROW_TILE - META, D_MODEL), F32)], axis=0)
    proj_m, dt_m = _inproj(xm, nw, w_main, w_dt, tm=ROW_TILE, tn=2048)
    at_m, dtt_m = _headscal(dt_m, dtb, alog, q=ROW_TILE, valid=META)
    _, ssm_m = ssd(proj_m, at_m, dtt_m, jnp.zeros((1, SSD_CONV_W - 1, CONV_DIM), F32),
                   state0=jnp.zeros((1, N_GROUPS, GROUP_W, N_STATE), F32),
                   n_seq=1, bs=1, q=ROW_TILE, nc=1, gps=2, shared_init=False)
    _, sc_m = _sconv(proj_m, jnp.zeros((1, SC_CONV_W - 1, D_MODEL), F32), scw,
                     n_seq=1, bs=1, q=ROW_TILE, nrt=1, valid=META, shared_init=False)
    conv_m = proj_m[META - (SSD_CONV_W - 1):META, OFF_X:OFF_X + CONV_DIM][None]

    xp = x_prompt.reshape(bp * seq, D_MODEL)
    proj_p, dt_p = _inproj(xp, nw, w_main, w_dt, tm=1024, tn=1024)
    at_p, dtt_p = _headscal(dt_p, dtb, alog, q=ROW_TILE, valid=ROW_TILE)
    yn_p, ssm_p = ssd(proj_p, at_p, dtt_p, conv_m, state0=ssm_m, n_seq=bp, bs=1, q=ROW_TILE,
                      nc=seq // ROW_TILE, gps=2, shared_init=True)
    v_p, sc_p = _sconv(proj_p, sc_m, scw, n_seq=bp, bs=1, q=256, nrt=seq // 256, valid=256,
                       shared_init=True)
    m_p = _merge(yn_p, v_p, proj_p, wa, wb, tm=256, tn=1024)
    y_p = _outproj(m_p, xp, wo, fw, tm=512)
    conv_p = proj_p.reshape(bp, seq, PROJ_MAIN)[:, seq - (SSD_CONV_W - 1):, OFF_X:OFF_X + CONV_DIM]

    xs = x_sample.reshape(bd * dec_seq, D_MODEL)
    sbs = ROW_TILE // dec_seq
    proj_s, dt_s = _inproj(xs, nw, w_main, w_dt, tm=1024, tn=1024)
    at_s, dtt_s = _headscal(dt_s, dtb, alog, q=dec_seq, valid=ROW_TILE)
    yn_s, ssm_s = ssd(proj_s, at_s, dtt_s, state_ssd_conv[0],
                      state0=state_ssm[0].reshape(bd, N_GROUPS, GROUP_W, N_STATE),
                      n_seq=bd, bs=sbs, q=dec_seq, nc=1, gps=1, shared_init=False)
    v_s, sc_s = _sconv(proj_s, state_sconv[0], scw, n_seq=bd, bs=sbs, q=dec_seq, nrt=1,
                       valid=dec_seq, shared_init=False)
    m_s = _merge(yn_s, v_s, proj_s, wa, wb, tm=256, tn=1024)
    y_s = _outproj(m_s, xs, wo, fw, tm=512)
    conv_s = proj_s.reshape(bd, dec_seq, PROJ_MAIN)[:, dec_seq - (SSD_CONV_W - 1):,
                                                    OFF_X:OFF_X + CONV_DIM]

    return (y_p.reshape(bp, seq, D_MODEL),
            y_s.reshape(bd, dec_seq, D_MODEL),
            conv_p[None],
            ssm_p.reshape(1, bp, N_HEADS, HEAD_DIM, N_STATE),
            sc_p[None],
            conv_s[None],
            ssm_s.reshape(1, bd, N_HEADS, HEAD_DIM, N_STATE),
            sc_s[None])
```

```python
import functools

import jax
import jax.numpy as jnp
from jax import lax
from jax.experimental import pallas as pl
from jax.experimental.pallas import tpu as pltpu

F32 = jnp.float32
BF16 = jnp.bfloat16

D_MODEL = 2048
D_INNER = 4096
N_HEADS = 64
HEAD_DIM = 64
N_STATE = 128
N_GROUPS = 8
GROUP_W = D_INNER // N_GROUPS
HEADS_PER_GROUP = N_HEADS // N_GROUPS
CONV_DIM = D_INNER + 2 * N_GROUPS * N_STATE
SSD_CONV_W = 4
SC_CONV_W = 3
META = 16
EPS = 1e-6

LANES = 128
SUBLANES = 8
ROW_TILE = 128
PROJ_MAIN = 2 * D_INNER + 2 * N_GROUPS * N_STATE + 6 * D_MODEL
OFF_Z = 0
OFF_X = D_INNER
OFF_B = 2 * D_INNER
OFF_C = 2 * D_INNER + N_GROUPS * N_STATE
OFF_SCB = CONV_DIM + D_INNER
OFF_SCC = OFF_SCB + D_MODEL
OFF_SCH = OFF_SCC + D_MODEL
OFF_SCZ = OFF_SCH + D_MODEL
OFF_GA = OFF_SCZ + D_MODEL
OFF_GB = OFF_GA + D_MODEL

VMEM_LIMIT = 52 * 1024 * 1024


def _silu(x):
    return x * jax.nn.sigmoid(x)


def _rms_bf16(x, w):
    ms = jnp.mean(x * x, axis=-1, keepdims=True)
    return (x * lax.rsqrt(ms + EPS) * w).astype(BF16)


def _norm_kernel(xp_ref, xs_ref, xm_ref, nw_ref, hs_ref, hm_ref, *, n_prompt):
    i = pl.program_id(0)

    @pl.when(i < n_prompt)
    def _():
        hs_ref[...] = _rms_bf16(xp_ref[...], nw_ref[...])

    @pl.when(i >= n_prompt)
    def _():
        hs_ref[...] = _rms_bf16(xs_ref[...], nw_ref[...])

    @pl.when(i == 0)
    def _():
        hm_ref[:META, :] = _rms_bf16(xm_ref[...], nw_ref[...])
        hm_ref[META:, :] = jnp.zeros((ROW_TILE - META, D_MODEL), BF16)


def _norm(xp, xs, xm, norm_w, *, tm=512):
    n_p, n_s = xp.shape[0] // tm, xs.shape[0] // tm
    kern = functools.partial(_norm_kernel, n_prompt=n_p)
    return pl.pallas_call(
        kern,
        grid=(n_p + n_s,),
        in_specs=[
            pl.BlockSpec((tm, D_MODEL), lambda i: (jnp.minimum(i, n_p - 1), 0)),
            pl.BlockSpec((tm, D_MODEL), lambda i: (jnp.maximum(i - n_p, 0), 0)),
            pl.BlockSpec((META, D_MODEL), lambda i: (0, 0)),
            pl.BlockSpec((1, D_MODEL), lambda i: (0, 0)),
        ],
        out_specs=[
            pl.BlockSpec((tm, D_MODEL), lambda i: (i, 0)),
            pl.BlockSpec((ROW_TILE, D_MODEL), lambda i: (0, 0)),
        ],
        out_shape=[
            jax.ShapeDtypeStruct((xp.shape[0] + xs.shape[0], D_MODEL), BF16),
            jax.ShapeDtypeStruct((ROW_TILE, D_MODEL), BF16),
        ],
        compiler_params=pltpu.CompilerParams(
            dimension_semantics=("arbitrary",), vmem_limit_bytes=VMEM_LIMIT),
        name="norm",
    )(xp, xs, xm, norm_w)


def _inproj_kernel(hs_ref, hm_ref, wt_ref, o_ref, om_ref, wb_ref, lhs_ref, *, tm):
    i = pl.program_id(1)

    @pl.when(i == 0)
    def _():
        wb_ref[...] = wt_ref[...].T.astype(BF16)
        lhs_ref[tm:, :] = hm_ref[...]

    lhs_ref[:tm, :] = hs_ref[...]
    r = jnp.dot(lhs_ref[...], wb_ref[...], preferred_element_type=F32)
    o_ref[...] = r[:tm]

    @pl.when(i == 0)
    def _():
        om_ref[:META, :] = r[tm:]
        om_ref[META:, :] = jnp.zeros((ROW_TILE - META, om_ref.shape[1]), F32)


def _inproj(hs, hm, w_t, *, tm=1024, tn=1024):
    rows = hs.shape[0]
    dt_lo = D_INNER + CONV_DIM
    assert dt_lo % tn == 0 and PROJ_MAIN % tn == 0 and N_HEADS % SUBLANES == 0
    n_before = dt_lo // tn

    def w_rows(j, i):
        return ((j * (tn // N_HEADS) + jnp.where(j < n_before, 0, 1)) * N_HEADS, 0)

    return pl.pallas_call(
        functools.partial(_inproj_kernel, tm=tm),
        grid=(PROJ_MAIN // tn, rows // tm),
        in_specs=[
            pl.BlockSpec((tm, D_MODEL), lambda j, i: (i, 0)),
            pl.BlockSpec((META, D_MODEL), lambda j, i: (0, 0)),
            pl.BlockSpec((pl.Element(tn), pl.Element(D_MODEL)), w_rows),
        ],
        out_specs=[
            pl.BlockSpec((tm, tn), lambda j, i: (i, j)),
            pl.BlockSpec((ROW_TILE, tn), lambda j, i: (0, j)),
        ],
        out_shape=[
            jax.ShapeDtypeStruct((rows, PROJ_MAIN), F32),
            jax.ShapeDtypeStruct((ROW_TILE, PROJ_MAIN), F32),
        ],
        scratch_shapes=[
            pltpu.VMEM((D_MODEL, tn), BF16),
            pltpu.VMEM((tm + META, D_MODEL), BF16),
        ],
        compiler_params=pltpu.CompilerParams(
            dimension_semantics=("arbitrary", "arbitrary"),
            vmem_limit_bytes=VMEM_LIMIT),
        name="inproj",
    )(hs, hm, w_t)


def _conv_rows(x, halo_ref, prev_ref, w_ref, *, first, bs, q, carry):
    taps = w_ref.shape[0]
    rt, width = x.shape

    @pl.when(first)
    def _():
        halo_ref[:, SUBLANES - (taps - 1):, :] = prev_ref[...]

    prev = halo_ref[...]
    acc = None
    if bs == 1:
        row = lax.broadcasted_iota(jnp.int32, (SUBLANES, width), 0)
        for s in range(taps - 1, 0, -1):
            rolled = pltpu.roll(x, s, 0)
            head = jnp.where(row < s, pltpu.roll(prev[0], s, 0), rolled[:SUBLANES])
            term = jnp.concatenate([head, rolled[SUBLANES:]], axis=0) * w_ref[taps - 1 - s:taps - s, :]
            acc = term if acc is None else acc + term
        acc = acc + x * w_ref[taps - 1:taps, :]
        if carry:
            halo_ref[0] = x[rt - SUBLANES:, :]
        return acc
    assert q == SUBLANES and not carry
    x3 = x.reshape(bs, q, width)
    row = lax.broadcasted_iota(jnp.int32, x3.shape, 1)
    for s in range(taps - 1, 0, -1):
        shifted = jnp.where(row < s, pltpu.roll(prev, s, 1), pltpu.roll(x3, s, 1))
        term = shifted * w_ref[taps - 1 - s:taps - s, :]
        acc = term if acc is None else acc + term
    acc = acc + x3 * w_ref[taps - 1:taps, :]
    return acc.reshape(rt, width)


def _seg_cumsum(a, q):
    pos = lax.broadcasted_iota(jnp.int32, a.shape, 0) & (q - 1)
    s = 1
    while s < q:
        shifted = pltpu.roll(a, s, 0)
        a = a + jnp.where(pos >= s, shifted, 0.0)
        s *= 2
    return a


def _headscal_kernel(hs_ref, wdt_ref, dtb_ref, alog_ref, at_ref, dtt_ref, *, q, valid):
    dtr = lax.dot_general(hs_ref[...], wdt_ref[...].astype(BF16), (((1,), (1,)), ((), ())),
                          preferred_element_type=F32)
    dtv = jax.nn.softplus(dtr + dtb_ref[...])
    if valid < dtv.shape[0]:
        rows = lax.broadcasted_iota(jnp.int32, dtv.shape, 0)
        dtv = jnp.where(rows < valid, dtv, 0.0)
    acum = _seg_cumsum(dtv * (-jnp.exp(alog_ref[...])), q)
    at_ref[0] = acum.T
    dtt_ref[0] = dtv.T


def _headscal(hs, w_t, dt_bias, a_log, *, q, valid, tile0, ntiles):
    kern = functools.partial(_headscal_kernel, q=q, valid=valid)
    shape = jax.ShapeDtypeStruct((ntiles, LANES, ROW_TILE), F32)
    dt_blk = (D_INNER + CONV_DIM) // LANES
    return pl.pallas_call(
        kern,
        grid=(ntiles,),
        in_specs=[
            pl.BlockSpec((ROW_TILE, D_MODEL), lambda t: (tile0 + t, 0)),
            pl.BlockSpec((LANES, D_MODEL), lambda t: (dt_blk, 0)),
            pl.BlockSpec((1, LANES), lambda t: (0, 0)),
            pl.BlockSpec((1, LANES), lambda t: (0, 0)),
        ],
        out_specs=[
            pl.BlockSpec((1, LANES, ROW_TILE), lambda t: (t, 0, 0)),
            pl.BlockSpec((1, LANES, ROW_TILE), lambda t: (t, 0, 0)),
        ],
        out_shape=[shape, shape],
        compiler_params=pltpu.CompilerParams(dimension_semantics=("arbitrary",)),
        name="headscal",
    )(hs, w_t, dt_bias, a_log)


def _ssd_kernel(z_ref, x_ref, b_ref, c_ref, at_ref, dtt_ref, cpx_ref, cpb_ref, cpc_ref,
                cwx_ref, cwb_ref, cwc_ref, cbx_ref, cbb_ref, cbc_ref, dsk_ref, nw_ref, s0_ref,
                yn_ref, sout_ref, hx, hb, hc, st_ref, *, bs, q, nc, gps):
    rt = bs * q
    nh = gps * HEADS_PER_GROUP
    c = pl.program_id(2)
    first = c == 0
    carry = nc > 1

    if carry:
        @pl.when(first)
        def _():
            st_ref[...] = s0_ref[...]

    conv = functools.partial(_conv_rows, first=first, bs=bs, q=q, carry=carry)
    xc = _silu(conv(x_ref[...], hx, cpx_ref, cwx_ref) + cbx_ref[...])
    bcb = _silu(conv(b_ref[...], hb, cpb_ref, cwb_ref) + cbb_ref[...]).astype(BF16)
    ccb = _silu(conv(c_ref[...], hc, cpc_ref, cwc_ref) + cbc_ref[...]).astype(BF16)

    a_t = at_ref[0]
    cols = jnp.concatenate(
        [a_t, dtt_ref[0], jnp.zeros((LANES - 2 * nh, rt), F32)], axis=0).T

    ri = lax.broadcasted_iota(jnp.int32, (rt, rt), 0)
    ci = lax.broadcasted_iota(jnp.int32, (rt, rt), 1)
    mask = (ri >= ci) & ((ri // q) == (ci // q))
    low = lax.broadcasted_iota(jnp.int32, (rt, LANES), 1) < HEAD_DIM
    seq_of_row = lax.broadcasted_iota(jnp.int32, (rt, N_STATE), 0) // q

    for k in range(gps):
        xg = xc[:, k * GROUP_W:(k + 1) * GROUP_W]
        bg = bcb[:, k * N_STATE:(k + 1) * N_STATE]
        cg = ccb[:, k * N_STATE:(k + 1) * N_STATE]
        cb = lax.dot_general(cg, bg, (((1,), (1,)), ((), ())), preferred_element_type=F32)

        ydiag, ea, xw, a_cols = [], [], [], []
        for pr in range(HEADS_PER_GROUP // 2):
            wts, ab, db = [], [], []
            for hh in range(2):
                h = k * HEADS_PER_GROUP + 2 * pr + hh
                a_col = jnp.broadcast_to(cols[:, h:h + 1], (rt, LANES))
                a_row = jnp.broadcast_to(a_t[h:h + 1, :], (rt, rt))
                seg = jnp.where(mask, a_col - a_row, -jnp.inf)
                wts.append((cb * jnp.exp(seg)).astype(BF16))
                ab.append(a_col)
                db.append(jnp.broadcast_to(cols[:, nh + h:nh + h + 1], (rt, LANES)))
            a_cols += ab
            a_pair = jnp.where(low, ab[0], ab[1])
            d_pair = jnp.where(low, db[0], db[1])
            a3 = a_pair.reshape(bs, q, LANES)
            a_last = jnp.broadcast_to(a3[:, q - 1:q, :], (bs, q, LANES)).reshape(rt, LANES)
            xdt = xg[:, pr * LANES:(pr + 1) * LANES] * d_pair
            xw.append(xdt * jnp.exp(a_last - a_pair))
            ea.append(jnp.exp(a_pair))
            xb = xdt.astype(BF16)
            zero = jnp.zeros_like(xb)
            rhs = jnp.concatenate([jnp.where(low, xb, zero), jnp.where(low, zero, xb)], axis=0)
            ydiag.append(jnp.dot(jnp.concatenate(wts, axis=1), rhs, preferred_element_type=F32))
        ydiag = jnp.concatenate(ydiag, axis=1)
        ea = jnp.concatenate(ea, axis=1)
        xwt = jnp.concatenate(xw, axis=1).T.astype(BF16)

        yoff = []
        for s in range(bs):
            st = st_ref[s, k] if carry else s0_ref[s, k]
            yoff.append(lax.dot_general(cg[s * q:(s + 1) * q, :], st.astype(BF16),
                                        (((1,), (1,)), ((), ())), preferred_element_type=F32))
            bsel = bg if bs == 1 else jnp.where(seq_of_row == s, bg, jnp.zeros_like(bg))
            upd = jnp.dot(xwt, bsel, preferred_element_type=F32)
            last = (s + 1) * q - 1
            dec = jnp.concatenate(
                [jnp.broadcast_to(jnp.exp(a_cols[h][last:last + 1, :]), (HEAD_DIM, N_STATE))
                 for h in range(HEADS_PER_GROUP)], axis=0)
            new = st * dec + upd
            if carry:
                st_ref[s, k] = new

                @pl.when(c == nc - 1)
                def _():
                    sout_ref[s, k] = new
            else:
                sout_ref[s, k] = new
        yoff = yoff[0] if bs == 1 else jnp.concatenate(yoff, axis=0)

        lanes = slice(k * GROUP_W, (k + 1) * GROUP_W)
        y = ydiag + yoff * ea + dsk_ref[:, lanes] * xg
        gz = y * _silu(z_ref[:, lanes])
        ms = jnp.mean(gz * gz, axis=-1, keepdims=True)
        yn_ref[:, lanes] = (gz * lax.rsqrt(ms + EPS) * nw_ref[:, lanes]).astype(BF16)


def _ssd(proj, a_t, dt_t, conv_prev, conv_w, conv_b, d_skip_x, norm_w, state0,
         *, n_seq, bs, q, nc, gps, shared_init, tile0=0):
    rt = bs * q
    assert rt == ROW_TILE and N_GROUPS % gps == 0
    rows = n_seq * q * nc
    nsb = n_seq // bs
    gw, gn, nh = gps * GROUP_W, gps * N_STATE, gps * HEADS_PER_GROUP
    kern = functools.partial(_ssd_kernel, bs=bs, q=q, nc=nc, gps=gps)
    bx, bb, bc_ = OFF_X // gw, OFF_B // gn, OFF_C // gn
    cvb, cvc = D_INNER // gn, (D_INNER + N_GROUPS * N_STATE) // gn
    sidx = (lambda s: 0) if shared_init else (lambda s: s)

    def tile(s, c):
        return s * nc + c

    in_specs = [
        pl.BlockSpec((rt, gw), lambda s, g, c: (tile0 + tile(s, c), g)),
        pl.BlockSpec((rt, gw), lambda s, g, c: (tile0 + tile(s, c), bx + g)),
        pl.BlockSpec((rt, gn), lambda s, g, c: (tile0 + tile(s, c), bb + g)),
        pl.BlockSpec((rt, gn), lambda s, g, c: (tile0 + tile(s, c), bc_ + g)),
        pl.BlockSpec((1, nh, rt), lambda s, g, c: (tile(s, c), g, 0)),
        pl.BlockSpec((1, nh, rt), lambda s, g, c: (tile(s, c), g, 0)),
        pl.BlockSpec((bs, SSD_CONV_W - 1, gw), lambda s, g, c: (sidx(s), 0, g)),
        pl.BlockSpec((bs, SSD_CONV_W - 1, gn), lambda s, g, c: (sidx(s), 0, cvb + g)),
        pl.BlockSpec((bs, SSD_CONV_W - 1, gn), lambda s, g, c: (sidx(s), 0, cvc + g)),
        pl.BlockSpec((SSD_CONV_W, gw), lambda s, g, c: (0, g)),
        pl.BlockSpec((SSD_CONV_W, gn), lambda s, g, c: (0, cvb + g)),
        pl.BlockSpec((SSD_CONV_W, gn), lambda s, g, c: (0, cvc + g)),
        pl.BlockSpec((1, gw), lambda s, g, c: (0, g)),
        pl.BlockSpec((1, gn), lambda s, g, c: (0, cvb + g)),
        pl.BlockSpec((1, gn), lambda s, g, c: (0, cvc + g)),
        pl.BlockSpec((1, gw), lambda s, g, c: (0, g)),
        pl.BlockSpec((1, gw), lambda s, g, c: (0, g)),
        pl.BlockSpec((bs, gps, GROUP_W, N_STATE), lambda s, g, c: (sidx(s), g, 0, 0)),
    ]
    out_specs = [
        pl.BlockSpec((rt, gw), lambda s, g, c: (tile(s, c), g)),
        pl.BlockSpec((bs, gps, GROUP_W, N_STATE), lambda s, g, c: (s, g, 0, 0)),
    ]
    st_shape = (bs, gps, GROUP_W, N_STATE) if nc > 1 else (1, 1, SUBLANES, N_STATE)
    return pl.pallas_call(
        kern,
        grid=(nsb, N_GROUPS // gps, nc),
        in_specs=in_specs,
        out_specs=out_specs,
        out_shape=[
            jax.ShapeDtypeStruct((rows, D_INNER), BF16),
            jax.ShapeDtypeStruct((n_seq, N_GROUPS, GROUP_W, N_STATE), F32),
        ],
        scratch_shapes=[
            pltpu.VMEM((bs, SUBLANES, gw), F32),
            pltpu.VMEM((bs, SUBLANES, gn), F32),
            pltpu.VMEM((bs, SUBLANES, gn), F32),
            pltpu.VMEM(st_shape, F32),
        ],
        compiler_params=pltpu.CompilerParams(
            dimension_semantics=("arbitrary", "arbitrary", "arbitrary"),
            vmem_limit_bytes=VMEM_LIMIT),
        name="ssd",
    )(proj, proj, proj, proj, a_t, dt_t, conv_prev, conv_prev, conv_prev,
      conv_w, conv_w, conv_w, conv_b, conv_b, conv_b, d_skip_x, norm_w, state0)


def _sconv_kernel(scb_ref, scc_ref, sch_ref, scz_ref, prev_ref, w_ref, v_ref, new_ref,
                  halo, *, bs, q, nrt, valid):
    r = pl.program_id(2)
    u = scc_ref[...] * sch_ref[...]
    uc = _conv_rows(u, halo, prev_ref, w_ref, first=r == 0, bs=bs, q=q, carry=nrt > 1)
    v_ref[...] = (scb_ref[...] * uc * _silu(scz_ref[...])).astype(BF16)

    @pl.when(r == nrt - 1)
    def _():
        keep = SC_CONV_W - 1
        if bs == 1:
            new_ref[0] = u[valid - keep:valid, :]
        else:
            new_ref[...] = u.reshape(bs, q, u.shape[1])[:, valid - keep:valid, :]


def _sconv(proj, prev, w, *, n_seq, bs, q, nrt, valid, shared_init, row0=0, width=512):
    rt = bs * q
    rows = n_seq * q * nrt
    nsb = n_seq // bs
    ncb = D_MODEL // width
    kern = functools.partial(_sconv_kernel, bs=bs, q=q, nrt=nrt, valid=valid)
    sidx = (lambda s: 0) if shared_init else (lambda s: s)
    assert row0 % rt == 0
    tile0 = row0 // rt

    def col(off):
        base = off // width
        return lambda s, cbk, r: (tile0 + s * nrt + r, base + cbk)

    return pl.pallas_call(
        kern,
        grid=(nsb, ncb, nrt),
        in_specs=[
            pl.BlockSpec((rt, width), col(OFF_SCB)),
            pl.BlockSpec((rt, width), col(OFF_SCC)),
            pl.BlockSpec((rt, width), col(OFF_SCH)),
            pl.BlockSpec((rt, width), col(OFF_SCZ)),
            pl.BlockSpec((bs, SC_CONV_W - 1, width), lambda s, cbk, r: (sidx(s), 0, cbk)),
            pl.BlockSpec((SC_CONV_W, width), lambda s, cbk, r: (0, cbk)),
        ],
        out_specs=[
            pl.BlockSpec((rt, width), lambda s, cbk, r: (s * nrt + r, cbk)),
            pl.BlockSpec((bs, SC_CONV_W - 1, width), lambda s, cbk, r: (s, 0, cbk)),
        ],
        out_shape=[
            jax.ShapeDtypeStruct((rows, D_MODEL), BF16),
            jax.ShapeDtypeStruct((n_seq, SC_CONV_W - 1, D_MODEL), F32),
        ],
        scratch_shapes=[pltpu.VMEM((bs, SUBLANES, width), F32)],
        compiler_params=pltpu.CompilerParams(
            dimension_semantics=("arbitrary", "arbitrary", "arbitrary"),
            vmem_limit_bytes=VMEM_LIMIT),
        name="sconv",
    )(proj, proj, proj, proj, prev, w)


def _merge_kernel(yn_ref, v_ref, ga_ref, gb_ref, wa_ref, wb_ref, o_ref):
    ya = jnp.dot(yn_ref[...], wa_ref[...], preferred_element_type=F32)
    yb = jnp.dot(v_ref[...], wb_ref[...], preferred_element_type=F32)
    o_ref[...] = (jax.nn.sigmoid(ga_ref[...]) * ya + jax.nn.sigmoid(gb_ref[...]) * yb).astype(BF16)


def _merge(yn, v, proj, wa, wb, *, tm, tn, row0=0):
    rows = yn.shape[0]
    ga0, gb0 = OFF_GA // tn, OFF_GB // tn
    assert row0 % tm == 0
    t0 = row0 // tm
    return pl.pallas_call(
        _merge_kernel,
        grid=(D_MODEL // tn, rows // tm),
        in_specs=[
            pl.BlockSpec((tm, D_INNER), lambda j, i: (i, 0)),
            pl.BlockSpec((tm, D_MODEL), lambda j, i: (i, 0)),
            pl.BlockSpec((tm, tn), lambda j, i: (t0 + i, ga0 + j)),
            pl.BlockSpec((tm, tn), lambda j, i: (t0 + i, gb0 + j)),
            pl.BlockSpec((D_INNER, tn), lambda j, i: (0, j)),
            pl.BlockSpec((D_MODEL, tn), lambda j, i: (0, j)),
        ],
        out_specs=pl.BlockSpec((tm, tn), lambda j, i: (i, j)),
        out_shape=jax.ShapeDtypeStruct((rows, D_MODEL), BF16),
        compiler_params=pltpu.CompilerParams(
            dimension_semantics=("arbitrary", "arbitrary"),
            vmem_limit_bytes=VMEM_LIMIT),
        name="merge",
    )(yn, v, proj, proj, wa, wb)


def _outproj_kernel(m_ref, x_ref, wo_ref, fw_ref, o_ref):
    y = x_ref[...] + jnp.dot(m_ref[...], wo_ref[...], preferred_element_type=F32)
    ms = jnp.mean(y * y, axis=-1, keepdims=True)
    o_ref[...] = y * lax.rsqrt(ms + EPS) * fw_ref[...]


def _outproj(m, x, wo, fw, *, tm):
    rows = m.shape[0]
    return pl.pallas_call(
        _outproj_kernel,
        grid=(rows // tm,),
        in_specs=[
            pl.BlockSpec((tm, D_MODEL), lambda i: (i, 0)),
            pl.BlockSpec((tm, D_MODEL), lambda i: (i, 0)),
            pl.BlockSpec((D_MODEL, D_MODEL), lambda i: (0, 0)),
            pl.BlockSpec((1, D_MODEL), lambda i: (0, 0)),
        ],
        out_specs=pl.BlockSpec((tm, D_MODEL), lambda i: (i, 0)),
        out_shape=jax.ShapeDtypeStruct((rows, D_MODEL), F32),
        compiler_params=pltpu.CompilerParams(
            dimension_semantics=("arbitrary",),
            vmem_limit_bytes=VMEM_LIMIT),
        name="outproj",
    )(m, x, wo, fw)


def kernel(x_prompt, x_sample, state_ssd_conv, state_ssm, state_sconv, meta_tokens, norm_w,
           w_in, ssd_conv_w, ssd_conv_b, dt_bias, a_log, d_skip, ssd_norm_w, w_ssd_out,
           sconv_w, w_sconv_out, w_o, final_norm_w):
    bp, seq = x_prompt.shape[0], x_prompt.shape[1]
    bd, dec_seq = x_sample.shape[0], x_sample.shape[1]

    w_t = jnp.transpose(w_in[0])
    wa = w_ssd_out[0].astype(BF16)
    wb = w_sconv_out[0].astype(BF16)
    wo = w_o[0].astype(BF16)
    nw = norm_w[0].reshape(1, D_MODEL)
    fw = final_norm_w.reshape(1, D_MODEL)
    conv_w = ssd_conv_w[0]
    conv_b = ssd_conv_b[0].reshape(1, CONV_DIM)
    dtb = jnp.pad(dt_bias[0], (0, LANES - N_HEADS)).reshape(1, LANES)
    alog = jnp.pad(a_log[0], (0, LANES - N_HEADS)).reshape(1, LANES)
    dsk = jnp.repeat(d_skip[0], HEAD_DIM).reshape(1, D_INNER)
    gnw = ssd_norm_w[0].reshape(1, D_INNER)
    scw = sconv_w[0]

    ssd = functools.partial(_ssd, conv_w=conv_w, conv_b=conv_b, d_skip_x=dsk, norm_w=gnw)

    xp = x_prompt.reshape(bp * seq, D_MODEL)
    xs = x_sample.reshape(bd * dec_seq, D_MODEL)
    n_p, n_s = bp * seq, bd * dec_seq
    hs, hm = _norm(xp, xs, meta_tokens, nw)
    proj, proj_m = _inproj(hs, hm, w_t)
    headscal = functools.partial(_headscal, w_t=w_t, dt_bias=dtb, a_log=alog)

    at_m, dtt_m = headscal(hm, q=ROW_TILE, valid=META, tile0=0, ntiles=1)
    _, ssm_m = ssd(proj_m, at_m, dtt_m, jnp.zeros((1, SSD_CONV_W - 1, CONV_DIM), F32),
                   state0=jnp.zeros((1, N_GROUPS, GROUP_W, N_STATE), F32),
                   n_seq=1, bs=1, q=ROW_TILE, nc=1, gps=2, shared_init=False)
    _, sc_m = _sconv(proj_m, jnp.zeros((1, SC_CONV_W - 1, D_MODEL), F32), scw,
                     n_seq=1, bs=1, q=ROW_TILE, nrt=1, valid=META, shared_init=False)
    conv_m = proj_m[META - (SSD_CONV_W - 1):META, OFF_X:OFF_X + CONV_DIM][None]

    at_p, dtt_p = headscal(hs, q=ROW_TILE, valid=ROW_TILE, tile0=0, ntiles=n_p // ROW_TILE)
    yn_p, ssm_p = ssd(proj, at_p, dtt_p, conv_m, state0=ssm_m, n_seq=bp, bs=1, q=ROW_TILE,
                      nc=seq // ROW_TILE, gps=2, shared_init=True)
    v_p, sc_p = _sconv(proj, sc_m, scw, n_seq=bp, bs=1, q=256, nrt=seq // 256, valid=256,
                       shared_init=True)
    m_p = _merge(yn_p, v_p, proj, wa, wb, tm=256, tn=1024)
    y_p = _outproj(m_p, xp, wo, fw, tm=512)
    conv_p = proj[:n_p].reshape(bp, seq, PROJ_MAIN)[:, seq - (SSD_CONV_W - 1):,
                                                    OFF_X:OFF_X + CONV_DIM]

    sbs = ROW_TILE // dec_seq
    at_s, dtt_s = headscal(hs, q=dec_seq, valid=ROW_TILE, tile0=n_p // ROW_TILE,
                           ntiles=n_s // ROW_TILE)
    yn_s, ssm_s = ssd(proj, at_s, dtt_s, state_ssd_conv[0],
                      state0=state_ssm[0].reshape(bd, N_GROUPS, GROUP_W, N_STATE),
                      n_seq=bd, bs=sbs, q=dec_seq, nc=1, gps=1, shared_init=False,
                      tile0=n_p // ROW_TILE)
    v_s, sc_s = _sconv(proj, state_sconv[0], scw, n_seq=bd, bs=sbs, q=dec_seq, nrt=1,
                       valid=dec_seq, shared_init=False, row0=n_p)
    m_s = _merge(yn_s, v_s, proj, wa, wb, tm=256, tn=1024, row0=n_p)
    y_s = _outproj(m_s, xs, wo, fw, tm=512)
    conv_s = proj[n_p:].reshape(bd, dec_seq, PROJ_MAIN)[:, dec_seq - (SSD_CONV_W - 1):,
                                                        OFF_X:OFF_X + CONV_DIM]

    return (y_p.reshape(bp, seq, D_MODEL),
            y_s.reshape(bd, dec_seq, D_MODEL),
            conv_p[None],
            ssm_p.reshape(1, bp, N_HEADS, HEAD_DIM, N_STATE),
            sc_p[None],
            conv_s[None],
            ssm_s.reshape(1, bd, N_HEADS, HEAD_DIM, N_STATE),
            sc_s[None])
```

```python
import functools

import jax
import jax.numpy as jnp
from jax import lax
from jax.experimental import pallas as pl
from jax.experimental.pallas import tpu as pltpu

F32 = jnp.float32
BF16 = jnp.bfloat16

D_MODEL = 2048
D_INNER = 4096
N_HEADS = 64
HEAD_DIM = 64
N_STATE = 128
N_GROUPS = 8
GROUP_W = D_INNER // N_GROUPS
HEADS_PER_GROUP = N_HEADS // N_GROUPS
CONV_DIM = D_INNER + 2 * N_GROUPS * N_STATE
SSD_CONV_W = 4
SC_CONV_W = 3
META = 16
EPS = 1e-6

LANES = 128
SUBLANES = 8
ROW_TILE = 128
PROJ_MAIN = 2 * D_INNER + 2 * N_GROUPS * N_STATE + 6 * D_MODEL
OFF_Z = 0
OFF_X = D_INNER
OFF_B = 2 * D_INNER
OFF_C = 2 * D_INNER + N_GROUPS * N_STATE
OFF_SCB = CONV_DIM + D_INNER
OFF_SCC = OFF_SCB + D_MODEL
OFF_SCH = OFF_SCC + D_MODEL
OFF_SCZ = OFF_SCH + D_MODEL
OFF_GA = OFF_SCZ + D_MODEL
OFF_GB = OFF_GA + D_MODEL

VMEM_LIMIT = 52 * 1024 * 1024


def _silu(x):
    return x * jax.nn.sigmoid(x)


def _rms_bf16(x, w):
    ms = jnp.mean(x * x, axis=-1, keepdims=True)
    return (x * lax.rsqrt(ms + EPS) * w).astype(BF16)


def _norm_kernel(xp_ref, xs_ref, xm_ref, nw_ref, hs_ref, hm_ref, *, n_prompt):
    i = pl.program_id(0)

    @pl.when(i < n_prompt)
    def _():
        hs_ref[...] = _rms_bf16(xp_ref[...], nw_ref[...])

    @pl.when(i >= n_prompt)
    def _():
        hs_ref[...] = _rms_bf16(xs_ref[...], nw_ref[...])

    @pl.when(i == 0)
    def _():
        hm_ref[:META, :] = _rms_bf16(xm_ref[...], nw_ref[...])
        hm_ref[META:, :] = jnp.zeros((ROW_TILE - META, D_MODEL), BF16)


def _norm(xp, xs, xm, norm_w, *, tm=512):
    n_p, n_s = xp.shape[0] // tm, xs.shape[0] // tm
    kern = functools.partial(_norm_kernel, n_prompt=n_p)
    return pl.pallas_call(
        kern,
        grid=(n_p + n_s,),
        in_specs=[
            pl.BlockSpec((tm, D_MODEL), lambda i: (jnp.minimum(i, n_p - 1), 0)),
            pl.BlockSpec((tm, D_MODEL), lambda i: (jnp.maximum(i - n_p, 0), 0)),
            pl.BlockSpec((META, D_MODEL), lambda i: (0, 0)),
            pl.BlockSpec((1, D_MODEL), lambda i: (0, 0)),
        ],
        out_specs=[
            pl.BlockSpec((tm, D_MODEL), lambda i: (i, 0)),
            pl.BlockSpec((ROW_TILE, D_MODEL), lambda i: (0, 0)),
        ],
        out_shape=[
            jax.ShapeDtypeStruct((xp.shape[0] + xs.shape[0], D_MODEL), BF16),
            jax.ShapeDtypeStruct((ROW_TILE, D_MODEL), BF16),
        ],
        compiler_params=pltpu.CompilerParams(
            dimension_semantics=("arbitrary",), vmem_limit_bytes=VMEM_LIMIT),
        name="norm",
    )(xp, xs, xm, norm_w)


def _inproj_kernel(hs_ref, hm_ref, wt_ref, o_ref, om_ref, wb_ref, lhs_ref, *, tm):
    i = pl.program_id(1)

    @pl.when(i == 0)
    def _():
        wb_ref[...] = wt_ref[...].T.astype(BF16)
        lhs_ref[tm:, :] = hm_ref[...]

    lhs_ref[:tm, :] = hs_ref[...]
    r = jnp.dot(lhs_ref[...], wb_ref[...], preferred_element_type=F32)
    o_ref[...] = r[:tm]

    @pl.when(i == 0)
    def _():
        om_ref[:META, :] = r[tm:]
        om_ref[META:, :] = jnp.zeros((ROW_TILE - META, om_ref.shape[1]), F32)


def _inproj(hs, hm, w_t, *, tm=1024, tn=1024):
    rows = hs.shape[0]
    dt_lo = D_INNER + CONV_DIM
    assert dt_lo % tn == 0 and PROJ_MAIN % tn == 0 and N_HEADS % SUBLANES == 0
    n_before = dt_lo // tn

    def w_rows(j, i):
        return ((j * (tn // N_HEADS) + jnp.where(j < n_before, 0, 1)) * N_HEADS, 0)

    return pl.pallas_call(
        functools.partial(_inproj_kernel, tm=tm),
        grid=(PROJ_MAIN // tn, rows // tm),
        in_specs=[
            pl.BlockSpec((tm, D_MODEL), lambda j, i: (i, 0)),
            pl.BlockSpec((META, D_MODEL), lambda j, i: (0, 0)),
            pl.BlockSpec((pl.Element(tn), pl.Element(D_MODEL)), w_rows),
        ],
        out_specs=[
            pl.BlockSpec((tm, tn), lambda j, i: (i, j)),
            pl.BlockSpec((ROW_TILE, tn), lambda j, i: (0, j)),
        ],
        out_shape=[
            jax.ShapeDtypeStruct((rows, PROJ_MAIN), F32),
            jax.ShapeDtypeStruct((ROW_TILE, PROJ_MAIN), F32),
        ],
        scratch_shapes=[
            pltpu.VMEM((D_MODEL, tn), BF16),
            pltpu.VMEM((tm + META, D_MODEL), BF16),
        ],
        compiler_params=pltpu.CompilerParams(
            dimension_semantics=("arbitrary", "arbitrary"),
            vmem_limit_bytes=VMEM_LIMIT),
        name="inproj",
    )(hs, hm, w_t)


def _conv_rows(x, halo_ref, prev_ref, w_ref, *, first, bs, q, carry):
    taps = w_ref.shape[0]
    rt, width = x.shape

    @pl.when(first)
    def _():
        halo_ref[:, SUBLANES - (taps - 1):, :] = prev_ref[...]

    prev = halo_ref[...]
    acc = None
    if bs == 1:
        row = lax.broadcasted_iota(jnp.int32, (SUBLANES, width), 0)
        for s in range(taps - 1, 0, -1):
            rolled = pltpu.roll(x, s, 0)
            head = jnp.where(row < s, pltpu.roll(prev[0], s, 0), rolled[:SUBLANES])
            term = jnp.concatenate([head, rolled[SUBLANES:]], axis=0) * w_ref[taps - 1 - s:taps - s, :]
            acc = term if acc is None else acc + term
        acc = acc + x * w_ref[taps - 1:taps, :]
        if carry:
            halo_ref[0] = x[rt - SUBLANES:, :]
        return acc
    assert q == SUBLANES and not carry
    x3 = x.reshape(bs, q, width)
    row = lax.broadcasted_iota(jnp.int32, x3.shape, 1)
    for s in range(taps - 1, 0, -1):
        shifted = jnp.where(row < s, pltpu.roll(prev, s, 1), pltpu.roll(x3, s, 1))
        term = shifted * w_ref[taps - 1 - s:taps - s, :]
        acc = term if acc is None else acc + term
    acc = acc + x3 * w_ref[taps - 1:taps, :]
    return acc.reshape(rt, width)


def _seg_cumsum(a, q):
    pos = lax.broadcasted_iota(jnp.int32, a.shape, 0) & (q - 1)
    s = 1
    while s < q:
        shifted = pltpu.roll(a, s, 0)
        a = a + jnp.where(pos >= s, shifted, 0.0)
        s *= 2
    return a


def _headscal_kernel(hs_ref, wdt_ref, dtb_ref, alog_ref, at_ref, dtt_ref, *, q, valid):
    dtr = lax.dot_general(hs_ref[...], wdt_ref[...].astype(BF16), (((1,), (1,)), ((), ())),
                          preferred_element_type=F32)
    dtv = jax.nn.softplus(dtr + dtb_ref[...])
    if valid < dtv.shape[0]:
        rows = lax.broadcasted_iota(jnp.int32, dtv.shape, 0)
        dtv = jnp.where(rows < valid, dtv, 0.0)
    acum = _seg_cumsum(dtv * (-jnp.exp(alog_ref[...])), q)
    at_ref[0] = acum.T
    dtt_ref[0] = dtv.T


def _headscal(hs, w_t, dt_bias, a_log, *, q, valid, tile0, ntiles):
    kern = functools.partial(_headscal_kernel, q=q, valid=valid)
    shape = jax.ShapeDtypeStruct((ntiles, LANES, ROW_TILE), F32)
    dt_blk = (D_INNER + CONV_DIM) // LANES
    return pl.pallas_call(
        kern,
        grid=(ntiles,),
        in_specs=[
            pl.BlockSpec((ROW_TILE, D_MODEL), lambda t: (tile0 + t, 0)),
            pl.BlockSpec((LANES, D_MODEL), lambda t: (dt_blk, 0)),
            pl.BlockSpec((1, LANES), lambda t: (0, 0)),
            pl.BlockSpec((1, LANES), lambda t: (0, 0)),
        ],
        out_specs=[
            pl.BlockSpec((1, LANES, ROW_TILE), lambda t: (t, 0, 0)),
            pl.BlockSpec((1, LANES, ROW_TILE), lambda t: (t, 0, 0)),
        ],
        out_shape=[shape, shape],
        compiler_params=pltpu.CompilerParams(dimension_semantics=("arbitrary",)),
        name="headscal",
    )(hs, w_t, dt_bias, a_log)


def _ssd_kernel(z_ref, x_ref, b_ref, c_ref, at_ref, dtt_ref, cpx_ref, cpb_ref, cpc_ref,
                cwx_ref, cwb_ref, cwc_ref, cbx_ref, cbb_ref, cbc_ref, dsk_ref, nw_ref, s0_ref,
                yn_ref, sout_ref, cnx_ref, cnb_ref, cnc_ref, hx, hb, hc, st_ref,
                *, bs, q, nc, gps, valid):
    rt = bs * q
    nh = gps * HEADS_PER_GROUP
    c = pl.program_id(2)
    first = c == 0
    carry = nc > 1

    if carry:
        @pl.when(first)
        def _():
            st_ref[...] = s0_ref[...]

    conv = functools.partial(_conv_rows, first=first, bs=bs, q=q, carry=carry)
    xc = _silu(conv(x_ref[...], hx, cpx_ref, cwx_ref) + cbx_ref[...])
    bcb = _silu(conv(b_ref[...], hb, cpb_ref, cwb_ref) + cbb_ref[...]).astype(BF16)
    ccb = _silu(conv(c_ref[...], hc, cpc_ref, cwc_ref) + cbc_ref[...]).astype(BF16)

    @pl.when(c == nc - 1)
    def _():
        keep = SSD_CONV_W - 1
        for src, dst in ((x_ref, cnx_ref), (b_ref, cnb_ref), (c_ref, cnc_ref)):
            if bs == 1:
                dst[0] = src[valid - keep:valid, :]
            else:
                dst[...] = src[...].reshape(bs, q, src.shape[1])[:, valid - keep:valid, :]

    a_t = at_ref[0]
    cols = jnp.concatenate(
        [a_t, dtt_ref[0], jnp.zeros((LANES - 2 * nh, rt), F32)], axis=0).T

    ri = lax.broadcasted_iota(jnp.int32, (rt, rt), 0)
    ci = lax.broadcasted_iota(jnp.int32, (rt, rt), 1)
    mask = (ri >= ci) & ((ri // q) == (ci // q))
    low = lax.broadcasted_iota(jnp.int32, (rt, LANES), 1) < HEAD_DIM
    seq_of_row = lax.broadcasted_iota(jnp.int32, (rt, N_STATE), 0) // q

    for k in range(gps):
        xg = xc[:, k * GROUP_W:(k + 1) * GROUP_W]
        bg = bcb[:, k * N_STATE:(k + 1) * N_STATE]
        cg = ccb[:, k * N_STATE:(k + 1) * N_STATE]
        cb = lax.dot_general(cg, bg, (((1,), (1,)), ((), ())), preferred_element_type=F32)

        ydiag, ea, xw, a_cols = [], [], [], []
        for pr in range(HEADS_PER_GROUP // 2):
            wts, ab, db = [], [], []
            for hh in range(2):
                h = k * HEADS_PER_GROUP + 2 * pr + hh
                a_col = jnp.broadcast_to(cols[:, h:h + 1], (rt, LANES))
                a_row = jnp.broadcast_to(a_t[h:h + 1, :], (rt, rt))
                seg = jnp.where(mask, a_col - a_row, -jnp.inf)
                wts.append((cb * jnp.exp(seg)).astype(BF16))
                ab.append(a_col)
                db.append(jnp.broadcast_to(cols[:, nh + h:nh + h + 1], (rt, LANES)))
            a_cols += ab
            a_pair = jnp.where(low, ab[0], ab[1])
            d_pair = jnp.where(low, db[0], db[1])
            a3 = a_pair.reshape(bs, q, LANES)
            a_last = jnp.broadcast_to(a3[:, q - 1:q, :], (bs, q, LANES)).reshape(rt, LANES)
            xdt = xg[:, pr * LANES:(pr + 1) * LANES] * d_pair
            xw.append(xdt * jnp.exp(a_last - a_pair))
            ea.append(jnp.exp(a_pair))
            xb = xdt.astype(BF16)
            zero = jnp.zeros_like(xb)
            rhs = jnp.concatenate([jnp.where(low, xb, zero), jnp.where(low, zero, xb)], axis=0)
            ydiag.append(jnp.dot(jnp.concatenate(wts, axis=1), rhs, preferred_element_type=F32))
        ydiag = jnp.concatenate(ydiag, axis=1)
        ea = jnp.concatenate(ea, axis=1)
        xwt = jnp.concatenate(xw, axis=1).T.astype(BF16)

        yoff = []
        for s in range(bs):
            st = st_ref[s, k] if carry else s0_ref[s, k]
            yoff.append(lax.dot_general(cg[s * q:(s + 1) * q, :], st.astype(BF16),
                                        (((1,), (1,)), ((), ())), preferred_element_type=F32))
            bsel = bg if bs == 1 else jnp.where(seq_of_row == s, bg, jnp.zeros_like(bg))
            upd = jnp.dot(xwt, bsel, preferred_element_type=F32)
            last = (s + 1) * q - 1
            dec = jnp.concatenate(
                [jnp.broadcast_to(jnp.exp(a_cols[h][last:last + 1, :]), (HEAD_DIM, N_STATE))
                 for h in range(HEADS_PER_GROUP)], axis=0)
            new = st * dec + upd
            if carry:
                st_ref[s, k] = new

                @pl.when(c == nc - 1)
                def _():
                    sout_ref[s, k] = new
            else:
                sout_ref[s, k] = new
        yoff = yoff[0] if bs == 1 else jnp.concatenate(yoff, axis=0)

        lanes = slice(k * GROUP_W, (k + 1) * GROUP_W)
        y = ydiag + yoff * ea + dsk_ref[:, lanes] * xg
        gz = y * _silu(z_ref[:, lanes])
        ms = jnp.mean(gz * gz, axis=-1, keepdims=True)
        yn_ref[:, lanes] = (gz * lax.rsqrt(ms + EPS) * nw_ref[:, lanes]).astype(BF16)


def _ssd(proj, a_t, dt_t, conv_prev, conv_w, conv_b, d_skip_x, norm_w, state0,
         *, n_seq, bs, q, nc, gps, valid, shared_init, tile0=0):
    rt = bs * q
    assert rt == ROW_TILE and N_GROUPS % gps == 0
    rows = n_seq * q * nc
    nsb = n_seq // bs
    gw, gn, nh = gps * GROUP_W, gps * N_STATE, gps * HEADS_PER_GROUP
    kern = functools.partial(_ssd_kernel, bs=bs, q=q, nc=nc, gps=gps, valid=valid)
    bx, bb, bc_ = OFF_X // gw, OFF_B // gn, OFF_C // gn
    cvb, cvc = D_INNER // gn, (D_INNER + N_GROUPS * N_STATE) // gn
    sidx = (lambda s: 0) if shared_init else (lambda s: s)

    def tile(s, c):
        return s * nc + c

    in_specs = [
        pl.BlockSpec((rt, gw), lambda s, g, c: (tile0 + tile(s, c), g)),
        pl.BlockSpec((rt, gw), lambda s, g, c: (tile0 + tile(s, c), bx + g)),
        pl.BlockSpec((rt, gn), lambda s, g, c: (tile0 + tile(s, c), bb + g)),
        pl.BlockSpec((rt, gn), lambda s, g, c: (tile0 + tile(s, c), bc_ + g)),
        pl.BlockSpec((1, nh, rt), lambda s, g, c: (tile(s, c), g, 0)),
        pl.BlockSpec((1, nh, rt), lambda s, g, c: (tile(s, c), g, 0)),
        pl.BlockSpec((bs, SSD_CONV_W - 1, gw), lambda s, g, c: (sidx(s), 0, g)),
        pl.BlockSpec((bs, SSD_CONV_W - 1, gn), lambda s, g, c: (sidx(s), 0, cvb + g)),
        pl.BlockSpec((bs, SSD_CONV_W - 1, gn), lambda s, g, c: (sidx(s), 0, cvc + g)),
        pl.BlockSpec((SSD_CONV_W, gw), lambda s, g, c: (0, g)),
        pl.BlockSpec((SSD_CONV_W, gn), lambda s, g, c: (0, cvb + g)),
        pl.BlockSpec((SSD_CONV_W, gn), lambda s, g, c: (0, cvc + g)),
        pl.BlockSpec((1, gw), lambda s, g, c: (0, g)),
        pl.BlockSpec((1, gn), lambda s, g, c: (0, cvb + g)),
        pl.BlockSpec((1, gn), lambda s, g, c: (0, cvc + g)),
        pl.BlockSpec((1, gw), lambda s, g, c: (0, g)),
        pl.BlockSpec((1, gw), lambda s, g, c: (0, g)),
        pl.BlockSpec((bs, gps, GROUP_W, N_STATE), lambda s, g, c: (sidx(s), g, 0, 0)),
    ]
    keep = SSD_CONV_W - 1
    out_specs = [
        pl.BlockSpec((rt, gw), lambda s, g, c: (tile(s, c), g)),
        pl.BlockSpec((bs, gps, GROUP_W, N_STATE), lambda s, g, c: (s, g, 0, 0)),
        pl.BlockSpec((bs, keep, gw), lambda s, g, c: (s, 0, g)),
        pl.BlockSpec((bs, keep, gn), lambda s, g, c: (s, 0, g)),
        pl.BlockSpec((bs, keep, gn), lambda s, g, c: (s, 0, g)),
    ]
    st_shape = (bs, gps, GROUP_W, N_STATE) if nc > 1 else (1, 1, SUBLANES, N_STATE)
    return pl.pallas_call(
        kern,
        grid=(nsb, N_GROUPS // gps, nc),
        in_specs=in_specs,
        out_specs=out_specs,
        out_shape=[
            jax.ShapeDtypeStruct((rows, D_INNER), BF16),
            jax.ShapeDtypeStruct((n_seq, N_GROUPS, GROUP_W, N_STATE), F32),
            jax.ShapeDtypeStruct((n_seq, keep, D_INNER), F32),
            jax.ShapeDtypeStruct((n_seq, keep, N_GROUPS * N_STATE), F32),
            jax.ShapeDtypeStruct((n_seq, keep, N_GROUPS * N_STATE), F32),
        ],
        scratch_shapes=[
            pltpu.VMEM((bs, SUBLANES, gw), F32),
            pltpu.VMEM((bs, SUBLANES, gn), F32),
            pltpu.VMEM((bs, SUBLANES, gn), F32),
            pltpu.VMEM(st_shape, F32),
        ],
        compiler_params=pltpu.CompilerParams(
            dimension_semantics=("arbitrary", "arbitrary", "arbitrary"),
            vmem_limit_bytes=VMEM_LIMIT),
        name="ssd",
    )(proj, proj, proj, proj, a_t, dt_t, conv_prev, conv_prev, conv_prev,
      conv_w, conv_w, conv_w, conv_b, conv_b, conv_b, d_skip_x, norm_w, state0)


def _sconv_kernel(scb_ref, scc_ref, sch_ref, scz_ref, prev_ref, w_ref, v_ref, new_ref,
                  halo, *, bs, q, nrt, valid):
    r = pl.program_id(2)
    u = scc_ref[...] * sch_ref[...]
    uc = _conv_rows(u, halo, prev_ref, w_ref, first=r == 0, bs=bs, q=q, carry=nrt > 1)
    v_ref[...] = (scb_ref[...] * uc * _silu(scz_ref[...])).astype(BF16)

    @pl.when(r == nrt - 1)
    def _():
        keep = SC_CONV_W - 1
        if bs == 1:
            new_ref[0] = u[valid - keep:valid, :]
        else:
            new_ref[...] = u.reshape(bs, q, u.shape[1])[:, valid - keep:valid, :]


def _sconv(proj, prev, w, *, n_seq, bs, q, nrt, valid, shared_init, row0=0, width=512):
    rt = bs * q
    rows = n_seq * q * nrt
    nsb = n_seq // bs
    ncb = D_MODEL // width
    kern = functools.partial(_sconv_kernel, bs=bs, q=q, nrt=nrt, valid=valid)
    sidx = (lambda s: 0) if shared_init else (lambda s: s)
    assert row0 % rt == 0
    tile0 = row0 // rt

    def col(off):
        base = off // width
        return lambda s, cbk, r: (tile0 + s * nrt + r, base + cbk)

    return pl.pallas_call(
        kern,
        grid=(nsb, ncb, nrt),
        in_specs=[
            pl.BlockSpec((rt, width), col(OFF_SCB)),
            pl.BlockSpec((rt, width), col(OFF_SCC)),
            pl.BlockSpec((rt, width), col(OFF_SCH)),
            pl.BlockSpec((rt, width), col(OFF_SCZ)),
            pl.BlockSpec((bs, SC_CONV_W - 1, width), lambda s, cbk, r: (sidx(s), 0, cbk)),
            pl.BlockSpec((SC_CONV_W, width), lambda s, cbk, r: (0, cbk)),
        ],
        out_specs=[
            pl.BlockSpec((rt, width), lambda s, cbk, r: (s * nrt + r, cbk)),
            pl.BlockSpec((bs, SC_CONV_W - 1, width), lambda s, cbk, r: (s, 0, cbk)),
        ],
        out_shape=[
            jax.ShapeDtypeStruct((rows, D_MODEL), BF16),
            jax.ShapeDtypeStruct((n_seq, SC_CONV_W - 1, D_MODEL), F32),
        ],
        scratch_shapes=[pltpu.VMEM((bs, SUBLANES, width), F32)],
        compiler_params=pltpu.CompilerParams(
            dimension_semantics=("arbitrary", "arbitrary", "arbitrary"),
            vmem_limit_bytes=VMEM_LIMIT),
        name="sconv",
    )(proj, proj, proj, proj, prev, w)


def _merge_kernel(yn_ref, v_ref, ga_ref, gb_ref, wa_ref, wb_ref, o_ref):
    ya = jnp.dot(yn_ref[...], wa_ref[...], preferred_element_type=F32)
    yb = jnp.dot(v_ref[...], wb_ref[...], preferred_element_type=F32)
    o_ref[...] = (jax.nn.sigmoid(ga_ref[...]) * ya + jax.nn.sigmoid(gb_ref[...]) * yb).astype(BF16)


def _merge(yn, v, proj, wa, wb, *, tm, tn, row0=0):
    rows = yn.shape[0]
    ga0, gb0 = OFF_GA // tn, OFF_GB // tn
    assert row0 % tm == 0
    t0 = row0 // tm
    return pl.pallas_call(
        _merge_kernel,
        grid=(D_MODEL // tn, rows // tm),
        in_specs=[
            pl.BlockSpec((tm, D_INNER), lambda j, i: (i, 0)),
            pl.BlockSpec((tm, D_MODEL), lambda j, i: (i, 0)),
            pl.BlockSpec((tm, tn), lambda j, i: (t0 + i, ga0 + j)),
            pl.BlockSpec((tm, tn), lambda j, i: (t0 + i, gb0 + j)),
            pl.BlockSpec((D_INNER, tn), lambda j, i: (0, j)),
            pl.BlockSpec((D_MODEL, tn), lambda j, i: (0, j)),
        ],
        out_specs=pl.BlockSpec((tm, tn), lambda j, i: (i, j)),
        out_shape=jax.ShapeDtypeStruct((rows, D_MODEL), BF16),
        compiler_params=pltpu.CompilerParams(
            dimension_semantics=("arbitrary", "arbitrary"),
            vmem_limit_bytes=VMEM_LIMIT),
        name="merge",
    )(yn, v, proj, proj, wa, wb)


def _outproj_kernel(m_ref, x_ref, wo_ref, fw_ref, o_ref):
    y = x_ref[...] + jnp.dot(m_ref[...], wo_ref[...], preferred_element_type=F32)
    ms = jnp.mean(y * y, axis=-1, keepdims=True)
    o_ref[...] = y * lax.rsqrt(ms + EPS) * fw_ref[...]


def _outproj(m, x, wo, fw, *, tm):
    rows = m.shape[0]
    return pl.pallas_call(
        _outproj_kernel,
        grid=(rows // tm,),
        in_specs=[
            pl.BlockSpec((tm, D_MODEL), lambda i: (i, 0)),
            pl.BlockSpec((tm, D_MODEL), lambda i: (i, 0)),
            pl.BlockSpec((D_MODEL, D_MODEL), lambda i: (0, 0)),
            pl.BlockSpec((1, D_MODEL), lambda i: (0, 0)),
        ],
        out_specs=pl.BlockSpec((tm, D_MODEL), lambda i: (i, 0)),
        out_shape=jax.ShapeDtypeStruct((rows, D_MODEL), F32),
        compiler_params=pltpu.CompilerParams(
            dimension_semantics=("arbitrary",),
            vmem_limit_bytes=VMEM_LIMIT),
        name="outproj",
    )(m, x, wo, fw)


def kernel(x_prompt, x_sample, state_ssd_conv, state_ssm, state_sconv, meta_tokens, norm_w,
           w_in, ssd_conv_w, ssd_conv_b, dt_bias, a_log, d_skip, ssd_norm_w, w_ssd_out,
           sconv_w, w_sconv_out, w_o, final_norm_w):
    bp, seq = x_prompt.shape[0], x_prompt.shape[1]
    bd, dec_seq = x_sample.shape[0], x_sample.shape[1]

    w_t = jnp.transpose(w_in[0])
    wa = w_ssd_out[0].astype(BF16)
    wb = w_sconv_out[0].astype(BF16)
    wo = w_o[0].astype(BF16)
    nw = norm_w[0].reshape(1, D_MODEL)
    fw = final_norm_w.reshape(1, D_MODEL)
    conv_w = ssd_conv_w[0]
    conv_b = ssd_conv_b[0].reshape(1, CONV_DIM)
    dtb = jnp.pad(dt_bias[0], (0, LANES - N_HEADS)).reshape(1, LANES)
    alog = jnp.pad(a_log[0], (0, LANES - N_HEADS)).reshape(1, LANES)
    dsk = jnp.repeat(d_skip[0], HEAD_DIM).reshape(1, D_INNER)
    gnw = ssd_norm_w[0].reshape(1, D_INNER)
    scw = sconv_w[0]

    ssd = functools.partial(_ssd, conv_w=conv_w, conv_b=conv_b, d_skip_x=dsk, norm_w=gnw)

    xp = x_prompt.reshape(bp * seq, D_MODEL)
    xs = x_sample.reshape(bd * dec_seq, D_MODEL)
    n_p, n_s = bp * seq, bd * dec_seq
    hs, hm = _norm(xp, xs, meta_tokens, nw)
    proj, proj_m = _inproj(hs, hm, w_t)
    headscal = functools.partial(_headscal, w_t=w_t, dt_bias=dtb, a_log=alog)

    at_m, dtt_m = headscal(hm, q=ROW_TILE, valid=META, tile0=0, ntiles=1)
    _, ssm_m, *cn_m = ssd(proj_m, at_m, dtt_m, jnp.zeros((1, SSD_CONV_W - 1, CONV_DIM), F32),
                          state0=jnp.zeros((1, N_GROUPS, GROUP_W, N_STATE), F32),
                          n_seq=1, bs=1, q=ROW_TILE, nc=1, gps=2, valid=META, shared_init=False)
    _, sc_m = _sconv(proj_m, jnp.zeros((1, SC_CONV_W - 1, D_MODEL), F32), scw,
                     n_seq=1, bs=1, q=ROW_TILE, nrt=1, valid=META, shared_init=False)
    conv_m = jnp.concatenate(cn_m, axis=-1)

    at_p, dtt_p = headscal(hs, q=ROW_TILE, valid=ROW_TILE, tile0=0, ntiles=n_p // ROW_TILE)
    yn_p, ssm_p, *cn_p = ssd(proj, at_p, dtt_p, conv_m, state0=ssm_m, n_seq=bp, bs=1,
                             q=ROW_TILE, nc=seq // ROW_TILE, gps=2, valid=ROW_TILE,
                             shared_init=True)
    v_p, sc_p = _sconv(proj, sc_m, scw, n_seq=bp, bs=1, q=256, nrt=seq // 256, valid=256,
                       shared_init=True)
    m_p = _merge(yn_p, v_p, proj, wa, wb, tm=256, tn=1024)
    y_p = _outproj(m_p, xp, wo, fw, tm=512)
    conv_p = jnp.concatenate(cn_p, axis=-1)

    sbs = ROW_TILE // dec_seq
    at_s, dtt_s = headscal(hs, q=dec_seq, valid=ROW_TILE, tile0=n_p // ROW_TILE,
                           ntiles=n_s // ROW_TILE)
    yn_s, ssm_s, *cn_s = ssd(proj, at_s, dtt_s, state_ssd_conv[0],
                             state0=state_ssm[0].reshape(bd, N_GROUPS, GROUP_W, N_STATE),
                             n_seq=bd, bs=sbs, q=dec_seq, nc=1, gps=1, valid=dec_seq,
                             shared_init=False, tile0=n_p // ROW_TILE)
    v_s, sc_s = _sconv(proj, state_sconv[0], scw, n_seq=bd, bs=sbs, q=dec_seq, nrt=1,
                       valid=dec_seq, shared_init=False, row0=n_p)
    m_s = _merge(yn_s, v_s, proj, wa, wb, tm=256, tn=1024, row0=n_p)
    y_s = _outproj(m_s, xs, wo, fw, tm=512)
    conv_s = jnp.concatenate(cn_s, axis=-1)

    return (y_p.reshape(bp, seq, D_MODEL),
            y_s.reshape(bd, dec_seq, D_MODEL),
            conv_p[None],
            ssm_p.reshape(1, bp, N_HEADS, HEAD_DIM, N_STATE),
            sc_p[None],
            conv_s[None],
            ssm_s.reshape(1, bd, N_HEADS, HEAD_DIM, N_STATE),
            sc_s[None])
```

```python
import functools

import jax
import jax.numpy as jnp
from jax import lax
from jax.experimental import pallas as pl
from jax.experimental.pallas import tpu as pltpu

F32 = jnp.float32
BF16 = jnp.bfloat16

D_MODEL = 2048
D_INNER = 4096
N_HEADS = 64
HEAD_DIM = 64
N_STATE = 128
N_GROUPS = 8
GROUP_W = D_INNER // N_GROUPS
HEADS_PER_GROUP = N_HEADS // N_GROUPS
CONV_DIM = D_INNER + 2 * N_GROUPS * N_STATE
SSD_CONV_W = 4
SC_CONV_W = 3
META = 16
EPS = 1e-6

LANES = 128
SUBLANES = 8
ROW_TILE = 128
W_DT = D_INNER + CONV_DIM
W_SC = W_DT + N_HEADS
W_GATE = W_SC + 4 * D_MODEL
PROJ_MAIN = D_INNER + CONV_DIM + 2 * D_MODEL
OFF_Z = 0
OFF_X = D_INNER
OFF_B = 2 * D_INNER
OFF_C = 2 * D_INNER + N_GROUPS * N_STATE
OFF_GA = D_INNER + CONV_DIM
OFF_GB = OFF_GA + D_MODEL

VMEM_LIMIT = 52 * 1024 * 1024


def _silu(x):
    return x * jax.nn.sigmoid(x)


def _rms_bf16(x, w):
    ms = jnp.mean(x * x, axis=-1, keepdims=True)
    return (x * lax.rsqrt(ms + EPS) * w).astype(BF16)


def _norm_kernel(xp_ref, xs_ref, xm_ref, nw_ref, hs_ref, hm_ref, *, n_prompt):
    i = pl.program_id(0)

    @pl.when(i < n_prompt)
    def _():
        hs_ref[...] = _rms_bf16(xp_ref[...], nw_ref[...])

    @pl.when(i >= n_prompt)
    def _():
        hs_ref[...] = _rms_bf16(xs_ref[...], nw_ref[...])

    @pl.when(i == 0)
    def _():
        hm_ref[:META, :] = _rms_bf16(xm_ref[...], nw_ref[...])
        hm_ref[META:, :] = jnp.zeros((ROW_TILE - META, D_MODEL), BF16)


def _norm(xp, xs, xm, norm_w, *, tm=512):
    n_p, n_s = xp.shape[0] // tm, xs.shape[0] // tm
    kern = functools.partial(_norm_kernel, n_prompt=n_p)
    return pl.pallas_call(
        kern,
        grid=(n_p + n_s,),
        in_specs=[
            pl.BlockSpec((tm, D_MODEL), lambda i: (jnp.minimum(i, n_p - 1), 0)),
            pl.BlockSpec((tm, D_MODEL), lambda i: (jnp.maximum(i - n_p, 0), 0)),
            pl.BlockSpec((META, D_MODEL), lambda i: (0, 0)),
            pl.BlockSpec((1, D_MODEL), lambda i: (0, 0)),
        ],
        out_specs=[
            pl.BlockSpec((tm, D_MODEL), lambda i: (i, 0)),
            pl.BlockSpec((ROW_TILE, D_MODEL), lambda i: (0, 0)),
        ],
        out_shape=[
            jax.ShapeDtypeStruct((xp.shape[0] + xs.shape[0], D_MODEL), BF16),
            jax.ShapeDtypeStruct((ROW_TILE, D_MODEL), BF16),
        ],
        compiler_params=pltpu.CompilerParams(
            dimension_semantics=("arbitrary",), vmem_limit_bytes=VMEM_LIMIT),
        name="norm",
    )(xp, xs, xm, norm_w)


def _inproj_kernel(hs_ref, hm_ref, wt_ref, o_ref, om_ref, wb_ref, lhs_ref, *, tm):
    i = pl.program_id(1)

    @pl.when(i == 0)
    def _():
        wb_ref[...] = wt_ref[...].T.astype(BF16)
        lhs_ref[tm:, :] = hm_ref[...]

    lhs_ref[:tm, :] = hs_ref[...]
    r = jnp.dot(lhs_ref[...], wb_ref[...], preferred_element_type=F32)
    o_ref[...] = r[:tm]

    @pl.when(i == 0)
    def _():
        om_ref[:META, :] = r[tm:]
        om_ref[META:, :] = jnp.zeros((ROW_TILE - META, om_ref.shape[1]), F32)


def _inproj(hs, hm, w_t, *, tm=1024, tn=1024):
    rows = hs.shape[0]
    assert W_DT % tn == 0 and PROJ_MAIN % tn == 0 and (W_GATE - W_DT) % N_HEADS == 0
    n_before = W_DT // tn

    def w_rows(j, i):
        skip = (W_GATE - W_DT) // N_HEADS
        return ((j * (tn // N_HEADS) + jnp.where(j < n_before, 0, skip)) * N_HEADS, 0)

    return pl.pallas_call(
        functools.partial(_inproj_kernel, tm=tm),
        grid=(PROJ_MAIN // tn, rows // tm),
        in_specs=[
            pl.BlockSpec((tm, D_MODEL), lambda j, i: (i, 0)),
            pl.BlockSpec((META, D_MODEL), lambda j, i: (0, 0)),
            pl.BlockSpec((pl.Element(tn), pl.Element(D_MODEL)), w_rows),
        ],
        out_specs=[
            pl.BlockSpec((tm, tn), lambda j, i: (i, j)),
            pl.BlockSpec((ROW_TILE, tn), lambda j, i: (0, j)),
        ],
        out_shape=[
            jax.ShapeDtypeStruct((rows, PROJ_MAIN), F32),
            jax.ShapeDtypeStruct((ROW_TILE, PROJ_MAIN), F32),
        ],
        scratch_shapes=[
            pltpu.VMEM((D_MODEL, tn), BF16),
            pltpu.VMEM((tm + META, D_MODEL), BF16),
        ],
        compiler_params=pltpu.CompilerParams(
            dimension_semantics=("arbitrary", "arbitrary"),
            vmem_limit_bytes=VMEM_LIMIT),
        name="inproj",
    )(hs, hm, w_t)


def _conv_rows(x, halo_ref, prev_ref, w_ref, *, first, bs, q, carry):
    taps = w_ref.shape[0]
    rt, width = x.shape

    @pl.when(first)
    def _():
        halo_ref[:, SUBLANES - (taps - 1):, :] = prev_ref[...]

    prev = halo_ref[...]
    acc = None
    if bs == 1:
        row = lax.broadcasted_iota(jnp.int32, (SUBLANES, width), 0)
        for s in range(taps - 1, 0, -1):
            rolled = pltpu.roll(x, s, 0)
            head = jnp.where(row < s, pltpu.roll(prev[0], s, 0), rolled[:SUBLANES])
            term = jnp.concatenate([head, rolled[SUBLANES:]], axis=0) * w_ref[taps - 1 - s:taps - s, :]
            acc = term if acc is None else acc + term
        acc = acc + x * w_ref[taps - 1:taps, :]
        if carry:
            halo_ref[0] = x[rt - SUBLANES:, :]
        return acc
    assert q == SUBLANES and not carry
    x3 = x.reshape(bs, q, width)
    row = lax.broadcasted_iota(jnp.int32, x3.shape, 1)
    for s in range(taps - 1, 0, -1):
        shifted = jnp.where(row < s, pltpu.roll(prev, s, 1), pltpu.roll(x3, s, 1))
        term = shifted * w_ref[taps - 1 - s:taps - s, :]
        acc = term if acc is None else acc + term
    acc = acc + x3 * w_ref[taps - 1:taps, :]
    return acc.reshape(rt, width)


def _seg_cumsum(a, q):
    pos = lax.broadcasted_iota(jnp.int32, a.shape, 0) & (q - 1)
    s = 1
    while s < q:
        shifted = pltpu.roll(a, s, 0)
        a = a + jnp.where(pos >= s, shifted, 0.0)
        s *= 2
    return a


def _headscal_kernel(hs_ref, wdt_ref, dtb_ref, alog_ref, at_ref, dtt_ref, *, q, valid, tps):
    dtr = lax.dot_general(hs_ref[...], wdt_ref[...].astype(BF16), (((1,), (1,)), ((), ())),
                          preferred_element_type=F32)
    dtv = jax.nn.softplus(dtr + dtb_ref[...])
    if valid < ROW_TILE:
        rows = lax.broadcasted_iota(jnp.int32, dtv.shape, 0)
        dtv = jnp.where(rows < valid, dtv, 0.0)
    acum = _seg_cumsum(dtv * (-jnp.exp(alog_ref[...])), q)
    for t in range(tps):
        at_ref[t] = acum[t * ROW_TILE:(t + 1) * ROW_TILE].T
        dtt_ref[t] = dtv[t * ROW_TILE:(t + 1) * ROW_TILE].T


def _headscal(hs, w_t, dt_bias, a_log, *, q, valid, tile0, ntiles):
    tps = min(ntiles, 8)
    assert ntiles % tps == 0 and tile0 % tps == 0 and q <= ROW_TILE
    kern = functools.partial(_headscal_kernel, q=q, valid=valid, tps=tps)
    shape = jax.ShapeDtypeStruct((ntiles, LANES, ROW_TILE), F32)
    dt_blk = W_DT // LANES
    return pl.pallas_call(
        kern,
        grid=(ntiles // tps,),
        in_specs=[
            pl.BlockSpec((tps * ROW_TILE, D_MODEL), lambda t: (tile0 // tps + t, 0)),
            pl.BlockSpec((LANES, D_MODEL), lambda t: (dt_blk, 0)),
            pl.BlockSpec((1, LANES), lambda t: (0, 0)),
            pl.BlockSpec((1, LANES), lambda t: (0, 0)),
        ],
        out_specs=[
            pl.BlockSpec((tps, LANES, ROW_TILE), lambda t: (t, 0, 0)),
            pl.BlockSpec((tps, LANES, ROW_TILE), lambda t: (t, 0, 0)),
        ],
        out_shape=[shape, shape],
        compiler_params=pltpu.CompilerParams(dimension_semantics=("arbitrary",)),
        name="headscal",
    )(hs, w_t, dt_bias, a_log)


def _ssd_kernel(z_ref, x_ref, b_ref, c_ref, at_ref, dtt_ref, cpx_ref, cpb_ref, cpc_ref,
                cwx_ref, cwb_ref, cwc_ref, cbx_ref, cbb_ref, cbc_ref, dsk_ref, nw_ref, s0_ref,
                yn_ref, sout_ref, cnx_ref, cnb_ref, cnc_ref, hx, hb, hc, st_ref,
                *, bs, q, nc, gps, valid):
    rt = bs * q
    nh = gps * HEADS_PER_GROUP
    c = pl.program_id(2)
    first = c == 0
    carry = nc > 1

    if carry:
        @pl.when(first)
        def _():
            st_ref[...] = s0_ref[...]

    conv = functools.partial(_conv_rows, first=first, bs=bs, q=q, carry=carry)
    xc = _silu(conv(x_ref[...], hx, cpx_ref, cwx_ref) + cbx_ref[...])
    bcb = _silu(conv(b_ref[...], hb, cpb_ref, cwb_ref) + cbb_ref[...]).astype(BF16)
    ccb = _silu(conv(c_ref[...], hc, cpc_ref, cwc_ref) + cbc_ref[...]).astype(BF16)

    @pl.when(c == nc - 1)
    def _():
        keep = SSD_CONV_W - 1
        for src, dst in ((x_ref, cnx_ref), (b_ref, cnb_ref), (c_ref, cnc_ref)):
            if bs == 1:
                dst[0] = src[valid - keep:valid, :]
            else:
                dst[...] = src[...].reshape(bs, q, src.shape[1])[:, valid - keep:valid, :]

    a_t = at_ref[0]
    cols = jnp.concatenate(
        [a_t, dtt_ref[0], jnp.zeros((LANES - 2 * nh, rt), F32)], axis=0).T

    ri = lax.broadcasted_iota(jnp.int32, (rt, rt), 0)
    ci = lax.broadcasted_iota(jnp.int32, (rt, rt), 1)
    mask = (ri >= ci) & ((ri // q) == (ci // q))
    low = lax.broadcasted_iota(jnp.int32, (rt, LANES), 1) < HEAD_DIM
    seq_of_row = lax.broadcasted_iota(jnp.int32, (rt, N_STATE), 0) // q

    for k in range(gps):
        xg = xc[:, k * GROUP_W:(k + 1) * GROUP_W]
        bg = bcb[:, k * N_STATE:(k + 1) * N_STATE]
        cg = ccb[:, k * N_STATE:(k + 1) * N_STATE]
        cb = lax.dot_general(cg, bg, (((1,), (1,)), ((), ())), preferred_element_type=F32)

        ydiag, ea, xw, a_cols = [], [], [], []
        for pr in range(HEADS_PER_GROUP // 2):
            wts, ab, db = [], [], []
            for hh in range(2):
                h = k * HEADS_PER_GROUP + 2 * pr + hh
                a_col = jnp.broadcast_to(cols[:, h:h + 1], (rt, LANES))
                a_row = jnp.broadcast_to(a_t[h:h + 1, :], (rt, rt))
                seg = jnp.where(mask, a_col - a_row, -jnp.inf)
                wts.append((cb * jnp.exp(seg)).astype(BF16))
                ab.append(a_col)
                db.append(jnp.broadcast_to(cols[:, nh + h:nh + h + 1], (rt, LANES)))
            a_cols += ab
            a_pair = jnp.where(low, ab[0], ab[1])
            d_pair = jnp.where(low, db[0], db[1])
            a3 = a_pair.reshape(bs, q, LANES)
            a_last = jnp.broadcast_to(a3[:, q - 1:q, :], (bs, q, LANES)).reshape(rt, LANES)
            xdt = xg[:, pr * LANES:(pr + 1) * LANES] * d_pair
            xw.append(xdt * jnp.exp(a_last - a_pair))
            ea.append(jnp.exp(a_pair))
            xb = xdt.astype(BF16)
            zero = jnp.zeros_like(xb)
            rhs = jnp.concatenate([jnp.where(low, xb, zero), jnp.where(low, zero, xb)], axis=0)
            ydiag.append(jnp.dot(jnp.concatenate(wts, axis=1), rhs, preferred_element_type=F32))
        ydiag = jnp.concatenate(ydiag, axis=1)
        ea = jnp.concatenate(ea, axis=1)
        xwt = jnp.concatenate(xw, axis=1).T.astype(BF16)

        yoff = []
        for s in range(bs):
            st = st_ref[s, k] if carry else s0_ref[s, k]
            yoff.append(lax.dot_general(cg[s * q:(s + 1) * q, :], st.astype(BF16),
                                        (((1,), (1,)), ((), ())), preferred_element_type=F32))
            bsel = bg if bs == 1 else jnp.where(seq_of_row == s, bg, jnp.zeros_like(bg))
            upd = jnp.dot(xwt, bsel, preferred_element_type=F32)
            last = (s + 1) * q - 1
            dec = jnp.concatenate(
                [jnp.broadcast_to(jnp.exp(a_cols[h][last:last + 1, :]), (HEAD_DIM, N_STATE))
                 for h in range(HEADS_PER_GROUP)], axis=0)
            new = st * dec + upd
            if carry:
                st_ref[s, k] = new

                @pl.when(c == nc - 1)
                def _():
                    sout_ref[s, k] = new
            else:
                sout_ref[s, k] = new
        yoff = yoff[0] if bs == 1 else jnp.concatenate(yoff, axis=0)

        lanes = slice(k * GROUP_W, (k + 1) * GROUP_W)
        y = ydiag + yoff * ea + dsk_ref[:, lanes] * xg
        gz = y * _silu(z_ref[:, lanes])
        ms = jnp.mean(gz * gz, axis=-1, keepdims=True)
        yn_ref[:, lanes] = (gz * lax.rsqrt(ms + EPS) * nw_ref[:, lanes]).astype(BF16)


def _ssd(proj, a_t, dt_t, conv_prev, conv_w, conv_b, d_skip_x, norm_w, state0,
         *, n_seq, bs, q, nc, gps, valid, shared_init, tile0=0):
    rt = bs * q
    assert rt == ROW_TILE and N_GROUPS % gps == 0
    rows = n_seq * q * nc
    nsb = n_seq // bs
    gw, gn, nh = gps * GROUP_W, gps * N_STATE, gps * HEADS_PER_GROUP
    kern = functools.partial(_ssd_kernel, bs=bs, q=q, nc=nc, gps=gps, valid=valid)
    bx, bb, bc_ = OFF_X // gw, OFF_B // gn, OFF_C // gn
    cvb, cvc = D_INNER // gn, (D_INNER + N_GROUPS * N_STATE) // gn
    sidx = (lambda s: 0) if shared_init else (lambda s: s)

    def tile(s, c):
        return s * nc + c

    in_specs = [
        pl.BlockSpec((rt, gw), lambda s, g, c: (tile0 + tile(s, c), g)),
        pl.BlockSpec((rt, gw), lambda s, g, c: (tile0 + tile(s, c), bx + g)),
        pl.BlockSpec((rt, gn), lambda s, g, c: (tile0 + tile(s, c), bb + g)),
        pl.BlockSpec((rt, gn), lambda s, g, c: (tile0 + tile(s, c), bc_ + g)),
        pl.BlockSpec((1, nh, rt), lambda s, g, c: (tile(s, c), g, 0)),
        pl.BlockSpec((1, nh, rt), lambda s, g, c: (tile(s, c), g, 0)),
        pl.BlockSpec((bs, SSD_CONV_W - 1, gw), lambda s, g, c: (sidx(s), 0, g)),
        pl.BlockSpec((bs, SSD_CONV_W - 1, gn), lambda s, g, c: (sidx(s), 0, cvb + g)),
        pl.BlockSpec((bs, SSD_CONV_W - 1, gn), lambda s, g, c: (sidx(s), 0, cvc + g)),
        pl.BlockSpec((SSD_CONV_W, gw), lambda s, g, c: (0, g)),
        pl.BlockSpec((SSD_CONV_W, gn), lambda s, g, c: (0, cvb + g)),
        pl.BlockSpec((SSD_CONV_W, gn), lambda s, g, c: (0, cvc + g)),
        pl.BlockSpec((1, gw), lambda s, g, c: (0, g)),
        pl.BlockSpec((1, gn), lambda s, g, c: (0, cvb + g)),
        pl.BlockSpec((1, gn), lambda s, g, c: (0, cvc + g)),
        pl.BlockSpec((1, gw), lambda s, g, c: (0, g)),
        pl.BlockSpec((1, gw), lambda s, g, c: (0, g)),
        pl.BlockSpec((bs, gps, GROUP_W, N_STATE), lambda s, g, c: (sidx(s), g, 0, 0)),
    ]
    keep = SSD_CONV_W - 1
    out_specs = [
        pl.BlockSpec((rt, gw), lambda s, g, c: (tile(s, c), g)),
        pl.BlockSpec((bs, gps, GROUP_W, N_STATE), lambda s, g, c: (s, g, 0, 0)),
        pl.BlockSpec((bs, keep, gw), lambda s, g, c: (s, 0, g)),
        pl.BlockSpec((bs, keep, gn), lambda s, g, c: (s, 0, g)),
        pl.BlockSpec((bs, keep, gn), lambda s, g, c: (s, 0, g)),
    ]
    st_shape = (bs, gps, GROUP_W, N_STATE) if nc > 1 else (1, 1, SUBLANES, N_STATE)
    return pl.pallas_call(
        kern,
        grid=(nsb, N_GROUPS // gps, nc),
        in_specs=in_specs,
        out_specs=out_specs,
        out_shape=[
            jax.ShapeDtypeStruct((rows, D_INNER), BF16),
            jax.ShapeDtypeStruct((n_seq, N_GROUPS, GROUP_W, N_STATE), F32),
            jax.ShapeDtypeStruct((n_seq, keep, D_INNER), F32),
            jax.ShapeDtypeStruct((n_seq, keep, N_GROUPS * N_STATE), F32),
            jax.ShapeDtypeStruct((n_seq, keep, N_GROUPS * N_STATE), F32),
        ],
        scratch_shapes=[
            pltpu.VMEM((bs, SUBLANES, gw), F32),
            pltpu.VMEM((bs, SUBLANES, gn), F32),
            pltpu.VMEM((bs, SUBLANES, gn), F32),
            pltpu.VMEM(st_shape, F32),
        ],
        compiler_params=pltpu.CompilerParams(
            dimension_semantics=("arbitrary", "arbitrary", "arbitrary"),
            vmem_limit_bytes=VMEM_LIMIT),
        name="ssd",
    )(proj, proj, proj, proj, a_t, dt_t, conv_prev, conv_prev, conv_prev,
      conv_w, conv_w, conv_w, conv_b, conv_b, conv_b, d_skip_x, norm_w, state0)


def _sconv_kernel(hs_ref, hm_ref, wb_ref, wc_ref, wh_ref, wz_ref, cw_ref, prev_ref,
                  v_ref, newp_ref, news_ref,
                  w_scr, lhs_scr, halo_p, halo_s, meta_u,
                  *, tm, width, n_prompt_tiles, tiles_per_seq, bs_sample, q_sample):
    i = pl.program_id(1)
    keep = SC_CONV_W - 1

    @pl.when(i == 0)
    def _():
        for k, w_ref in enumerate((wb_ref, wc_ref, wh_ref, wz_ref)):
            w_scr[:, k * width:(k + 1) * width] = w_ref[...].T.astype(BF16)
        lhs_scr[tm:, :] = hm_ref[...]

    lhs_scr[:tm, :] = hs_ref[...]
    r = jnp.dot(lhs_scr[...], w_scr[...], preferred_element_type=F32)
    u_all = r[:, width:2 * width] * r[:, 2 * width:3 * width]
    u = u_all[:tm]

    @pl.when(i == 0)
    def _():
        meta_u[0] = u_all[tm + META - keep:tm + META]

    def finish(uc):
        v_ref[...] = (r[:tm, :width] * uc * _silu(r[:tm, 3 * width:])).astype(BF16)

    @pl.when(i < n_prompt_tiles)
    def _():
        finish(_conv_rows(u, halo_p, meta_u, cw_ref, first=(i % tiles_per_seq) == 0,
                          bs=1, q=tm, carry=True))
        newp_ref[0] = u[tm - keep:]

    @pl.when(i >= n_prompt_tiles)
    def _():
        finish(_conv_rows(u, halo_s, prev_ref, cw_ref, first=i >= n_prompt_tiles,
                          bs=bs_sample, q=q_sample, carry=False))
        news_ref[...] = u.reshape(bs_sample, q_sample, width)[:, q_sample - keep:, :]


def _sconv(hs, hm, w_t, cw, prev_s, *, n_prompt, seq, n_seq_p, n_seq_s, q_sample,
           tm=1024, width=256):
    rows = hs.shape[0]
    keep = SC_CONV_W - 1
    assert seq % tm == 0 and (rows - n_prompt) == tm and tm == n_seq_s * q_sample
    tiles_per_seq = seq // tm
    n_pt = n_prompt // tm
    kern = functools.partial(_sconv_kernel, tm=tm, width=width, n_prompt_tiles=n_pt,
                             tiles_per_seq=tiles_per_seq, bs_sample=n_seq_s, q_sample=q_sample)

    def w_rows(k):
        base = (W_SC + k * D_MODEL) // N_HEADS
        return lambda cbk, i: ((base + cbk * (width // N_HEADS)) * N_HEADS, 0)

    w_specs = [pl.BlockSpec((pl.Element(width), pl.Element(D_MODEL)), w_rows(k)) for k in range(4)]
    return pl.pallas_call(
        kern,
        grid=(D_MODEL // width, rows // tm),
        in_specs=[
            pl.BlockSpec((tm, D_MODEL), lambda cbk, i: (i, 0)),
            pl.BlockSpec((META, D_MODEL), lambda cbk, i: (0, 0)),
            *w_specs,
            pl.BlockSpec((SC_CONV_W, width), lambda cbk, i: (0, cbk)),
            pl.BlockSpec((n_seq_s, keep, width), lambda cbk, i: (0, 0, cbk)),
        ],
        out_specs=[
            pl.BlockSpec((tm, width), lambda cbk, i: (i, cbk)),
            pl.BlockSpec((1, keep, width),
                         lambda cbk, i: (jnp.minimum(i, n_pt - 1) // tiles_per_seq, 0, cbk)),
            pl.BlockSpec((n_seq_s, keep, width), lambda cbk, i: (0, 0, cbk)),
        ],
        out_shape=[
            jax.ShapeDtypeStruct((rows, D_MODEL), BF16),
            jax.ShapeDtypeStruct((n_seq_p, keep, D_MODEL), F32),
            jax.ShapeDtypeStruct((n_seq_s, keep, D_MODEL), F32),
        ],
        scratch_shapes=[
            pltpu.VMEM((D_MODEL, 4 * width), BF16),
            pltpu.VMEM((tm + META, D_MODEL), BF16),
            pltpu.VMEM((1, SUBLANES, width), F32),
            pltpu.VMEM((n_seq_s, SUBLANES, width), F32),
            pltpu.VMEM((1, keep, width), F32),
        ],
        compiler_params=pltpu.CompilerParams(
            dimension_semantics=("arbitrary", "arbitrary"),
            vmem_limit_bytes=VMEM_LIMIT),
        name="sconv",
    )(hs, hm, w_t, w_t, w_t, w_t, cw, prev_s)


def _merge_kernel(yn_ref, v_ref, ga_ref, gb_ref, wa_ref, wb_ref, o_ref):
    ya = jnp.dot(yn_ref[...], wa_ref[...], preferred_element_type=F32)
    yb = jnp.dot(v_ref[...], wb_ref[...], preferred_element_type=F32)
    o_ref[...] = (jax.nn.sigmoid(ga_ref[...]) * ya + jax.nn.sigmoid(gb_ref[...]) * yb).astype(BF16)


def _merge(yn, v, proj, wa, wb, *, tm, tn, row0=0):
    rows = yn.shape[0]
    ga0, gb0 = OFF_GA // tn, OFF_GB // tn
    assert row0 % tm == 0
    t0 = row0 // tm
    return pl.pallas_call(
        _merge_kernel,
        grid=(D_MODEL // tn, rows // tm),
        in_specs=[
            pl.BlockSpec((tm, D_INNER), lambda j, i: (i, 0)),
            pl.BlockSpec((tm, D_MODEL), lambda j, i: (t0 + i, 0)),
            pl.BlockSpec((tm, tn), lambda j, i: (t0 + i, ga0 + j)),
            pl.BlockSpec((tm, tn), lambda j, i: (t0 + i, gb0 + j)),
            pl.BlockSpec((D_INNER, tn), lambda j, i: (0, j)),
            pl.BlockSpec((D_MODEL, tn), lambda j, i: (0, j)),
        ],
        out_specs=pl.BlockSpec((tm, tn), lambda j, i: (i, j)),
        out_shape=jax.ShapeDtypeStruct((rows, D_MODEL), BF16),
        compiler_params=pltpu.CompilerParams(
            dimension_semantics=("arbitrary", "arbitrary"),
            vmem_limit_bytes=VMEM_LIMIT),
        name="merge",
    )(yn, v, proj, proj, wa, wb)


def _outproj_kernel(m_ref, x_ref, wo_ref, fw_ref, o_ref):
    y = x_ref[...] + jnp.dot(m_ref[...], wo_ref[...], preferred_element_type=F32)
    ms = jnp.mean(y * y, axis=-1, keepdims=True)
    o_ref[...] = y * lax.rsqrt(ms + EPS) * fw_ref[...]


def _outproj(m, x, wo, fw, *, tm):
    rows = m.shape[0]
    return pl.pallas_call(
        _outproj_kernel,
        grid=(rows // tm,),
        in_specs=[
            pl.BlockSpec((tm, D_MODEL), lambda i: (i, 0)),
            pl.BlockSpec((tm, D_MODEL), lambda i: (i, 0)),
            pl.BlockSpec((D_MODEL, D_MODEL), lambda i: (0, 0)),
            pl.BlockSpec((1, D_MODEL), lambda i: (0, 0)),
        ],
        out_specs=pl.BlockSpec((tm, D_MODEL), lambda i: (i, 0)),
        out_shape=jax.ShapeDtypeStruct((rows, D_MODEL), F32),
        compiler_params=pltpu.CompilerParams(
            dimension_semantics=("arbitrary",),
            vmem_limit_bytes=VMEM_LIMIT),
        name="outproj",
    )(m, x, wo, fw)


def kernel(x_prompt, x_sample, state_ssd_conv, state_ssm, state_sconv, meta_tokens, norm_w,
           w_in, ssd_conv_w, ssd_conv_b, dt_bias, a_log, d_skip, ssd_norm_w, w_ssd_out,
           sconv_w, w_sconv_out, w_o, final_norm_w):
    bp, seq = x_prompt.shape[0], x_prompt.shape[1]
    bd, dec_seq = x_sample.shape[0], x_sample.shape[1]

    w_t = jnp.transpose(w_in[0])
    wa = w_ssd_out[0].astype(BF16)
    wb = w_sconv_out[0].astype(BF16)
    wo = w_o[0].astype(BF16)
    nw = norm_w[0].reshape(1, D_MODEL)
    fw = final_norm_w.reshape(1, D_MODEL)
    conv_w = ssd_conv_w[0]
    conv_b = ssd_conv_b[0].reshape(1, CONV_DIM)
    dtb = jnp.pad(dt_bias[0], (0, LANES - N_HEADS)).reshape(1, LANES)
    alog = jnp.pad(a_log[0], (0, LANES - N_HEADS)).reshape(1, LANES)
    dsk = jnp.repeat(d_skip[0], HEAD_DIM).reshape(1, D_INNER)
    gnw = ssd_norm_w[0].reshape(1, D_INNER)
    scw = sconv_w[0]

    ssd = functools.partial(_ssd, conv_w=conv_w, conv_b=conv_b, d_skip_x=dsk, norm_w=gnw)

    xp = x_prompt.reshape(bp * seq, D_MODEL)
    xs = x_sample.reshape(bd * dec_seq, D_MODEL)
    n_p, n_s = bp * seq, bd * dec_seq
    hs, hm = _norm(xp, xs, meta_tokens, nw)
    proj, proj_m = _inproj(hs, hm, w_t)
    v_all, sc_p, sc_s = _sconv(hs, hm, w_t, scw, state_sconv[0], n_prompt=n_p, seq=seq,
                                  n_seq_p=bp, n_seq_s=bd, q_sample=dec_seq)
    headscal =functools.partial(_headscal, w_t=w_t, dt_bias=dtb, a_log=alog)

    at_m, dtt_m = headscal(hm, q=ROW_TILE, valid=META, tile0=0, ntiles=1)
    _, ssm_m, *cn_m = ssd(proj_m, at_m, dtt_m, jnp.zeros((1, SSD_CONV_W - 1, CONV_DIM), F32),
                          state0=jnp.zeros((1, N_GROUPS, GROUP_W, N_STATE), F32),
                          n_seq=1, bs=1, q=ROW_TILE, nc=1, gps=2, valid=META, shared_init=False)
    conv_m = jnp.concatenate(cn_m, axis=-1)

    at_p, dtt_p = headscal(hs, q=ROW_TILE, valid=ROW_TILE, tile0=0, ntiles=n_p // ROW_TILE)
    yn_p, ssm_p, *cn_p = ssd(proj, at_p, dtt_p, conv_m, state0=ssm_m, n_seq=bp, bs=1,
                             q=ROW_TILE, nc=seq // ROW_TILE, gps=2, valid=ROW_TILE,
                             shared_init=True)
    m_p = _merge(yn_p, v_all, proj, wa, wb, tm=256, tn=1024)
    y_p = _outproj(m_p, xp, wo, fw, tm=512)
    conv_p = jnp.concatenate(cn_p, axis=-1)

    sbs = ROW_TILE // dec_seq
    at_s, dtt_s = headscal(hs, q=dec_seq, valid=ROW_TILE, tile0=n_p // ROW_TILE,
                           ntiles=n_s // ROW_TILE)
    yn_s, ssm_s, *cn_s = ssd(proj, at_s, dtt_s, state_ssd_conv[0],
                             state0=state_ssm[0].reshape(bd, N_GROUPS, GROUP_W, N_STATE),
                             n_seq=bd, bs=sbs, q=dec_seq, nc=1, gps=1, valid=dec_seq,
                             shared_init=False, tile0=n_p // ROW_TILE)
    m_s = _merge(yn_s, v_all, proj, wa, wb, tm=256, tn=1024, row0=n_p)
    y_s = _outproj(m_s, xs, wo, fw, tm=512)
    conv_s = jnp.concatenate(cn_s, axis=-1)

    return (y_p.reshape(bp, seq, D_MODEL),
            y_s.reshape(bd, dec_seq, D_MODEL),
            conv_p[None],
            ssm_p.reshape(1, bp, N_HEADS, HEAD_DIM, N_STATE),
            sc_p[None],
            conv_s[None],
            ssm_s.reshape(1, bd, N_HEADS, HEAD_DIM, N_STATE),
            sc_s[None])
```

```python
import functools

import jax
import jax.numpy as jnp
from jax import lax
from jax.experimental import pallas as pl
from jax.experimental.pallas import tpu as pltpu

F32 = jnp.float32
BF16 = jnp.bfloat16

D_MODEL = 2048
D_INNER = 4096
N_HEADS = 64
HEAD_DIM = 64
N_STATE = 128
N_GROUPS = 8
GROUP_W = D_INNER // N_GROUPS
HEADS_PER_GROUP = N_HEADS // N_GROUPS
CONV_DIM = D_INNER + 2 * N_GROUPS * N_STATE
SSD_CONV_W = 4
SC_CONV_W = 3
META = 16
EPS = 1e-6

LANES = 128
SUBLANES = 8
ROW_TILE = 128
W_DT = D_INNER + CONV_DIM
W_SC = W_DT + N_HEADS
W_GATE = W_SC + 4 * D_MODEL

VMEM_LIMIT = 52 * 1024 * 1024


def _silu(x):
    return x * jax.nn.sigmoid(x)


def _rms_bf16(x, w):
    ms = jnp.mean(x * x, axis=-1, keepdims=True)
    return (x * lax.rsqrt(ms + EPS) * w).astype(BF16)


def _norm_kernel(xp_ref, xs_ref, xm_ref, nw_ref, hs_ref, hm_ref, *, n_prompt):
    i = pl.program_id(0)

    @pl.when(i < n_prompt)
    def _():
        hs_ref[...] = _rms_bf16(xp_ref[...], nw_ref[...])

    @pl.when(i >= n_prompt)
    def _():
        hs_ref[...] = _rms_bf16(xs_ref[...], nw_ref[...])

    @pl.when(i == 0)
    def _():
        hm_ref[:META, :] = _rms_bf16(xm_ref[...], nw_ref[...])
        hm_ref[META:, :] = jnp.zeros((ROW_TILE - META, D_MODEL), BF16)


def _norm(xp, xs, xm, norm_w, *, tm=512):
    n_p, n_s = xp.shape[0] // tm, xs.shape[0] // tm
    kern = functools.partial(_norm_kernel, n_prompt=n_p)
    return pl.pallas_call(
        kern,
        grid=(n_p + n_s,),
        in_specs=[
            pl.BlockSpec((tm, D_MODEL), lambda i: (jnp.minimum(i, n_p - 1), 0)),
            pl.BlockSpec((tm, D_MODEL), lambda i: (jnp.maximum(i - n_p, 0), 0)),
            pl.BlockSpec((META, D_MODEL), lambda i: (0, 0)),
            pl.BlockSpec((1, D_MODEL), lambda i: (0, 0)),
        ],
        out_specs=[
            pl.BlockSpec((tm, D_MODEL), lambda i: (i, 0)),
            pl.BlockSpec((ROW_TILE, D_MODEL), lambda i: (0, 0)),
        ],
        out_shape=[
            jax.ShapeDtypeStruct((xp.shape[0] + xs.shape[0], D_MODEL), BF16),
            jax.ShapeDtypeStruct((ROW_TILE, D_MODEL), BF16),
        ],
        compiler_params=pltpu.CompilerParams(
            dimension_semantics=("arbitrary",), vmem_limit_bytes=VMEM_LIMIT),
        name="norm",
    )(xp, xs, xm, norm_w)


def _proj_kernel(hs_ref, wt_ref, o_ref, wb_ref, *, silu):
    @pl.when(pl.program_id(1) == 0)
    def _():
        wb_ref[...] = wt_ref[...].T.astype(BF16)

    r = jnp.dot(hs_ref[...], wb_ref[...], preferred_element_type=F32)
    o_ref[...] = _silu(r) if silu else r


def _proj(hs, w_t, *, w_row0, ncols, silu, tm=1024, tn=1024):
    rows = hs.shape[0]
    assert w_row0 % N_HEADS == 0 and ncols % tn == 0 and tn % N_HEADS == 0

    def w_rows(j, i):
        return ((w_row0 // N_HEADS + j * (tn // N_HEADS)) * N_HEADS, 0)

    return pl.pallas_call(
        functools.partial(_proj_kernel, silu=silu),
        grid=(ncols // tn, rows // tm),
        in_specs=[
            pl.BlockSpec((tm, D_MODEL), lambda j, i: (i, 0)),
            pl.BlockSpec((pl.Element(tn), pl.Element(D_MODEL)), w_rows),
        ],
        out_specs=pl.BlockSpec((tm, tn), lambda j, i: (i, j)),
        out_shape=jax.ShapeDtypeStruct((rows, ncols), F32),
        scratch_shapes=[pltpu.VMEM((D_MODEL, tn), BF16)],
        compiler_params=pltpu.CompilerParams(
            dimension_semantics=("arbitrary", "arbitrary"),
            vmem_limit_bytes=VMEM_LIMIT),
        name="proj",
    )(hs, w_t)


def _conv_rows(x, halo_ref, prev_ref, w_ref, *, first, bs, q, carry):
    taps = w_ref.shape[0]
    rt, width = x.shape

    @pl.when(first)
    def _():
        halo_ref[:, SUBLANES - (taps - 1):, :] = prev_ref[...]

    prev = halo_ref[...]
    acc = None
    if bs == 1:
        row = lax.broadcasted_iota(jnp.int32, (SUBLANES, width), 0)
        for s in range(taps - 1, 0, -1):
            rolled = pltpu.roll(x, s, 0)
            head = jnp.where(row < s, pltpu.roll(prev[0], s, 0), rolled[:SUBLANES])
            term = jnp.concatenate([head, rolled[SUBLANES:]], axis=0) * w_ref[taps - 1 - s:taps - s, :]
            acc = term if acc is None else acc + term
        acc = acc + x * w_ref[taps - 1:taps, :]
        if carry:
            halo_ref[0] = x[rt - SUBLANES:, :]
        return acc
    assert q == SUBLANES and not carry
    x3 = x.reshape(bs, q, width)
    row = lax.broadcasted_iota(jnp.int32, x3.shape, 1)
    for s in range(taps - 1, 0, -1):
        shifted = jnp.where(row < s, pltpu.roll(prev, s, 1), pltpu.roll(x3, s, 1))
        term = shifted * w_ref[taps - 1 - s:taps - s, :]
        acc = term if acc is None else acc + term
    acc = acc + x3 * w_ref[taps - 1:taps, :]
    return acc.reshape(rt, width)


def _seg_cumsum(a, q):
    pos = lax.broadcasted_iota(jnp.int32, a.shape, 0) & (q - 1)
    s = 1
    while s < q:
        shifted = pltpu.roll(a, s, 0)
        a = a + jnp.where(pos >= s, shifted, 0.0)
        s *= 2
    return a


def _headscal_kernel(hs_ref, wdt_ref, dtb_ref, alog_ref, at_ref, dtt_ref, *, q, valid, tps):
    dtr = lax.dot_general(hs_ref[...], wdt_ref[...].astype(BF16), (((1,), (1,)), ((), ())),
                          preferred_element_type=F32)
    dtv = jax.nn.softplus(dtr + dtb_ref[...])
    if valid < ROW_TILE:
        rows = lax.broadcasted_iota(jnp.int32, dtv.shape, 0)
        dtv = jnp.where(rows < valid, dtv, 0.0)
    acum = _seg_cumsum(dtv * (-jnp.exp(alog_ref[...])), q)
    for t in range(tps):
        at_ref[t] = acum[t * ROW_TILE:(t + 1) * ROW_TILE].T
        dtt_ref[t] = dtv[t * ROW_TILE:(t + 1) * ROW_TILE].T


def _headscal(hs, w_t, dt_bias, a_log, *, q, valid, tile0, ntiles):
    tps = min(ntiles, 8)
    assert ntiles % tps == 0 and tile0 % tps == 0 and q <= ROW_TILE
    kern = functools.partial(_headscal_kernel, q=q, valid=valid, tps=tps)
    shape = jax.ShapeDtypeStruct((ntiles, LANES, ROW_TILE), F32)
    dt_blk = W_DT // LANES
    return pl.pallas_call(
        kern,
        grid=(ntiles // tps,),
        in_specs=[
            pl.BlockSpec((tps * ROW_TILE, D_MODEL), lambda t: (tile0 // tps + t, 0)),
            pl.BlockSpec((LANES, D_MODEL), lambda t: (dt_blk, 0)),
            pl.BlockSpec((1, LANES), lambda t: (0, 0)),
            pl.BlockSpec((1, LANES), lambda t: (0, 0)),
        ],
        out_specs=[
            pl.BlockSpec((tps, LANES, ROW_TILE), lambda t: (t, 0, 0)),
            pl.BlockSpec((tps, LANES, ROW_TILE), lambda t: (t, 0, 0)),
        ],
        out_shape=[shape, shape],
        compiler_params=pltpu.CompilerParams(dimension_semantics=("arbitrary",)),
        name="headscal",
    )(hs, w_t, dt_bias, a_log)


def _xbc_kernel(hs_ref, hm_ref, wt_ref, cw_ref, cb_ref, prev_ref,
                o_ref, om_ref, cnp_ref, cns_ref,
                wb_scr, lhs_scr, halo_p, halo_s, halo_m, meta_prev, zero_prev,
                *, tm, n_prompt_tiles, tiles_per_seq, bs_sample, q_sample):
    i = pl.program_id(1)
    keep = SSD_CONV_W - 1
    tn = o_ref.shape[1]

    @pl.when(i == 0)
    def _():
        wb_scr[...] = wt_ref[...].T.astype(BF16)
        lhs_scr[tm:, :] = hm_ref[...]

    lhs_scr[:tm, :] = hs_ref[...]
    r = jnp.dot(lhs_scr[...], wb_scr[...], preferred_element_type=F32)
    raw = r[:tm]

    def activate(conv):
        return _silu(conv + cb_ref[...])

    @pl.when(i == 0)
    def _():
        raw_m = r[tm:]
        meta_prev[0] = raw_m[META - keep:]
        zero_prev[...] = jnp.zeros(zero_prev.shape, F32)
        conv_m = _conv_rows(raw_m, halo_m, zero_prev, cw_ref, first=i == 0, bs=1, q=META,
                            carry=False)
        om_ref[:META, :] = activate(conv_m)
        om_ref[META:, :] = jnp.zeros((ROW_TILE - META, tn), F32)

    @pl.when(i < n_prompt_tiles)
    def _():
        o_ref[...] = activate(_conv_rows(raw, halo_p, meta_prev, cw_ref,
                                         first=(i % tiles_per_seq) == 0, bs=1, q=tm, carry=True))
        cnp_ref[0] = raw[tm - keep:]

    @pl.when(i >= n_prompt_tiles)
    def _():
        o_ref[...] = activate(_conv_rows(raw, halo_s, prev_ref, cw_ref, first=i >= n_prompt_tiles,
                                         bs=bs_sample, q=q_sample, carry=False))
        cns_ref[...] = raw.reshape(bs_sample, q_sample, tn)[:, q_sample - keep:, :]


def _xbc(hs, hm, w_t, cw, cb, prev_s, *, n_prompt, seq, n_seq_p, n_seq_s, q_sample,
         tm=1024, tn=512):
    rows = hs.shape[0]
    keep = SSD_CONV_W - 1
    assert seq % tm == 0 and (rows - n_prompt) == tm and tm == n_seq_s * q_sample
    assert D_INNER % tn == 0 and CONV_DIM % tn == 0
    tiles_per_seq = seq // tm
    n_pt = n_prompt // tm
    kern = functools.partial(_xbc_kernel, tm=tm, n_prompt_tiles=n_pt, tiles_per_seq=tiles_per_seq,
                             bs_sample=n_seq_s, q_sample=q_sample)
    return pl.pallas_call(
        kern,
        grid=(CONV_DIM // tn, rows // tm),
        in_specs=[
            pl.BlockSpec((tm, D_MODEL), lambda j, i: (i, 0)),
            pl.BlockSpec((META, D_MODEL), lambda j, i: (0, 0)),
            pl.BlockSpec((tn, D_MODEL), lambda j, i: (D_INNER // tn + j, 0)),
            pl.BlockSpec((SSD_CONV_W, tn), lambda j, i: (0, j)),
            pl.BlockSpec((1, tn), lambda j, i: (0, j)),
            pl.BlockSpec((n_seq_s, keep, tn), lambda j, i: (0, 0, j)),
        ],
        out_specs=[
            pl.BlockSpec((tm, tn), lambda j, i: (i, j)),
            pl.BlockSpec((ROW_TILE, tn), lambda j, i: (0, j)),
            pl.BlockSpec((1, keep, tn),
                         lambda j, i: (jnp.minimum(i, n_pt - 1) // tiles_per_seq, 0, j)),
            pl.BlockSpec((n_seq_s, keep, tn), lambda j, i: (0, 0, j)),
        ],
        out_shape=[
            jax.ShapeDtypeStruct((rows, CONV_DIM), F32),
            jax.ShapeDtypeStruct((ROW_TILE, CONV_DIM), F32),
            jax.ShapeDtypeStruct((n_seq_p, keep, CONV_DIM), F32),
            jax.ShapeDtypeStruct((n_seq_s, keep, CONV_DIM), F32),
        ],
        scratch_shapes=[
            pltpu.VMEM((D_MODEL, tn), BF16),
            pltpu.VMEM((tm + META, D_MODEL), BF16),
            pltpu.VMEM((1, SUBLANES, tn), F32),
            pltpu.VMEM((n_seq_s, SUBLANES, tn), F32),
            pltpu.VMEM((1, SUBLANES, tn), F32),
            pltpu.VMEM((1, keep, tn), F32),
            pltpu.VMEM((1, keep, tn), F32),
        ],
        compiler_params=pltpu.CompilerParams(
            dimension_semantics=("arbitrary", "arbitrary"),
            vmem_limit_bytes=VMEM_LIMIT),
        name="xbc",
    )(hs, hm, w_t, cw, cb, prev_s)


def _ssd_kernel(sz_ref, x_ref, b_ref, c_ref, at_ref, dtt_ref, dsk_ref, nw_ref, s0_ref,
                yn_ref, sout_ref, st_ref, *, bs, q, nc, gps):
    rt = bs * q
    nh = gps * HEADS_PER_GROUP
    c = pl.program_id(2)
    carry = nc > 1

    if carry:
        @pl.when(c == 0)
        def _():
            st_ref[...] = s0_ref[...]

    xc = x_ref[...]
    bcb = b_ref[...].astype(BF16)
    ccb = c_ref[...].astype(BF16)

    a_t = at_ref[0]
    cols = jnp.concatenate(
        [a_t, dtt_ref[0], jnp.zeros((LANES - 2 * nh, rt), F32)], axis=0).T

    ri = lax.broadcasted_iota(jnp.int32, (rt, rt), 0)
    ci = lax.broadcasted_iota(jnp.int32, (rt, rt), 1)
    mask = (ri >= ci) & ((ri // q) == (ci // q))
    low = lax.broadcasted_iota(jnp.int32, (rt, LANES), 1) < HEAD_DIM
    seq_of_row = lax.broadcasted_iota(jnp.int32, (rt, N_STATE), 0) // q

    for k in range(gps):
        xg = xc[:, k * GROUP_W:(k + 1) * GROUP_W]
        bg = bcb[:, k * N_STATE:(k + 1) * N_STATE]
        cg = ccb[:, k * N_STATE:(k + 1) * N_STATE]
        cb = lax.dot_general(cg, bg, (((1,), (1,)), ((), ())), preferred_element_type=F32)

        ydiag, ea, xw, a_cols = [], [], [], []
        for pr in range(HEADS_PER_GROUP // 2):
            wts, ab, db = [], [], []
            for hh in range(2):
                h = k * HEADS_PER_GROUP + 2 * pr + hh
                a_col = jnp.broadcast_to(cols[:, h:h + 1], (rt, LANES))
                a_row = jnp.broadcast_to(a_t[h:h + 1, :], (rt, rt))
                seg = jnp.where(mask, a_col - a_row, -jnp.inf)
                wts.append((cb * jnp.exp(seg)).astype(BF16))
                ab.append(a_col)
                db.append(jnp.broadcast_to(cols[:, nh + h:nh + h + 1], (rt, LANES)))
            a_cols += ab
            a_pair = jnp.where(low, ab[0], ab[1])
            d_pair = jnp.where(low, db[0], db[1])
            a3 = a_pair.reshape(bs, q, LANES)
            a_last = jnp.broadcast_to(a3[:, q - 1:q, :], (bs, q, LANES)).reshape(rt, LANES)
            xdt = xg[:, pr * LANES:(pr + 1) * LANES] * d_pair
            xw.append(xdt * jnp.exp(a_last - a_pair))
            ea.append(jnp.exp(a_pair))
            xb = xdt.astype(BF16)
            zero = jnp.zeros_like(xb)
            rhs = jnp.concatenate([jnp.where(low, xb, zero), jnp.where(low, zero, xb)], axis=0)
            ydiag.append(jnp.dot(jnp.concatenate(wts, axis=1), rhs, preferred_element_type=F32))
        ydiag = jnp.concatenate(ydiag, axis=1)
        ea = jnp.concatenate(ea, axis=1)
        xwt = jnp.concatenate(xw, axis=1).T.astype(BF16)

        yoff = []
        for s in range(bs):
            st = st_ref[s, k] if carry else s0_ref[s, k]
            yoff.append(lax.dot_general(cg[s * q:(s + 1) * q, :], st.astype(BF16),
                                        (((1,), (1,)), ((), ())), preferred_element_type=F32))
            bsel = bg if bs == 1 else jnp.where(seq_of_row == s, bg, jnp.zeros_like(bg))
            upd = jnp.dot(xwt, bsel, preferred_element_type=F32)
            last = (s + 1) * q - 1
            dec = jnp.concatenate(
                [jnp.broadcast_to(jnp.exp(a_cols[h][last:last + 1, :]), (HEAD_DIM, N_STATE))
                 for h in range(HEADS_PER_GROUP)], axis=0)
            new = st * dec + upd
            if carry:
                st_ref[s, k] = new

                @pl.when(c == nc - 1)
                def _():
                    sout_ref[s, k] = new
            else:
                sout_ref[s, k] = new
        yoff = yoff[0] if bs == 1 else jnp.concatenate(yoff, axis=0)

        lanes = slice(k * GROUP_W, (k + 1) * GROUP_W)
        y = ydiag + yoff * ea + dsk_ref[:, lanes] * xg
        gz = y * sz_ref[:, lanes]
        ms = jnp.mean(gz * gz, axis=-1, keepdims=True)
        yn_ref[:, lanes] = (gz * lax.rsqrt(ms + EPS) * nw_ref[:, lanes]).astype(BF16)


def _ssd(sz, xbc, a_t, dt_t, d_skip_x, norm_w, state0,
         *, n_seq, bs, q, nc, gps, shared_init, tile0=0):
    rt = bs * q
    assert rt == ROW_TILE and N_GROUPS % gps == 0
    rows = n_seq * q * nc
    nsb = n_seq // bs
    gw, gn, nh = gps * GROUP_W, gps * N_STATE, gps * HEADS_PER_GROUP
    kern = functools.partial(_ssd_kernel, bs=bs, q=q, nc=nc, gps=gps)
    bb, bc_ = D_INNER // gn, (D_INNER + N_GROUPS * N_STATE) // gn
    sidx = (lambda s: 0) if shared_init else (lambda s: s)

    def tile(s, c):
        return s * nc + c

    in_specs = [
        pl.BlockSpec((rt, gw), lambda s, g, c: (tile0 + tile(s, c), g)),
        pl.BlockSpec((rt, gw), lambda s, g, c: (tile0 + tile(s, c), g)),
        pl.BlockSpec((rt, gn), lambda s, g, c: (tile0 + tile(s, c), bb + g)),
        pl.BlockSpec((rt, gn), lambda s, g, c: (tile0 + tile(s, c), bc_ + g)),
        pl.BlockSpec((1, nh, rt), lambda s, g, c: (tile(s, c), g, 0)),
        pl.BlockSpec((1, nh, rt), lambda s, g, c: (tile(s, c), g, 0)),
        pl.BlockSpec((1, gw), lambda s, g, c: (0, g)),
        pl.BlockSpec((1, gw), lambda s, g, c: (0, g)),
        pl.BlockSpec((bs, gps, GROUP_W, N_STATE), lambda s, g, c: (sidx(s), g, 0, 0)),
    ]
    out_specs = [
        pl.BlockSpec((rt, gw), lambda s, g, c: (tile(s, c), g)),
        pl.BlockSpec((bs, gps, GROUP_W, N_STATE), lambda s, g, c: (s, g, 0, 0)),
    ]
    st_shape = (bs, gps, GROUP_W, N_STATE) if nc > 1 else (1, 1, SUBLANES, N_STATE)
    return pl.pallas_call(
        kern,
        grid=(nsb, N_GROUPS // gps, nc),
        in_specs=in_specs,
        out_specs=out_specs,
        out_shape=[
            jax.ShapeDtypeStruct((rows, D_INNER), BF16),
            jax.ShapeDtypeStruct((n_seq, N_GROUPS, GROUP_W, N_STATE), F32),
        ],
        scratch_shapes=[pltpu.VMEM(st_shape, F32)],
        compiler_params=pltpu.CompilerParams(
            dimension_semantics=("arbitrary", "arbitrary", "arbitrary"),
            vmem_limit_bytes=VMEM_LIMIT),
        name="ssd",
    )(sz, xbc, xbc, xbc, a_t, dt_t, d_skip_x, norm_w, state0)


def _sconv_kernel(hs_ref, hm_ref, wb_ref, wc_ref, wh_ref, wz_ref, cw_ref, prev_ref,
                  v_ref, newp_ref, news_ref,
                  w_scr, lhs_scr, halo_p, halo_s, meta_u,
                  *, tm, width, n_prompt_tiles, tiles_per_seq, bs_sample, q_sample):
    i = pl.program_id(1)
    keep = SC_CONV_W - 1

    @pl.when(i == 0)
    def _():
        for k, w_ref in enumerate((wb_ref, wc_ref, wh_ref, wz_ref)):
            w_scr[:, k * width:(k + 1) * width] = w_ref[...].T.astype(BF16)
        lhs_scr[tm:, :] = hm_ref[...]

    lhs_scr[:tm, :] = hs_ref[...]
    r = jnp.dot(lhs_scr[...], w_scr[...], preferred_element_type=F32)
    u_all = r[:, width:2 * width] * r[:, 2 * width:3 * width]
    u = u_all[:tm]

    @pl.when(i == 0)
    def _():
        meta_u[0] = u_all[tm + META - keep:tm + META]

    def finish(uc):
        v_ref[...] = (r[:tm, :width] * uc * _silu(r[:tm, 3 * width:])).astype(BF16)

    @pl.when(i < n_prompt_tiles)
    def _():
        finish(_conv_rows(u, halo_p, meta_u, cw_ref, first=(i % tiles_per_seq) == 0,
                          bs=1, q=tm, carry=True))
        newp_ref[0] = u[tm - keep:]

    @pl.when(i >= n_prompt_tiles)
    def _():
        finish(_conv_rows(u, halo_s, prev_ref, cw_ref, first=i >= n_prompt_tiles,
                          bs=bs_sample, q=q_sample, carry=False))
        news_ref[...] = u.reshape(bs_sample, q_sample, width)[:, q_sample - keep:, :]


def _sconv(hs, hm, w_t, cw, prev_s, *, n_prompt, seq, n_seq_p, n_seq_s, q_sample,
           tm=1024, width=256):
    rows = hs.shape[0]
    keep = SC_CONV_W - 1
    assert seq % tm == 0 and (rows - n_prompt) == tm and tm == n_seq_s * q_sample
    tiles_per_seq = seq // tm
    n_pt = n_prompt // tm
    kern = functools.partial(_sconv_kernel, tm=tm, width=width, n_prompt_tiles=n_pt,
                             tiles_per_seq=tiles_per_seq, bs_sample=n_seq_s, q_sample=q_sample)

    def w_rows(k):
        base = (W_SC + k * D_MODEL) // N_HEADS
        return lambda cbk, i: ((base + cbk * (width // N_HEADS)) * N_HEADS, 0)

    w_specs = [pl.BlockSpec((pl.Element(width), pl.Element(D_MODEL)), w_rows(k)) for k in range(4)]
    return pl.pallas_call(
        kern,
        grid=(D_MODEL // width, rows // tm),
        in_specs=[
            pl.BlockSpec((tm, D_MODEL), lambda cbk, i: (i, 0)),
            pl.BlockSpec((META, D_MODEL), lambda cbk, i: (0, 0)),
            *w_specs,
            pl.BlockSpec((SC_CONV_W, width), lambda cbk, i: (0, cbk)),
            pl.BlockSpec((n_seq_s, keep, width), lambda cbk, i: (0, 0, cbk)),
        ],
        out_specs=[
            pl.BlockSpec((tm, width), lambda cbk, i: (i, cbk)),
            pl.BlockSpec((1, keep, width),
                         lambda cbk, i: (jnp.minimum(i, n_pt - 1) // tiles_per_seq, 0, cbk)),
            pl.BlockSpec((n_seq_s, keep, width), lambda cbk, i: (0, 0, cbk)),
        ],
        out_shape=[
            jax.ShapeDtypeStruct((rows, D_MODEL), BF16),
            jax.ShapeDtypeStruct((n_seq_p, keep, D_MODEL), F32),
            jax.ShapeDtypeStruct((n_seq_s, keep, D_MODEL), F32),
        ],
        scratch_shapes=[
            pltpu.VMEM((D_MODEL, 4 * width), BF16),
            pltpu.VMEM((tm + META, D_MODEL), BF16),
            pltpu.VMEM((1, SUBLANES, width), F32),
            pltpu.VMEM((n_seq_s, SUBLANES, width), F32),
            pltpu.VMEM((1, keep, width), F32),
        ],
        compiler_params=pltpu.CompilerParams(
            dimension_semantics=("arbitrary", "arbitrary"),
            vmem_limit_bytes=VMEM_LIMIT),
        name="sconv",
    )(hs, hm, w_t, w_t, w_t, w_t, cw, prev_s)


def _merge_kernel(yn_ref, v_ref, ga_ref, gb_ref, wa_ref, wb_ref, o_ref):
    ya = jnp.dot(yn_ref[...], wa_ref[...], preferred_element_type=F32)
    yb = jnp.dot(v_ref[...], wb_ref[...], preferred_element_type=F32)
    o_ref[...] = (jax.nn.sigmoid(ga_ref[...]) * ya + jax.nn.sigmoid(gb_ref[...]) * yb).astype(BF16)


def _merge(yn, v, gates, wa, wb, *, tm, tn, row0=0):
    rows = yn.shape[0]
    ga0, gb0 = 0, D_MODEL // tn
    assert row0 % tm == 0
    t0 = row0 // tm
    return pl.pallas_call(
        _merge_kernel,
        grid=(D_MODEL // tn, rows // tm),
        in_specs=[
            pl.BlockSpec((tm, D_INNER), lambda j, i: (i, 0)),
            pl.BlockSpec((tm, D_MODEL), lambda j, i: (t0 + i, 0)),
            pl.BlockSpec((tm, tn), lambda j, i: (t0 + i, ga0 + j)),
            pl.BlockSpec((tm, tn), lambda j, i: (t0 + i, gb0 + j)),
            pl.BlockSpec((D_INNER, tn), lambda j, i: (0, j)),
            pl.BlockSpec((D_MODEL, tn), lambda j, i: (0, j)),
        ],
        out_specs=pl.BlockSpec((tm, tn), lambda j, i: (i, j)),
        out_shape=jax.ShapeDtypeStruct((rows, D_MODEL), BF16),
        compiler_params=pltpu.CompilerParams(
            dimension_semantics=("arbitrary", "arbitrary"),
            vmem_limit_bytes=VMEM_LIMIT),
        name="merge",
    )(yn, v, gates, gates, wa, wb)


def _outproj_kernel(m_ref, x_ref, wo_ref, fw_ref, o_ref):
    y = x_ref[...] + jnp.dot(m_ref[...], wo_ref[...], preferred_element_type=F32)
    ms = jnp.mean(y * y, axis=-1, keepdims=True)
    o_ref[...] = y * lax.rsqrt(ms + EPS) * fw_ref[...]


def _outproj(m, x, wo, fw, *, tm):
    rows = m.shape[0]
    return pl.pallas_call(
        _outproj_kernel,
        grid=(rows // tm,),
        in_specs=[
            pl.BlockSpec((tm, D_MODEL), lambda i: (i, 0)),
            pl.BlockSpec((tm, D_MODEL), lambda i: (i, 0)),
            pl.BlockSpec((D_MODEL, D_MODEL), lambda i: (0, 0)),
            pl.BlockSpec((1, D_MODEL), lambda i: (0, 0)),
        ],
        out_specs=pl.BlockSpec((tm, D_MODEL), lambda i: (i, 0)),
        out_shape=jax.ShapeDtypeStruct((rows, D_MODEL), F32),
        compiler_params=pltpu.CompilerParams(
            dimension_semantics=("arbitrary",),
            vmem_limit_bytes=VMEM_LIMIT),
        name="outproj",
    )(m, x, wo, fw)


def kernel(x_prompt, x_sample, state_ssd_conv, state_ssm, state_sconv, meta_tokens, norm_w,
           w_in, ssd_conv_w, ssd_conv_b, dt_bias, a_log, d_skip, ssd_norm_w, w_ssd_out,
           sconv_w, w_sconv_out, w_o, final_norm_w):
    bp, seq = x_prompt.shape[0], x_prompt.shape[1]
    bd, dec_seq = x_sample.shape[0], x_sample.shape[1]

    w_t = jnp.transpose(w_in[0])
    wa = w_ssd_out[0].astype(BF16)
    wb = w_sconv_out[0].astype(BF16)
    wo = w_o[0].astype(BF16)
    nw = norm_w[0].reshape(1, D_MODEL)
    fw = final_norm_w.reshape(1, D_MODEL)
    conv_w = ssd_conv_w[0]
    conv_b = ssd_conv_b[0].reshape(1, CONV_DIM)
    dtb = jnp.pad(dt_bias[0], (0, LANES - N_HEADS)).reshape(1, LANES)
    alog = jnp.pad(a_log[0], (0, LANES - N_HEADS)).reshape(1, LANES)
    dsk = jnp.repeat(d_skip[0], HEAD_DIM).reshape(1, D_INNER)
    gnw = ssd_norm_w[0].reshape(1, D_INNER)
    scw = sconv_w[0]

    xp = x_prompt.reshape(bp * seq, D_MODEL)
    xs = x_sample.reshape(bd * dec_seq, D_MODEL)
    n_p, n_s = bp * seq, bd * dec_seq
    streams = dict(n_prompt=n_p, seq=seq, n_seq_p=bp, n_seq_s=bd, q_sample=dec_seq)

    hs, hm = _norm(xp, xs, meta_tokens, nw)
    sz = _proj(hs, w_t, w_row0=0, ncols=D_INNER, silu=True)
    gates = _proj(hs, w_t, w_row0=W_GATE, ncols=2 * D_MODEL, silu=False)
    xbc, xbc_m, conv_p, conv_s = _xbc(hs, hm, w_t, conv_w, conv_b, state_ssd_conv[0], **streams)
    v_all, sc_p, sc_s = _sconv(hs, hm, w_t, scw, state_sconv[0], **streams)
    headscal = functools.partial(_headscal, w_t=w_t, dt_bias=dtb, a_log=alog)
    ssd = functools.partial(_ssd, d_skip_x=dsk, norm_w=gnw)

    at_m, dtt_m = headscal(hm, q=ROW_TILE, valid=META, tile0=0, ntiles=1)
    _, ssm_m = ssd(jnp.zeros((ROW_TILE, D_INNER), F32), xbc_m, at_m, dtt_m,
                   state0=jnp.zeros((1, N_GROUPS, GROUP_W, N_STATE), F32),
                   n_seq=1, bs=1, q=ROW_TILE, nc=1, gps=2, shared_init=False)

    at_p, dtt_p = headscal(hs, q=ROW_TILE, valid=ROW_TILE, tile0=0, ntiles=n_p // ROW_TILE)
    yn_p, ssm_p = ssd(sz, xbc, at_p, dtt_p, state0=ssm_m, n_seq=bp, bs=1, q=ROW_TILE,
                      nc=seq // ROW_TILE, gps=2, shared_init=True)
    m_p = _merge(yn_p, v_all, gates, wa, wb, tm=256, tn=1024)
    y_p = _outproj(m_p, xp, wo, fw, tm=512)

    sbs = ROW_TILE // dec_seq
    at_s, dtt_s = headscal(hs, q=dec_seq, valid=ROW_TILE, tile0=n_p // ROW_TILE,
                           ntiles=n_s // ROW_TILE)
    yn_s, ssm_s = ssd(sz, xbc, at_s, dtt_s,
                      state0=state_ssm[0].reshape(bd, N_GROUPS, GROUP_W, N_STATE),
                      n_seq=bd, bs=sbs, q=dec_seq, nc=1, gps=1, shared_init=False,
                      tile0=n_p // ROW_TILE)
    m_s = _merge(yn_s, v_all, gates, wa, wb, tm=256, tn=1024, row0=n_p)
    y_s = _outproj(m_s, xs, wo, fw, tm=512)

    return (y_p.reshape(bp, seq, D_MODEL),
            y_s.reshape(bd, dec_seq, D_MODEL),
            conv_p[None],
            ssm_p.reshape(1, bp, N_HEADS, HEAD_DIM, N_STATE),
            sc_p[None],
            conv_s[None],
            ssm_s.reshape(1, bd, N_HEADS, HEAD_DIM, N_STATE),
            sc_s[None])
```

```python
import functools

import jax
import jax.numpy as jnp
from jax import lax
from jax.experimental import pallas as pl
from jax.experimental.pallas import tpu as pltpu

F32 = jnp.float32
BF16 = jnp.bfloat16

D_MODEL = 2048
D_INNER = 4096
N_HEADS = 64
HEAD_DIM = 64
N_STATE = 128
N_GROUPS = 8
GROUP_W = D_INNER // N_GROUPS
HEADS_PER_GROUP = N_HEADS // N_GROUPS
CONV_DIM = D_INNER + 2 * N_GROUPS * N_STATE
SSD_CONV_W = 4
SC_CONV_W = 3
META = 16
EPS = 1e-6

LANES = 128
SUBLANES = 8
ROW_TILE = 128
W_DT = D_INNER + CONV_DIM
W_SC = W_DT + N_HEADS
W_GATE = W_SC + 4 * D_MODEL

VMEM_LIMIT = 52 * 1024 * 1024


def _silu(x):
    return x * jax.nn.sigmoid(x)


def _rms_bf16(x, w):
    ms = jnp.mean(x * x, axis=-1, keepdims=True)
    return (x * lax.rsqrt(ms + EPS) * w).astype(BF16)


def _norm_kernel(xp_ref, xs_ref, xm_ref, nw_ref, hs_ref, hm_ref, *, n_prompt):
    i = pl.program_id(0)

    @pl.when(i < n_prompt)
    def _():
        hs_ref[...] = _rms_bf16(xp_ref[...], nw_ref[...])

    @pl.when(i >= n_prompt)
    def _():
        hs_ref[...] = _rms_bf16(xs_ref[...], nw_ref[...])

    @pl.when(i == 0)
    def _():
        hm_ref[:META, :] = _rms_bf16(xm_ref[...], nw_ref[...])
        hm_ref[META:, :] = jnp.zeros((ROW_TILE - META, D_MODEL), BF16)


def _norm(xp, xs, xm, norm_w, *, tm=512):
    n_p, n_s = xp.shape[0] // tm, xs.shape[0] // tm
    kern = functools.partial(_norm_kernel, n_prompt=n_p)
    return pl.pallas_call(
        kern,
        grid=(n_p + n_s,),
        in_specs=[
            pl.BlockSpec((tm, D_MODEL), lambda i: (jnp.minimum(i, n_p - 1), 0)),
            pl.BlockSpec((tm, D_MODEL), lambda i: (jnp.maximum(i - n_p, 0), 0)),
            pl.BlockSpec((META, D_MODEL), lambda i: (0, 0)),
            pl.BlockSpec((1, D_MODEL), lambda i: (0, 0)),
        ],
        out_specs=[
            pl.BlockSpec((tm, D_MODEL), lambda i: (i, 0)),
            pl.BlockSpec((ROW_TILE, D_MODEL), lambda i: (0, 0)),
        ],
        out_shape=[
            jax.ShapeDtypeStruct((xp.shape[0] + xs.shape[0], D_MODEL), BF16),
            jax.ShapeDtypeStruct((ROW_TILE, D_MODEL), BF16),
        ],
        compiler_params=pltpu.CompilerParams(
            dimension_semantics=("arbitrary",), vmem_limit_bytes=VMEM_LIMIT),
        name="norm",
    )(xp, xs, xm, norm_w)


def _proj_kernel(hs_ref, wt_ref, o_ref, wb_ref, *, silu):
    @pl.when(pl.program_id(1) == 0)
    def _():
        wb_ref[...] = wt_ref[...].T.astype(BF16)

    r = jnp.dot(hs_ref[...], wb_ref[...], preferred_element_type=F32)
    o_ref[...] = _silu(r) if silu else r


def _proj(hs, w_t, *, w_row0, ncols, silu, tm=1024, tn=1024):
    rows = hs.shape[0]
    assert w_row0 % N_HEADS == 0 and ncols % tn == 0 and tn % N_HEADS == 0

    def w_rows(j, i):
        return ((w_row0 // N_HEADS + j * (tn // N_HEADS)) * N_HEADS, 0)

    return pl.pallas_call(
        functools.partial(_proj_kernel, silu=silu),
        grid=(ncols // tn, rows // tm),
        in_specs=[
            pl.BlockSpec((tm, D_MODEL), lambda j, i: (i, 0)),
            pl.BlockSpec((pl.Element(tn), pl.Element(D_MODEL)), w_rows),
        ],
        out_specs=pl.BlockSpec((tm, tn), lambda j, i: (i, j)),
        out_shape=jax.ShapeDtypeStruct((rows, ncols), F32),
        scratch_shapes=[pltpu.VMEM((D_MODEL, tn), BF16)],
        compiler_params=pltpu.CompilerParams(
            dimension_semantics=("arbitrary", "arbitrary"),
            vmem_limit_bytes=VMEM_LIMIT),
        name="proj",
    )(hs, w_t)


def _conv_rows(x, halo_ref, prev_ref, w_ref, *, first, bs, q, carry):
    taps = w_ref.shape[0]
    rt, width = x.shape

    @pl.when(first)
    def _():
        halo_ref[:, SUBLANES - (taps - 1):, :] = prev_ref[...]

    prev = halo_ref[...]
    acc = None
    if bs == 1:
        row = lax.broadcasted_iota(jnp.int32, (SUBLANES, width), 0)
        for s in range(taps - 1, 0, -1):
            rolled = pltpu.roll(x, s, 0)
            head = jnp.where(row < s, pltpu.roll(prev[0], s, 0), rolled[:SUBLANES])
            term = jnp.concatenate([head, rolled[SUBLANES:]], axis=0) * w_ref[taps - 1 - s:taps - s, :]
            acc = term if acc is None else acc + term
        acc = acc + x * w_ref[taps - 1:taps, :]
        if carry:
            halo_ref[0] = x[rt - SUBLANES:, :]
        return acc
    assert q == SUBLANES and not carry
    x3 = x.reshape(bs, q, width)
    row = lax.broadcasted_iota(jnp.int32, x3.shape, 1)
    for s in range(taps - 1, 0, -1):
        shifted = jnp.where(row < s, pltpu.roll(prev, s, 1), pltpu.roll(x3, s, 1))
        term = shifted * w_ref[taps - 1 - s:taps - s, :]
        acc = term if acc is None else acc + term
    acc = acc + x3 * w_ref[taps - 1:taps, :]
    return acc.reshape(rt, width)


def _seg_cumsum(a, q):
    pos = lax.broadcasted_iota(jnp.int32, a.shape, 0) & (q - 1)
    s = 1
    while s < q:
        shifted = pltpu.roll(a, s, 0)
        a = a + jnp.where(pos >= s, shifted, 0.0)
        s *= 2
    return a


def _headscal_kernel(hs_ref, wdt_ref, dtb_ref, alog_ref, at_ref, bt_ref, *, q, valid, tps):
    dtr = lax.dot_general(hs_ref[...], wdt_ref[...].astype(BF16), (((1,), (1,)), ((), ())),
                          preferred_element_type=F32)
    dtv = jax.nn.softplus(dtr + dtb_ref[...])
    if valid < ROW_TILE:
        rows = lax.broadcasted_iota(jnp.int32, dtv.shape, 0)
        dtv = jnp.where(rows < valid, dtv, 0.0)
    acum = _seg_cumsum(dtv * (-jnp.exp(alog_ref[...])), q)
    a_minus_logdt = acum - jnp.log(dtv)
    for t in range(tps):
        at_ref[t] = acum[t * ROW_TILE:(t + 1) * ROW_TILE].T
        bt_ref[t] = a_minus_logdt[t * ROW_TILE:(t + 1) * ROW_TILE].T


def _headscal(hs, w_t, dt_bias, a_log, *, q, valid, tile0, ntiles):
    tps = min(ntiles, 8)
    assert ntiles % tps == 0 and tile0 % tps == 0 and q <= ROW_TILE
    kern = functools.partial(_headscal_kernel, q=q, valid=valid, tps=tps)
    shape = jax.ShapeDtypeStruct((ntiles, LANES, ROW_TILE), F32)
    dt_blk = W_DT // LANES
    return pl.pallas_call(
        kern,
        grid=(ntiles // tps,),
        in_specs=[
            pl.BlockSpec((tps * ROW_TILE, D_MODEL), lambda t: (tile0 // tps + t, 0)),
            pl.BlockSpec((LANES, D_MODEL), lambda t: (dt_blk, 0)),
            pl.BlockSpec((1, LANES), lambda t: (0, 0)),
            pl.BlockSpec((1, LANES), lambda t: (0, 0)),
        ],
        out_specs=[
            pl.BlockSpec((tps, LANES, ROW_TILE), lambda t: (t, 0, 0)),
            pl.BlockSpec((tps, LANES, ROW_TILE), lambda t: (t, 0, 0)),
        ],
        out_shape=[shape, shape],
        compiler_params=pltpu.CompilerParams(dimension_semantics=("arbitrary",)),
        name="headscal",
    )(hs, w_t, dt_bias, a_log)


def _xbc_kernel(hs_ref, hm_ref, wt_ref, cw_ref, cb_ref, prev_ref,
                o_ref, om_ref, cnp_ref, cns_ref,
                wb_scr, lhs_scr, halo_p, halo_s, halo_m, meta_prev, zero_prev,
                *, tm, n_prompt_tiles, tiles_per_seq, bs_sample, q_sample):
    i = pl.program_id(1)
    keep = SSD_CONV_W - 1
    tn = o_ref.shape[1]

    @pl.when(i == 0)
    def _():
        wb_scr[...] = wt_ref[...].T.astype(BF16)
        lhs_scr[tm:, :] = hm_ref[...]

    lhs_scr[:tm, :] = hs_ref[...]
    r = jnp.dot(lhs_scr[...], wb_scr[...], preferred_element_type=F32)
    raw = r[:tm]

    def activate(conv):
        return _silu(conv + cb_ref[...])

    @pl.when(i == 0)
    def _():
        raw_m = r[tm:]
        meta_prev[0] = raw_m[META - keep:]
        zero_prev[...] = jnp.zeros(zero_prev.shape, F32)
        conv_m = _conv_rows(raw_m, halo_m, zero_prev, cw_ref, first=i == 0, bs=1, q=META,
                            carry=False)
        om_ref[:META, :] = activate(conv_m)
        om_ref[META:, :] = jnp.zeros((ROW_TILE - META, tn), F32)

    @pl.when(i < n_prompt_tiles)
    def _():
        o_ref[...] = activate(_conv_rows(raw, halo_p, meta_prev, cw_ref,
                                         first=(i % tiles_per_seq) == 0, bs=1, q=tm, carry=True))
        cnp_ref[0] = raw[tm - keep:]

    @pl.when(i >= n_prompt_tiles)
    def _():
        o_ref[...] = activate(_conv_rows(raw, halo_s, prev_ref, cw_ref, first=i >= n_prompt_tiles,
                                         bs=bs_sample, q=q_sample, carry=False))
        cns_ref[...] = raw.reshape(bs_sample, q_sample, tn)[:, q_sample - keep:, :]


def _xbc(hs, hm, w_t, cw, cb, prev_s, *, n_prompt, seq, n_seq_p, n_seq_s, q_sample,
         tm=1024, tn=512):
    rows = hs.shape[0]
    keep = SSD_CONV_W - 1
    assert seq % tm == 0 and (rows - n_prompt) == tm and tm == n_seq_s * q_sample
    assert D_INNER % tn == 0 and CONV_DIM % tn == 0
    tiles_per_seq = seq // tm
    n_pt = n_prompt // tm
    kern = functools.partial(_xbc_kernel, tm=tm, n_prompt_tiles=n_pt, tiles_per_seq=tiles_per_seq,
                             bs_sample=n_seq_s, q_sample=q_sample)
    return pl.pallas_call(
        kern,
        grid=(CONV_DIM // tn, rows // tm),
        in_specs=[
            pl.BlockSpec((tm, D_MODEL), lambda j, i: (i, 0)),
            pl.BlockSpec((META, D_MODEL), lambda j, i: (0, 0)),
            pl.BlockSpec((tn, D_MODEL), lambda j, i: (D_INNER // tn + j, 0)),
            pl.BlockSpec((SSD_CONV_W, tn), lambda j, i: (0, j)),
            pl.BlockSpec((1, tn), lambda j, i: (0, j)),
            pl.BlockSpec((n_seq_s, keep, tn), lambda j, i: (0, 0, j)),
        ],
        out_specs=[
            pl.BlockSpec((tm, tn), lambda j, i: (i, j)),
            pl.BlockSpec((ROW_TILE, tn), lambda j, i: (0, j)),
            pl.BlockSpec((1, keep, tn),
                         lambda j, i: (jnp.minimum(i, n_pt - 1) // tiles_per_seq, 0, j)),
            pl.BlockSpec((n_seq_s, keep, tn), lambda j, i: (0, 0, j)),
        ],
        out_shape=[
            jax.ShapeDtypeStruct((rows, CONV_DIM), F32),
            jax.ShapeDtypeStruct((ROW_TILE, CONV_DIM), F32),
            jax.ShapeDtypeStruct((n_seq_p, keep, CONV_DIM), F32),
            jax.ShapeDtypeStruct((n_seq_s, keep, CONV_DIM), F32),
        ],
        scratch_shapes=[
            pltpu.VMEM((D_MODEL, tn), BF16),
            pltpu.VMEM((tm + META, D_MODEL), BF16),
            pltpu.VMEM((1, SUBLANES, tn), F32),
            pltpu.VMEM((n_seq_s, SUBLANES, tn), F32),
            pltpu.VMEM((1, SUBLANES, tn), F32),
            pltpu.VMEM((1, keep, tn), F32),
            pltpu.VMEM((1, keep, tn), F32),
        ],
        compiler_params=pltpu.CompilerParams(
            dimension_semantics=("arbitrary", "arbitrary"),
            vmem_limit_bytes=VMEM_LIMIT),
        name="xbc",
    )(hs, hm, w_t, cw, cb, prev_s)


def _ssd_kernel(sz_ref, x_ref, b_ref, c_ref, at_ref, bt_ref, dsk_ref, nw_ref, s0_ref,
                yn_ref, sout_ref, st_ref, *, bs, q, nc, gps):
    rt = bs * q
    nh = gps * HEADS_PER_GROUP
    c = pl.program_id(2)
    carry = nc > 1

    if carry:
        @pl.when(c == 0)
        def _():
            st_ref[...] = s0_ref[...]

    xc = x_ref[...]
    bcb = b_ref[...].astype(BF16)
    ccb = c_ref[...].astype(BF16)

    a_t = at_ref[0]
    b_t = bt_ref[0]
    cols = jnp.concatenate([a_t, jnp.zeros((LANES - nh, rt), F32)], axis=0).T

    pos = lax.broadcasted_iota(jnp.int32, (nh, rt), 1) & (q - 1)
    a_end = a_t
    s = 1
    while s < q:
        a_end = jnp.where(pos + s < q, pltpu.roll(a_end, rt - s, 1), a_end)
        s *= 2
    to_end = jnp.exp(a_end - b_t)

    nblk = rt // SUBLANES
    ri = lax.broadcasted_iota(jnp.int32, (nblk, SUBLANES, rt), 0) * SUBLANES + \
        lax.broadcasted_iota(jnp.int32, (nblk, SUBLANES, rt), 1)
    ci = lax.broadcasted_iota(jnp.int32, (nblk, SUBLANES, rt), 2)
    mask = (ri >= ci) & ((ri // q) == (ci // q))
    low = lax.broadcasted_iota(jnp.int32, (rt, LANES), 1) < HEAD_DIM
    seq_of_row = lax.broadcasted_iota(jnp.int32, (rt, N_STATE), 0) // q

    for k in range(gps):
        xg = xc[:, k * GROUP_W:(k + 1) * GROUP_W]
        bg = bcb[:, k * N_STATE:(k + 1) * N_STATE]
        cg = ccb[:, k * N_STATE:(k + 1) * N_STATE]
        cb = lax.dot_general(cg, bg, (((1,), (1,)), ((), ())), preferred_element_type=F32)
        cb3 = cb.reshape(nblk, SUBLANES, rt)
        xt = xg.T

        ydiag, ea, xw, a_cols = [], [], [], []
        for pr in range(HEADS_PER_GROUP // 2):
            wts, ab = [], []
            for hh in range(2):
                h = k * HEADS_PER_GROUP + 2 * pr + hh
                a_col = jnp.broadcast_to(cols[:, h:h + 1], (rt, LANES))
                b_row = jnp.broadcast_to(b_t[h:h + 1, :], (SUBLANES, rt))
                seg = jnp.where(mask, a_col.reshape(nblk, SUBLANES, rt) - b_row[None], -jnp.inf)
                wts.append((cb3 * jnp.exp(seg)).reshape(rt, rt).astype(BF16))
                ab.append(a_col)
                rows = slice((2 * pr + hh) * HEAD_DIM, (2 * pr + hh + 1) * HEAD_DIM)
                xw.append(xt[rows] * to_end[h:h + 1, :])
            a_cols += ab
            ea.append(jnp.exp(jnp.where(low, ab[0], ab[1])))
            xb = xg[:, pr * LANES:(pr + 1) * LANES].astype(BF16)
            zero = jnp.zeros_like(xb)
            rhs = jnp.concatenate([jnp.where(low, xb, zero), jnp.where(low, zero, xb)], axis=0)
            ydiag.append(jnp.dot(jnp.concatenate(wts, axis=1), rhs, preferred_element_type=F32))
        ydiag = jnp.concatenate(ydiag, axis=1)
        ea = jnp.concatenate(ea, axis=1)
        xwt = jnp.concatenate(xw, axis=0).astype(BF16)

        yoff = []
        for s in range(bs):
            st = st_ref[s, k] if carry else s0_ref[s, k]
            yoff.append(lax.dot_general(cg[s * q:(s + 1) * q, :], st.astype(BF16),
                                        (((1,), (1,)), ((), ())), preferred_element_type=F32))
            bsel = bg if bs == 1 else jnp.where(seq_of_row == s, bg, jnp.zeros_like(bg))
            upd = jnp.dot(xwt, bsel, preferred_element_type=F32)
            last = (s + 1) * q - 1
            dec = jnp.concatenate(
                [jnp.broadcast_to(jnp.exp(a_cols[h][last:last + 1, :]), (HEAD_DIM, N_STATE))
                 for h in range(HEADS_PER_GROUP)], axis=0)
            new = st * dec + upd
            if carry:
                st_ref[s, k] = new

                @pl.when(c == nc - 1)
                def _():
                    sout_ref[s, k] = new
            else:
                sout_ref[s, k] = new
        yoff = yoff[0] if bs == 1 else jnp.concatenate(yoff, axis=0)

        lanes = slice(k * GROUP_W, (k + 1) * GROUP_W)
        y = ydiag + yoff * ea + dsk_ref[:, lanes] * xg
        gz = y * sz_ref[:, lanes]
        ms = jnp.mean(gz * gz, axis=-1, keepdims=True)
        yn_ref[:, lanes] = (gz * lax.rsqrt(ms + EPS) * nw_ref[:, lanes]).astype(BF16)


def _ssd(sz, xbc, a_t, dt_t, d_skip_x, norm_w, state0,
         *, n_seq, bs, q, nc, gps, shared_init, tile0=0):
    rt = bs * q
    assert rt == ROW_TILE and N_GROUPS % gps == 0
    rows = n_seq * q * nc
    nsb = n_seq // bs
    gw, gn, nh = gps * GROUP_W, gps * N_STATE, gps * HEADS_PER_GROUP
    kern = functools.partial(_ssd_kernel, bs=bs, q=q, nc=nc, gps=gps)
    bb, bc_ = D_INNER // gn, (D_INNER + N_GROUPS * N_STATE) // gn
    sidx = (lambda s: 0) if shared_init else (lambda s: s)

    def tile(s, c):
        return s * nc + c

    in_specs = [
        pl.BlockSpec((rt, gw), lambda s, g, c: (tile0 + tile(s, c), g)),
        pl.BlockSpec((rt, gw), lambda s, g, c: (tile0 + tile(s, c), g)),
        pl.BlockSpec((rt, gn), lambda s, g, c: (tile0 + tile(s, c), bb + g)),
        pl.BlockSpec((rt, gn), lambda s, g, c: (tile0 + tile(s, c), bc_ + g)),
        pl.BlockSpec((1, nh, rt), lambda s, g, c: (tile(s, c), g, 0)),
        pl.BlockSpec((1, nh, rt), lambda s, g, c: (tile(s, c), g, 0)),
        pl.BlockSpec((1, gw), lambda s, g, c: (0, g)),
        pl.BlockSpec((1, gw), lambda s, g, c: (0, g)),
        pl.BlockSpec((bs, gps, GROUP_W, N_STATE), lambda s, g, c: (sidx(s), g, 0, 0)),
    ]
    out_specs = [
        pl.BlockSpec((rt, gw), lambda s, g, c: (tile(s, c), g)),
        pl.BlockSpec((bs, gps, GROUP_W, N_STATE), lambda s, g, c: (s, g, 0, 0)),
    ]
    st_shape = (bs, gps, GROUP_W, N_STATE) if nc > 1 else (1, 1, SUBLANES, N_STATE)
    return pl.pallas_call(
        kern,
        grid=(nsb, N_GROUPS // gps, nc),
        in_specs=in_specs,
        out_specs=out_specs,
        out_shape=[
            jax.ShapeDtypeStruct((rows, D_INNER), BF16),
            jax.ShapeDtypeStruct((n_seq, N_GROUPS, GROUP_W, N_STATE), F32),
        ],
        scratch_shapes=[pltpu.VMEM(st_shape, F32)],
        compiler_params=pltpu.CompilerParams(
            dimension_semantics=("arbitrary", "arbitrary", "arbitrary"),
            vmem_limit_bytes=VMEM_LIMIT),
        name="ssd",
    )(sz, xbc, xbc, xbc, a_t, dt_t, d_skip_x, norm_w, state0)


def _sconv_kernel(hs_ref, hm_ref, wb_ref, wc_ref, wh_ref, wz_ref, cw_ref, prev_ref,
                  v_ref, newp_ref, news_ref,
                  w_scr, lhs_scr, halo_p, halo_s, meta_u,
                  *, tm, width, n_prompt_tiles, tiles_per_seq, bs_sample, q_sample):
    i = pl.program_id(1)
    keep = SC_CONV_W - 1

    @pl.when(i == 0)
    def _():
        for k, w_ref in enumerate((wb_ref, wc_ref, wh_ref, wz_ref)):
            w_scr[:, k * width:(k + 1) * width] = w_ref[...].T.astype(BF16)
        lhs_scr[tm:, :] = hm_ref[...]

    lhs_scr[:tm, :] = hs_ref[...]
    r = jnp.dot(lhs_scr[...], w_scr[...], preferred_element_type=F32)
    u_all = r[:, width:2 * width] * r[:, 2 * width:3 * width]
    u = u_all[:tm]

    @pl.when(i == 0)
    def _():
        meta_u[0] = u_all[tm + META - keep:tm + META]

    def finish(uc):
        v_ref[...] = (r[:tm, :width] * uc * _silu(r[:tm, 3 * width:])).astype(BF16)

    @pl.when(i < n_prompt_tiles)
    def _():
        finish(_conv_rows(u, halo_p, meta_u, cw_ref, first=(i % tiles_per_seq) == 0,
                          bs=1, q=tm, carry=True))
        newp_ref[0] = u[tm - keep:]

    @pl.when(i >= n_prompt_tiles)
    def _():
        finish(_conv_rows(u, halo_s, prev_ref, cw_ref, first=i >= n_prompt_tiles,
                          bs=bs_sample, q=q_sample, carry=False))
        news_ref[...] = u.reshape(bs_sample, q_sample, width)[:, q_sample - keep:, :]


def _sconv(hs, hm, w_t, cw, prev_s, *, n_prompt, seq, n_seq_p, n_seq_s, q_sample,
           tm=1024, width=256):
    rows = hs.shape[0]
    keep = SC_CONV_W - 1
    assert seq % tm == 0 and (rows - n_prompt) == tm and tm == n_seq_s * q_sample
    tiles_per_seq = seq // tm
    n_pt = n_prompt // tm
    kern = functools.partial(_sconv_kernel, tm=tm, width=width, n_prompt_tiles=n_pt,
                             tiles_per_seq=tiles_per_seq, bs_sample=n_seq_s, q_sample=q_sample)

    def w_rows(k):
        base = (W_SC + k * D_MODEL) // N_HEADS
        return lambda cbk, i: ((base + cbk * (width // N_HEADS)) * N_HEADS, 0)

    w_specs = [pl.BlockSpec((pl.Element(width), pl.Element(D_MODEL)), w_rows(k)) for k in range(4)]
    return pl.pallas_call(
        kern,
        grid=(D_MODEL // width, rows // tm),
        in_specs=[
            pl.BlockSpec((tm, D_MODEL), lambda cbk, i: (i, 0)),
            pl.BlockSpec((META, D_MODEL), lambda cbk, i: (0, 0)),
            *w_specs,
            pl.BlockSpec((SC_CONV_W, width), lambda cbk, i: (0, cbk)),
            pl.BlockSpec((n_seq_s, keep, width), lambda cbk, i: (0, 0, cbk)),
        ],
        out_specs=[
            pl.BlockSpec((tm, width), lambda cbk, i: (i, cbk)),
            pl.BlockSpec((1, keep, width),
                         lambda cbk, i: (jnp.minimum(i, n_pt - 1) // tiles_per_seq, 0, cbk)),
            pl.BlockSpec((n_seq_s, keep, width), lambda cbk, i: (0, 0, cbk)),
        ],
        out_shape=[
            jax.ShapeDtypeStruct((rows, D_MODEL), BF16),
            jax.ShapeDtypeStruct((n_seq_p, keep, D_MODEL), F32),
            jax.ShapeDtypeStruct((n_seq_s, keep, D_MODEL), F32),
        ],
        scratch_shapes=[
            pltpu.VMEM((D_MODEL, 4 * width), BF16),
            pltpu.VMEM((tm + META, D_MODEL), BF16),
            pltpu.VMEM((1, SUBLANES, width), F32),
            pltpu.VMEM((n_seq_s, SUBLANES, width), F32),
            pltpu.VMEM((1, keep, width), F32),
        ],
        compiler_params=pltpu.CompilerParams(
            dimension_semantics=("arbitrary", "arbitrary"),
            vmem_limit_bytes=VMEM_LIMIT),
        name="sconv",
    )(hs, hm, w_t, w_t, w_t, w_t, cw, prev_s)


def _merge_kernel(yn_ref, v_ref, ga_ref, gb_ref, wa_ref, wb_ref, o_ref):
    ya = jnp.dot(yn_ref[...], wa_ref[...], preferred_element_type=F32)
    yb = jnp.dot(v_ref[...], wb_ref[...], preferred_element_type=F32)
    o_ref[...] = (jax.nn.sigmoid(ga_ref[...]) * ya + jax.nn.sigmoid(gb_ref[...]) * yb).astype(BF16)


def _merge(yn, v, gates, wa, wb, *, tm, tn, row0=0):
    rows = yn.shape[0]
    ga0, gb0 = 0, D_MODEL // tn
    assert row0 % tm == 0
    t0 = row0 // tm
    return pl.pallas_call(
        _merge_kernel,
        grid=(D_MODEL // tn, rows // tm),
        in_specs=[
            pl.BlockSpec((tm, D_INNER), lambda j, i: (i, 0)),
            pl.BlockSpec((tm, D_MODEL), lambda j, i: (t0 + i, 0)),
            pl.BlockSpec((tm, tn), lambda j, i: (t0 + i, ga0 + j)),
            pl.BlockSpec((tm, tn), lambda j, i: (t0 + i, gb0 + j)),
            pl.BlockSpec((D_INNER, tn), lambda j, i: (0, j)),
            pl.BlockSpec((D_MODEL, tn), lambda j, i: (0, j)),
        ],
        out_specs=pl.BlockSpec((tm, tn), lambda j, i: (i, j)),
        out_shape=jax.ShapeDtypeStruct((rows, D_MODEL), BF16),
        compiler_params=pltpu.CompilerParams(
            dimension_semantics=("arbitrary", "arbitrary"),
            vmem_limit_bytes=VMEM_LIMIT),
        name="merge",
    )(yn, v, gates, gates, wa, wb)


def _outproj_kernel(m_ref, x_ref, wo_ref, fw_ref, o_ref):
    y = x_ref[...] + jnp.dot(m_ref[...], wo_ref[...], preferred_element_type=F32)
    ms = jnp.mean(y * y, axis=-1, keepdims=True)
    o_ref[...] = y * lax.rsqrt(ms + EPS) * fw_ref[...]


def _outproj(m, x, wo, fw, *, tm):
    rows = m.shape[0]
    return pl.pallas_call(
        _outproj_kernel,
        grid=(rows // tm,),
        in_specs=[
            pl.BlockSpec((tm, D_MODEL), lambda i: (i, 0)),
            pl.BlockSpec((tm, D_MODEL), lambda i: (i, 0)),
            pl.BlockSpec((D_MODEL, D_MODEL), lambda i: (0, 0)),
            pl.BlockSpec((1, D_MODEL), lambda i: (0, 0)),
        ],
        out_specs=pl.BlockSpec((tm, D_MODEL), lambda i: (i, 0)),
        out_shape=jax.ShapeDtypeStruct((rows, D_MODEL), F32),
        compiler_params=pltpu.CompilerParams(
            dimension_semantics=("arbitrary",),
            vmem_limit_bytes=VMEM_LIMIT),
        name="outproj",
    )(m, x, wo, fw)


def kernel(x_prompt, x_sample, state_ssd_conv, state_ssm, state_sconv, meta_tokens, norm_w,
           w_in, ssd_conv_w, ssd_conv_b, dt_bias, a_log, d_skip, ssd_norm_w, w_ssd_out,
           sconv_w, w_sconv_out, w_o, final_norm_w):
    bp, seq = x_prompt.shape[0], x_prompt.shape[1]
    bd, dec_seq = x_sample.shape[0], x_sample.shape[1]

    w_t = jnp.transpose(w_in[0])
    wa = w_ssd_out[0].astype(BF16)
    wb = w_sconv_out[0].astype(BF16)
    wo = w_o[0].astype(BF16)
    nw = norm_w[0].reshape(1, D_MODEL)
    fw = final_norm_w.reshape(1, D_MODEL)
    conv_w = ssd_conv_w[0]
    conv_b = ssd_conv_b[0].reshape(1, CONV_DIM)
    dtb = jnp.pad(dt_bias[0], (0, LANES - N_HEADS)).reshape(1, LANES)
    alog = jnp.pad(a_log[0], (0, LANES - N_HEADS)).reshape(1, LANES)
    dsk = jnp.repeat(d_skip[0], HEAD_DIM).reshape(1, D_INNER)
    gnw = ssd_norm_w[0].reshape(1, D_INNER)
    scw = sconv_w[0]

    xp = x_prompt.reshape(bp * seq, D_MODEL)
    xs = x_sample.reshape(bd * dec_seq, D_MODEL)
    n_p, n_s = bp * seq, bd * dec_seq
    streams = dict(n_prompt=n_p, seq=seq, n_seq_p=bp, n_seq_s=bd, q_sample=dec_seq)

    hs, hm = _norm(xp, xs, meta_tokens, nw)
    sz = _proj(hs, w_t, w_row0=0, ncols=D_INNER, silu=True)
    gates = _proj(hs, w_t, w_row0=W_GATE, ncols=2 * D_MODEL, silu=False)
    xbc, xbc_m, conv_p, conv_s = _xbc(hs, hm, w_t, conv_w, conv_b, state_ssd_conv[0], **streams)
    v_all, sc_p, sc_s = _sconv(hs, hm, w_t, scw, state_sconv[0], **streams)
    headscal = functools.partial(_headscal, w_t=w_t, dt_bias=dtb, a_log=alog)
    ssd = functools.partial(_ssd, d_skip_x=dsk, norm_w=gnw)

    at_m, dtt_m = headscal(hm, q=ROW_TILE, valid=META, tile0=0, ntiles=1)
    _, ssm_m = ssd(jnp.zeros((ROW_TILE, D_INNER), F32), xbc_m, at_m, dtt_m,
                   state0=jnp.zeros((1, N_GROUPS, GROUP_W, N_STATE), F32),
                   n_seq=1, bs=1, q=ROW_TILE, nc=1, gps=2, shared_init=False)

    at_p, dtt_p = headscal(hs, q=ROW_TILE, valid=ROW_TILE, tile0=0, ntiles=n_p // ROW_TILE)
    yn_p, ssm_p = ssd(sz, xbc, at_p, dtt_p, state0=ssm_m, n_seq=bp, bs=1, q=ROW_TILE,
                      nc=seq // ROW_TILE, gps=8, shared_init=True)
    m_p = _merge(yn_p, v_all, gates, wa, wb, tm=256, tn=1024)
    y_p = _outproj(m_p, xp, wo, fw, tm=512)

    sbs = ROW_TILE // dec_seq
    at_s, dtt_s = headscal(hs, q=dec_seq, valid=ROW_TILE, tile0=n_p // ROW_TILE,
                           ntiles=n_s // ROW_TILE)
    yn_s, ssm_s = ssd(sz, xbc, at_s, dtt_s,
                      state0=state_ssm[0].reshape(bd, N_GROUPS, GROUP_W, N_STATE),
                      n_seq=bd, bs=sbs, q=dec_seq, nc=1, gps=2, shared_init=False,
                      tile0=n_p // ROW_TILE)
    m_s = _merge(yn_s, v_all, gates, wa, wb, tm=256, tn=1024, row0=n_p)
    y_s = _outproj(m_s, xs, wo, fw, tm=512)

    return (y_p.reshape(bp, seq, D_MODEL),
            y_s.reshape(bd, dec_seq, D_MODEL),
            conv_p[None],
            ssm_p.reshape(1, bp, N_HEADS, HEAD_DIM, N_STATE),
            sc_p[None],
            conv_s[None],
            ssm_s.reshape(1, bd, N_HEADS, HEAD_DIM, N_STATE),
            sc_s[None])
```

```python
import functools

import jax
import jax.numpy as jnp
from jax import lax
from jax.experimental import pallas as pl
from jax.experimental.pallas import tpu as pltpu

F32 = jnp.float32
BF16 = jnp.bfloat16

D_MODEL = 2048
D_INNER = 4096
N_HEADS = 64
HEAD_DIM = 64
N_STATE = 128
N_GROUPS = 8
GROUP_W = D_INNER // N_GROUPS
HEADS_PER_GROUP = N_HEADS // N_GROUPS
CONV_DIM = D_INNER + 2 * N_GROUPS * N_STATE
SSD_CONV_W = 4
SC_CONV_W = 3
META = 16
EPS = 1e-6

LANES = 128
SUBLANES = 8
ROW_TILE = 128
W_DT = D_INNER + CONV_DIM
W_SC = W_DT + N_HEADS
W_GATE = W_SC + 4 * D_MODEL

VMEM_LIMIT = 52 * 1024 * 1024


def _silu(x):
    h = 0.5 * x
    return h * (1.0 + jnp.tanh(h))


def _rms_bf16(x, w):
    ms = jnp.mean(x * x, axis=-1, keepdims=True)
    return (x * lax.rsqrt(ms + EPS) * w).astype(BF16)


def _norm_kernel(xp_ref, xs_ref, xm_ref, nw_ref, hs_ref, hm_ref, *, n_prompt):
    i = pl.program_id(0)

    @pl.when(i < n_prompt)
    def _():
        hs_ref[...] = _rms_bf16(xp_ref[...], nw_ref[...])

    @pl.when(i >= n_prompt)
    def _():
        hs_ref[...] = _rms_bf16(xs_ref[...], nw_ref[...])

    @pl.when(i == 0)
    def _():
        hm_ref[:META, :] = _rms_bf16(xm_ref[...], nw_ref[...])
        hm_ref[META:, :] = jnp.zeros((ROW_TILE - META, D_MODEL), BF16)


def _norm(xp, xs, xm, norm_w, *, tm=512):
    n_p, n_s = xp.shape[0] // tm, xs.shape[0] // tm
    kern = functools.partial(_norm_kernel, n_prompt=n_p)
    return pl.pallas_call(
        kern,
        grid=(n_p + n_s,),
        in_specs=[
            pl.BlockSpec((tm, D_MODEL), lambda i: (jnp.minimum(i, n_p - 1), 0)),
            pl.BlockSpec((tm, D_MODEL), lambda i: (jnp.maximum(i - n_p, 0), 0)),
            pl.BlockSpec((META, D_MODEL), lambda i: (0, 0)),
            pl.BlockSpec((1, D_MODEL), lambda i: (0, 0)),
        ],
        out_specs=[
            pl.BlockSpec((tm, D_MODEL), lambda i: (i, 0)),
            pl.BlockSpec((ROW_TILE, D_MODEL), lambda i: (0, 0)),
        ],
        out_shape=[
            jax.ShapeDtypeStruct((xp.shape[0] + xs.shape[0], D_MODEL), BF16),
            jax.ShapeDtypeStruct((ROW_TILE, D_MODEL), BF16),
        ],
        compiler_params=pltpu.CompilerParams(
            dimension_semantics=("arbitrary",), vmem_limit_bytes=VMEM_LIMIT),
        name="norm",
    )(xp, xs, xm, norm_w)


def _proj_kernel(hs_ref, wt_ref, o_ref, wb_ref, *, silu):
    @pl.when(pl.program_id(1) == 0)
    def _():
        wb_ref[...] = wt_ref[...].T.astype(BF16)

    r = jnp.dot(hs_ref[...], wb_ref[...], preferred_element_type=F32)
    o_ref[...] = _silu(r) if silu else r


def _proj(hs, w_t, *, w_row0, ncols, silu, tm=1024, tn=1024):
    rows = hs.shape[0]
    assert w_row0 % N_HEADS == 0 and ncols % tn == 0 and tn % N_HEADS == 0

    def w_rows(j, i):
        return ((w_row0 // N_HEADS + j * (tn // N_HEADS)) * N_HEADS, 0)

    return pl.pallas_call(
        functools.partial(_proj_kernel, silu=silu),
        grid=(ncols // tn, rows // tm),
        in_specs=[
            pl.BlockSpec((tm, D_MODEL), lambda j, i: (i, 0)),
            pl.BlockSpec((pl.Element(tn), pl.Element(D_MODEL)), w_rows),
        ],
        out_specs=pl.BlockSpec((tm, tn), lambda j, i: (i, j)),
        out_shape=jax.ShapeDtypeStruct((rows, ncols), F32),
        scratch_shapes=[pltpu.VMEM((D_MODEL, tn), BF16)],
        compiler_params=pltpu.CompilerParams(
            dimension_semantics=("arbitrary", "arbitrary"),
            vmem_limit_bytes=VMEM_LIMIT),
        name="proj",
    )(hs, w_t)


def _conv_rows(x, halo_ref, prev_ref, w_ref, *, first, bs, q, carry):
    taps = w_ref.shape[0]
    rt, width = x.shape

    @pl.when(first)
    def _():
        halo_ref[:, SUBLANES - (taps - 1):, :] = prev_ref[...]

    prev = halo_ref[...]
    acc = None
    if bs == 1:
        row = lax.broadcasted_iota(jnp.int32, (SUBLANES, width), 0)
        for s in range(taps - 1, 0, -1):
            rolled = pltpu.roll(x, s, 0)
            head = jnp.where(row < s, pltpu.roll(prev[0], s, 0), rolled[:SUBLANES])
            term = jnp.concatenate([head, rolled[SUBLANES:]], axis=0) * w_ref[taps - 1 - s:taps - s, :]
            acc = term if acc is None else acc + term
        acc = acc + x * w_ref[taps - 1:taps, :]
        if carry:
            halo_ref[0] = x[rt - SUBLANES:, :]
        return acc
    assert q == SUBLANES and not carry
    x3 = x.reshape(bs, q, width)
    row = lax.broadcasted_iota(jnp.int32, x3.shape, 1)
    for s in range(taps - 1, 0, -1):
        shifted = jnp.where(row < s, pltpu.roll(prev, s, 1), pltpu.roll(x3, s, 1))
        term = shifted * w_ref[taps - 1 - s:taps - s, :]
        acc = term if acc is None else acc + term
    acc = acc + x3 * w_ref[taps - 1:taps, :]
    return acc.reshape(rt, width)


def _seg_cumsum(a, q):
    pos = lax.broadcasted_iota(jnp.int32, a.shape, 0) & (q - 1)
    s = 1
    while s < q:
        shifted = pltpu.roll(a, s, 0)
        a = a + jnp.where(pos >= s, shifted, 0.0)
        s *= 2
    return a


def _headscal_kernel(hs_ref, wdt_ref, dtb_ref, alog_ref, at_ref, bt_ref, *, q, valid, tps):
    dtr = lax.dot_general(hs_ref[...], wdt_ref[...].astype(BF16), (((1,), (1,)), ((), ())),
                          preferred_element_type=F32)
    dtv = jax.nn.softplus(dtr + dtb_ref[...])
    if valid < ROW_TILE:
        rows = lax.broadcasted_iota(jnp.int32, dtv.shape, 0)
        dtv = jnp.where(rows < valid, dtv, 0.0)
    acum = _seg_cumsum(dtv * (-jnp.exp(alog_ref[...])), q)
    a_minus_logdt = acum - jnp.log(dtv)
    for t in range(tps):
        at_ref[t] = acum[t * ROW_TILE:(t + 1) * ROW_TILE].T
        bt_ref[t] = a_minus_logdt[t * ROW_TILE:(t + 1) * ROW_TILE].T


def _headscal(hs, w_t, dt_bias, a_log, *, q, valid, tile0, ntiles):
    tps = min(ntiles, 8)
    assert ntiles % tps == 0 and tile0 % tps == 0 and q <= ROW_TILE
    kern = functools.partial(_headscal_kernel, q=q, valid=valid, tps=tps)
    shape = jax.ShapeDtypeStruct((ntiles, LANES, ROW_TILE), F32)
    dt_blk = W_DT // LANES
    return pl.pallas_call(
        kern,
        grid=(ntiles // tps,),
        in_specs=[
            pl.BlockSpec((tps * ROW_TILE, D_MODEL), lambda t: (tile0 // tps + t, 0)),
            pl.BlockSpec((LANES, D_MODEL), lambda t: (dt_blk, 0)),
            pl.BlockSpec((1, LANES), lambda t: (0, 0)),
            pl.BlockSpec((1, LANES), lambda t: (0, 0)),
        ],
        out_specs=[
            pl.BlockSpec((tps, LANES, ROW_TILE), lambda t: (t, 0, 0)),
            pl.BlockSpec((tps, LANES, ROW_TILE), lambda t: (t, 0, 0)),
        ],
        out_shape=[shape, shape],
        compiler_params=pltpu.CompilerParams(dimension_semantics=("arbitrary",)),
        name="headscal",
    )(hs, w_t, dt_bias, a_log)


def _xbc_kernel(hs_ref, hm_ref, wt_ref, cw_ref, cb_ref, prev_ref,
                o_ref, om_ref, cnp_ref, cns_ref,
                wb_scr, lhs_scr, halo_p, halo_s, halo_m, meta_prev, zero_prev,
                *, tm, n_prompt_tiles, tiles_per_seq, bs_sample, q_sample):
    i = pl.program_id(1)
    keep = SSD_CONV_W - 1
    tn = o_ref.shape[1]

    @pl.when(i == 0)
    def _():
        wb_scr[...] = wt_ref[...].T.astype(BF16)
        lhs_scr[tm:, :] = hm_ref[...]

    lhs_scr[:tm, :] = hs_ref[...]
    r = jnp.dot(lhs_scr[...], wb_scr[...], preferred_element_type=F32)
    raw = r[:tm]

    def activate(conv):
        return _silu(conv + cb_ref[...])

    @pl.when(i == 0)
    def _():
        raw_m = r[tm:]
        meta_prev[0] = raw_m[META - keep:]
        zero_prev[...] = jnp.zeros(zero_prev.shape, F32)
        conv_m = _conv_rows(raw_m, halo_m, zero_prev, cw_ref, first=i == 0, bs=1, q=META,
                            carry=False)
        om_ref[:META, :] = activate(conv_m)
        om_ref[META:, :] = jnp.zeros((ROW_TILE - META, tn), F32)

    @pl.when(i < n_prompt_tiles)
    def _():
        o_ref[...] = activate(_conv_rows(raw, halo_p, meta_prev, cw_ref,
                                         first=(i % tiles_per_seq) == 0, bs=1, q=tm, carry=True))
        cnp_ref[0] = raw[tm - keep:]

    @pl.when(i >= n_prompt_tiles)
    def _():
        o_ref[...] = activate(_conv_rows(raw, halo_s, prev_ref, cw_ref, first=i >= n_prompt_tiles,
                                         bs=bs_sample, q=q_sample, carry=False))
        cns_ref[...] = raw.reshape(bs_sample, q_sample, tn)[:, q_sample - keep:, :]


def _xbc(hs, hm, w_t, cw, cb, prev_s, *, n_prompt, seq, n_seq_p, n_seq_s, q_sample,
         tm=1024, tn=512):
    rows = hs.shape[0]
    keep = SSD_CONV_W - 1
    assert seq % tm == 0 and (rows - n_prompt) == tm and tm == n_seq_s * q_sample
    assert D_INNER % tn == 0 and CONV_DIM % tn == 0
    tiles_per_seq = seq // tm
    n_pt = n_prompt // tm
    kern = functools.partial(_xbc_kernel, tm=tm, n_prompt_tiles=n_pt, tiles_per_seq=tiles_per_seq,
                             bs_sample=n_seq_s, q_sample=q_sample)
    return pl.pallas_call(
        kern,
        grid=(CONV_DIM // tn, rows // tm),
        in_specs=[
            pl.BlockSpec((tm, D_MODEL), lambda j, i: (i, 0)),
            pl.BlockSpec((META, D_MODEL), lambda j, i: (0, 0)),
            pl.BlockSpec((tn, D_MODEL), lambda j, i: (D_INNER // tn + j, 0)),
            pl.BlockSpec((SSD_CONV_W, tn), lambda j, i: (0, j)),
            pl.BlockSpec((1, tn), lambda j, i: (0, j)),
            pl.BlockSpec((n_seq_s, keep, tn), lambda j, i: (0, 0, j)),
        ],
        out_specs=[
            pl.BlockSpec((tm, tn), lambda j, i: (i, j)),
            pl.BlockSpec((ROW_TILE, tn), lambda j, i: (0, j)),
            pl.BlockSpec((1, keep, tn),
                         lambda j, i: (jnp.minimum(i, n_pt - 1) // tiles_per_seq, 0, j)),
            pl.BlockSpec((n_seq_s, keep, tn), lambda j, i: (0, 0, j)),
        ],
        out_shape=[
            jax.ShapeDtypeStruct((rows, CONV_DIM), F32),
            jax.ShapeDtypeStruct((ROW_TILE, CONV_DIM), F32),
            jax.ShapeDtypeStruct((n_seq_p, keep, CONV_DIM), F32),
            jax.ShapeDtypeStruct((n_seq_s, keep, CONV_DIM), F32),
        ],
        scratch_shapes=[
            pltpu.VMEM((D_MODEL, tn), BF16),
            pltpu.VMEM((tm + META, D_MODEL), BF16),
            pltpu.VMEM((1, SUBLANES, tn), F32),
            pltpu.VMEM((n_seq_s, SUBLANES, tn), F32),
            pltpu.VMEM((1, SUBLANES, tn), F32),
            pltpu.VMEM((1, keep, tn), F32),
            pltpu.VMEM((1, keep, tn), F32),
        ],
        compiler_params=pltpu.CompilerParams(
            dimension_semantics=("arbitrary", "arbitrary"),
            vmem_limit_bytes=VMEM_LIMIT),
        name="xbc",
    )(hs, hm, w_t, cw, cb, prev_s)


def _ssd_kernel(sz_ref, x_ref, b_ref, c_ref, at_ref, bt_ref, dsk_ref, nw_ref, s0_ref,
                *rest, bs, q, nc, gps):
    yn_ref, sout_ref, st_ref = rest[-3:]
    rt = bs * q
    nh = gps * HEADS_PER_GROUP
    c = pl.program_id(2)
    carry = nc > 1

    if carry:
        @pl.when(c == 0)
        def _():
            st_ref[...] = s0_ref[...]

    xc = x_ref[...]
    bcb = b_ref[...].astype(BF16)
    ccb = c_ref[...].astype(BF16)

    a_t = at_ref[0]
    b_t = bt_ref[0]
    cols = jnp.concatenate([a_t, jnp.zeros((LANES - nh, rt), F32)], axis=0).T

    pos = lax.broadcasted_iota(jnp.int32, (nh, rt), 1) & (q - 1)
    a_end = a_t
    s = 1
    while s < q:
        a_end = jnp.where(pos + s < q, pltpu.roll(a_end, rt - s, 1), a_end)
        s *= 2
    to_end = jnp.exp(a_end - b_t)

    nblk = rt // SUBLANES
    ri = lax.broadcasted_iota(jnp.int32, (nblk, SUBLANES, rt), 0) * SUBLANES + \
        lax.broadcasted_iota(jnp.int32, (nblk, SUBLANES, rt), 1)
    ci = lax.broadcasted_iota(jnp.int32, (nblk, SUBLANES, rt), 2)
    mask = (ri >= ci) & ((ri // q) == (ci // q))
    low = lax.broadcasted_iota(jnp.int32, (rt, LANES), 1) < HEAD_DIM
    seq_of_row = lax.broadcasted_iota(jnp.int32, (rt, N_STATE), 0) // q

    for k in range(gps):
        xg = xc[:, k * GROUP_W:(k + 1) * GROUP_W]
        bg = bcb[:, k * N_STATE:(k + 1) * N_STATE]
        cg = ccb[:, k * N_STATE:(k + 1) * N_STATE]
        cb = lax.dot_general(cg, bg, (((1,), (1,)), ((), ())), preferred_element_type=F32)
        cb3 = cb.reshape(nblk, SUBLANES, rt)
        xt = xg.T

        ydiag, ea, xw, a_cols = [], [], [], []
        for pr in range(HEADS_PER_GROUP // 2):
            wts, ab = [], []
            for hh in range(2):
                h = k * HEADS_PER_GROUP + 2 * pr + hh
                a_col = jnp.broadcast_to(cols[:, h:h + 1], (rt, LANES))
                b_row = jnp.broadcast_to(b_t[h:h + 1, :], (SUBLANES, rt))
                seg = jnp.where(mask, a_col.reshape(nblk, SUBLANES, rt) - b_row[None], -jnp.inf)
                wts.append((cb3 * jnp.exp(seg)).reshape(rt, rt).astype(BF16))
                ab.append(a_col)
                rows = slice((2 * pr + hh) * HEAD_DIM, (2 * pr + hh + 1) * HEAD_DIM)
                xw.append(xt[rows] * to_end[h:h + 1, :])
            a_cols += ab
            ea.append(jnp.exp(jnp.where(low, ab[0], ab[1])))
            xb = xg[:, pr * LANES:(pr + 1) * LANES].astype(BF16)
            zero = jnp.zeros_like(xb)
            rhs = jnp.concatenate([jnp.where(low, xb, zero), jnp.where(low, zero, xb)], axis=0)
            ydiag.append(jnp.dot(jnp.concatenate(wts, axis=1), rhs, preferred_element_type=F32))
        ydiag = jnp.concatenate(ydiag, axis=1)
        ea = jnp.concatenate(ea, axis=1)
        xwt = jnp.concatenate(xw, axis=0).astype(BF16)

        yoff = []
        for s in range(bs):
            st = st_ref[s, k] if carry else s0_ref[s, k]
            yoff.append(lax.dot_general(cg[s * q:(s + 1) * q, :], st.astype(BF16),
                                        (((1,), (1,)), ((), ())), preferred_element_type=F32))
            bsel = bg if bs == 1 else jnp.where(seq_of_row == s, bg, jnp.zeros_like(bg))
            upd = jnp.dot(xwt, bsel, preferred_element_type=F32)
            last = (s + 1) * q - 1
            dec = jnp.concatenate(
                [jnp.broadcast_to(jnp.exp(a_cols[h][last:last + 1, :]), (HEAD_DIM, N_STATE))
                 for h in range(HEADS_PER_GROUP)], axis=0)
            new = st * dec + upd
            if carry:
                st_ref[s, k] = new

                @pl.when(c == nc - 1)
                def _():
                    sout_ref[s, k] = new
            else:
                sout_ref[s, k] = new
        yoff = yoff[0] if bs == 1 else jnp.concatenate(yoff, axis=0)

        lanes = slice(k * GROUP_W, (k + 1) * GROUP_W)
        y = ydiag + yoff * ea + dsk_ref[:, lanes] * xg
        gz = y * sz_ref[:, lanes]
        ms = jnp.mean(gz * gz, axis=-1, keepdims=True)
        yn_ref[:, lanes] = (gz * lax.rsqrt(ms + EPS) * nw_ref[:, lanes]).astype(BF16)


def _ssd(sz, xbc, a_t, dt_t, d_skip_x, norm_w, state0,
         *, n_seq, bs, q, nc, gps, shared_init, tile0=0, out_rows=None, yn_into=None):
    rt = bs * q
    assert rt == ROW_TILE and N_GROUPS % gps == 0
    rows = n_seq * q * nc if out_rows is None else out_rows
    otile0 = 0 if out_rows is None else tile0
    nsb = n_seq // bs
    gw, gn, nh = gps * GROUP_W, gps * N_STATE, gps * HEADS_PER_GROUP
    kern = functools.partial(_ssd_kernel, bs=bs, q=q, nc=nc, gps=gps)
    bb, bc_ = D_INNER // gn, (D_INNER + N_GROUPS * N_STATE) // gn
    sidx = (lambda s: 0) if shared_init else (lambda s: s)

    def tile(s, c):
        return s * nc + c

    in_specs = [
        pl.BlockSpec((rt, gw), lambda s, g, c: (tile0 + tile(s, c), g)),
        pl.BlockSpec((rt, gw), lambda s, g, c: (tile0 + tile(s, c), g)),
        pl.BlockSpec((rt, gn), lambda s, g, c: (tile0 + tile(s, c), bb + g)),
        pl.BlockSpec((rt, gn), lambda s, g, c: (tile0 + tile(s, c), bc_ + g)),
        pl.BlockSpec((1, nh, rt), lambda s, g, c: (tile(s, c), g, 0)),
        pl.BlockSpec((1, nh, rt), lambda s, g, c: (tile(s, c), g, 0)),
        pl.BlockSpec((1, gw), lambda s, g, c: (0, g)),
        pl.BlockSpec((1, gw), lambda s, g, c: (0, g)),
        pl.BlockSpec((bs, gps, GROUP_W, N_STATE), lambda s, g, c: (sidx(s), g, 0, 0)),
    ]
    out_specs = [
        pl.BlockSpec((rt, gw), lambda s, g, c: (otile0 + tile(s, c), g)),
        pl.BlockSpec((bs, gps, GROUP_W, N_STATE), lambda s, g, c: (s, g, 0, 0)),
    ]
    operands = [sz, xbc, xbc, xbc, a_t, dt_t, d_skip_x, norm_w, state0]
    aliases = {}
    if yn_into is not None:
        assert yn_into.shape == (rows, D_INNER)
        aliases = {len(operands): 0}
        in_specs.append(pl.BlockSpec(memory_space=pl.ANY))
        operands.append(yn_into)
    st_shape = (bs, gps, GROUP_W, N_STATE) if nc > 1 else (1, 1, SUBLANES, N_STATE)
    return pl.pallas_call(
        kern,
        grid=(nsb, N_GROUPS // gps, nc),
        in_specs=in_specs,
        out_specs=out_specs,
        out_shape=[
            jax.ShapeDtypeStruct((rows, D_INNER), BF16),
            jax.ShapeDtypeStruct((n_seq, N_GROUPS, GROUP_W, N_STATE), F32),
        ],
        scratch_shapes=[pltpu.VMEM(st_shape, F32)],
        input_output_aliases=aliases,
        compiler_params=pltpu.CompilerParams(
            dimension_semantics=("arbitrary", "arbitrary", "arbitrary"),
            vmem_limit_bytes=VMEM_LIMIT),
        name="ssd",
    )(*operands)


def _sconv_kernel(hs_ref, hm_ref, wb_ref, wc_ref, wh_ref, wz_ref, cw_ref, prev_ref,
                  v_ref, newp_ref, news_ref,
                  w_scr, lhs_scr, halo_p, halo_s, meta_u,
                  *, tm, width, n_prompt_tiles, tiles_per_seq, bs_sample, q_sample):
    i = pl.program_id(1)
    keep = SC_CONV_W - 1

    @pl.when(i == 0)
    def _():
        for k, w_ref in enumerate((wb_ref, wc_ref, wh_ref, wz_ref)):
            w_scr[:, k * width:(k + 1) * width] = w_ref[...].T.astype(BF16)
        lhs_scr[tm:, :] = hm_ref[...]

    lhs_scr[:tm, :] = hs_ref[...]
    r = jnp.dot(lhs_scr[...], w_scr[...], preferred_element_type=F32)
    u_all = r[:, width:2 * width] * r[:, 2 * width:3 * width]
    u = u_all[:tm]

    @pl.when(i == 0)
    def _():
        meta_u[0] = u_all[tm + META - keep:tm + META]

    def finish(uc):
        v_ref[...] = (r[:tm, :width] * uc * _silu(r[:tm, 3 * width:])).astype(BF16)

    @pl.when(i < n_prompt_tiles)
    def _():
        finish(_conv_rows(u, halo_p, meta_u, cw_ref, first=(i % tiles_per_seq) == 0,
                          bs=1, q=tm, carry=True))
        newp_ref[0] = u[tm - keep:]

    @pl.when(i >= n_prompt_tiles)
    def _():
        finish(_conv_rows(u, halo_s, prev_ref, cw_ref, first=i >= n_prompt_tiles,
                          bs=bs_sample, q=q_sample, carry=False))
        news_ref[...] = u.reshape(bs_sample, q_sample, width)[:, q_sample - keep:, :]


def _sconv(hs, hm, w_t, cw, prev_s, *, n_prompt, seq, n_seq_p, n_seq_s, q_sample,
           tm=1024, width=256):
    rows = hs.shape[0]
    keep = SC_CONV_W - 1
    assert seq % tm == 0 and (rows - n_prompt) == tm and tm == n_seq_s * q_sample
    tiles_per_seq = seq // tm
    n_pt = n_prompt // tm
    kern = functools.partial(_sconv_kernel, tm=tm, width=width, n_prompt_tiles=n_pt,
                             tiles_per_seq=tiles_per_seq, bs_sample=n_seq_s, q_sample=q_sample)

    def w_rows(k):
        base = (W_SC + k * D_MODEL) // N_HEADS
        return lambda cbk, i: ((base + cbk * (width // N_HEADS)) * N_HEADS, 0)

    w_specs = [pl.BlockSpec((pl.Element(width), pl.Element(D_MODEL)), w_rows(k)) for k in range(4)]
    return pl.pallas_call(
        kern,
        grid=(D_MODEL // width, rows // tm),
        in_specs=[
            pl.BlockSpec((tm, D_MODEL), lambda cbk, i: (i, 0)),
            pl.BlockSpec((META, D_MODEL), lambda cbk, i: (0, 0)),
            *w_specs,
            pl.BlockSpec((SC_CONV_W, width), lambda cbk, i: (0, cbk)),
            pl.BlockSpec((n_seq_s, keep, width), lambda cbk, i: (0, 0, cbk)),
        ],
        out_specs=[
            pl.BlockSpec((tm, width), lambda cbk, i: (i, cbk)),
            pl.BlockSpec((1, keep, width),
                         lambda cbk, i: (jnp.minimum(i, n_pt - 1) // tiles_per_seq, 0, cbk)),
            pl.BlockSpec((n_seq_s, keep, width), lambda cbk, i: (0, 0, cbk)),
        ],
        out_shape=[
            jax.ShapeDtypeStruct((rows, D_MODEL), BF16),
            jax.ShapeDtypeStruct((n_seq_p, keep, D_MODEL), F32),
            jax.ShapeDtypeStruct((n_seq_s, keep, D_MODEL), F32),
        ],
        scratch_shapes=[
            pltpu.VMEM((D_MODEL, 4 * width), BF16),
            pltpu.VMEM((tm + META, D_MODEL), BF16),
            pltpu.VMEM((1, SUBLANES, width), F32),
            pltpu.VMEM((n_seq_s, SUBLANES, width), F32),
            pltpu.VMEM((1, keep, width), F32),
        ],
        compiler_params=pltpu.CompilerParams(
            dimension_semantics=("arbitrary", "arbitrary"),
            vmem_limit_bytes=VMEM_LIMIT),
        name="sconv",
    )(hs, hm, w_t, w_t, w_t, w_t, cw, prev_s)


def _merge_kernel(yn_ref, v_ref, ga_ref, gb_ref, wa_ref, wb_ref, o_ref, wa_scr, wb_scr):
    @pl.when(pl.program_id(1) == 0)
    def _():
        wa_scr[...] = wa_ref[...].astype(BF16)
        wb_scr[...] = wb_ref[...].astype(BF16)

    ya = jnp.dot(yn_ref[...], wa_scr[...], preferred_element_type=F32)
    yb = jnp.dot(v_ref[...], wb_scr[...], preferred_element_type=F32)
    o_ref[...] = (jax.nn.sigmoid(ga_ref[...]) * ya + jax.nn.sigmoid(gb_ref[...]) * yb).astype(BF16)


def _merge(yn, v, gates, wa, wb, *, tm=256, tn=1024):
    rows = yn.shape[0]
    once = pl.Buffered(1)
    return pl.pallas_call(
        _merge_kernel,
        grid=(D_MODEL // tn, rows // tm),
        in_specs=[
            pl.BlockSpec((tm, D_INNER), lambda j, i: (i, 0)),
            pl.BlockSpec((tm, D_MODEL), lambda j, i: (i, 0)),
            pl.BlockSpec((tm, tn), lambda j, i: (i, j)),
            pl.BlockSpec((tm, tn), lambda j, i: (i, D_MODEL // tn + j)),
            pl.BlockSpec((D_INNER, tn), lambda j, i: (0, j), pipeline_mode=once),
            pl.BlockSpec((D_MODEL, tn), lambda j, i: (0, j), pipeline_mode=once),
        ],
        out_specs=pl.BlockSpec((tm, tn), lambda j, i: (i, j)),
        out_shape=jax.ShapeDtypeStruct((rows, D_MODEL), BF16),
        scratch_shapes=[pltpu.VMEM((D_INNER, tn), BF16), pltpu.VMEM((D_MODEL, tn), BF16)],
        compiler_params=pltpu.CompilerParams(
            dimension_semantics=("arbitrary", "arbitrary"),
            vmem_limit_bytes=VMEM_LIMIT),
        name="merge",
    )(yn, v, gates, gates, wa, wb)


def _outproj_kernel(m_ref, xp_ref, xs_ref, wo_ref, fw_ref, op_ref, os_ref, *, n_prompt_tiles):
    i = pl.program_id(0)
    d = jnp.dot(m_ref[...], wo_ref[...], preferred_element_type=F32)

    def finish(x_ref, o_ref):
        y = x_ref[...] + d
        ms = jnp.mean(y * y, axis=-1, keepdims=True)
        o_ref[...] = y * lax.rsqrt(ms + EPS) * fw_ref[...]

    @pl.when(i < n_prompt_tiles)
    def _():
        finish(xp_ref, op_ref)

    @pl.when(i >= n_prompt_tiles)
    def _():
        finish(xs_ref, os_ref)


def _outproj(m, xp, xs, wo, fw, *, tm=256):
    n_p, n_s = xp.shape[0] // tm, xs.shape[0] // tm
    return pl.pallas_call(
        functools.partial(_outproj_kernel, n_prompt_tiles=n_p),
        grid=(n_p + n_s,),
        in_specs=[
            pl.BlockSpec((tm, D_MODEL), lambda i: (i, 0)),
            pl.BlockSpec((tm, D_MODEL), lambda i: (jnp.minimum(i, n_p - 1), 0)),
            pl.BlockSpec((tm, D_MODEL), lambda i: (jnp.maximum(i - n_p, 0), 0)),
            pl.BlockSpec((D_MODEL, D_MODEL), lambda i: (0, 0), pipeline_mode=pl.Buffered(1)),
            pl.BlockSpec((1, D_MODEL), lambda i: (0, 0)),
        ],
        out_specs=[
            pl.BlockSpec((tm, D_MODEL), lambda i: (jnp.minimum(i, n_p - 1), 0)),
            pl.BlockSpec((tm, D_MODEL), lambda i: (jnp.maximum(i - n_p, 0), 0)),
        ],
        out_shape=[
            jax.ShapeDtypeStruct(xp.shape, F32),
            jax.ShapeDtypeStruct(xs.shape, F32),
        ],
        compiler_params=pltpu.CompilerParams(
            dimension_semantics=("arbitrary",),
            vmem_limit_bytes=VMEM_LIMIT),
        name="outproj",
    )(m, xp, xs, wo, fw)


def kernel(x_prompt, x_sample, state_ssd_conv, state_ssm, state_sconv, meta_tokens, norm_w,
           w_in, ssd_conv_w, ssd_conv_b, dt_bias, a_log, d_skip, ssd_norm_w, w_ssd_out,
           sconv_w, w_sconv_out, w_o, final_norm_w):
    bp, seq = x_prompt.shape[0], x_prompt.shape[1]
    bd, dec_seq = x_sample.shape[0], x_sample.shape[1]

    w_t = jnp.transpose(w_in[0])
    nw = norm_w[0].reshape(1, D_MODEL)
    fw = final_norm_w.reshape(1, D_MODEL)
    conv_w = ssd_conv_w[0]
    conv_b = ssd_conv_b[0].reshape(1, CONV_DIM)
    dtb = jnp.pad(dt_bias[0], (0, LANES - N_HEADS)).reshape(1, LANES)
    alog = jnp.pad(a_log[0], (0, LANES - N_HEADS)).reshape(1, LANES)
    dsk = jnp.repeat(d_skip[0], HEAD_DIM).reshape(1, D_INNER)
    gnw = ssd_norm_w[0].reshape(1, D_INNER)
    scw = sconv_w[0]

    xp = x_prompt.reshape(bp * seq, D_MODEL)
    xs = x_sample.reshape(bd * dec_seq, D_MODEL)
    n_p, n_s = bp * seq, bd * dec_seq
    streams = dict(n_prompt=n_p, seq=seq, n_seq_p=bp, n_seq_s=bd, q_sample=dec_seq)

    hs, hm = _norm(xp, xs, meta_tokens, nw)
    sz = _proj(hs, w_t, w_row0=0, ncols=D_INNER, silu=True)
    gates = _proj(hs, w_t, w_row0=W_GATE, ncols=2 * D_MODEL, silu=False)
    xbc, xbc_m, conv_p, conv_s = _xbc(hs, hm, w_t, conv_w, conv_b, state_ssd_conv[0], **streams)
    v_all, sc_p, sc_s = _sconv(hs, hm, w_t, scw, state_sconv[0], **streams)
    headscal = functools.partial(_headscal, w_t=w_t, dt_bias=dtb, a_log=alog)
    ssd = functools.partial(_ssd, d_skip_x=dsk, norm_w=gnw)

    at_m, dtt_m = headscal(hm, q=ROW_TILE, valid=META, tile0=0, ntiles=1)
    _, ssm_m = ssd(jnp.zeros((ROW_TILE, D_INNER), F32), xbc_m, at_m, dtt_m,
                   state0=jnp.zeros((1, N_GROUPS, GROUP_W, N_STATE), F32),
                   n_seq=1, bs=1, q=ROW_TILE, nc=1, gps=2, shared_init=False)

    at_p, dtt_p = headscal(hs, q=ROW_TILE, valid=ROW_TILE, tile0=0, ntiles=n_p // ROW_TILE)
    yn, ssm_p = ssd(sz, xbc, at_p, dtt_p, state0=ssm_m, n_seq=bp, bs=1, q=ROW_TILE,
                    nc=seq // ROW_TILE, gps=8, shared_init=True, out_rows=n_p + n_s)

    sbs = ROW_TILE // dec_seq
    at_s, dtt_s = headscal(hs, q=dec_seq, valid=ROW_TILE, tile0=n_p // ROW_TILE,
                           ntiles=n_s // ROW_TILE)
    yn, ssm_s = ssd(sz, xbc, at_s, dtt_s,
                    state0=state_ssm[0].reshape(bd, N_GROUPS, GROUP_W, N_STATE),
                    n_seq=bd, bs=sbs, q=dec_seq, nc=1, gps=2, shared_init=False,
                    tile0=n_p // ROW_TILE, out_rows=n_p + n_s, yn_into=yn)

    merged = _merge(yn, v_all, gates, w_ssd_out[0], w_sconv_out[0])
    y_p, y_s = _outproj(merged, xp, xs, w_o[0].astype(BF16), fw)

    return (y_p.reshape(bp, seq, D_MODEL),
            y_s.reshape(bd, dec_seq, D_MODEL),
            conv_p[None],
            ssm_p.reshape(1, bp, N_HEADS, HEAD_DIM, N_STATE),
            sc_p[None],
            conv_s[None],
            ssm_s.reshape(1, bd, N_HEADS, HEAD_DIM, N_STATE),
            sc_s[None])
```

```python
import functools

import jax
import jax.numpy as jnp
from jax import lax
from jax.experimental import pallas as pl
from jax.experimental.pallas import tpu as pltpu

F32 = jnp.float32
BF16 = jnp.bfloat16

D_MODEL = 2048
D_INNER = 4096
N_HEADS = 64
HEAD_DIM = 64
N_STATE = 128
N_GROUPS = 8
GROUP_W = D_INNER // N_GROUPS
HEADS_PER_GROUP = N_HEADS // N_GROUPS
CONV_DIM = D_INNER + 2 * N_GROUPS * N_STATE
SSD_CONV_W = 4
SC_CONV_W = 3
META = 16
EPS = 1e-6

LANES = 128
SUBLANES = 8
ROW_TILE = 128
W_DT = D_INNER + CONV_DIM
W_SC = W_DT + N_HEADS
W_GATE = W_SC + 4 * D_MODEL

VMEM_LIMIT = 52 * 1024 * 1024


def _silu(x):
    h = 0.5 * x
    return h * (1.0 + jnp.tanh(h))


def _rms_bf16(x, w):
    ms = jnp.mean(x * x, axis=-1, keepdims=True)
    return (x * lax.rsqrt(ms + EPS) * w).astype(BF16)


def _norm_kernel(xp_ref, xs_ref, xm_ref, nw_ref, hs_ref, hm_ref, *, n_prompt):
    i = pl.program_id(0)

    @pl.when(i < n_prompt)
    def _():
        hs_ref[...] = _rms_bf16(xp_ref[...], nw_ref[...])

    @pl.when(i >= n_prompt)
    def _():
        hs_ref[...] = _rms_bf16(xs_ref[...], nw_ref[...])

    @pl.when(i == 0)
    def _():
        hm_ref[:META, :] = _rms_bf16(xm_ref[...], nw_ref[...])
        hm_ref[META:, :] = jnp.zeros((ROW_TILE - META, D_MODEL), BF16)


def _norm(xp, xs, xm, norm_w, *, tm=512):
    n_p, n_s = xp.shape[0] // tm, xs.shape[0] // tm
    kern = functools.partial(_norm_kernel, n_prompt=n_p)
    return pl.pallas_call(
        kern,
        grid=(n_p + n_s,),
        in_specs=[
            pl.BlockSpec((tm, D_MODEL), lambda i: (jnp.minimum(i, n_p - 1), 0)),
            pl.BlockSpec((tm, D_MODEL), lambda i: (jnp.maximum(i - n_p, 0), 0)),
            pl.BlockSpec((META, D_MODEL), lambda i: (0, 0)),
            pl.BlockSpec((1, D_MODEL), lambda i: (0, 0)),
        ],
        out_specs=[
            pl.BlockSpec((tm, D_MODEL), lambda i: (i, 0)),
            pl.BlockSpec((ROW_TILE, D_MODEL), lambda i: (0, 0)),
        ],
        out_shape=[
            jax.ShapeDtypeStruct((xp.shape[0] + xs.shape[0], D_MODEL), BF16),
            jax.ShapeDtypeStruct((ROW_TILE, D_MODEL), BF16),
        ],
        compiler_params=pltpu.CompilerParams(
            dimension_semantics=("arbitrary",), vmem_limit_bytes=VMEM_LIMIT),
        name="norm",
    )(xp, xs, xm, norm_w)


def _proj_kernel(hs_ref, wt_ref, o_ref, wb_ref, *, silu):
    @pl.when(pl.program_id(1) == 0)
    def _():
        wb_ref[...] = wt_ref[...].T.astype(BF16)

    r = jnp.dot(hs_ref[...], wb_ref[...], preferred_element_type=F32)
    o_ref[...] = _silu(r) if silu else r


def _proj(hs, w_t, *, w_row0, ncols, silu, tm=1024, tn=1024):
    rows = hs.shape[0]
    assert w_row0 % N_HEADS == 0 and ncols % tn == 0 and tn % N_HEADS == 0

    def w_rows(j, i):
        return ((w_row0 // N_HEADS + j * (tn // N_HEADS)) * N_HEADS, 0)

    return pl.pallas_call(
        functools.partial(_proj_kernel, silu=silu),
        grid=(ncols // tn, rows // tm),
        in_specs=[
            pl.BlockSpec((tm, D_MODEL), lambda j, i: (i, 0)),
            pl.BlockSpec((pl.Element(tn), pl.Element(D_MODEL)), w_rows),
        ],
        out_specs=pl.BlockSpec((tm, tn), lambda j, i: (i, j)),
        out_shape=jax.ShapeDtypeStruct((rows, ncols), F32),
        scratch_shapes=[pltpu.VMEM((D_MODEL, tn), BF16)],
        compiler_params=pltpu.CompilerParams(
            dimension_semantics=("arbitrary", "arbitrary"),
            vmem_limit_bytes=VMEM_LIMIT),
        name="proj",
    )(hs, w_t)


def _conv_rows(x, halo_ref, prev_ref, w_ref, *, first, bs, q, carry):
    taps = w_ref.shape[0]
    rt, width = x.shape

    @pl.when(first)
    def _():
        halo_ref[:, SUBLANES - (taps - 1):, :] = prev_ref[...]

    prev = halo_ref[...]
    acc = None
    if bs == 1:
        row = lax.broadcasted_iota(jnp.int32, (SUBLANES, width), 0)
        for s in range(taps - 1, 0, -1):
            rolled = pltpu.roll(x, s, 0)
            head = jnp.where(row < s, pltpu.roll(prev[0], s, 0), rolled[:SUBLANES])
            term = jnp.concatenate([head, rolled[SUBLANES:]], axis=0) * w_ref[taps - 1 - s:taps - s, :]
            acc = term if acc is None else acc + term
        acc = acc + x * w_ref[taps - 1:taps, :]
        if carry:
            halo_ref[0] = x[rt - SUBLANES:, :]
        return acc
    assert q == SUBLANES and not carry
    x3 = x.reshape(bs, q, width)
    row = lax.broadcasted_iota(jnp.int32, x3.shape, 1)
    for s in range(taps - 1, 0, -1):
        shifted = jnp.where(row < s, pltpu.roll(prev, s, 1), pltpu.roll(x3, s, 1))
        term = shifted * w_ref[taps - 1 - s:taps - s, :]
        acc = term if acc is None else acc + term
    acc = acc + x3 * w_ref[taps - 1:taps, :]
    return acc.reshape(rt, width)


def _seg_cumsum(a, q):
    pos = lax.broadcasted_iota(jnp.int32, a.shape, 0) & (q - 1)
    s = 1
    while s < q:
        shifted = pltpu.roll(a, s, 0)
        a = a + jnp.where(pos >= s, shifted, 0.0)
        s *= 2
    return a


def _headscal_kernel(hs_ref, wdt_ref, dtb_ref, alog_ref, at_ref, bt_ref, *, q, valid, tps):
    dtr = lax.dot_general(hs_ref[...], wdt_ref[...].astype(BF16), (((1,), (1,)), ((), ())),
                          preferred_element_type=F32)
    dtv = jax.nn.softplus(dtr + dtb_ref[...])
    if valid < ROW_TILE:
        rows = lax.broadcasted_iota(jnp.int32, dtv.shape, 0)
        dtv = jnp.where(rows < valid, dtv, 0.0)
    acum = _seg_cumsum(dtv * (-jnp.exp(alog_ref[...])), q)
    a_minus_logdt = acum - jnp.log(dtv)
    for t in range(tps):
        at_ref[t] = acum[t * ROW_TILE:(t + 1) * ROW_TILE].T
        bt_ref[t] = a_minus_logdt[t * ROW_TILE:(t + 1) * ROW_TILE].T


def _headscal(hs, w_t, dt_bias, a_log, *, q, valid, tile0, ntiles):
    tps = min(ntiles, 8)
    assert ntiles % tps == 0 and tile0 % tps == 0 and q <= ROW_TILE
    kern = functools.partial(_headscal_kernel, q=q, valid=valid, tps=tps)
    shape = jax.ShapeDtypeStruct((ntiles, LANES, ROW_TILE), F32)
    dt_blk = W_DT // LANES
    return pl.pallas_call(
        kern,
        grid=(ntiles // tps,),
        in_specs=[
            pl.BlockSpec((tps * ROW_TILE, D_MODEL), lambda t: (tile0 // tps + t, 0)),
            pl.BlockSpec((LANES, D_MODEL), lambda t: (dt_blk, 0)),
            pl.BlockSpec((1, LANES), lambda t: (0, 0)),
            pl.BlockSpec((1, LANES), lambda t: (0, 0)),
        ],
        out_specs=[
            pl.BlockSpec((tps, LANES, ROW_TILE), lambda t: (t, 0, 0)),
            pl.BlockSpec((tps, LANES, ROW_TILE), lambda t: (t, 0, 0)),
        ],
        out_shape=[shape, shape],
        compiler_params=pltpu.CompilerParams(dimension_semantics=("arbitrary",)),
        name="headscal",
    )(hs, w_t, dt_bias, a_log)


def _xbc_kernel(hs_ref, hm_ref, wt_ref, cw_ref, cb_ref, prev_ref,
                o_ref, om_ref, cnp_ref, cns_ref,
                wb_scr, halo_p, halo_s, halo_m, meta_prev, zero_prev,
                *, tm, n_prompt_tiles, tiles_per_seq, bs_sample, q_sample):
    i = pl.program_id(1)
    keep = SSD_CONV_W - 1
    tn = o_ref.shape[1]

    @pl.when(i == 0)
    def _():
        wb_scr[...] = wt_ref[...].T.astype(BF16)

    raw = jnp.dot(hs_ref[...], wb_scr[...], preferred_element_type=F32)

    def activate(conv):
        return _silu(conv + cb_ref[...])

    @pl.when(i == 0)
    def _():
        raw_m = jnp.dot(hm_ref[...], wb_scr[...], preferred_element_type=F32)
        meta_prev[0] = raw_m[META - keep:]
        zero_prev[...] = jnp.zeros(zero_prev.shape, F32)
        conv_m = _conv_rows(raw_m, halo_m, zero_prev, cw_ref, first=i == 0, bs=1, q=META,
                            carry=False)
        om_ref[:META, :] = activate(conv_m)
        om_ref[META:, :] = jnp.zeros((ROW_TILE - META, tn), F32)

    @pl.when(i < n_prompt_tiles)
    def _():
        o_ref[...] = activate(_conv_rows(raw, halo_p, meta_prev, cw_ref,
                                         first=(i % tiles_per_seq) == 0, bs=1, q=tm, carry=True))
        cnp_ref[0] = raw[tm - keep:]

    @pl.when(i >= n_prompt_tiles)
    def _():
        o_ref[...] = activate(_conv_rows(raw, halo_s, prev_ref, cw_ref, first=i >= n_prompt_tiles,
                                         bs=bs_sample, q=q_sample, carry=False))
        cns_ref[...] = raw.reshape(bs_sample, q_sample, tn)[:, q_sample - keep:, :]


def _xbc(hs, hm, w_t, cw, cb, prev_s, *, n_prompt, seq, n_seq_p, n_seq_s, q_sample,
         tm=1024, tn=512):
    rows = hs.shape[0]
    keep = SSD_CONV_W - 1
    assert seq % tm == 0 and (rows - n_prompt) == tm and tm == n_seq_s * q_sample
    assert D_INNER % tn == 0 and CONV_DIM % tn == 0
    tiles_per_seq = seq // tm
    n_pt = n_prompt // tm
    kern = functools.partial(_xbc_kernel, tm=tm, n_prompt_tiles=n_pt, tiles_per_seq=tiles_per_seq,
                             bs_sample=n_seq_s, q_sample=q_sample)
    return pl.pallas_call(
        kern,
        grid=(CONV_DIM // tn, rows // tm),
        in_specs=[
            pl.BlockSpec((tm, D_MODEL), lambda j, i: (i, 0)),
            pl.BlockSpec((META, D_MODEL), lambda j, i: (0, 0)),
            pl.BlockSpec((tn, D_MODEL), lambda j, i: (D_INNER // tn + j, 0)),
            pl.BlockSpec((SSD_CONV_W, tn), lambda j, i: (0, j)),
            pl.BlockSpec((1, tn), lambda j, i: (0, j)),
            pl.BlockSpec((n_seq_s, keep, tn), lambda j, i: (0, 0, j)),
        ],
        out_specs=[
            pl.BlockSpec((tm, tn), lambda j, i: (i, j)),
            pl.BlockSpec((ROW_TILE, tn), lambda j, i: (0, j)),
            pl.BlockSpec((1, keep, tn),
                         lambda j, i: (jnp.minimum(i, n_pt - 1) // tiles_per_seq, 0, j)),
            pl.BlockSpec((n_seq_s, keep, tn), lambda j, i: (0, 0, j)),
        ],
        out_shape=[
            jax.ShapeDtypeStruct((rows, CONV_DIM), F32),
            jax.ShapeDtypeStruct((ROW_TILE, CONV_DIM), F32),
            jax.ShapeDtypeStruct((n_seq_p, keep, CONV_DIM), F32),
            jax.ShapeDtypeStruct((n_seq_s, keep, CONV_DIM), F32),
        ],
        scratch_shapes=[
            pltpu.VMEM((D_MODEL, tn), BF16),
            pltpu.VMEM((1, SUBLANES, tn), F32),
            pltpu.VMEM((n_seq_s, SUBLANES, tn), F32),
            pltpu.VMEM((1, SUBLANES, tn), F32),
            pltpu.VMEM((1, keep, tn), F32),
            pltpu.VMEM((1, keep, tn), F32),
        ],
        compiler_params=pltpu.CompilerParams(
            dimension_semantics=("arbitrary", "arbitrary"),
            vmem_limit_bytes=VMEM_LIMIT),
        name="xbc",
    )(hs, hm, w_t, cw, cb, prev_s)


def _ssd_kernel(sz_ref, x_ref, b_ref, c_ref, at_ref, bt_ref, dsk_ref, nw_ref, s0_ref,
                *rest, bs, q, nc, gps):
    yn_ref, sout_ref, st_ref = rest[-3:]
    rt = bs * q
    nh = gps * HEADS_PER_GROUP
    c = pl.program_id(2)
    carry = nc > 1

    if carry:
        @pl.when(c == 0)
        def _():
            st_ref[...] = s0_ref[...]

    xc = x_ref[...]
    bcb = b_ref[...].astype(BF16)
    ccb = c_ref[...].astype(BF16)

    a_t = at_ref[0]
    b_t = bt_ref[0]
    cols = jnp.concatenate([a_t, jnp.zeros((LANES - nh, rt), F32)], axis=0).T

    pos = lax.broadcasted_iota(jnp.int32, (nh, rt), 1) & (q - 1)
    a_end = a_t
    s = 1
    while s < q:
        a_end = jnp.where(pos + s < q, pltpu.roll(a_end, rt - s, 1), a_end)
        s *= 2
    to_end = jnp.exp(a_end - b_t)

    nblk = rt // SUBLANES
    ri = lax.broadcasted_iota(jnp.int32, (nblk, SUBLANES, rt), 0) * SUBLANES + \
        lax.broadcasted_iota(jnp.int32, (nblk, SUBLANES, rt), 1)
    ci = lax.broadcasted_iota(jnp.int32, (nblk, SUBLANES, rt), 2)
    mask = (ri >= ci) & ((ri // q) == (ci // q))
    low = lax.broadcasted_iota(jnp.int32, (rt, LANES), 1) < HEAD_DIM
    seq_of_row = lax.broadcasted_iota(jnp.int32, (rt, N_STATE), 0) // q

    for k in range(gps):
        xg = xc[:, k * GROUP_W:(k + 1) * GROUP_W]
        bg = bcb[:, k * N_STATE:(k + 1) * N_STATE]
        cg = ccb[:, k * N_STATE:(k + 1) * N_STATE]
        cb = lax.dot_general(cg, bg, (((1,), (1,)), ((), ())), preferred_element_type=F32)
        cb3 = cb.reshape(nblk, SUBLANES, rt)
        xt = xg.T

        ydiag, ea, xw, a_cols = [], [], [], []
        for pr in range(HEADS_PER_GROUP // 2):
            wts, ab = [], []
            for hh in range(2):
                h = k * HEADS_PER_GROUP + 2 * pr + hh
                a_col = jnp.broadcast_to(cols[:, h:h + 1], (rt, LANES))
                b_row = jnp.broadcast_to(b_t[h:h + 1, :], (SUBLANES, rt))
                seg = jnp.where(mask, a_col.reshape(nblk, SUBLANES, rt) - b_row[None], -jnp.inf)
                wts.append((cb3 * jnp.exp(seg)).reshape(rt, rt).astype(BF16))
                ab.append(a_col)
                rows = slice((2 * pr + hh) * HEAD_DIM, (2 * pr + hh + 1) * HEAD_DIM)
                xw.append(xt[rows] * to_end[h:h + 1, :])
            a_cols += ab
            ea.append(jnp.exp(jnp.where(low, ab[0], ab[1])))
            xb = xg[:, pr * LANES:(pr + 1) * LANES].astype(BF16)
            zero = jnp.zeros_like(xb)
            rhs = jnp.concatenate([jnp.where(low, xb, zero), jnp.where(low, zero, xb)], axis=0)
            ydiag.append(jnp.dot(jnp.concatenate(wts, axis=1), rhs, preferred_element_type=F32))
        ydiag = jnp.concatenate(ydiag, axis=1)
        ea = jnp.concatenate(ea, axis=1)
        xwt = jnp.concatenate(xw, axis=0).astype(BF16)

        yoff = []
        for s in range(bs):
            st = st_ref[s, k] if carry else s0_ref[s, k]
            yoff.append(lax.dot_general(cg[s * q:(s + 1) * q, :], st.astype(BF16),
                                        (((1,), (1,)), ((), ())), preferred_element_type=F32))
            bsel = bg if bs == 1 else jnp.where(seq_of_row == s, bg, jnp.zeros_like(bg))
            upd = jnp.dot(xwt, bsel, preferred_element_type=F32)
            last = (s + 1) * q - 1
            dec = jnp.concatenate(
                [jnp.broadcast_to(jnp.exp(a_cols[h][last:last + 1, :]), (HEAD_DIM, N_STATE))
                 for h in range(HEADS_PER_GROUP)], axis=0)
            new = st * dec + upd
            if carry:
                st_ref[s, k] = new

                @pl.when(c == nc - 1)
                def _():
                    sout_ref[s, k] = new
            else:
                sout_ref[s, k] = new
        yoff = yoff[0] if bs == 1 else jnp.concatenate(yoff, axis=0)

        lanes = slice(k * GROUP_W, (k + 1) * GROUP_W)
        y = ydiag + yoff * ea + dsk_ref[:, lanes] * xg
        gz = y * sz_ref[:, lanes]
        ms = jnp.mean(gz * gz, axis=-1, keepdims=True)
        yn_ref[:, lanes] = (gz * lax.rsqrt(ms + EPS) * nw_ref[:, lanes]).astype(BF16)


def _ssd(sz, xbc, a_t, dt_t, d_skip_x, norm_w, state0,
         *, n_seq, bs, q, nc, gps, shared_init, tile0=0, out_rows=None, yn_into=None):
    rt = bs * q
    assert rt == ROW_TILE and N_GROUPS % gps == 0
    rows = n_seq * q * nc if out_rows is None else out_rows
    otile0 = 0 if out_rows is None else tile0
    nsb = n_seq // bs
    gw, gn, nh = gps * GROUP_W, gps * N_STATE, gps * HEADS_PER_GROUP
    kern = functools.partial(_ssd_kernel, bs=bs, q=q, nc=nc, gps=gps)
    bb, bc_ = D_INNER // gn, (D_INNER + N_GROUPS * N_STATE) // gn
    sidx = (lambda s: 0) if shared_init else (lambda s: s)

    def tile(s, c):
        return s * nc + c

    in_specs = [
        pl.BlockSpec((rt, gw), lambda s, g, c: (tile0 + tile(s, c), g)),
        pl.BlockSpec((rt, gw), lambda s, g, c: (tile0 + tile(s, c), g)),
        pl.BlockSpec((rt, gn), lambda s, g, c: (tile0 + tile(s, c), bb + g)),
        pl.BlockSpec((rt, gn), lambda s, g, c: (tile0 + tile(s, c), bc_ + g)),
        pl.BlockSpec((1, nh, rt), lambda s, g, c: (tile(s, c), g, 0)),
        pl.BlockSpec((1, nh, rt), lambda s, g, c: (tile(s, c), g, 0)),
        pl.BlockSpec((1, gw), lambda s, g, c: (0, g)),
        pl.BlockSpec((1, gw), lambda s, g, c: (0, g)),
        pl.BlockSpec((bs, gps, GROUP_W, N_STATE), lambda s, g, c: (sidx(s), g, 0, 0)),
    ]
    out_specs = [
        pl.BlockSpec((rt, gw), lambda s, g, c: (otile0 + tile(s, c), g)),
        pl.BlockSpec((bs, gps, GROUP_W, N_STATE), lambda s, g, c: (s, g, 0, 0)),
    ]
    operands = [sz, xbc, xbc, xbc, a_t, dt_t, d_skip_x, norm_w, state0]
    aliases = {}
    if yn_into is not None:
        assert yn_into.shape == (rows, D_INNER)
        aliases = {len(operands): 0}
        in_specs.append(pl.BlockSpec(memory_space=pl.ANY))
        operands.append(yn_into)
    st_shape = (bs, gps, GROUP_W, N_STATE) if nc > 1 else (1, 1, SUBLANES, N_STATE)
    return pl.pallas_call(
        kern,
        grid=(nsb, N_GROUPS // gps, nc),
        in_specs=in_specs,
        out_specs=out_specs,
        out_shape=[
            jax.ShapeDtypeStruct((rows, D_INNER), BF16),
            jax.ShapeDtypeStruct((n_seq, N_GROUPS, GROUP_W, N_STATE), F32),
        ],
        scratch_shapes=[pltpu.VMEM(st_shape, F32)],
        input_output_aliases=aliases,
        compiler_params=pltpu.CompilerParams(
            dimension_semantics=("arbitrary", "arbitrary", "arbitrary"),
            vmem_limit_bytes=VMEM_LIMIT),
        name="ssd",
    )(*operands)


def _sconv_kernel(hs_ref, hm_ref, wb_ref, wc_ref, wh_ref, wz_ref, cw_ref, prev_ref,
                  v_ref, newp_ref, news_ref,
                  w_scr, halo_p, halo_s, meta_u,
                  *, tm, width, n_prompt_tiles, tiles_per_seq, bs_sample, q_sample):
    i = pl.program_id(1)
    keep = SC_CONV_W - 1

    @pl.when(i == 0)
    def _():
        for k, w_ref in enumerate((wb_ref, wc_ref, wh_ref, wz_ref)):
            w_scr[:, k * width:(k + 1) * width] = w_ref[...].T.astype(BF16)
        rm = jnp.dot(hm_ref[...], w_scr[:, width:3 * width], preferred_element_type=F32)
        meta_u[0] = (rm[:, :width] * rm[:, width:])[META - keep:]

    r = jnp.dot(hs_ref[...], w_scr[...], preferred_element_type=F32)
    u = r[:, width:2 * width] * r[:, 2 * width:3 * width]

    def finish(uc):
        v_ref[...] = (r[:, :width] * uc * _silu(r[:, 3 * width:])).astype(BF16)

    @pl.when(i < n_prompt_tiles)
    def _():
        finish(_conv_rows(u, halo_p, meta_u, cw_ref, first=(i % tiles_per_seq) == 0,
                          bs=1, q=tm, carry=True))
        newp_ref[0] = u[tm - keep:]

    @pl.when(i >= n_prompt_tiles)
    def _():
        finish(_conv_rows(u, halo_s, prev_ref, cw_ref, first=i >= n_prompt_tiles,
                          bs=bs_sample, q=q_sample, carry=False))
        news_ref[...] = u.reshape(bs_sample, q_sample, width)[:, q_sample - keep:, :]


def _sconv(hs, hm, w_t, cw, prev_s, *, n_prompt, seq, n_seq_p, n_seq_s, q_sample,
           tm=1024, width=256):
    rows = hs.shape[0]
    keep = SC_CONV_W - 1
    assert seq % tm == 0 and (rows - n_prompt) == tm and tm == n_seq_s * q_sample
    tiles_per_seq = seq // tm
    n_pt = n_prompt // tm
    kern = functools.partial(_sconv_kernel, tm=tm, width=width, n_prompt_tiles=n_pt,
                             tiles_per_seq=tiles_per_seq, bs_sample=n_seq_s, q_sample=q_sample)

    def w_rows(k):
        base = (W_SC + k * D_MODEL) // N_HEADS
        return lambda cbk, i: ((base + cbk * (width // N_HEADS)) * N_HEADS, 0)

    w_specs = [pl.BlockSpec((pl.Element(width), pl.Element(D_MODEL)), w_rows(k)) for k in range(4)]
    return pl.pallas_call(
        kern,
        grid=(D_MODEL // width, rows // tm),
        in_specs=[
            pl.BlockSpec((tm, D_MODEL), lambda cbk, i: (i, 0)),
            pl.BlockSpec((META, D_MODEL), lambda cbk, i: (0, 0)),
            *w_specs,
            pl.BlockSpec((SC_CONV_W, width), lambda cbk, i: (0, cbk)),
            pl.BlockSpec((n_seq_s, keep, width), lambda cbk, i: (0, 0, cbk)),
        ],
        out_specs=[
            pl.BlockSpec((tm, width), lambda cbk, i: (i, cbk)),
            pl.BlockSpec((1, keep, width),
                         lambda cbk, i: (jnp.minimum(i, n_pt - 1) // tiles_per_seq, 0, cbk)),
            pl.BlockSpec((n_seq_s, keep, width), lambda cbk, i: (0, 0, cbk)),
        ],
        out_shape=[
            jax.ShapeDtypeStruct((rows, D_MODEL), BF16),
            jax.ShapeDtypeStruct((n_seq_p, keep, D_MODEL), F32),
            jax.ShapeDtypeStruct((n_seq_s, keep, D_MODEL), F32),
        ],
        scratch_shapes=[
            pltpu.VMEM((D_MODEL, 4 * width), BF16),
            pltpu.VMEM((1, SUBLANES, width), F32),
            pltpu.VMEM((n_seq_s, SUBLANES, width), F32),
            pltpu.VMEM((1, keep, width), F32),
        ],
        compiler_params=pltpu.CompilerParams(
            dimension_semantics=("arbitrary", "arbitrary"),
            vmem_limit_bytes=VMEM_LIMIT),
        name="sconv",
    )(hs, hm, w_t, w_t, w_t, w_t, cw, prev_s)


def _merge_kernel(yn_ref, v_ref, ga_ref, gb_ref, wa_ref, wb_ref, o_ref, wa_scr, wb_scr):
    @pl.when(pl.program_id(1) == 0)
    def _():
        wa_scr[...] = wa_ref[...].astype(BF16)
        wb_scr[...] = wb_ref[...].astype(BF16)

    ya = jnp.dot(yn_ref[...], wa_scr[...], preferred_element_type=F32)
    yb = jnp.dot(v_ref[...], wb_scr[...], preferred_element_type=F32)
    o_ref[...] = (jax.nn.sigmoid(ga_ref[...]) * ya + jax.nn.sigmoid(gb_ref[...]) * yb).astype(BF16)


def _merge(yn, v, gates, wa, wb, *, tm=512, tn=512):
    rows = yn.shape[0]
    return pl.pallas_call(
        _merge_kernel,
        grid=(D_MODEL // tn, rows // tm),
        in_specs=[
            pl.BlockSpec((tm, D_INNER), lambda j, i: (i, 0)),
            pl.BlockSpec((tm, D_MODEL), lambda j, i: (i, 0)),
            pl.BlockSpec((tm, tn), lambda j, i: (i, j)),
            pl.BlockSpec((tm, tn), lambda j, i: (i, D_MODEL // tn + j)),
            pl.BlockSpec((D_INNER, tn), lambda j, i: (0, j)),
            pl.BlockSpec((D_MODEL, tn), lambda j, i: (0, j)),
        ],
        out_specs=pl.BlockSpec((tm, tn), lambda j, i: (i, j)),
        out_shape=jax.ShapeDtypeStruct((rows, D_MODEL), BF16),
        scratch_shapes=[pltpu.VMEM((D_INNER, tn), BF16), pltpu.VMEM((D_MODEL, tn), BF16)],
        compiler_params=pltpu.CompilerParams(
            dimension_semantics=("arbitrary", "arbitrary"),
            vmem_limit_bytes=VMEM_LIMIT),
        name="merge",
    )(yn, v, gates, gates, wa, wb)


def _outproj_kernel(m_ref, xp_ref, xs_ref, wo_ref, fw_ref, op_ref, os_ref, *, n_prompt_tiles):
    i = pl.program_id(0)
    d = jnp.dot(m_ref[...], wo_ref[...], preferred_element_type=F32)

    def finish(x_ref, o_ref):
        y = x_ref[...] + d
        ms = jnp.mean(y * y, axis=-1, keepdims=True)
        o_ref[...] = y * lax.rsqrt(ms + EPS) * fw_ref[...]

    @pl.when(i < n_prompt_tiles)
    def _():
        finish(xp_ref, op_ref)

    @pl.when(i >= n_prompt_tiles)
    def _():
        finish(xs_ref, os_ref)


def _outproj(m, xp, xs, wo, fw, *, tm=256):
    n_p, n_s = xp.shape[0] // tm, xs.shape[0] // tm
    return pl.pallas_call(
        functools.partial(_outproj_kernel, n_prompt_tiles=n_p),
        grid=(n_p + n_s,),
        in_specs=[
            pl.BlockSpec((tm, D_MODEL), lambda i: (i, 0)),
            pl.BlockSpec((tm, D_MODEL), lambda i: (jnp.minimum(i, n_p - 1), 0)),
            pl.BlockSpec((tm, D_MODEL), lambda i: (jnp.maximum(i - n_p, 0), 0)),
            pl.BlockSpec((D_MODEL, D_MODEL), lambda i: (0, 0), pipeline_mode=pl.Buffered(1)),
            pl.BlockSpec((1, D_MODEL), lambda i: (0, 0)),
        ],
        out_specs=[
            pl.BlockSpec((tm, D_MODEL), lambda i: (jnp.minimum(i, n_p - 1), 0)),
            pl.BlockSpec((tm, D_MODEL), lambda i: (jnp.maximum(i - n_p, 0), 0)),
        ],
        out_shape=[
            jax.ShapeDtypeStruct(xp.shape, F32),
            jax.ShapeDtypeStruct(xs.shape, F32),
        ],
        compiler_params=pltpu.CompilerParams(
            dimension_semantics=("arbitrary",),
            vmem_limit_bytes=VMEM_LIMIT),
        name="outproj",
    )(m, xp, xs, wo, fw)


def kernel(x_prompt, x_sample, state_ssd_conv, state_ssm, state_sconv, meta_tokens, norm_w,
           w_in, ssd_conv_w, ssd_conv_b, dt_bias, a_log, d_skip, ssd_norm_w, w_ssd_out,
           sconv_w, w_sconv_out, w_o, final_norm_w):
    bp, seq = x_prompt.shape[0], x_prompt.shape[1]
    bd, dec_seq = x_sample.shape[0], x_sample.shape[1]

    w_t = jnp.transpose(w_in[0])
    nw = norm_w[0].reshape(1, D_MODEL)
    fw = final_norm_w.reshape(1, D_MODEL)
    conv_w = ssd_conv_w[0]
    conv_b = ssd_conv_b[0].reshape(1, CONV_DIM)
    dtb = jnp.pad(dt_bias[0], (0, LANES - N_HEADS)).reshape(1, LANES)
    alog = jnp.pad(a_log[0], (0, LANES - N_HEADS)).reshape(1, LANES)
    dsk = jnp.repeat(d_skip[0], HEAD_DIM).reshape(1, D_INNER)
    gnw = ssd_norm_w[0].reshape(1, D_INNER)
    scw = sconv_w[0]

    xp = x_prompt.reshape(bp * seq, D_MODEL)
    xs = x_sample.reshape(bd * dec_seq, D_MODEL)
    n_p, n_s = bp * seq, bd * dec_seq
    streams = dict(n_prompt=n_p, seq=seq, n_seq_p=bp, n_seq_s=bd, q_sample=dec_seq)

    hs, hm = _norm(xp, xs, meta_tokens, nw)
    sz = _proj(hs, w_t, w_row0=0, ncols=D_INNER, silu=True)
    gates = _proj(hs, w_t, w_row0=W_GATE, ncols=2 * D_MODEL, silu=False)
    xbc, xbc_m, conv_p, conv_s = _xbc(hs, hm, w_t, conv_w, conv_b, state_ssd_conv[0], **streams)
    v_all, sc_p, sc_s = _sconv(hs, hm, w_t, scw, state_sconv[0], **streams)
    headscal = functools.partial(_headscal, w_t=w_t, dt_bias=dtb, a_log=alog)
    ssd = functools.partial(_ssd, d_skip_x=dsk, norm_w=gnw)

    at_m, dtt_m = headscal(hm, q=ROW_TILE, valid=META, tile0=0, ntiles=1)
    _, ssm_m = ssd(jnp.zeros((ROW_TILE, D_INNER), F32), xbc_m, at_m, dtt_m,
                   state0=jnp.zeros((1, N_GROUPS, GROUP_W, N_STATE), F32),
                   n_seq=1, bs=1, q=ROW_TILE, nc=1, gps=2, shared_init=False)

    at_p, dtt_p = headscal(hs, q=ROW_TILE, valid=ROW_TILE, tile0=0, ntiles=n_p // ROW_TILE)
    yn, ssm_p = ssd(sz, xbc, at_p, dtt_p, state0=ssm_m, n_seq=bp, bs=1, q=ROW_TILE,
                    nc=seq // ROW_TILE, gps=8, shared_init=True, out_rows=n_p + n_s)

    sbs = ROW_TILE // dec_seq
    at_s, dtt_s = headscal(hs, q=dec_seq, valid=ROW_TILE, tile0=n_p // ROW_TILE,
                           ntiles=n_s // ROW_TILE)
    yn, ssm_s = ssd(sz, xbc, at_s, dtt_s,
                    state0=state_ssm[0].reshape(bd, N_GROUPS, GROUP_W, N_STATE),
                    n_seq=bd, bs=sbs, q=dec_seq, nc=1, gps=2, shared_init=False,
                    tile0=n_p // ROW_TILE, out_rows=n_p + n_s, yn_into=yn)

    merged = _merge(yn, v_all, gates, w_ssd_out[0], w_sconv_out[0])
    y_p, y_s = _outproj(merged, xp, xs, w_o[0].astype(BF16), fw)

    return (y_p.reshape(bp, seq, D_MODEL),
            y_s.reshape(bd, dec_seq, D_MODEL),
            conv_p[None],
            ssm_p.reshape(1, bp, N_HEADS, HEAD_DIM, N_STATE),
            sc_p[None],
            conv_s[None],
            ssm_s.reshape(1, bd, N_HEADS, HEAD_DIM, N_STATE),
            sc_s[None])
```

```python
import functools

import jax
import jax.numpy as jnp
from jax import lax
from jax.experimental import pallas as pl
from jax.experimental.pallas import tpu as pltpu

F32 = jnp.float32
BF16 = jnp.bfloat16

D_MODEL = 2048
D_INNER = 4096
N_HEADS = 64
HEAD_DIM = 64
N_STATE = 128
N_GROUPS = 8
GROUP_W = D_INNER // N_GROUPS
HEADS_PER_GROUP = N_HEADS // N_GROUPS
CONV_DIM = D_INNER + 2 * N_GROUPS * N_STATE
SSD_CONV_W = 4
SC_CONV_W = 3
META = 16
EPS = 1e-6

LANES = 128
SUBLANES = 8
ROW_TILE = 128
W_DT = D_INNER + CONV_DIM
W_SC = W_DT + N_HEADS
W_GATE = W_SC + 4 * D_MODEL

VMEM_LIMIT = 52 * 1024 * 1024


def _silu(x):
    h = 0.5 * x
    return h * (1.0 + jnp.tanh(h))


def _rms_bf16(x, w):
    ms = jnp.mean(x * x, axis=-1, keepdims=True)
    return (x * lax.rsqrt(ms + EPS) * w).astype(BF16)


def _norm_kernel(xp_ref, xs_ref, xm_ref, nw_ref, hs_ref, hm_ref, *, n_prompt):
    i = pl.program_id(0)

    @pl.when(i < n_prompt)
    def _():
        hs_ref[...] = _rms_bf16(xp_ref[...], nw_ref[...])

    @pl.when(i >= n_prompt)
    def _():
        hs_ref[...] = _rms_bf16(xs_ref[...], nw_ref[...])

    @pl.when(i == 0)
    def _():
        hm_ref[:META, :] = _rms_bf16(xm_ref[...], nw_ref[...])
        hm_ref[META:, :] = jnp.zeros((ROW_TILE - META, D_MODEL), BF16)


def _norm(xp, xs, xm, norm_w, *, tm=512):
    n_p, n_s = xp.shape[0] // tm, xs.shape[0] // tm
    kern = functools.partial(_norm_kernel, n_prompt=n_p)
    return pl.pallas_call(
        kern,
        grid=(n_p + n_s,),
        in_specs=[
            pl.BlockSpec((tm, D_MODEL), lambda i: (jnp.minimum(i, n_p - 1), 0)),
            pl.BlockSpec((tm, D_MODEL), lambda i: (jnp.maximum(i - n_p, 0), 0)),
            pl.BlockSpec((META, D_MODEL), lambda i: (0, 0)),
            pl.BlockSpec((1, D_MODEL), lambda i: (0, 0)),
        ],
        out_specs=[
            pl.BlockSpec((tm, D_MODEL), lambda i: (i, 0)),
            pl.BlockSpec((ROW_TILE, D_MODEL), lambda i: (0, 0)),
        ],
        out_shape=[
            jax.ShapeDtypeStruct((xp.shape[0] + xs.shape[0], D_MODEL), BF16),
            jax.ShapeDtypeStruct((ROW_TILE, D_MODEL), BF16),
        ],
        compiler_params=pltpu.CompilerParams(
            dimension_semantics=("arbitrary",), vmem_limit_bytes=VMEM_LIMIT),
        name="norm",
    )(xp, xs, xm, norm_w)


def _proj_kernel(hs_ref, wt_ref, o_ref, wb_ref, *, silu):
    @pl.when(pl.program_id(1) == 0)
    def _():
        wb_ref[...] = wt_ref[...].T.astype(BF16)

    r = jnp.dot(hs_ref[...], wb_ref[...], preferred_element_type=F32)
    o_ref[...] = _silu(r) if silu else r


def _proj(hs, w_t, *, w_row0, ncols, silu, tm=1024, tn=1024):
    rows = hs.shape[0]
    assert w_row0 % N_HEADS == 0 and ncols % tn == 0 and tn % N_HEADS == 0

    def w_rows(j, i):
        return ((w_row0 // N_HEADS + j * (tn // N_HEADS)) * N_HEADS, 0)

    return pl.pallas_call(
        functools.partial(_proj_kernel, silu=silu),
        grid=(ncols // tn, rows // tm),
        in_specs=[
            pl.BlockSpec((tm, D_MODEL), lambda j, i: (i, 0)),
            pl.BlockSpec((pl.Element(tn), pl.Element(D_MODEL)), w_rows),
        ],
        out_specs=pl.BlockSpec((tm, tn), lambda j, i: (i, j)),
        out_shape=jax.ShapeDtypeStruct((rows, ncols), F32),
        scratch_shapes=[pltpu.VMEM((D_MODEL, tn), BF16)],
        compiler_params=pltpu.CompilerParams(
            dimension_semantics=("arbitrary", "arbitrary"),
            vmem_limit_bytes=VMEM_LIMIT),
        name="proj",
    )(hs, w_t)


def _conv_rows(x, halo_ref, prev_ref, w_ref, *, first, bs, q, carry):
    taps = w_ref.shape[0]
    rt, width = x.shape

    @pl.when(first)
    def _():
        halo_ref[:, SUBLANES - (taps - 1):, :] = prev_ref[...]

    prev = halo_ref[...]
    acc = None
    if bs == 1:
        row = lax.broadcasted_iota(jnp.int32, (SUBLANES, width), 0)
        for s in range(taps - 1, 0, -1):
            rolled = pltpu.roll(x, s, 0)
            head = jnp.where(row < s, pltpu.roll(prev[0], s, 0), rolled[:SUBLANES])
            term = jnp.concatenate([head, rolled[SUBLANES:]], axis=0) * w_ref[taps - 1 - s:taps - s, :]
            acc = term if acc is None else acc + term
        acc = acc + x * w_ref[taps - 1:taps, :]
        if carry:
            halo_ref[0] = x[rt - SUBLANES:, :]
        return acc
    assert q == SUBLANES and not carry
    x3 = x.reshape(bs, q, width)
    row = lax.broadcasted_iota(jnp.int32, x3.shape, 1)
    for s in range(taps - 1, 0, -1):
        shifted = jnp.where(row < s, pltpu.roll(prev, s, 1), pltpu.roll(x3, s, 1))
        term = shifted * w_ref[taps - 1 - s:taps - s, :]
        acc = term if acc is None else acc + term
    acc = acc + x3 * w_ref[taps - 1:taps, :]
    return acc.reshape(rt, width)


def _seg_cumsum(a, q):
    pos = lax.broadcasted_iota(jnp.int32, a.shape, 0) & (q - 1)
    s = 1
    while s < q:
        shifted = pltpu.roll(a, s, 0)
        a = a + jnp.where(pos >= s, shifted, 0.0)
        s *= 2
    return a


def _headscal_kernel(hs_ref, wdt_ref, dtb_ref, alog_ref, at_ref, bt_ref, *, q, valid, tps):
    dtr = lax.dot_general(hs_ref[...], wdt_ref[...].astype(BF16), (((1,), (1,)), ((), ())),
                          preferred_element_type=F32)
    dtv = jax.nn.softplus(dtr + dtb_ref[...])
    if valid < ROW_TILE:
        rows = lax.broadcasted_iota(jnp.int32, dtv.shape, 0)
        dtv = jnp.where(rows < valid, dtv, 0.0)
    acum = _seg_cumsum(dtv * (-jnp.exp(alog_ref[...])), q)
    a_minus_logdt = acum - jnp.log(dtv)
    for t in range(tps):
        at_ref[t] = acum[t * ROW_TILE:(t + 1) * ROW_TILE].T
        bt_ref[t] = a_minus_logdt[t * ROW_TILE:(t + 1) * ROW_TILE].T


def _headscal(hs, w_t, dt_bias, a_log, *, q, valid, tile0, ntiles):
    tps = min(ntiles, 8)
    assert ntiles % tps == 0 and tile0 % tps == 0 and q <= ROW_TILE
    kern = functools.partial(_headscal_kernel, q=q, valid=valid, tps=tps)
    shape = jax.ShapeDtypeStruct((ntiles, LANES, ROW_TILE), F32)
    dt_blk = W_DT // LANES
    return pl.pallas_call(
        kern,
        grid=(ntiles // tps,),
        in_specs=[
            pl.BlockSpec((tps * ROW_TILE, D_MODEL), lambda t: (tile0 // tps + t, 0)),
            pl.BlockSpec((LANES, D_MODEL), lambda t: (dt_blk, 0)),
            pl.BlockSpec((1, LANES), lambda t: (0, 0)),
            pl.BlockSpec((1, LANES), lambda t: (0, 0)),
        ],
        out_specs=[
            pl.BlockSpec((tps, LANES, ROW_TILE), lambda t: (t, 0, 0)),
            pl.BlockSpec((tps, LANES, ROW_TILE), lambda t: (t, 0, 0)),
        ],
        out_shape=[shape, shape],
        compiler_params=pltpu.CompilerParams(dimension_semantics=("arbitrary",)),
        name="headscal",
    )(hs, w_t, dt_bias, a_log)


def _xbc_kernel(hs_ref, hm_ref, wt_ref, cw_ref, cb_ref, prev_ref,
                o_ref, om_ref, cnp_ref, cns_ref,
                wb_scr, halo_p, halo_s, halo_m, meta_prev, zero_prev,
                *, tm, n_prompt_tiles, tiles_per_seq, bs_sample, q_sample):
    i = pl.program_id(1)
    keep = SSD_CONV_W - 1
    tn = o_ref.shape[1]

    @pl.when(i == 0)
    def _():
        wb_scr[...] = wt_ref[...].T.astype(BF16)

    raw = jnp.dot(hs_ref[...], wb_scr[...], preferred_element_type=F32)

    def activate(conv):
        return _silu(conv + cb_ref[...])

    @pl.when(i == 0)
    def _():
        raw_m = jnp.dot(hm_ref[...], wb_scr[...], preferred_element_type=F32)
        meta_prev[0] = raw_m[META - keep:]
        zero_prev[...] = jnp.zeros(zero_prev.shape, F32)
        conv_m = _conv_rows(raw_m, halo_m, zero_prev, cw_ref, first=i == 0, bs=1, q=META,
                            carry=False)
        om_ref[:META, :] = activate(conv_m)
        om_ref[META:, :] = jnp.zeros((ROW_TILE - META, tn), F32)

    @pl.when(i < n_prompt_tiles)
    def _():
        o_ref[...] = activate(_conv_rows(raw, halo_p, meta_prev, cw_ref,
                                         first=(i % tiles_per_seq) == 0, bs=1, q=tm, carry=True))
        cnp_ref[0] = raw[tm - keep:]

    @pl.when(i >= n_prompt_tiles)
    def _():
        o_ref[...] = activate(_conv_rows(raw, halo_s, prev_ref, cw_ref, first=i >= n_prompt_tiles,
                                         bs=bs_sample, q=q_sample, carry=False))
        cns_ref[...] = raw.reshape(bs_sample, q_sample, tn)[:, q_sample - keep:, :]


def _xbc(hs, hm, w_t, cw, cb, prev_s, *, n_prompt, seq, n_seq_p, n_seq_s, q_sample,
         tm=1024, tn=512):
    rows = hs.shape[0]
    keep = SSD_CONV_W - 1
    assert seq % tm == 0 and (rows - n_prompt) == tm and tm == n_seq_s * q_sample
    assert D_INNER % tn == 0 and CONV_DIM % tn == 0
    tiles_per_seq = seq // tm
    n_pt = n_prompt // tm
    kern = functools.partial(_xbc_kernel, tm=tm, n_prompt_tiles=n_pt, tiles_per_seq=tiles_per_seq,
                             bs_sample=n_seq_s, q_sample=q_sample)
    return pl.pallas_call(
        kern,
        grid=(CONV_DIM // tn, rows // tm),
        in_specs=[
            pl.BlockSpec((tm, D_MODEL), lambda j, i: (i, 0)),
            pl.BlockSpec((META, D_MODEL), lambda j, i: (0, 0)),
            pl.BlockSpec((tn, D_MODEL), lambda j, i: (D_INNER // tn + j, 0)),
            pl.BlockSpec((SSD_CONV_W, tn), lambda j, i: (0, j)),
            pl.BlockSpec((1, tn), lambda j, i: (0, j)),
            pl.BlockSpec((n_seq_s, keep, tn), lambda j, i: (0, 0, j)),
        ],
        out_specs=[
            pl.BlockSpec((tm, tn), lambda j, i: (i, j)),
            pl.BlockSpec((ROW_TILE, tn), lambda j, i: (0, j)),
            pl.BlockSpec((1, keep, tn),
                         lambda j, i: (jnp.minimum(i, n_pt - 1) // tiles_per_seq, 0, j)),
            pl.BlockSpec((n_seq_s, keep, tn), lambda j, i: (0, 0, j)),
        ],
        out_shape=[
            jax.ShapeDtypeStruct((rows, CONV_DIM), F32),
            jax.ShapeDtypeStruct((ROW_TILE, CONV_DIM), F32),
            jax.ShapeDtypeStruct((n_seq_p, keep, CONV_DIM), F32),
            jax.ShapeDtypeStruct((n_seq_s, keep, CONV_DIM), F32),
        ],
        scratch_shapes=[
            pltpu.VMEM((D_MODEL, tn), BF16),
            pltpu.VMEM((1, SUBLANES, tn), F32),
            pltpu.VMEM((n_seq_s, SUBLANES, tn), F32),
            pltpu.VMEM((1, SUBLANES, tn), F32),
            pltpu.VMEM((1, keep, tn), F32),
            pltpu.VMEM((1, keep, tn), F32),
        ],
        compiler_params=pltpu.CompilerParams(
            dimension_semantics=("arbitrary", "arbitrary"),
            vmem_limit_bytes=VMEM_LIMIT),
        name="xbc",
    )(hs, hm, w_t, cw, cb, prev_s)


def _ssd_kernel(*refs, n_real, **static):
    yn_ref = refs[-3]
    s = pl.program_id(0)

    @pl.when(s < n_real)
    def _():
        _ssd_body(*refs, **static)

    @pl.when(s >= n_real)
    def _():
        yn_ref[...] = jnp.zeros(yn_ref.shape, yn_ref.dtype)


def _ssd_body(sz_ref, x_ref, b_ref, c_ref, at_ref, bt_ref, dsk_ref, nw_ref, s0_ref,
              *rest, bs, q, nc, gps):
    yn_ref, sout_ref, st_ref = rest[-3:]
    rt = bs * q
    nh = gps * HEADS_PER_GROUP
    c = pl.program_id(2)
    carry = nc > 1

    if carry:
        @pl.when(c == 0)
        def _():
            st_ref[...] = s0_ref[...]

    xc = x_ref[...]
    bcb = b_ref[...].astype(BF16)
    ccb = c_ref[...].astype(BF16)

    a_t = at_ref[0]
    b_t = bt_ref[0]
    cols = jnp.concatenate([a_t, jnp.zeros((LANES - nh, rt), F32)], axis=0).T

    pos = lax.broadcasted_iota(jnp.int32, (nh, rt), 1) & (q - 1)
    a_end = a_t
    s = 1
    while s < q:
        a_end = jnp.where(pos + s < q, pltpu.roll(a_end, rt - s, 1), a_end)
        s *= 2
    to_end = jnp.exp(a_end - b_t)

    nblk = rt // SUBLANES
    ri = lax.broadcasted_iota(jnp.int32, (nblk, SUBLANES, rt), 0) * SUBLANES + \
        lax.broadcasted_iota(jnp.int32, (nblk, SUBLANES, rt), 1)
    ci = lax.broadcasted_iota(jnp.int32, (nblk, SUBLANES, rt), 2)
    mask = (ri >= ci) & ((ri // q) == (ci // q))
    low = lax.broadcasted_iota(jnp.int32, (rt, LANES), 1) < HEAD_DIM
    seq_of_row = lax.broadcasted_iota(jnp.int32, (rt, N_STATE), 0) // q

    for k in range(gps):
        xg = xc[:, k * GROUP_W:(k + 1) * GROUP_W]
        bg = bcb[:, k * N_STATE:(k + 1) * N_STATE]
        cg = ccb[:, k * N_STATE:(k + 1) * N_STATE]
        cb = lax.dot_general(cg, bg, (((1,), (1,)), ((), ())), preferred_element_type=F32)
        cb3 = cb.reshape(nblk, SUBLANES, rt)
        xt = xg.T

        ydiag, ea, xw, a_cols = [], [], [], []
        for pr in range(HEADS_PER_GROUP // 2):
            wts, ab = [], []
            for hh in range(2):
                h = k * HEADS_PER_GROUP + 2 * pr + hh
                a_col = jnp.broadcast_to(cols[:, h:h + 1], (rt, LANES))
                b_row = jnp.broadcast_to(b_t[h:h + 1, :], (SUBLANES, rt))
                seg = jnp.where(mask, a_col.reshape(nblk, SUBLANES, rt) - b_row[None], -jnp.inf)
                wts.append((cb3 * jnp.exp(seg)).reshape(rt, rt).astype(BF16))
                ab.append(a_col)
                rows = slice((2 * pr + hh) * HEAD_DIM, (2 * pr + hh + 1) * HEAD_DIM)
                xw.append(xt[rows] * to_end[h:h + 1, :])
            a_cols += ab
            ea.append(jnp.exp(jnp.where(low, ab[0], ab[1])))
            xb = xg[:, pr * LANES:(pr + 1) * LANES].astype(BF16)
            zero = jnp.zeros_like(xb)
            rhs = jnp.concatenate([jnp.where(low, xb, zero), jnp.where(low, zero, xb)], axis=0)
            ydiag.append(jnp.dot(jnp.concatenate(wts, axis=1), rhs, preferred_element_type=F32))
        ydiag = jnp.concatenate(ydiag, axis=1)
        ea = jnp.concatenate(ea, axis=1)
        xwt = jnp.concatenate(xw, axis=0).astype(BF16)

        yoff = []
        for s in range(bs):
            st = st_ref[s, k] if carry else s0_ref[s, k]
            yoff.append(lax.dot_general(cg[s * q:(s + 1) * q, :], st.astype(BF16),
                                        (((1,), (1,)), ((), ())), preferred_element_type=F32))
            bsel = bg if bs == 1 else jnp.where(seq_of_row == s, bg, jnp.zeros_like(bg))
            upd = jnp.dot(xwt, bsel, preferred_element_type=F32)
            last = (s + 1) * q - 1
            dec = jnp.concatenate(
                [jnp.broadcast_to(jnp.exp(a_cols[h][last:last + 1, :]), (HEAD_DIM, N_STATE))
                 for h in range(HEADS_PER_GROUP)], axis=0)
            new = st * dec + upd
            if carry:
                st_ref[s, k] = new

                @pl.when(c == nc - 1)
                def _():
                    sout_ref[s, k] = new
            else:
                sout_ref[s, k] = new
        yoff = yoff[0] if bs == 1 else jnp.concatenate(yoff, axis=0)

        lanes = slice(k * GROUP_W, (k + 1) * GROUP_W)
        y = ydiag + yoff * ea + dsk_ref[:, lanes] * xg
        gz = y * sz_ref[:, lanes]
        ms = jnp.mean(gz * gz, axis=-1, keepdims=True)
        yn_ref[:, lanes] = (gz * lax.rsqrt(ms + EPS) * nw_ref[:, lanes]).astype(BF16)


def _ssd(sz, xbc, a_t, dt_t, d_skip_x, norm_w, state0,
         *, n_seq, bs, q, nc, gps, shared_init, tile0=0, out_rows=None, zero_tail_tiles=0,
         yn_into=None):
    rt = bs * q
    assert rt == ROW_TILE and N_GROUPS % gps == 0
    rows = n_seq * q * nc if out_rows is None else out_rows
    otile0 = 0 if out_rows is None else tile0
    nsb = n_seq // bs
    n_pad = pl.cdiv(zero_tail_tiles, nc)
    gw, gn, nh = gps * GROUP_W, gps * N_STATE, gps * HEADS_PER_GROUP
    kern = functools.partial(_ssd_kernel, n_real=nsb, bs=bs, q=q, nc=nc, gps=gps)
    bb, bc_ = D_INNER // gn, (D_INNER + N_GROUPS * N_STATE) // gn

    def real(s):
        return jnp.minimum(s, nsb - 1) if n_pad else s

    def tile(s, c):
        return real(s) * nc + c

    def otile(s, c):
        if not n_pad:
            return otile0 + s * nc + c
        tail = jnp.minimum((s - nsb) * nc + c, zero_tail_tiles - 1)
        return otile0 + jnp.where(s < nsb, s * nc + c, nsb * nc + tail)

    sidx = (lambda s: 0) if shared_init else real
    in_specs = [
        pl.BlockSpec((rt, gw), lambda s, g, c: (tile0 + tile(s, c), g)),
        pl.BlockSpec((rt, gw), lambda s, g, c: (tile0 + tile(s, c), g)),
        pl.BlockSpec((rt, gn), lambda s, g, c: (tile0 + tile(s, c), bb + g)),
        pl.BlockSpec((rt, gn), lambda s, g, c: (tile0 + tile(s, c), bc_ + g)),
        pl.BlockSpec((1, nh, rt), lambda s, g, c: (tile(s, c), g, 0)),
        pl.BlockSpec((1, nh, rt), lambda s, g, c: (tile(s, c), g, 0)),
        pl.BlockSpec((1, gw), lambda s, g, c: (0, g)),
        pl.BlockSpec((1, gw), lambda s, g, c: (0, g)),
        pl.BlockSpec((bs, gps, GROUP_W, N_STATE), lambda s, g, c: (sidx(s), g, 0, 0)),
    ]
    out_specs = [
        pl.BlockSpec((rt, gw), lambda s, g, c: (otile(s, c), g)),
        pl.BlockSpec((bs, gps, GROUP_W, N_STATE), lambda s, g, c: (real(s), g, 0, 0)),
    ]
    operands = [sz, xbc, xbc, xbc, a_t, dt_t, d_skip_x, norm_w, state0]
    aliases = {}
    if yn_into is not None:
        assert yn_into.shape == (rows, D_INNER)
        aliases = {len(operands): 0}
        in_specs.append(pl.BlockSpec(memory_space=pl.ANY))
        operands.append(yn_into)
    st_shape = (bs, gps, GROUP_W, N_STATE) if nc > 1 else (1, 1, SUBLANES, N_STATE)
    return pl.pallas_call(
        kern,
        grid=(nsb + n_pad, N_GROUPS // gps, nc),
        in_specs=in_specs,
        out_specs=out_specs,
        out_shape=[
            jax.ShapeDtypeStruct((rows, D_INNER), BF16),
            jax.ShapeDtypeStruct((n_seq, N_GROUPS, GROUP_W, N_STATE), F32),
        ],
        scratch_shapes=[pltpu.VMEM(st_shape, F32)],
        input_output_aliases=aliases,
        compiler_params=pltpu.CompilerParams(
            dimension_semantics=("arbitrary", "arbitrary", "arbitrary"),
            vmem_limit_bytes=VMEM_LIMIT),
        name="ssd",
    )(*operands)


def _sconv_kernel(hs_ref, hm_ref, wb_ref, wc_ref, wh_ref, wz_ref, cw_ref, prev_ref,
                  v_ref, newp_ref, news_ref,
                  w_scr, halo_p, halo_s, meta_u,
                  *, tm, width, n_prompt_tiles, tiles_per_seq, bs_sample, q_sample):
    i = pl.program_id(1)
    keep = SC_CONV_W - 1

    @pl.when(i == 0)
    def _():
        for k, w_ref in enumerate((wb_ref, wc_ref, wh_ref, wz_ref)):
            w_scr[:, k * width:(k + 1) * width] = w_ref[...].T.astype(BF16)
        rm = jnp.dot(hm_ref[...], w_scr[:, width:3 * width], preferred_element_type=F32)
        meta_u[0] = (rm[:, :width] * rm[:, width:])[META - keep:]

    r = jnp.dot(hs_ref[...], w_scr[...], preferred_element_type=F32)
    u = r[:, width:2 * width] * r[:, 2 * width:3 * width]

    def finish(uc):
        v_ref[...] = (r[:, :width] * uc * _silu(r[:, 3 * width:])).astype(BF16)

    @pl.when(i < n_prompt_tiles)
    def _():
        finish(_conv_rows(u, halo_p, meta_u, cw_ref, first=(i % tiles_per_seq) == 0,
                          bs=1, q=tm, carry=True))
        newp_ref[0] = u[tm - keep:]

    @pl.when(i >= n_prompt_tiles)
    def _():
        finish(_conv_rows(u, halo_s, prev_ref, cw_ref, first=i >= n_prompt_tiles,
                          bs=bs_sample, q=q_sample, carry=False))
        news_ref[...] = u.reshape(bs_sample, q_sample, width)[:, q_sample - keep:, :]


def _sconv(hs, hm, w_t, cw, prev_s, *, n_prompt, seq, n_seq_p, n_seq_s, q_sample,
           tm=1024, width=256):
    rows = hs.shape[0]
    keep = SC_CONV_W - 1
    assert seq % tm == 0 and (rows - n_prompt) == tm and tm == n_seq_s * q_sample
    tiles_per_seq = seq // tm
    n_pt = n_prompt // tm
    kern = functools.partial(_sconv_kernel, tm=tm, width=width, n_prompt_tiles=n_pt,
                             tiles_per_seq=tiles_per_seq, bs_sample=n_seq_s, q_sample=q_sample)

    def w_rows(k):
        base = (W_SC + k * D_MODEL) // N_HEADS
        return lambda cbk, i: ((base + cbk * (width // N_HEADS)) * N_HEADS, 0)

    w_specs = [pl.BlockSpec((pl.Element(width), pl.Element(D_MODEL)), w_rows(k)) for k in range(4)]
    return pl.pallas_call(
        kern,
        grid=(D_MODEL // width, rows // tm),
        in_specs=[
            pl.BlockSpec((tm, D_MODEL), lambda cbk, i: (i, 0)),
            pl.BlockSpec((META, D_MODEL), lambda cbk, i: (0, 0)),
            *w_specs,
            pl.BlockSpec((SC_CONV_W, width), lambda cbk, i: (0, cbk)),
            pl.BlockSpec((n_seq_s, keep, width), lambda cbk, i: (0, 0, cbk)),
        ],
        out_specs=[
            pl.BlockSpec((tm, width), lambda cbk, i: (i, cbk)),
            pl.BlockSpec((1, keep, width),
                         lambda cbk, i: (jnp.minimum(i, n_pt - 1) // tiles_per_seq, 0, cbk)),
            pl.BlockSpec((n_seq_s, keep, width), lambda cbk, i: (0, 0, cbk)),
        ],
        out_shape=[
            jax.ShapeDtypeStruct((rows, D_MODEL), BF16),
            jax.ShapeDtypeStruct((n_seq_p, keep, D_MODEL), F32),
            jax.ShapeDtypeStruct((n_seq_s, keep, D_MODEL), F32),
        ],
        scratch_shapes=[
            pltpu.VMEM((D_MODEL, 4 * width), BF16),
            pltpu.VMEM((1, SUBLANES, width), F32),
            pltpu.VMEM((n_seq_s, SUBLANES, width), F32),
            pltpu.VMEM((1, keep, width), F32),
        ],
        compiler_params=pltpu.CompilerParams(
            dimension_semantics=("arbitrary", "arbitrary"),
            vmem_limit_bytes=VMEM_LIMIT),
        name="sconv",
    )(hs, hm, w_t, w_t, w_t, w_t, cw, prev_s)


def _merge_kernel(yn_ref, v_ref, ga_ref, gb_ref, wa_ref, wb_ref, o_ref, wa_scr, wb_scr):
    @pl.when(pl.program_id(1) == 0)
    def _():
        wa_scr[...] = wa_ref[...].astype(BF16)
        wb_scr[...] = wb_ref[...].astype(BF16)

    ya = jnp.dot(yn_ref[...], wa_scr[...], preferred_element_type=F32)
    yb = jnp.dot(v_ref[...], wb_scr[...], preferred_element_type=F32)
    o_ref[...] = (jax.nn.sigmoid(ga_ref[...]) * ya + jax.nn.sigmoid(gb_ref[...]) * yb).astype(BF16)


def _merge(yn, v, gates, wa, wb, *, tm=512, tn=512):
    rows = yn.shape[0]
    return pl.pallas_call(
        _merge_kernel,
        grid=(D_MODEL // tn, rows // tm),
        in_specs=[
            pl.BlockSpec((tm, D_INNER), lambda j, i: (i, 0)),
            pl.BlockSpec((tm, D_MODEL), lambda j, i: (i, 0)),
            pl.BlockSpec((tm, tn), lambda j, i: (i, j)),
            pl.BlockSpec((tm, tn), lambda j, i: (i, D_MODEL // tn + j)),
            pl.BlockSpec((D_INNER, tn), lambda j, i: (0, j)),
            pl.BlockSpec((D_MODEL, tn), lambda j, i: (0, j)),
        ],
        out_specs=pl.BlockSpec((tm, tn), lambda j, i: (i, j)),
        out_shape=jax.ShapeDtypeStruct((rows, D_MODEL), BF16),
        scratch_shapes=[pltpu.VMEM((D_INNER, tn), BF16), pltpu.VMEM((D_MODEL, tn), BF16)],
        compiler_params=pltpu.CompilerParams(
            dimension_semantics=("arbitrary", "arbitrary"),
            vmem_limit_bytes=VMEM_LIMIT),
        name="merge",
    )(yn, v, gates, gates, wa, wb)


def _outproj_kernel(m_ref, xp_ref, xs_ref, wo_ref, fw_ref, op_ref, os_ref, *, n_prompt_tiles):
    i = pl.program_id(0)
    d = jnp.dot(m_ref[...], wo_ref[...], preferred_element_type=F32)

    def finish(x_ref, o_ref):
        y = x_ref[...] + d
        ms = jnp.mean(y * y, axis=-1, keepdims=True)
        o_ref[...] = y * lax.rsqrt(ms + EPS) * fw_ref[...]

    @pl.when(i < n_prompt_tiles)
    def _():
        finish(xp_ref, op_ref)

    @pl.when(i >= n_prompt_tiles)
    def _():
        finish(xs_ref, os_ref)


def _outproj(m, xp, xs, wo, fw, *, tm=256):
    n_p, n_s = xp.shape[0] // tm, xs.shape[0] // tm
    return pl.pallas_call(
        functools.partial(_outproj_kernel, n_prompt_tiles=n_p),
        grid=(n_p + n_s,),
        in_specs=[
            pl.BlockSpec((tm, D_MODEL), lambda i: (i, 0)),
            pl.BlockSpec((tm, D_MODEL), lambda i: (jnp.minimum(i, n_p - 1), 0)),
            pl.BlockSpec((tm, D_MODEL), lambda i: (jnp.maximum(i - n_p, 0), 0)),
            pl.BlockSpec((D_MODEL, D_MODEL), lambda i: (0, 0), pipeline_mode=pl.Buffered(1)),
            pl.BlockSpec((1, D_MODEL), lambda i: (0, 0)),
        ],
        out_specs=[
            pl.BlockSpec((tm, D_MODEL), lambda i: (jnp.minimum(i, n_p - 1), 0)),
            pl.BlockSpec((tm, D_MODEL), lambda i: (jnp.maximum(i - n_p, 0), 0)),
        ],
        out_shape=[
            jax.ShapeDtypeStruct(xp.shape, F32),
            jax.ShapeDtypeStruct(xs.shape, F32),
        ],
        compiler_params=pltpu.CompilerParams(
            dimension_semantics=("arbitrary",),
            vmem_limit_bytes=VMEM_LIMIT),
        name="outproj",
    )(m, xp, xs, wo, fw)


def kernel(x_prompt, x_sample, state_ssd_conv, state_ssm, state_sconv, meta_tokens, norm_w,
           w_in, ssd_conv_w, ssd_conv_b, dt_bias, a_log, d_skip, ssd_norm_w, w_ssd_out,
           sconv_w, w_sconv_out, w_o, final_norm_w):
    bp, seq = x_prompt.shape[0], x_prompt.shape[1]
    bd, dec_seq = x_sample.shape[0], x_sample.shape[1]

    w_t = jnp.transpose(w_in[0])
    nw = norm_w[0].reshape(1, D_MODEL)
    fw = final_norm_w.reshape(1, D_MODEL)
    conv_w = ssd_conv_w[0]
    conv_b = ssd_conv_b[0].reshape(1, CONV_DIM)
    dtb = jnp.pad(dt_bias[0], (0, LANES - N_HEADS)).reshape(1, LANES)
    alog = jnp.pad(a_log[0], (0, LANES - N_HEADS)).reshape(1, LANES)
    dsk = jnp.repeat(d_skip[0], HEAD_DIM).reshape(1, D_INNER)
    gnw = ssd_norm_w[0].reshape(1, D_INNER)
    scw = sconv_w[0]

    xp = x_prompt.reshape(bp * seq, D_MODEL)
    xs = x_sample.reshape(bd * dec_seq, D_MODEL)
    n_p, n_s = bp * seq, bd * dec_seq
    streams = dict(n_prompt=n_p, seq=seq, n_seq_p=bp, n_seq_s=bd, q_sample=dec_seq)

    hs, hm = _norm(xp, xs, meta_tokens, nw)
    sz = _proj(hs, w_t, w_row0=0, ncols=D_INNER, silu=True)
    gates = _proj(hs, w_t, w_row0=W_GATE, ncols=2 * D_MODEL, silu=False)
    xbc, xbc_m, conv_p, conv_s = _xbc(hs, hm, w_t, conv_w, conv_b, state_ssd_conv[0], **streams)
    v_all, sc_p, sc_s = _sconv(hs, hm, w_t, scw, state_sconv[0], **streams)
    headscal = functools.partial(_headscal, w_t=w_t, dt_bias=dtb, a_log=alog)
    ssd = functools.partial(_ssd, d_skip_x=dsk, norm_w=gnw)

    at_m, dtt_m = headscal(hm, q=ROW_TILE, valid=META, tile0=0, ntiles=1)
    _, ssm_m = ssd(jnp.zeros((ROW_TILE, D_INNER), F32), xbc_m, at_m, dtt_m,
                   state0=jnp.zeros((1, N_GROUPS, GROUP_W, N_STATE), F32),
                   n_seq=1, bs=1, q=ROW_TILE, nc=1, gps=2, shared_init=False)

    at_p, dtt_p = headscal(hs, q=ROW_TILE, valid=ROW_TILE, tile0=0, ntiles=n_p // ROW_TILE)
    yn, ssm_p = ssd(sz, xbc, at_p, dtt_p, state0=ssm_m, n_seq=bp, bs=1, q=ROW_TILE,
                    nc=seq // ROW_TILE, gps=8, shared_init=True, out_rows=n_p + n_s,
                    zero_tail_tiles=n_s // ROW_TILE)

    sbs = ROW_TILE // dec_seq
    at_s, dtt_s = headscal(hs, q=dec_seq, valid=ROW_TILE, tile0=n_p // ROW_TILE,
                           ntiles=n_s // ROW_TILE)
    yn, ssm_s = ssd(sz, xbc, at_s, dtt_s,
                    state0=state_ssm[0].reshape(bd, N_GROUPS, GROUP_W, N_STATE),
                    n_seq=bd, bs=sbs, q=dec_seq, nc=1, gps=2, shared_init=False,
                    tile0=n_p // ROW_TILE, out_rows=n_p + n_s, yn_into=yn)

    merged = _merge(yn, v_all, gates, w_ssd_out[0], w_sconv_out[0])
    y_p, y_s = _outproj(merged, xp, xs, w_o[0].astype(BF16), fw)

    return (y_p.reshape(bp, seq, D_MODEL),
            y_s.reshape(bd, dec_seq, D_MODEL),
            conv_p[None],
            ssm_p.reshape(1, bp, N_HEADS, HEAD_DIM, N_STATE),
            sc_p[None],
            conv_s[None],
            ssm_s.reshape(1, bd, N_HEADS, HEAD_DIM, N_STATE),
            sc_s[None])
```

```python
import functools

import jax
import jax.numpy as jnp
from jax import lax
from jax.experimental import pallas as pl
from jax.experimental.pallas import tpu as pltpu

F32 = jnp.float32
BF16 = jnp.bfloat16

D_MODEL = 2048
D_INNER = 4096
N_HEADS = 64
HEAD_DIM = 64
N_STATE = 128
N_GROUPS = 8
GROUP_W = D_INNER // N_GROUPS
HEADS_PER_GROUP = N_HEADS // N_GROUPS
CONV_DIM = D_INNER + 2 * N_GROUPS * N_STATE
SSD_CONV_W = 4
SC_CONV_W = 3
META = 16
EPS = 1e-6

LANES = 128
SUBLANES = 8
ROW_TILE = 128
W_DT = D_INNER + CONV_DIM
W_SC = W_DT + N_HEADS
W_GATE = W_SC + 4 * D_MODEL

VMEM_LIMIT = 52 * 1024 * 1024


def _silu(x):
    h = 0.5 * x
    return h * (1.0 + jnp.tanh(h))


def _rms_bf16(x, w):
    ms = jnp.mean(x * x, axis=-1, keepdims=True)
    return (x * lax.rsqrt(ms + EPS) * w).astype(BF16)


def _norm_kernel(xp_ref, xs_ref, xm_ref, nw_ref, hs_ref, hm_ref, *, n_prompt):
    i = pl.program_id(0)

    @pl.when(i < n_prompt)
    def _():
        hs_ref[...] = _rms_bf16(xp_ref[...], nw_ref[...])

    @pl.when(i >= n_prompt)
    def _():
        hs_ref[...] = _rms_bf16(xs_ref[...], nw_ref[...])

    @pl.when(i == 0)
    def _():
        hm_ref[:META, :] = _rms_bf16(xm_ref[...], nw_ref[...])
        hm_ref[META:, :] = jnp.zeros((ROW_TILE - META, D_MODEL), BF16)


def _norm(xp, xs, xm, norm_w, *, tm=512):
    n_p, n_s = xp.shape[0] // tm, xs.shape[0] // tm
    kern = functools.partial(_norm_kernel, n_prompt=n_p)
    return pl.pallas_call(
        kern,
        grid=(n_p + n_s,),
        in_specs=[
            pl.BlockSpec((tm, D_MODEL), lambda i: (jnp.minimum(i, n_p - 1), 0)),
            pl.BlockSpec((tm, D_MODEL), lambda i: (jnp.maximum(i - n_p, 0), 0)),
            pl.BlockSpec((META, D_MODEL), lambda i: (0, 0)),
            pl.BlockSpec((1, D_MODEL), lambda i: (0, 0)),
        ],
        out_specs=[
            pl.BlockSpec((tm, D_MODEL), lambda i: (i, 0)),
            pl.BlockSpec((ROW_TILE, D_MODEL), lambda i: (0, 0)),
        ],
        out_shape=[
            jax.ShapeDtypeStruct((xp.shape[0] + xs.shape[0], D_MODEL), BF16),
            jax.ShapeDtypeStruct((ROW_TILE, D_MODEL), BF16),
        ],
        compiler_params=pltpu.CompilerParams(
            dimension_semantics=("arbitrary",), vmem_limit_bytes=VMEM_LIMIT),
        name="norm",
    )(xp, xs, xm, norm_w)


def _proj_kernel(hs_ref, wt_ref, o_ref, wb_ref, *, silu):
    @pl.when(pl.program_id(1) == 0)
    def _():
        wb_ref[...] = wt_ref[...].T.astype(BF16)

    r = jnp.dot(hs_ref[...], wb_ref[...], preferred_element_type=F32)
    o_ref[...] = _silu(r) if silu else r


def _proj(hs, w_t, *, w_row0, ncols, silu, tm=1024, tn=1024):
    rows = hs.shape[0]
    assert w_row0 % N_HEADS == 0 and ncols % tn == 0 and tn % N_HEADS == 0

    def w_rows(j, i):
        return ((w_row0 // N_HEADS + j * (tn // N_HEADS)) * N_HEADS, 0)

    return pl.pallas_call(
        functools.partial(_proj_kernel, silu=silu),
        grid=(ncols // tn, rows // tm),
        in_specs=[
            pl.BlockSpec((tm, D_MODEL), lambda j, i: (i, 0)),
            pl.BlockSpec((pl.Element(tn), pl.Element(D_MODEL)), w_rows),
        ],
        out_specs=pl.BlockSpec((tm, tn), lambda j, i: (i, j)),
        out_shape=jax.ShapeDtypeStruct((rows, ncols), F32),
        scratch_shapes=[pltpu.VMEM((D_MODEL, tn), BF16)],
        compiler_params=pltpu.CompilerParams(
            dimension_semantics=("arbitrary", "arbitrary"),
            vmem_limit_bytes=VMEM_LIMIT),
        name="proj",
    )(hs, w_t)


def _conv_rows(x, halo_ref, prev_ref, w_ref, *, first, bs, q, carry):
    taps = w_ref.shape[0]
    rt, width = x.shape

    @pl.when(first)
    def _():
        halo_ref[:, SUBLANES - (taps - 1):, :] = prev_ref[...]

    prev = halo_ref[...]
    acc = None
    if bs == 1:
        row = lax.broadcasted_iota(jnp.int32, (SUBLANES, width), 0)
        for s in range(taps - 1, 0, -1):
            rolled = pltpu.roll(x, s, 0)
            head = jnp.where(row < s, pltpu.roll(prev[0], s, 0), rolled[:SUBLANES])
            term = jnp.concatenate([head, rolled[SUBLANES:]], axis=0) * w_ref[taps - 1 - s:taps - s, :]
            acc = term if acc is None else acc + term
        acc = acc + x * w_ref[taps - 1:taps, :]
        if carry:
            halo_ref[0] = x[rt - SUBLANES:, :]
        return acc
    assert q == SUBLANES and not carry
    x3 = x.reshape(bs, q, width)
    row = lax.broadcasted_iota(jnp.int32, x3.shape, 1)
    for s in range(taps - 1, 0, -1):
        shifted = jnp.where(row < s, pltpu.roll(prev, s, 1), pltpu.roll(x3, s, 1))
        term = shifted * w_ref[taps - 1 - s:taps - s, :]
        acc = term if acc is None else acc + term
    acc = acc + x3 * w_ref[taps - 1:taps, :]
    return acc.reshape(rt, width)


def _seg_cumsum(a, q):
    pos = lax.broadcasted_iota(jnp.int32, a.shape, 0) & (q - 1)
    s = 1
    while s < q:
        shifted = pltpu.roll(a, s, 0)
        a = a + jnp.where(pos >= s, shifted, 0.0)
        s *= 2
    return a


def _headscal_kernel(hs_ref, wdt_ref, dtb_ref, alog_ref, at_ref, bt_ref, *, q, valid, tps):
    dtr = lax.dot_general(hs_ref[...], wdt_ref[...].astype(BF16), (((1,), (1,)), ((), ())),
                          preferred_element_type=F32)
    dtv = jax.nn.softplus(dtr + dtb_ref[...])
    if valid < ROW_TILE:
        rows = lax.broadcasted_iota(jnp.int32, dtv.shape, 0)
        dtv = jnp.where(rows < valid, dtv, 0.0)
    acum = _seg_cumsum(dtv * (-jnp.exp(alog_ref[...])), q)
    a_minus_logdt = acum - jnp.log(dtv)
    for t in range(tps):
        at_ref[t] = acum[t * ROW_TILE:(t + 1) * ROW_TILE].T
        bt_ref[t] = a_minus_logdt[t * ROW_TILE:(t + 1) * ROW_TILE].T


def _headscal(hs, w_t, dt_bias, a_log, *, q, valid, tile0, ntiles):
    tps = min(ntiles, 8)
    assert ntiles % tps == 0 and tile0 % tps == 0 and q <= ROW_TILE
    kern = functools.partial(_headscal_kernel, q=q, valid=valid, tps=tps)
    shape = jax.ShapeDtypeStruct((ntiles, LANES, ROW_TILE), F32)
    dt_blk = W_DT // LANES
    return pl.pallas_call(
        kern,
        grid=(ntiles // tps,),
        in_specs=[
            pl.BlockSpec((tps * ROW_TILE, D_MODEL), lambda t: (tile0 // tps + t, 0)),
            pl.BlockSpec((LANES, D_MODEL), lambda t: (dt_blk, 0)),
            pl.BlockSpec((1, LANES), lambda t: (0, 0)),
            pl.BlockSpec((1, LANES), lambda t: (0, 0)),
        ],
        out_specs=[
            pl.BlockSpec((tps, LANES, ROW_TILE), lambda t: (t, 0, 0)),
            pl.BlockSpec((tps, LANES, ROW_TILE), lambda t: (t, 0, 0)),
        ],
        out_shape=[shape, shape],
        compiler_params=pltpu.CompilerParams(dimension_semantics=("arbitrary",)),
        name="headscal",
    )(hs, w_t, dt_bias, a_log)


def _xbc_kernel(hs_ref, hm_ref, wt_ref, cw_ref, cb_ref, prev_ref,
                o_ref, om_ref, cnp_ref, cns_ref,
                wb_scr, halo_p, halo_s, halo_m, meta_prev, zero_prev,
                *, tm, n_prompt_tiles, tiles_per_seq, bs_sample, q_sample):
    i = pl.program_id(1)
    keep = SSD_CONV_W - 1
    tn = o_ref.shape[1]

    @pl.when(i == 0)
    def _():
        wb_scr[...] = wt_ref[...].T.astype(BF16)

    raw = jnp.dot(hs_ref[...], wb_scr[...], preferred_element_type=F32)

    def activate(conv):
        return _silu(conv + cb_ref[...])

    @pl.when(i == 0)
    def _():
        raw_m = jnp.dot(hm_ref[...], wb_scr[...], preferred_element_type=F32)
        meta_prev[0] = raw_m[META - keep:]
        zero_prev[...] = jnp.zeros(zero_prev.shape, F32)
        conv_m = _conv_rows(raw_m, halo_m, zero_prev, cw_ref, first=i == 0, bs=1, q=META,
                            carry=False)
        om_ref[:META, :] = activate(conv_m)
        om_ref[META:, :] = jnp.zeros((ROW_TILE - META, tn), F32)

    @pl.when(i < n_prompt_tiles)
    def _():
        o_ref[...] = activate(_conv_rows(raw, halo_p, meta_prev, cw_ref,
                                         first=(i % tiles_per_seq) == 0, bs=1, q=tm, carry=True))
        cnp_ref[0] = raw[tm - keep:]

    @pl.when(i >= n_prompt_tiles)
    def _():
        o_ref[...] = activate(_conv_rows(raw, halo_s, prev_ref, cw_ref, first=i >= n_prompt_tiles,
                                         bs=bs_sample, q=q_sample, carry=False))
        cns_ref[...] = raw.reshape(bs_sample, q_sample, tn)[:, q_sample - keep:, :]


def _xbc(hs, hm, w_t, cw, cb, prev_s, *, n_prompt, seq, n_seq_p, n_seq_s, q_sample,
         tm=1024, tn=512):
    rows = hs.shape[0]
    keep = SSD_CONV_W - 1
    assert seq % tm == 0 and (rows - n_prompt) == tm and tm == n_seq_s * q_sample
    assert D_INNER % tn == 0 and CONV_DIM % tn == 0
    tiles_per_seq = seq // tm
    n_pt = n_prompt // tm
    kern = functools.partial(_xbc_kernel, tm=tm, n_prompt_tiles=n_pt, tiles_per_seq=tiles_per_seq,
                             bs_sample=n_seq_s, q_sample=q_sample)
    return pl.pallas_call(
        kern,
        grid=(CONV_DIM // tn, rows // tm),
        in_specs=[
            pl.BlockSpec((tm, D_MODEL), lambda j, i: (i, 0)),
            pl.BlockSpec((META, D_MODEL), lambda j, i: (0, 0)),
            pl.BlockSpec((tn, D_MODEL), lambda j, i: (D_INNER // tn + j, 0)),
            pl.BlockSpec((SSD_CONV_W, tn), lambda j, i: (0, j)),
            pl.BlockSpec((1, tn), lambda j, i: (0, j)),
            pl.BlockSpec((n_seq_s, keep, tn), lambda j, i: (0, 0, j)),
        ],
        out_specs=[
            pl.BlockSpec((tm, tn), lambda j, i: (i, j)),
            pl.BlockSpec((ROW_TILE, tn), lambda j, i: (0, j)),
            pl.BlockSpec((1, keep, tn),
                         lambda j, i: (jnp.minimum(i, n_pt - 1) // tiles_per_seq, 0, j)),
            pl.BlockSpec((n_seq_s, keep, tn), lambda j, i: (0, 0, j)),
        ],
        out_shape=[
            jax.ShapeDtypeStruct((rows, CONV_DIM), F32),
            jax.ShapeDtypeStruct((ROW_TILE, CONV_DIM), F32),
            jax.ShapeDtypeStruct((n_seq_p, keep, CONV_DIM), F32),
            jax.ShapeDtypeStruct((n_seq_s, keep, CONV_DIM), F32),
        ],
        scratch_shapes=[
            pltpu.VMEM((D_MODEL, tn), BF16),
            pltpu.VMEM((1, SUBLANES, tn), F32),
            pltpu.VMEM((n_seq_s, SUBLANES, tn), F32),
            pltpu.VMEM((1, SUBLANES, tn), F32),
            pltpu.VMEM((1, keep, tn), F32),
            pltpu.VMEM((1, keep, tn), F32),
        ],
        compiler_params=pltpu.CompilerParams(
            dimension_semantics=("arbitrary", "arbitrary"),
            vmem_limit_bytes=VMEM_LIMIT),
        name="xbc",
    )(hs, hm, w_t, cw, cb, prev_s)


def _ssd_kernel(*refs, n_real, **static):
    yn_ref = refs[-3]
    s = pl.program_id(0)

    @pl.when(s < n_real)
    def _():
        _ssd_body(*refs, **static)

    @pl.when(s >= n_real)
    def _():
        yn_ref[...] = jnp.zeros(yn_ref.shape, yn_ref.dtype)


def _ssd_body(sz_ref, x_ref, b_ref, c_ref, at_ref, bt_ref, dsk_ref, nw_ref, s0_ref,
              *rest, bs, q, nc, gps):
    yn_ref, sout_ref, st_ref = rest[-3:]
    rt = bs * q
    nh = gps * HEADS_PER_GROUP
    c = pl.program_id(2)
    carry = nc > 1

    if carry:
        @pl.when(c == 0)
        def _():
            st_ref[...] = s0_ref[...]

    xc = x_ref[...]
    bcb = b_ref[...].astype(BF16)
    ccb = c_ref[...].astype(BF16)

    a_t = at_ref[0]
    b_t = bt_ref[0]
    cols = jnp.concatenate([a_t, jnp.zeros((LANES - nh, rt), F32)], axis=0).T

    pos = lax.broadcasted_iota(jnp.int32, (nh, rt), 1) & (q - 1)
    a_end = a_t
    s = 1
    while s < q:
        a_end = jnp.where(pos + s < q, pltpu.roll(a_end, rt - s, 1), a_end)
        s *= 2
    to_end = jnp.exp(a_end - b_t)

    nblk = rt // SUBLANES
    ri = lax.broadcasted_iota(jnp.int32, (nblk, SUBLANES, rt), 0) * SUBLANES + \
        lax.broadcasted_iota(jnp.int32, (nblk, SUBLANES, rt), 1)
    ci = lax.broadcasted_iota(jnp.int32, (nblk, SUBLANES, rt), 2)
    mask = (ri >= ci) & ((ri // q) == (ci // q))
    low = lax.broadcasted_iota(jnp.int32, (rt, LANES), 1) < HEAD_DIM
    seq_of_row = lax.broadcasted_iota(jnp.int32, (rt, N_STATE), 0) // q

    for k in range(gps):
        xg = xc[:, k * GROUP_W:(k + 1) * GROUP_W]
        bg = bcb[:, k * N_STATE:(k + 1) * N_STATE]
        cg = ccb[:, k * N_STATE:(k + 1) * N_STATE]
        cb = lax.dot_general(cg, bg, (((1,), (1,)), ((), ())), preferred_element_type=F32)
        cb3 = cb.reshape(nblk, SUBLANES, rt)
        xt = xg.T

        ydiag, ea, xw, a_cols = [], [], [], []
        for pr in range(HEADS_PER_GROUP // 2):
            wts, ab = [], []
            for hh in range(2):
                h = k * HEADS_PER_GROUP + 2 * pr + hh
                a_col = jnp.broadcast_to(cols[:, h:h + 1], (rt, LANES))
                b_row = jnp.broadcast_to(b_t[h:h + 1, :], (SUBLANES, rt))
                seg = jnp.where(mask, a_col.reshape(nblk, SUBLANES, rt) - b_row[None], -jnp.inf)
                wts.append((cb3 * jnp.exp(seg)).reshape(rt, rt).astype(BF16))
                ab.append(a_col)
                rows = slice((2 * pr + hh) * HEAD_DIM, (2 * pr + hh + 1) * HEAD_DIM)
                xw.append(xt[rows] * to_end[h:h + 1, :])
            a_cols += ab
            ea.append(jnp.exp(jnp.where(low, ab[0], ab[1])))
            xb = xg[:, pr * LANES:(pr + 1) * LANES].astype(BF16)
            zero = jnp.zeros_like(xb)
            rhs = jnp.concatenate([jnp.where(low, xb, zero), jnp.where(low, zero, xb)], axis=0)
            ydiag.append(jnp.dot(jnp.concatenate(wts, axis=1), rhs, preferred_element_type=F32))
        ydiag = jnp.concatenate(ydiag, axis=1)
        ea = jnp.concatenate(ea, axis=1)
        xwt = jnp.concatenate(xw, axis=0).astype(BF16)

        yoff = []
        for s in range(bs):
            st = st_ref[s, k] if carry else s0_ref[s, k]
            yoff.append(lax.dot_general(cg[s * q:(s + 1) * q, :], st.astype(BF16),
                                        (((1,), (1,)), ((), ())), preferred_element_type=F32))
            bsel = bg if bs == 1 else jnp.where(seq_of_row == s, bg, jnp.zeros_like(bg))
            upd = jnp.dot(xwt, bsel, preferred_element_type=F32)
            last = (s + 1) * q - 1
            dec = jnp.concatenate(
                [jnp.broadcast_to(jnp.exp(a_cols[h][last:last + 1, :]), (HEAD_DIM, N_STATE))
                 for h in range(HEADS_PER_GROUP)], axis=0)
            new = st * dec + upd
            if carry:
                st_ref[s, k] = new

                @pl.when(c == nc - 1)
                def _():
                    sout_ref[s, k] = new
            else:
                sout_ref[s, k] = new
        yoff = yoff[0] if bs == 1 else jnp.concatenate(yoff, axis=0)

        lanes = slice(k * GROUP_W, (k + 1) * GROUP_W)
        y = ydiag + yoff * ea + dsk_ref[:, lanes] * xg
        gz = y * sz_ref[:, lanes]
        ms = jnp.mean(gz * gz, axis=-1, keepdims=True)
        yn_ref[:, lanes] = (gz * lax.rsqrt(ms + EPS) * nw_ref[:, lanes]).astype(BF16)


def _ssd(sz, xbc, a_t, dt_t, d_skip_x, norm_w, state0,
         *, n_seq, bs, q, nc, gps, shared_init, tile0=0, out_rows=None, zero_tail_tiles=0,
         yn_into=None):
    rt = bs * q
    assert rt == ROW_TILE and N_GROUPS % gps == 0
    rows = n_seq * q * nc if out_rows is None else out_rows
    otile0 = 0 if out_rows is None else tile0
    nsb = n_seq // bs
    n_pad = pl.cdiv(zero_tail_tiles, nc)
    gw, gn, nh = gps * GROUP_W, gps * N_STATE, gps * HEADS_PER_GROUP
    static = dict(bs=bs, q=q, nc=nc, gps=gps)
    kern = (functools.partial(_ssd_kernel, n_real=nsb, **static) if n_pad
            else functools.partial(_ssd_body, **static))
    bb, bc_ = D_INNER // gn, (D_INNER + N_GROUPS * N_STATE) // gn

    def real(s):
        return jnp.minimum(s, nsb - 1) if n_pad else s

    def tile(s, c):
        return jnp.where(s < nsb, s * nc + c, nsb * nc - 1) if n_pad else s * nc + c

    def otile(s, c):
        if not n_pad:
            return otile0 + s * nc + c
        tail = jnp.minimum((s - nsb) * nc + c, zero_tail_tiles - 1)
        return otile0 + jnp.where(s < nsb, s * nc + c, nsb * nc + tail)

    sidx = (lambda s: 0) if shared_init else real
    in_specs = [
        pl.BlockSpec((rt, gw), lambda s, g, c: (tile0 + tile(s, c), g)),
        pl.BlockSpec((rt, gw), lambda s, g, c: (tile0 + tile(s, c), g)),
        pl.BlockSpec((rt, gn), lambda s, g, c: (tile0 + tile(s, c), bb + g)),
        pl.BlockSpec((rt, gn), lambda s, g, c: (tile0 + tile(s, c), bc_ + g)),
        pl.BlockSpec((1, nh, rt), lambda s, g, c: (tile(s, c), g, 0)),
        pl.BlockSpec((1, nh, rt), lambda s, g, c: (tile(s, c), g, 0)),
        pl.BlockSpec((1, gw), lambda s, g, c: (0, g)),
        pl.BlockSpec((1, gw), lambda s, g, c: (0, g)),
        pl.BlockSpec((bs, gps, GROUP_W, N_STATE), lambda s, g, c: (sidx(s), g, 0, 0)),
    ]
    out_specs = [
        pl.BlockSpec((rt, gw), lambda s, g, c: (otile(s, c), g)),
        pl.BlockSpec((bs, gps, GROUP_W, N_STATE), lambda s, g, c: (real(s), g, 0, 0)),
    ]
    operands = [sz, xbc, xbc, xbc, a_t, dt_t, d_skip_x, norm_w, state0]
    aliases = {}
    if yn_into is not None:
        assert yn_into.shape == (rows, D_INNER)
        aliases = {len(operands): 0}
        in_specs.append(pl.BlockSpec(memory_space=pl.ANY))
        operands.append(yn_into)
    st_shape = (bs, gps, GROUP_W, N_STATE) if nc > 1 else (1, 1, SUBLANES, N_STATE)
    return pl.pallas_call(
        kern,
        grid=(nsb + n_pad, N_GROUPS // gps, nc),
        in_specs=in_specs,
        out_specs=out_specs,
        out_shape=[
            jax.ShapeDtypeStruct((rows, D_INNER), BF16),
            jax.ShapeDtypeStruct((n_seq, N_GROUPS, GROUP_W, N_STATE), F32),
        ],
        scratch_shapes=[pltpu.VMEM(st_shape, F32)],
        input_output_aliases=aliases,
        compiler_params=pltpu.CompilerParams(
            dimension_semantics=("arbitrary", "arbitrary", "arbitrary"),
            vmem_limit_bytes=VMEM_LIMIT),
        name="ssd",
    )(*operands)


def _sconv_kernel(hs_ref, hm_ref, wb_ref, wc_ref, wh_ref, wz_ref, cw_ref, prev_ref,
                  v_ref, newp_ref, news_ref,
                  w_scr, halo_p, halo_s, meta_u,
                  *, tm, width, n_prompt_tiles, tiles_per_seq, bs_sample, q_sample):
    i = pl.program_id(1)
    keep = SC_CONV_W - 1

    @pl.when(i == 0)
    def _():
        for k, w_ref in enumerate((wb_ref, wc_ref, wh_ref, wz_ref)):
            w_scr[:, k * width:(k + 1) * width] = w_ref[...].T.astype(BF16)
        rm = jnp.dot(hm_ref[...], w_scr[:, width:3 * width], preferred_element_type=F32)
        meta_u[0] = (rm[:, :width] * rm[:, width:])[META - keep:]

    r = jnp.dot(hs_ref[...], w_scr[...], preferred_element_type=F32)
    u = r[:, width:2 * width] * r[:, 2 * width:3 * width]

    def finish(uc):
        v_ref[...] = (r[:, :width] * uc * _silu(r[:, 3 * width:])).astype(BF16)

    @pl.when(i < n_prompt_tiles)
    def _():
        finish(_conv_rows(u, halo_p, meta_u, cw_ref, first=(i % tiles_per_seq) == 0,
                          bs=1, q=tm, carry=True))
        newp_ref[0] = u[tm - keep:]

    @pl.when(i >= n_prompt_tiles)
    def _():
        finish(_conv_rows(u, halo_s, prev_ref, cw_ref, first=i >= n_prompt_tiles,
                          bs=bs_sample, q=q_sample, carry=False))
        news_ref[...] = u.reshape(bs_sample, q_sample, width)[:, q_sample - keep:, :]


def _sconv(hs, hm, w_t, cw, prev_s, *, n_prompt, seq, n_seq_p, n_seq_s, q_sample,
           tm=1024, width=256):
    rows = hs.shape[0]
    keep = SC_CONV_W - 1
    assert seq % tm == 0 and (rows - n_prompt) == tm and tm == n_seq_s * q_sample
    tiles_per_seq = seq // tm
    n_pt = n_prompt // tm
    kern = functools.partial(_sconv_kernel, tm=tm, width=width, n_prompt_tiles=n_pt,
                             tiles_per_seq=tiles_per_seq, bs_sample=n_seq_s, q_sample=q_sample)

    def w_rows(k):
        base = (W_SC + k * D_MODEL) // N_HEADS
        return lambda cbk, i: ((base + cbk * (width // N_HEADS)) * N_HEADS, 0)

    w_specs = [pl.BlockSpec((pl.Element(width), pl.Element(D_MODEL)), w_rows(k)) for k in range(4)]
    return pl.pallas_call(
        kern,
        grid=(D_MODEL // width, rows // tm),
        in_specs=[
            pl.BlockSpec((tm, D_MODEL), lambda cbk, i: (i, 0)),
            pl.BlockSpec((META, D_MODEL), lambda cbk, i: (0, 0)),
            *w_specs,
            pl.BlockSpec((SC_CONV_W, width), lambda cbk, i: (0, cbk)),
            pl.BlockSpec((n_seq_s, keep, width), lambda cbk, i: (0, 0, cbk)),
        ],
        out_specs=[
            pl.BlockSpec((tm, width), lambda cbk, i: (i, cbk)),
            pl.BlockSpec((1, keep, width),
                         lambda cbk, i: (jnp.minimum(i, n_pt - 1) // tiles_per_seq, 0, cbk)),
            pl.BlockSpec((n_seq_s, keep, width), lambda cbk, i: (0, 0, cbk)),
        ],
        out_shape=[
            jax.ShapeDtypeStruct((rows, D_MODEL), BF16),
            jax.ShapeDtypeStruct((n_seq_p, keep, D_MODEL), F32),
            jax.ShapeDtypeStruct((n_seq_s, keep, D_MODEL), F32),
        ],
        scratch_shapes=[
            pltpu.VMEM((D_MODEL, 4 * width), BF16),
            pltpu.VMEM((1, SUBLANES, width), F32),
            pltpu.VMEM((n_seq_s, SUBLANES, width), F32),
            pltpu.VMEM((1, keep, width), F32),
        ],
        compiler_params=pltpu.CompilerParams(
            dimension_semantics=("arbitrary", "arbitrary"),
            vmem_limit_bytes=VMEM_LIMIT),
        name="sconv",
    )(hs, hm, w_t, w_t, w_t, w_t, cw, prev_s)


def _merge_kernel(yn_ref, v_ref, ga_ref, gb_ref, wa_ref, wb_ref, o_ref, wa_scr, wb_scr):
    @pl.when(pl.program_id(1) == 0)
    def _():
        wa_scr[...] = wa_ref[...].astype(BF16)
        wb_scr[...] = wb_ref[...].astype(BF16)

    ya = jnp.dot(yn_ref[...], wa_scr[...], preferred_element_type=F32)
    yb = jnp.dot(v_ref[...], wb_scr[...], preferred_element_type=F32)
    o_ref[...] = (jax.nn.sigmoid(ga_ref[...]) * ya + jax.nn.sigmoid(gb_ref[...]) * yb).astype(BF16)


def _merge(yn, v, gates, wa, wb, *, tm=512, tn=512):
    rows = yn.shape[0]
    return pl.pallas_call(
        _merge_kernel,
        grid=(D_MODEL // tn, rows // tm),
        in_specs=[
            pl.BlockSpec((tm, D_INNER), lambda j, i: (i, 0)),
            pl.BlockSpec((tm, D_MODEL), lambda j, i: (i, 0)),
            pl.BlockSpec((tm, tn), lambda j, i: (i, j)),
            pl.BlockSpec((tm, tn), lambda j, i: (i, D_MODEL // tn + j)),
            pl.BlockSpec((D_INNER, tn), lambda j, i: (0, j)),
            pl.BlockSpec((D_MODEL, tn), lambda j, i: (0, j)),
        ],
        out_specs=pl.BlockSpec((tm, tn), lambda j, i: (i, j)),
        out_shape=jax.ShapeDtypeStruct((rows, D_MODEL), BF16),
        scratch_shapes=[pltpu.VMEM((D_INNER, tn), BF16), pltpu.VMEM((D_MODEL, tn), BF16)],
        compiler_params=pltpu.CompilerParams(
            dimension_semantics=("arbitrary", "arbitrary"),
            vmem_limit_bytes=VMEM_LIMIT),
        name="merge",
    )(yn, v, gates, gates, wa, wb)


def _outproj_kernel(m_ref, x_ref, wo_ref, fw_ref, o_ref):
    y = x_ref[...] + jnp.dot(m_ref[...], wo_ref[...], preferred_element_type=F32)
    ms = jnp.mean(y * y, axis=-1, keepdims=True)
    o_ref[...] = y * lax.rsqrt(ms + EPS) * fw_ref[...]


def _outproj(m, x, wo, fw, *, row0, tm=512):
    rows = x.shape[0]
    assert row0 % tm == 0 and rows % tm == 0
    t0 = row0 // tm
    return pl.pallas_call(
        _outproj_kernel,
        grid=(rows // tm,),
        in_specs=[
            pl.BlockSpec((tm, D_MODEL), lambda i: (t0 + i, 0)),
            pl.BlockSpec((tm, D_MODEL), lambda i: (i, 0)),
            pl.BlockSpec((D_MODEL, D_MODEL), lambda i: (0, 0), pipeline_mode=pl.Buffered(1)),
            pl.BlockSpec((1, D_MODEL), lambda i: (0, 0)),
        ],
        out_specs=pl.BlockSpec((tm, D_MODEL), lambda i: (i, 0)),
        out_shape=jax.ShapeDtypeStruct((rows, D_MODEL), F32),
        compiler_params=pltpu.CompilerParams(
            dimension_semantics=("arbitrary",),
            vmem_limit_bytes=VMEM_LIMIT),
        name="outproj",
    )(m, x, wo, fw)


def kernel(x_prompt, x_sample, state_ssd_conv, state_ssm, state_sconv, meta_tokens, norm_w,
           w_in, ssd_conv_w, ssd_conv_b, dt_bias, a_log, d_skip, ssd_norm_w, w_ssd_out,
           sconv_w, w_sconv_out, w_o, final_norm_w):
    bp, seq = x_prompt.shape[0], x_prompt.shape[1]
    bd, dec_seq = x_sample.shape[0], x_sample.shape[1]

    w_t = jnp.transpose(w_in[0])
    nw = norm_w[0].reshape(1, D_MODEL)
    fw = final_norm_w.reshape(1, D_MODEL)
    conv_w = ssd_conv_w[0]
    conv_b = ssd_conv_b[0].reshape(1, CONV_DIM)
    dtb = jnp.pad(dt_bias[0], (0, LANES - N_HEADS)).reshape(1, LANES)
    alog = jnp.pad(a_log[0], (0, LANES - N_HEADS)).reshape(1, LANES)
    dsk = jnp.repeat(d_skip[0], HEAD_DIM).reshape(1, D_INNER)
    gnw = ssd_norm_w[0].reshape(1, D_INNER)
    scw = sconv_w[0]

    xp = x_prompt.reshape(bp * seq, D_MODEL)
    xs = x_sample.reshape(bd * dec_seq, D_MODEL)
    n_p, n_s = bp * seq, bd * dec_seq
    streams = dict(n_prompt=n_p, seq=seq, n_seq_p=bp, n_seq_s=bd, q_sample=dec_seq)

    hs, hm = _norm(xp, xs, meta_tokens, nw)
    sz = _proj(hs, w_t, w_row0=0, ncols=D_INNER, silu=True)
    gates = _proj(hs, w_t, w_row0=W_GATE, ncols=2 * D_MODEL, silu=False)
    xbc, xbc_m, conv_p, conv_s = _xbc(hs, hm, w_t, conv_w, conv_b, state_ssd_conv[0], **streams)
    v_all, sc_p, sc_s = _sconv(hs, hm, w_t, scw, state_sconv[0], **streams)
    headscal = functools.partial(_headscal, w_t=w_t, dt_bias=dtb, a_log=alog)
    ssd = functools.partial(_ssd, d_skip_x=dsk, norm_w=gnw)

    at_m, dtt_m = headscal(hm, q=ROW_TILE, valid=META, tile0=0, ntiles=1)
    _, ssm_m = ssd(jnp.zeros((ROW_TILE, D_INNER), F32), xbc_m, at_m, dtt_m,
                   state0=jnp.zeros((1, N_GROUPS, GROUP_W, N_STATE), F32),
                   n_seq=1, bs=1, q=ROW_TILE, nc=1, gps=2, shared_init=False)

    at_p, dtt_p = headscal(hs, q=ROW_TILE, valid=ROW_TILE, tile0=0, ntiles=n_p // ROW_TILE)
    yn, ssm_p = ssd(sz, xbc, at_p, dtt_p, state0=ssm_m, n_seq=bp, bs=1, q=ROW_TILE,
                    nc=seq // ROW_TILE, gps=8, shared_init=True, out_rows=n_p + n_s,
                    zero_tail_tiles=n_s // ROW_TILE)

    sbs = ROW_TILE // dec_seq
    at_s, dtt_s = headscal(hs, q=dec_seq, valid=ROW_TILE, tile0=n_p // ROW_TILE,
                           ntiles=n_s // ROW_TILE)
    yn, ssm_s = ssd(sz, xbc, at_s, dtt_s,
                    state0=state_ssm[0].reshape(bd, N_GROUPS, GROUP_W, N_STATE),
                    n_seq=bd, bs=sbs, q=dec_seq, nc=1, gps=2, shared_init=False,
                    tile0=n_p // ROW_TILE, out_rows=n_p + n_s, yn_into=yn)

    merged = _merge(yn, v_all, gates, w_ssd_out[0], w_sconv_out[0])
    wo = w_o[0].astype(BF16)
    y_p = _outproj(merged, xp, wo, fw, row0=0)
    y_s = _outproj(merged, xs, wo, fw, row0=n_p)

    return (y_p.reshape(bp, seq, D_MODEL),
            y_s.reshape(bd, dec_seq, D_MODEL),
            conv_p[None],
            ssm_p.reshape(1, bp, N_HEADS, HEAD_DIM, N_STATE),
            sc_p[None],
            conv_s[None],
            ssm_s.reshape(1, bd, N_HEADS, HEAD_DIM, N_STATE),
            sc_s[None])
```

```python
import functools

import jax
import jax.numpy as jnp
from jax import lax
from jax.experimental import pallas as pl
from jax.experimental.pallas import tpu as pltpu

F32 = jnp.float32
BF16 = jnp.bfloat16

D_MODEL = 2048
D_INNER = 4096
N_HEADS = 64
HEAD_DIM = 64
N_STATE = 128
N_GROUPS = 8
GROUP_W = D_INNER // N_GROUPS
HEADS_PER_GROUP = N_HEADS // N_GROUPS
CONV_DIM = D_INNER + 2 * N_GROUPS * N_STATE
SSD_CONV_W = 4
SC_CONV_W = 3
META = 16
EPS = 1e-6
LOG2_E = 1.4426950408889634

LANES = 128
SUBLANES = 8
ROW_TILE = 128
W_DT = D_INNER + CONV_DIM
W_SC = W_DT + N_HEADS
W_GATE = W_SC + 4 * D_MODEL

VMEM_LIMIT = 52 * 1024 * 1024


def _silu(x):
    h = 0.5 * x
    return h * (1.0 + jnp.tanh(h))


def _rms_bf16(x, w):
    ms = jnp.mean(x * x, axis=-1, keepdims=True)
    return (x * lax.rsqrt(ms + EPS) * w).astype(BF16)


def _norm_kernel(xp_ref, xs_ref, xm_ref, nw_ref, hs_ref, hm_ref, *, n_prompt):
    i = pl.program_id(0)

    @pl.when(i < n_prompt)
    def _():
        hs_ref[...] = _rms_bf16(xp_ref[...], nw_ref[...])

    @pl.when(i >= n_prompt)
    def _():
        hs_ref[...] = _rms_bf16(xs_ref[...], nw_ref[...])

    @pl.when(i == 0)
    def _():
        hm_ref[:META, :] = _rms_bf16(xm_ref[...], nw_ref[...])
        hm_ref[META:, :] = jnp.zeros((ROW_TILE - META, D_MODEL), BF16)


def _norm(xp, xs, xm, norm_w, *, tm=512):
    n_p, n_s = xp.shape[0] // tm, xs.shape[0] // tm
    kern = functools.partial(_norm_kernel, n_prompt=n_p)
    return pl.pallas_call(
        kern,
        grid=(n_p + n_s,),
        in_specs=[
            pl.BlockSpec((tm, D_MODEL), lambda i: (jnp.minimum(i, n_p - 1), 0)),
            pl.BlockSpec((tm, D_MODEL), lambda i: (jnp.maximum(i - n_p, 0), 0)),
            pl.BlockSpec((META, D_MODEL), lambda i: (0, 0)),
            pl.BlockSpec((1, D_MODEL), lambda i: (0, 0)),
        ],
        out_specs=[
            pl.BlockSpec((tm, D_MODEL), lambda i: (i, 0)),
            pl.BlockSpec((ROW_TILE, D_MODEL), lambda i: (0, 0)),
        ],
        out_shape=[
            jax.ShapeDtypeStruct((xp.shape[0] + xs.shape[0], D_MODEL), BF16),
            jax.ShapeDtypeStruct((ROW_TILE, D_MODEL), BF16),
        ],
        compiler_params=pltpu.CompilerParams(
            dimension_semantics=("arbitrary",), vmem_limit_bytes=VMEM_LIMIT),
        name="norm",
    )(xp, xs, xm, norm_w)


def _proj_kernel(hs_ref, wt_ref, o_ref, wb_ref, *, silu):
    @pl.when(pl.program_id(1) == 0)
    def _():
        wb_ref[...] = wt_ref[...].T.astype(BF16)

    r = jnp.dot(hs_ref[...], wb_ref[...], preferred_element_type=F32)
    o_ref[...] = _silu(r) if silu else r


def _proj(hs, w_t, *, w_row0, ncols, silu, tm=1024, tn=1024):
    rows = hs.shape[0]
    assert w_row0 % N_HEADS == 0 and ncols % tn == 0 and tn % N_HEADS == 0

    def w_rows(j, i):
        return ((w_row0 // N_HEADS + j * (tn // N_HEADS)) * N_HEADS, 0)

    return pl.pallas_call(
        functools.partial(_proj_kernel, silu=silu),
        grid=(ncols // tn, rows // tm),
        in_specs=[
            pl.BlockSpec((tm, D_MODEL), lambda j, i: (i, 0)),
            pl.BlockSpec((pl.Element(tn), pl.Element(D_MODEL)), w_rows),
        ],
        out_specs=pl.BlockSpec((tm, tn), lambda j, i: (i, j)),
        out_shape=jax.ShapeDtypeStruct((rows, ncols), F32),
        scratch_shapes=[pltpu.VMEM((D_MODEL, tn), BF16)],
        compiler_params=pltpu.CompilerParams(
            dimension_semantics=("arbitrary", "arbitrary"),
            vmem_limit_bytes=VMEM_LIMIT),
        name="proj",
    )(hs, w_t)


def _conv_rows(x, halo_ref, prev_ref, w_ref, *, first, bs, q, carry):
    taps = w_ref.shape[0]
    rt, width = x.shape

    @pl.when(first)
    def _():
        halo_ref[:, SUBLANES - (taps - 1):, :] = prev_ref[...]

    prev = halo_ref[...]
    acc = None
    if bs == 1:
        row = lax.broadcasted_iota(jnp.int32, (SUBLANES, width), 0)
        for s in range(taps - 1, 0, -1):
            rolled = pltpu.roll(x, s, 0)
            head = jnp.where(row < s, pltpu.roll(prev[0], s, 0), rolled[:SUBLANES])
            term = jnp.concatenate([head, rolled[SUBLANES:]], axis=0) * w_ref[taps - 1 - s:taps - s, :]
            acc = term if acc is None else acc + term
        acc = acc + x * w_ref[taps - 1:taps, :]
        if carry:
            halo_ref[0] = x[rt - SUBLANES:, :]
        return acc
    assert q == SUBLANES and not carry
    x3 = x.reshape(bs, q, width)
    row = lax.broadcasted_iota(jnp.int32, x3.shape, 1)
    for s in range(taps - 1, 0, -1):
        shifted = jnp.where(row < s, pltpu.roll(prev, s, 1), pltpu.roll(x3, s, 1))
        term = shifted * w_ref[taps - 1 - s:taps - s, :]
        acc = term if acc is None else acc + term
    acc = acc + x3 * w_ref[taps - 1:taps, :]
    return acc.reshape(rt, width)


def _seg_cumsum(a, q):
    pos = lax.broadcasted_iota(jnp.int32, a.shape, 0) & (q - 1)
    s = 1
    while s < q:
        shifted = pltpu.roll(a, s, 0)
        a = a + jnp.where(pos >= s, shifted, 0.0)
        s *= 2
    return a


def _headscal_kernel(hs_ref, wdt_ref, dtb_ref, alog_ref, at_ref, bt_ref, *, q, valid, tps):
    dtr = lax.dot_general(hs_ref[...], wdt_ref[...].astype(BF16), (((1,), (1,)), ((), ())),
                          preferred_element_type=F32)
    dtv = jax.nn.softplus(dtr + dtb_ref[...])
    if valid < ROW_TILE:
        rows = lax.broadcasted_iota(jnp.int32, dtv.shape, 0)
        dtv = jnp.where(rows < valid, dtv, 0.0)
    acum = _seg_cumsum(dtv * (-jnp.exp(alog_ref[...])), q) * LOG2_E
    a_minus_logdt = acum - jnp.log(dtv) * LOG2_E
    for t in range(tps):
        at_ref[t] = acum[t * ROW_TILE:(t + 1) * ROW_TILE].T
        bt_ref[t] = a_minus_logdt[t * ROW_TILE:(t + 1) * ROW_TILE].T


def _headscal(hs, w_t, dt_bias, a_log, *, q, valid, tile0, ntiles):
    tps = min(ntiles, 8)
    assert ntiles % tps == 0 and tile0 % tps == 0 and q <= ROW_TILE
    kern = functools.partial(_headscal_kernel, q=q, valid=valid, tps=tps)
    shape = jax.ShapeDtypeStruct((ntiles, LANES, ROW_TILE), F32)
    dt_blk = W_DT // LANES
    return pl.pallas_call(
        kern,
        grid=(ntiles // tps,),
        in_specs=[
            pl.BlockSpec((tps * ROW_TILE, D_MODEL), lambda t: (tile0 // tps + t, 0)),
            pl.BlockSpec((LANES, D_MODEL), lambda t: (dt_blk, 0)),
            pl.BlockSpec((1, LANES), lambda t: (0, 0)),
            pl.BlockSpec((1, LANES), lambda t: (0, 0)),
        ],
        out_specs=[
            pl.BlockSpec((tps, LANES, ROW_TILE), lambda t: (t, 0, 0)),
            pl.BlockSpec((tps, LANES, ROW_TILE), lambda t: (t, 0, 0)),
        ],
        out_shape=[shape, shape],
        compiler_params=pltpu.CompilerParams(dimension_semantics=("arbitrary",)),
        name="headscal",
    )(hs, w_t, dt_bias, a_log)


def _xbc_kernel(hs_ref, hm_ref, wt_ref, cw_ref, cb_ref, prev_ref,
                o_ref, om_ref, cnp_ref, cns_ref,
                wb_scr, halo_p, halo_s, halo_m, meta_prev, zero_prev,
                *, tm, n_prompt_tiles, tiles_per_seq, bs_sample, q_sample):
    i = pl.program_id(1)
    keep = SSD_CONV_W - 1
    tn = o_ref.shape[1]

    @pl.when(i == 0)
    def _():
        wb_scr[...] = wt_ref[...].T.astype(BF16)

    raw = jnp.dot(hs_ref[...], wb_scr[...], preferred_element_type=F32)

    def activate(conv):
        return _silu(conv + cb_ref[...])

    @pl.when(i == 0)
    def _():
        raw_m = jnp.dot(hm_ref[...], wb_scr[...], preferred_element_type=F32)
        meta_prev[0] = raw_m[META - keep:]
        zero_prev[...] = jnp.zeros(zero_prev.shape, F32)
        conv_m = _conv_rows(raw_m, halo_m, zero_prev, cw_ref, first=i == 0, bs=1, q=META,
                            carry=False)
        om_ref[:META, :] = activate(conv_m)
        om_ref[META:, :] = jnp.zeros((ROW_TILE - META, tn), F32)

    @pl.when(i < n_prompt_tiles)
    def _():
        o_ref[...] = activate(_conv_rows(raw, halo_p, meta_prev, cw_ref,
                                         first=(i % tiles_per_seq) == 0, bs=1, q=tm, carry=True))
        cnp_ref[0] = raw[tm - keep:]

    @pl.when(i >= n_prompt_tiles)
    def _():
        o_ref[...] = activate(_conv_rows(raw, halo_s, prev_ref, cw_ref, first=i >= n_prompt_tiles,
                                         bs=bs_sample, q=q_sample, carry=False))
        cns_ref[...] = raw.reshape(bs_sample, q_sample, tn)[:, q_sample - keep:, :]


def _xbc(hs, hm, w_t, cw, cb, prev_s, *, n_prompt, seq, n_seq_p, n_seq_s, q_sample,
         tm=1024, tn=512):
    rows = hs.shape[0]
    keep = SSD_CONV_W - 1
    assert seq % tm == 0 and (rows - n_prompt) == tm and tm == n_seq_s * q_sample
    assert D_INNER % tn == 0 and CONV_DIM % tn == 0
    tiles_per_seq = seq // tm
    n_pt = n_prompt // tm
    kern = functools.partial(_xbc_kernel, tm=tm, n_prompt_tiles=n_pt, tiles_per_seq=tiles_per_seq,
                             bs_sample=n_seq_s, q_sample=q_sample)
    return pl.pallas_call(
        kern,
        grid=(CONV_DIM // tn, rows // tm),
        in_specs=[
            pl.BlockSpec((tm, D_MODEL), lambda j, i: (i, 0)),
            pl.BlockSpec((META, D_MODEL), lambda j, i: (0, 0)),
            pl.BlockSpec((tn, D_MODEL), lambda j, i: (D_INNER // tn + j, 0)),
            pl.BlockSpec((SSD_CONV_W, tn), lambda j, i: (0, j)),
            pl.BlockSpec((1, tn), lambda j, i: (0, j)),
            pl.BlockSpec((n_seq_s, keep, tn), lambda j, i: (0, 0, j)),
        ],
        out_specs=[
            pl.BlockSpec((tm, tn), lambda j, i: (i, j)),
            pl.BlockSpec((ROW_TILE, tn), lambda j, i: (0, j)),
            pl.BlockSpec((1, keep, tn),
                         lambda j, i: (jnp.minimum(i, n_pt - 1) // tiles_per_seq, 0, j)),
            pl.BlockSpec((n_seq_s, keep, tn), lambda j, i: (0, 0, j)),
        ],
        out_shape=[
            jax.ShapeDtypeStruct((rows, CONV_DIM), F32),
            jax.ShapeDtypeStruct((ROW_TILE, CONV_DIM), F32),
            jax.ShapeDtypeStruct((n_seq_p, keep, CONV_DIM), F32),
            jax.ShapeDtypeStruct((n_seq_s, keep, CONV_DIM), F32),
        ],
        scratch_shapes=[
            pltpu.VMEM((D_MODEL, tn), BF16),
            pltpu.VMEM((1, SUBLANES, tn), F32),
            pltpu.VMEM((n_seq_s, SUBLANES, tn), F32),
            pltpu.VMEM((1, SUBLANES, tn), F32),
            pltpu.VMEM((1, keep, tn), F32),
            pltpu.VMEM((1, keep, tn), F32),
        ],
        compiler_params=pltpu.CompilerParams(
            dimension_semantics=("arbitrary", "arbitrary"),
            vmem_limit_bytes=VMEM_LIMIT),
        name="xbc",
    )(hs, hm, w_t, cw, cb, prev_s)


def _ssd_kernel(*refs, n_real, **static):
    yn_ref = refs[-3]
    s = pl.program_id(0)

    @pl.when(s < n_real)
    def _():
        _ssd_body(*refs, **static)

    @pl.when(s >= n_real)
    def _():
        yn_ref[...] = jnp.zeros(yn_ref.shape, yn_ref.dtype)


def _ssd_body(sz_ref, x_ref, b_ref, c_ref, at_ref, bt_ref, dsk_ref, nw_ref, s0_ref,
              *rest, bs, q, nc, gps):
    yn_ref, sout_ref, st_ref = rest[-3:]
    rt = bs * q
    nh = gps * HEADS_PER_GROUP
    c = pl.program_id(2)
    carry = nc > 1

    if carry:
        @pl.when(c == 0)
        def _():
            st_ref[...] = s0_ref[...]

    xc = x_ref[...]
    bcb = b_ref[...].astype(BF16)
    ccb = c_ref[...].astype(BF16)

    a_t = at_ref[0]
    b_t = bt_ref[0]
    cols = jnp.concatenate([a_t, jnp.zeros((LANES - nh, rt), F32)], axis=0).T

    pos = lax.broadcasted_iota(jnp.int32, (nh, rt), 1) & (q - 1)
    a_end = a_t
    s = 1
    while s < q:
        a_end = jnp.where(pos + s < q, pltpu.roll(a_end, rt - s, 1), a_end)
        s *= 2
    to_end = jnp.exp2(a_end - b_t)

    nblk = rt // SUBLANES
    ri = lax.broadcasted_iota(jnp.int32, (nblk, SUBLANES, rt), 0) * SUBLANES + \
        lax.broadcasted_iota(jnp.int32, (nblk, SUBLANES, rt), 1)
    ci = lax.broadcasted_iota(jnp.int32, (nblk, SUBLANES, rt), 2)
    mask = (ri >= ci) & ((ri // q) == (ci // q))
    low = lax.broadcasted_iota(jnp.int32, (rt, LANES), 1) < HEAD_DIM
    seq_of_row = lax.broadcasted_iota(jnp.int32, (rt, N_STATE), 0) // q

    for k in range(gps):
        xg = xc[:, k * GROUP_W:(k + 1) * GROUP_W]
        bg = bcb[:, k * N_STATE:(k + 1) * N_STATE]
        cg = ccb[:, k * N_STATE:(k + 1) * N_STATE]
        cb = lax.dot_general(cg, bg, (((1,), (1,)), ((), ())), preferred_element_type=F32)
        cb3 = cb.reshape(nblk, SUBLANES, rt)
        xt = xg.T

        ydiag, ea, xw, a_cols = [], [], [], []
        for pr in range(HEADS_PER_GROUP // 2):
            wts, ab = [], []
            for hh in range(2):
                h = k * HEADS_PER_GROUP + 2 * pr + hh
                a_col = jnp.broadcast_to(cols[:, h:h + 1], (rt, LANES))
                b_row = jnp.broadcast_to(b_t[h:h + 1, :], (SUBLANES, rt))
                seg = jnp.where(mask, a_col.reshape(nblk, SUBLANES, rt) - b_row[None], -jnp.inf)
                wts.append((cb3 * jnp.exp2(seg)).reshape(rt, rt).astype(BF16))
                ab.append(a_col)
                rows = slice((2 * pr + hh) * HEAD_DIM, (2 * pr + hh + 1) * HEAD_DIM)
                xw.append(xt[rows] * to_end[h:h + 1, :])
            a_cols += ab
            ea.append(jnp.exp2(jnp.where(low, ab[0], ab[1])))
            xb = xg[:, pr * LANES:(pr + 1) * LANES].astype(BF16)
            zero = jnp.zeros_like(xb)
            rhs = jnp.concatenate([jnp.where(low, xb, zero), jnp.where(low, zero, xb)], axis=0)
            ydiag.append(jnp.dot(jnp.concatenate(wts, axis=1), rhs, preferred_element_type=F32))
        ydiag = jnp.concatenate(ydiag, axis=1)
        ea = jnp.concatenate(ea, axis=1)
        xwt = jnp.concatenate(xw, axis=0).astype(BF16)

        yoff = []
        for s in range(bs):
            st = st_ref[s, k] if carry else s0_ref[s, k]
            yoff.append(lax.dot_general(cg[s * q:(s + 1) * q, :], st.astype(BF16),
                                        (((1,), (1,)), ((), ())), preferred_element_type=F32))
            bsel = bg if bs == 1 else jnp.where(seq_of_row == s, bg, jnp.zeros_like(bg))
            upd = jnp.dot(xwt, bsel, preferred_element_type=F32)
            last = (s + 1) * q - 1
            dec = jnp.concatenate(
                [jnp.broadcast_to(jnp.exp2(a_cols[h][last:last + 1, :]), (HEAD_DIM, N_STATE))
                 for h in range(HEADS_PER_GROUP)], axis=0)
            new = st * dec + upd
            if carry:
                st_ref[s, k] = new
            else:
                sout_ref[s, k] = new
        yoff = yoff[0] if bs == 1 else jnp.concatenate(yoff, axis=0)

        lanes = slice(k * GROUP_W, (k + 1) * GROUP_W)
        y = ydiag + yoff * ea + dsk_ref[:, lanes] * xg
        gz = y * sz_ref[:, lanes]
        ms = jnp.mean(gz * gz, axis=-1, keepdims=True)
        yn_ref[:, lanes] = (gz * lax.rsqrt(ms + EPS) * nw_ref[:, lanes]).astype(BF16)

    if carry:
        @pl.when(c == nc - 1)
        def _():
            sout_ref[...] = st_ref[...]


def _ssd(sz, xbc, a_t, dt_t, d_skip_x, norm_w, state0,
         *, n_seq, bs, q, nc, gps, shared_init, tile0=0, out_rows=None, zero_tail_tiles=0,
         yn_into=None):
    rt = bs * q
    assert rt == ROW_TILE and N_GROUPS % gps == 0
    rows = n_seq * q * nc if out_rows is None else out_rows
    otile0 = 0 if out_rows is None else tile0
    nsb = n_seq // bs
    n_pad = pl.cdiv(zero_tail_tiles, nc)
    gw, gn, nh = gps * GROUP_W, gps * N_STATE, gps * HEADS_PER_GROUP
    static = dict(bs=bs, q=q, nc=nc, gps=gps)
    kern = (functools.partial(_ssd_kernel, n_real=nsb, **static) if n_pad
            else functools.partial(_ssd_body, **static))
    bb, bc_ = D_INNER // gn, (D_INNER + N_GROUPS * N_STATE) // gn

    def real(s):
        return jnp.minimum(s, nsb - 1) if n_pad else s

    def tile(s, c):
        return jnp.where(s < nsb, s * nc + c, nsb * nc - 1) if n_pad else s * nc + c

    def otile(s, c):
        if not n_pad:
            return otile0 + s * nc + c
        tail = jnp.minimum((s - nsb) * nc + c, zero_tail_tiles - 1)
        return otile0 + jnp.where(s < nsb, s * nc + c, nsb * nc + tail)

    sidx = (lambda s: 0) if shared_init else real
    in_specs = [
        pl.BlockSpec((rt, gw), lambda s, g, c: (tile0 + tile(s, c), g)),
        pl.BlockSpec((rt, gw), lambda s, g, c: (tile0 + tile(s, c), g)),
        pl.BlockSpec((rt, gn), lambda s, g, c: (tile0 + tile(s, c), bb + g)),
        pl.BlockSpec((rt, gn), lambda s, g, c: (tile0 + tile(s, c), bc_ + g)),
        pl.BlockSpec((1, nh, rt), lambda s, g, c: (tile(s, c), g, 0)),
        pl.BlockSpec((1, nh, rt), lambda s, g, c: (tile(s, c), g, 0)),
        pl.BlockSpec((1, gw), lambda s, g, c: (0, g)),
        pl.BlockSpec((1, gw), lambda s, g, c: (0, g)),
        pl.BlockSpec((bs, gps, GROUP_W, N_STATE), lambda s, g, c: (sidx(s), g, 0, 0)),
    ]
    out_specs = [
        pl.BlockSpec((rt, gw), lambda s, g, c: (otile(s, c), g)),
        pl.BlockSpec((bs, gps, GROUP_W, N_STATE), lambda s, g, c: (real(s), g, 0, 0)),
    ]
    operands = [sz, xbc, xbc, xbc, a_t, dt_t, d_skip_x, norm_w, state0]
    aliases = {}
    if yn_into is not None:
        assert yn_into.shape == (rows, D_INNER)
        aliases = {len(operands): 0}
        in_specs.append(pl.BlockSpec(memory_space=pl.ANY))
        operands.append(yn_into)
    st_shape = (bs, gps, GROUP_W, N_STATE) if nc > 1 else (1, 1, SUBLANES, N_STATE)
    return pl.pallas_call(
        kern,
        grid=(nsb + n_pad, N_GROUPS // gps, nc),
        in_specs=in_specs,
        out_specs=out_specs,
        out_shape=[
            jax.ShapeDtypeStruct((rows, D_INNER), BF16),
            jax.ShapeDtypeStruct((n_seq, N_GROUPS, GROUP_W, N_STATE), F32),
        ],
        scratch_shapes=[pltpu.VMEM(st_shape, F32)],
        input_output_aliases=aliases,
        compiler_params=pltpu.CompilerParams(
            dimension_semantics=("arbitrary", "arbitrary", "arbitrary"),
            vmem_limit_bytes=VMEM_LIMIT),
        name="ssd",
    )(*operands)


def _sconv_kernel(hs_ref, hm_ref, wb_ref, wc_ref, wh_ref, wz_ref, cw_ref, prev_ref,
                  v_ref, newp_ref, news_ref,
                  w_scr, halo_p, halo_s, meta_u,
                  *, tm, width, n_prompt_tiles, tiles_per_seq, bs_sample, q_sample):
    i = pl.program_id(1)
    keep = SC_CONV_W - 1

    @pl.when(i == 0)
    def _():
        for k, w_ref in enumerate((wb_ref, wc_ref, wh_ref, wz_ref)):
            w_scr[:, k * width:(k + 1) * width] = w_ref[...].T.astype(BF16)
        rm = jnp.dot(hm_ref[...], w_scr[:, width:3 * width], preferred_element_type=F32)
        meta_u[0] = (rm[:, :width] * rm[:, width:])[META - keep:]

    r = jnp.dot(hs_ref[...], w_scr[...], preferred_element_type=F32)
    u = r[:, width:2 * width] * r[:, 2 * width:3 * width]

    def finish(uc):
        v_ref[...] = (r[:, :width] * uc * _silu(r[:, 3 * width:])).astype(BF16)

    @pl.when(i < n_prompt_tiles)
    def _():
        finish(_conv_rows(u, halo_p, meta_u, cw_ref, first=(i % tiles_per_seq) == 0,
                          bs=1, q=tm, carry=True))
        newp_ref[0] = u[tm - keep:]

    @pl.when(i >= n_prompt_tiles)
    def _():
        finish(_conv_rows(u, halo_s, prev_ref, cw_ref, first=i >= n_prompt_tiles,
                          bs=bs_sample, q=q_sample, carry=False))
        news_ref[...] = u.reshape(bs_sample, q_sample, width)[:, q_sample - keep:, :]


def _sconv(hs, hm, w_t, cw, prev_s, *, n_prompt, seq, n_seq_p, n_seq_s, q_sample,
           tm=1024, width=256):
    rows = hs.shape[0]
    keep = SC_CONV_W - 1
    assert seq % tm == 0 and (rows - n_prompt) == tm and tm == n_seq_s * q_sample
    tiles_per_seq = seq // tm
    n_pt = n_prompt // tm
    kern = functools.partial(_sconv_kernel, tm=tm, width=width, n_prompt_tiles=n_pt,
                             tiles_per_seq=tiles_per_seq, bs_sample=n_seq_s, q_sample=q_sample)

    def w_rows(k):
        base = (W_SC + k * D_MODEL) // N_HEADS
        return lambda cbk, i: ((base + cbk * (width // N_HEADS)) * N_HEADS, 0)

    w_specs = [pl.BlockSpec((pl.Element(width), pl.Element(D_MODEL)), w_rows(k)) for k in range(4)]
    return pl.pallas_call(
        kern,
        grid=(D_MODEL // width, rows // tm),
        in_specs=[
            pl.BlockSpec((tm, D_MODEL), lambda cbk, i: (i, 0)),
            pl.BlockSpec((META, D_MODEL), lambda cbk, i: (0, 0)),
            *w_specs,
            pl.BlockSpec((SC_CONV_W, width), lambda cbk, i: (0, cbk)),
            pl.BlockSpec((n_seq_s, keep, width), lambda cbk, i: (0, 0, cbk)),
        ],
        out_specs=[
            pl.BlockSpec((tm, width), lambda cbk, i: (i, cbk)),
            pl.BlockSpec((1, keep, width),
                         lambda cbk, i: (jnp.minimum(i, n_pt - 1) // tiles_per_seq, 0, cbk)),
            pl.BlockSpec((n_seq_s, keep, width), lambda cbk, i: (0, 0, cbk)),
        ],
        out_shape=[
            jax.ShapeDtypeStruct((rows, D_MODEL), BF16),
            jax.ShapeDtypeStruct((n_seq_p, keep, D_MODEL), F32),
            jax.ShapeDtypeStruct((n_seq_s, keep, D_MODEL), F32),
        ],
        scratch_shapes=[
            pltpu.VMEM((D_MODEL, 4 * width), BF16),
            pltpu.VMEM((1, SUBLANES, width), F32),
            pltpu.VMEM((n_seq_s, SUBLANES, width), F32),
            pltpu.VMEM((1, keep, width), F32),
        ],
        compiler_params=pltpu.CompilerParams(
            dimension_semantics=("arbitrary", "arbitrary"),
            vmem_limit_bytes=VMEM_LIMIT),
        name="sconv",
    )(hs, hm, w_t, w_t, w_t, w_t, cw, prev_s)


def _merge_kernel(yn_ref, v_ref, ga_ref, gb_ref, wa_ref, wb_ref, o_ref, wa_scr, wb_scr):
    @pl.when(pl.program_id(1) == 0)
    def _():
        wa_scr[...] = wa_ref[...].astype(BF16)
        wb_scr[...] = wb_ref[...].astype(BF16)

    ya = jnp.dot(yn_ref[...], wa_scr[...], preferred_element_type=F32)
    yb = jnp.dot(v_ref[...], wb_scr[...], preferred_element_type=F32)
    o_ref[...] = (jax.nn.sigmoid(ga_ref[...]) * ya + jax.nn.sigmoid(gb_ref[...]) * yb).astype(BF16)


def _merge(yn, v, gates, wa, wb, *, tm=512, tn=512):
    rows = yn.shape[0]
    return pl.pallas_call(
        _merge_kernel,
        grid=(D_MODEL // tn, rows // tm),
        in_specs=[
            pl.BlockSpec((tm, D_INNER), lambda j, i: (i, 0)),
            pl.BlockSpec((tm, D_MODEL), lambda j, i: (i, 0)),
            pl.BlockSpec((tm, tn), lambda j, i: (i, j)),
            pl.BlockSpec((tm, tn), lambda j, i: (i, D_MODEL // tn + j)),
            pl.BlockSpec((D_INNER, tn), lambda j, i: (0, j)),
            pl.BlockSpec((D_MODEL, tn), lambda j, i: (0, j)),
        ],
        out_specs=pl.BlockSpec((tm, tn), lambda j, i: (i, j)),
        out_shape=jax.ShapeDtypeStruct((rows, D_MODEL), BF16),
        scratch_shapes=[pltpu.VMEM((D_INNER, tn), BF16), pltpu.VMEM((D_MODEL, tn), BF16)],
        compiler_params=pltpu.CompilerParams(
            dimension_semantics=("arbitrary", "arbitrary"),
            vmem_limit_bytes=VMEM_LIMIT),
        name="merge",
    )(yn, v, gates, gates, wa, wb)


def _outproj_kernel(m_ref, x_ref, wo_ref, fw_ref, o_ref):
    y = x_ref[...] + jnp.dot(m_ref[...], wo_ref[...], preferred_element_type=F32)
    ms = jnp.mean(y * y, axis=-1, keepdims=True)
    o_ref[...] = y * lax.rsqrt(ms + EPS) * fw_ref[...]


def _outproj(m, x, wo, fw, *, row0, tm=512):
    rows = x.shape[0]
    assert row0 % tm == 0 and rows % tm == 0
    t0 = row0 // tm
    return pl.pallas_call(
        _outproj_kernel,
        grid=(rows // tm,),
        in_specs=[
            pl.BlockSpec((tm, D_MODEL), lambda i: (t0 + i, 0)),
            pl.BlockSpec((tm, D_MODEL), lambda i: (i, 0)),
            pl.BlockSpec((D_MODEL, D_MODEL), lambda i: (0, 0), pipeline_mode=pl.Buffered(1)),
            pl.BlockSpec((1, D_MODEL), lambda i: (0, 0)),
        ],
        out_specs=pl.BlockSpec((tm, D_MODEL), lambda i: (i, 0)),
        out_shape=jax.ShapeDtypeStruct((rows, D_MODEL), F32),
        compiler_params=pltpu.CompilerParams(
            dimension_semantics=("arbitrary",),
            vmem_limit_bytes=VMEM_LIMIT),
        name="outproj",
    )(m, x, wo, fw)


def kernel(x_prompt, x_sample, state_ssd_conv, state_ssm, state_sconv, meta_tokens, norm_w,
           w_in, ssd_conv_w, ssd_conv_b, dt_bias, a_log, d_skip, ssd_norm_w, w_ssd_out,
           sconv_w, w_sconv_out, w_o, final_norm_w):
    bp, seq = x_prompt.shape[0], x_prompt.shape[1]
    bd, dec_seq = x_sample.shape[0], x_sample.shape[1]

    w_t = jnp.transpose(w_in[0])
    nw = norm_w[0].reshape(1, D_MODEL)
    fw = final_norm_w.reshape(1, D_MODEL)
    conv_w = ssd_conv_w[0]
    conv_b = ssd_conv_b[0].reshape(1, CONV_DIM)
    dtb = jnp.pad(dt_bias[0], (0, LANES - N_HEADS)).reshape(1, LANES)
    alog = jnp.pad(a_log[0], (0, LANES - N_HEADS)).reshape(1, LANES)
    dsk = jnp.repeat(d_skip[0], HEAD_DIM).reshape(1, D_INNER)
    gnw = ssd_norm_w[0].reshape(1, D_INNER)
    scw = sconv_w[0]

    xp = x_prompt.reshape(bp * seq, D_MODEL)
    xs = x_sample.reshape(bd * dec_seq, D_MODEL)
    n_p, n_s = bp * seq, bd * dec_seq
    streams = dict(n_prompt=n_p, seq=seq, n_seq_p=bp, n_seq_s=bd, q_sample=dec_seq)

    hs, hm = _norm(xp, xs, meta_tokens, nw)
    sz = _proj(hs, w_t, w_row0=0, ncols=D_INNER, silu=True)
    gates = _proj(hs, w_t, w_row0=W_GATE, ncols=2 * D_MODEL, silu=False)
    xbc, xbc_m, conv_p, conv_s = _xbc(hs, hm, w_t, conv_w, conv_b, state_ssd_conv[0], **streams)
    v_all, sc_p, sc_s = _sconv(hs, hm, w_t, scw, state_sconv[0], **streams)
    headscal = functools.partial(_headscal, w_t=w_t, dt_bias=dtb, a_log=alog)
    ssd = functools.partial(_ssd, d_skip_x=dsk, norm_w=gnw)

    at_m, dtt_m = headscal(hm, q=ROW_TILE, valid=META, tile0=0, ntiles=1)
    _, ssm_m = ssd(jnp.zeros((ROW_TILE, D_INNER), F32), xbc_m, at_m, dtt_m,
                   state0=jnp.zeros((1, N_GROUPS, GROUP_W, N_STATE), F32),
                   n_seq=1, bs=1, q=ROW_TILE, nc=1, gps=2, shared_init=False)

    at_p, dtt_p = headscal(hs, q=ROW_TILE, valid=ROW_TILE, tile0=0, ntiles=n_p // ROW_TILE)
    yn, ssm_p = ssd(sz, xbc, at_p, dtt_p, state0=ssm_m, n_seq=bp, bs=1, q=ROW_TILE,
                    nc=seq // ROW_TILE, gps=8, shared_init=True, out_rows=n_p + n_s,
                    zero_tail_tiles=n_s // ROW_TILE)

    sbs = ROW_TILE // dec_seq
    at_s, dtt_s = headscal(hs, q=dec_seq, valid=ROW_TILE, tile0=n_p // ROW_TILE,
                           ntiles=n_s // ROW_TILE)
    yn, ssm_s = ssd(sz, xbc, at_s, dtt_s,
                    state0=state_ssm[0].reshape(bd, N_GROUPS, GROUP_W, N_STATE),
                    n_seq=bd, bs=sbs, q=dec_seq, nc=1, gps=2, shared_init=False,
                    tile0=n_p // ROW_TILE, out_rows=n_p + n_s, yn_into=yn)

    merged = _merge(yn, v_all, gates, w_ssd_out[0], w_sconv_out[0])
    wo = w_o[0].astype(BF16)
    y_p = _outproj(merged, xp, wo, fw, row0=0)
    y_s = _outproj(merged, xs, wo, fw, row0=n_p)

    return (y_p.reshape(bp, seq, D_MODEL),
            y_s.reshape(bd, dec_seq, D_MODEL),
            conv_p[None],
            ssm_p.reshape(1, bp, N_HEADS, HEAD_DIM, N_STATE),
            sc_p[None],
            conv_s[None],
            ssm_s.reshape(1, bd, N_HEADS, HEAD_DIM, N_STATE),
            sc_s[None])
```

```python
import functools

import jax
import jax.numpy as jnp
from jax import lax
from jax.experimental import pallas as pl
from jax.experimental.pallas import tpu as pltpu

F32 = jnp.float32
BF16 = jnp.bfloat16

D_MODEL = 2048
D_INNER = 4096
N_HEADS = 64
HEAD_DIM = 64
N_STATE = 128
N_GROUPS = 8
GROUP_W = D_INNER // N_GROUPS
HEADS_PER_GROUP = N_HEADS // N_GROUPS
CONV_DIM = D_INNER + 2 * N_GROUPS * N_STATE
SSD_CONV_W = 4
SC_CONV_W = 3
META = 16
EPS = 1e-6
LOG2_E = 1.4426950408889634

LANES = 128
SUBLANES = 8
ROW_TILE = 128
W_DT = D_INNER + CONV_DIM
W_SC = W_DT + N_HEADS
W_GATE = W_SC + 4 * D_MODEL

VMEM_LIMIT = 52 * 1024 * 1024
NORM_ROWS = 512
PROJ_ROWS, PROJ_COLS = 1024, 1024
XBC_COLS = 512
SCONV_CHANNELS = 256
MERGE_ROWS, MERGE_COLS = 512, 512
OUTPROJ_ROWS = 512
HEADSCAL_TILES = 8
GROUPS_PER_STEP_PROMPT = N_GROUPS
GROUPS_PER_STEP_SAMPLE = 2
GROUPS_PER_STEP_META = 2


def _sigmoid(x):
    return 0.5 * (1.0 + jnp.tanh(0.5 * x))


def _silu(x):
    h = 0.5 * x
    return h * (1.0 + jnp.tanh(h))


def _rms_bf16(x, w):
    ms = jnp.mean(x * x, axis=-1, keepdims=True)
    return (x * lax.rsqrt(ms + EPS) * w).astype(BF16)


def _norm_kernel(xp_ref, xs_ref, xm_ref, nw_ref, hs_ref, hm_ref, *, n_prompt):
    i = pl.program_id(0)

    @pl.when(i < n_prompt)
    def _():
        hs_ref[...] = _rms_bf16(xp_ref[...], nw_ref[...])

    @pl.when(i >= n_prompt)
    def _():
        hs_ref[...] = _rms_bf16(xs_ref[...], nw_ref[...])

    @pl.when(i == 0)
    def _():
        hm_ref[:META, :] = _rms_bf16(xm_ref[...], nw_ref[...])
        hm_ref[META:, :] = jnp.zeros((ROW_TILE - META, D_MODEL), BF16)


def _norm(xp, xs, xm, norm_w, *, tm=NORM_ROWS):
    n_p, n_s = xp.shape[0] // tm, xs.shape[0] // tm
    kern = functools.partial(_norm_kernel, n_prompt=n_p)
    return pl.pallas_call(
        kern,
        grid=(n_p + n_s,),
        in_specs=[
            pl.BlockSpec((tm, D_MODEL), lambda i: (jnp.minimum(i, n_p - 1), 0)),
            pl.BlockSpec((tm, D_MODEL), lambda i: (jnp.maximum(i - n_p, 0), 0)),
            pl.BlockSpec((META, D_MODEL), lambda i: (0, 0)),
            pl.BlockSpec((1, D_MODEL), lambda i: (0, 0)),
        ],
        out_specs=[
            pl.BlockSpec((tm, D_MODEL), lambda i: (i, 0)),
            pl.BlockSpec((ROW_TILE, D_MODEL), lambda i: (0, 0)),
        ],
        out_shape=[
            jax.ShapeDtypeStruct((xp.shape[0] + xs.shape[0], D_MODEL), BF16),
            jax.ShapeDtypeStruct((ROW_TILE, D_MODEL), BF16),
        ],
        compiler_params=pltpu.CompilerParams(
            dimension_semantics=("arbitrary",), vmem_limit_bytes=VMEM_LIMIT),
        name="norm",
    )(xp, xs, xm, norm_w)


def _proj_kernel(hs_ref, wt_ref, o_ref, wb_ref, *, silu):
    @pl.when(pl.program_id(1) == 0)
    def _():
        wb_ref[...] = wt_ref[...].T.astype(BF16)

    r = jnp.dot(hs_ref[...], wb_ref[...], preferred_element_type=F32)
    o_ref[...] = _silu(r) if silu else r


def _proj(hs, w_t, *, w_row0, ncols, silu, tm=PROJ_ROWS, tn=PROJ_COLS):
    rows = hs.shape[0]
    assert w_row0 % N_HEADS == 0 and ncols % tn == 0 and tn % N_HEADS == 0

    def w_rows(j, i):
        return ((w_row0 // N_HEADS + j * (tn // N_HEADS)) * N_HEADS, 0)

    return pl.pallas_call(
        functools.partial(_proj_kernel, silu=silu),
        grid=(ncols // tn, rows // tm),
        in_specs=[
            pl.BlockSpec((tm, D_MODEL), lambda j, i: (i, 0)),
            pl.BlockSpec((pl.Element(tn), pl.Element(D_MODEL)), w_rows),
        ],
        out_specs=pl.BlockSpec((tm, tn), lambda j, i: (i, j)),
        out_shape=jax.ShapeDtypeStruct((rows, ncols), F32),
        scratch_shapes=[pltpu.VMEM((D_MODEL, tn), BF16)],
        compiler_params=pltpu.CompilerParams(
            dimension_semantics=("arbitrary", "arbitrary"),
            vmem_limit_bytes=VMEM_LIMIT),
        name="proj",
    )(hs, w_t)


def _conv_rows(x, halo_ref, prev_ref, w_ref, *, first, bs, q, carry):
    taps = w_ref.shape[0]
    rt, width = x.shape

    @pl.when(first)
    def _():
        halo_ref[:, SUBLANES - (taps - 1):, :] = prev_ref[...]

    prev = halo_ref[...]
    acc = None
    if bs == 1:
        row = lax.broadcasted_iota(jnp.int32, (SUBLANES, width), 0)
        for s in range(taps - 1, 0, -1):
            rolled = pltpu.roll(x, s, 0)
            head = jnp.where(row < s, pltpu.roll(prev[0], s, 0), rolled[:SUBLANES])
            term = jnp.concatenate([head, rolled[SUBLANES:]], axis=0) * w_ref[taps - 1 - s:taps - s, :]
            acc = term if acc is None else acc + term
        acc = acc + x * w_ref[taps - 1:taps, :]
        if carry:
            halo_ref[0] = x[rt - SUBLANES:, :]
        return acc
    assert q == SUBLANES and not carry
    x3 = x.reshape(bs, q, width)
    row = lax.broadcasted_iota(jnp.int32, x3.shape, 1)
    for s in range(taps - 1, 0, -1):
        shifted = jnp.where(row < s, pltpu.roll(prev, s, 1), pltpu.roll(x3, s, 1))
        term = shifted * w_ref[taps - 1 - s:taps - s, :]
        acc = term if acc is None else acc + term
    acc = acc + x3 * w_ref[taps - 1:taps, :]
    return acc.reshape(rt, width)


def _seg_cumsum(a, q):
    pos = lax.broadcasted_iota(jnp.int32, a.shape, 0) & (q - 1)
    s = 1
    while s < q:
        shifted = pltpu.roll(a, s, 0)
        a = a + jnp.where(pos >= s, shifted, 0.0)
        s *= 2
    return a


def _headscal_kernel(hs_ref, wdt_ref, dtb_ref, alog_ref, at_ref, bt_ref, *, q, valid, tps):
    dtr = lax.dot_general(hs_ref[...], wdt_ref[...].astype(BF16), (((1,), (1,)), ((), ())),
                          preferred_element_type=F32)
    dtv = jax.nn.softplus(dtr + dtb_ref[...])
    if valid < ROW_TILE:
        rows = lax.broadcasted_iota(jnp.int32, dtv.shape, 0)
        dtv = jnp.where(rows < valid, dtv, 0.0)
    acum = _seg_cumsum(dtv * (-jnp.exp(alog_ref[...])), q) * LOG2_E
    a_minus_logdt = acum - jnp.log(dtv) * LOG2_E
    for t in range(tps):
        at_ref[t] = acum[t * ROW_TILE:(t + 1) * ROW_TILE].T
        bt_ref[t] = a_minus_logdt[t * ROW_TILE:(t + 1) * ROW_TILE].T


def _headscal(hs, w_t, dt_bias, a_log, *, q, valid, tile0, ntiles):
    tps = min(ntiles, HEADSCAL_TILES)
    assert ntiles % tps == 0 and tile0 % tps == 0 and q <= ROW_TILE
    kern = functools.partial(_headscal_kernel, q=q, valid=valid, tps=tps)
    shape = jax.ShapeDtypeStruct((ntiles, LANES, ROW_TILE), F32)
    dt_blk = W_DT // LANES
    return pl.pallas_call(
        kern,
        grid=(ntiles // tps,),
        in_specs=[
            pl.BlockSpec((tps * ROW_TILE, D_MODEL), lambda t: (tile0 // tps + t, 0)),
            pl.BlockSpec((LANES, D_MODEL), lambda t: (dt_blk, 0)),
            pl.BlockSpec((1, LANES), lambda t: (0, 0)),
            pl.BlockSpec((1, LANES), lambda t: (0, 0)),
        ],
        out_specs=[
            pl.BlockSpec((tps, LANES, ROW_TILE), lambda t: (t, 0, 0)),
            pl.BlockSpec((tps, LANES, ROW_TILE), lambda t: (t, 0, 0)),
        ],
        out_shape=[shape, shape],
        compiler_params=pltpu.CompilerParams(dimension_semantics=("arbitrary",)),
        name="headscal",
    )(hs, w_t, dt_bias, a_log)


def _xbc_kernel(hs_ref, hm_ref, wt_ref, cw_ref, cb_ref, prev_ref,
                o_ref, om_ref, cnp_ref, cns_ref,
                wb_scr, halo_p, halo_s, halo_m, meta_prev, zero_prev,
                *, tm, n_prompt_tiles, tiles_per_seq, bs_sample, q_sample):
    i = pl.program_id(1)
    keep = SSD_CONV_W - 1
    tn = o_ref.shape[1]

    @pl.when(i == 0)
    def _():
        wb_scr[...] = wt_ref[...].T.astype(BF16)

    raw = jnp.dot(hs_ref[...], wb_scr[...], preferred_element_type=F32)

    def activate(conv):
        return _silu(conv + cb_ref[...])

    @pl.when(i == 0)
    def _():
        raw_m = jnp.dot(hm_ref[...], wb_scr[...], preferred_element_type=F32)
        meta_prev[0] = raw_m[META - keep:]
        zero_prev[...] = jnp.zeros(zero_prev.shape, F32)
        conv_m = _conv_rows(raw_m, halo_m, zero_prev, cw_ref, first=i == 0, bs=1, q=META,
                            carry=False)
        om_ref[:META, :] = activate(conv_m)
        om_ref[META:, :] = jnp.zeros((ROW_TILE - META, tn), F32)

    @pl.when(i < n_prompt_tiles)
    def _():
        o_ref[...] = activate(_conv_rows(raw, halo_p, meta_prev, cw_ref,
                                         first=(i % tiles_per_seq) == 0, bs=1, q=tm, carry=True))
        cnp_ref[0] = raw[tm - keep:]

    @pl.when(i >= n_prompt_tiles)
    def _():
        o_ref[...] = activate(_conv_rows(raw, halo_s, prev_ref, cw_ref, first=i >= n_prompt_tiles,
                                         bs=bs_sample, q=q_sample, carry=False))
        cns_ref[...] = raw.reshape(bs_sample, q_sample, tn)[:, q_sample - keep:, :]


def _xbc(hs, hm, w_t, cw, cb, prev_s, *, n_prompt, seq, n_seq_p, n_seq_s, q_sample,
         tm=PROJ_ROWS, tn=XBC_COLS):
    rows = hs.shape[0]
    keep = SSD_CONV_W - 1
    assert seq % tm == 0 and (rows - n_prompt) == tm and tm == n_seq_s * q_sample
    assert D_INNER % tn == 0 and CONV_DIM % tn == 0
    tiles_per_seq = seq // tm
    n_pt = n_prompt // tm
    kern = functools.partial(_xbc_kernel, tm=tm, n_prompt_tiles=n_pt, tiles_per_seq=tiles_per_seq,
                             bs_sample=n_seq_s, q_sample=q_sample)
    return pl.pallas_call(
        kern,
        grid=(CONV_DIM // tn, rows // tm),
        in_specs=[
            pl.BlockSpec((tm, D_MODEL), lambda j, i: (i, 0)),
            pl.BlockSpec((META, D_MODEL), lambda j, i: (0, 0)),
            pl.BlockSpec((tn, D_MODEL), lambda j, i: (D_INNER // tn + j, 0)),
            pl.BlockSpec((SSD_CONV_W, tn), lambda j, i: (0, j)),
            pl.BlockSpec((1, tn), lambda j, i: (0, j)),
            pl.BlockSpec((n_seq_s, keep, tn), lambda j, i: (0, 0, j)),
        ],
        out_specs=[
            pl.BlockSpec((tm, tn), lambda j, i: (i, j)),
            pl.BlockSpec((ROW_TILE, tn), lambda j, i: (0, j)),
            pl.BlockSpec((1, keep, tn),
                         lambda j, i: (jnp.minimum(i, n_pt - 1) // tiles_per_seq, 0, j)),
            pl.BlockSpec((n_seq_s, keep, tn), lambda j, i: (0, 0, j)),
        ],
        out_shape=[
            jax.ShapeDtypeStruct((rows, CONV_DIM), F32),
            jax.ShapeDtypeStruct((ROW_TILE, CONV_DIM), F32),
            jax.ShapeDtypeStruct((n_seq_p, keep, CONV_DIM), F32),
            jax.ShapeDtypeStruct((n_seq_s, keep, CONV_DIM), F32),
        ],
        scratch_shapes=[
            pltpu.VMEM((D_MODEL, tn), BF16),
            pltpu.VMEM((1, SUBLANES, tn), F32),
            pltpu.VMEM((n_seq_s, SUBLANES, tn), F32),
            pltpu.VMEM((1, SUBLANES, tn), F32),
            pltpu.VMEM((1, keep, tn), F32),
            pltpu.VMEM((1, keep, tn), F32),
        ],
        compiler_params=pltpu.CompilerParams(
            dimension_semantics=("arbitrary", "arbitrary"),
            vmem_limit_bytes=VMEM_LIMIT),
        name="xbc",
    )(hs, hm, w_t, cw, cb, prev_s)


def _ssd_kernel(*refs, n_real, **static):
    yn_ref = refs[-3]
    s = pl.program_id(0)

    @pl.when(s < n_real)
    def _():
        _ssd_body(*refs, **static)

    @pl.when(s >= n_real)
    def _():
        yn_ref[...] = jnp.zeros(yn_ref.shape, yn_ref.dtype)


def _ssd_body(sz_ref, x_ref, b_ref, c_ref, at_ref, bt_ref, dsk_ref, nw_ref, s0_ref,
              *rest, bs, q, nc, gps):
    yn_ref, sout_ref, st_ref = rest[-3:]
    rt = bs * q
    nh = gps * HEADS_PER_GROUP
    c = pl.program_id(2)
    carry = nc > 1

    if carry:
        @pl.when(c == 0)
        def _():
            st_ref[...] = s0_ref[...]

    xc = x_ref[...]
    bcb = b_ref[...].astype(BF16)
    ccb = c_ref[...].astype(BF16)

    a_t = at_ref[0]
    b_t = bt_ref[0]
    cols = jnp.concatenate([a_t, jnp.zeros((LANES - nh, rt), F32)], axis=0).T

    pos = lax.broadcasted_iota(jnp.int32, (nh, rt), 1) & (q - 1)
    a_end = a_t
    s = 1
    while s < q:
        a_end = jnp.where(pos + s < q, pltpu.roll(a_end, rt - s, 1), a_end)
        s *= 2
    to_end = jnp.exp2(a_end - b_t)

    nblk = rt // SUBLANES
    ri = lax.broadcasted_iota(jnp.int32, (nblk, SUBLANES, rt), 0) * SUBLANES + \
        lax.broadcasted_iota(jnp.int32, (nblk, SUBLANES, rt), 1)
    ci = lax.broadcasted_iota(jnp.int32, (nblk, SUBLANES, rt), 2)
    mask = (ri >= ci) & ((ri // q) == (ci // q))
    low = lax.broadcasted_iota(jnp.int32, (rt, LANES), 1) < HEAD_DIM
    seq_of_row = lax.broadcasted_iota(jnp.int32, (rt, N_STATE), 0) // q

    for k in range(gps):
        xg = xc[:, k * GROUP_W:(k + 1) * GROUP_W]
        bg = bcb[:, k * N_STATE:(k + 1) * N_STATE]
        cg = ccb[:, k * N_STATE:(k + 1) * N_STATE]
        cb = lax.dot_general(cg, bg, (((1,), (1,)), ((), ())), preferred_element_type=F32)
        cb3 = cb.reshape(nblk, SUBLANES, rt)
        xt = xg.T

        ydiag, ea, xw, a_cols = [], [], [], []
        for pr in range(HEADS_PER_GROUP // 2):
            wts, ab = [], []
            for hh in range(2):
                h = k * HEADS_PER_GROUP + 2 * pr + hh
                a_col = jnp.broadcast_to(cols[:, h:h + 1], (rt, LANES))
                b_row = jnp.broadcast_to(b_t[h:h + 1, :], (SUBLANES, rt))
                seg = jnp.where(mask, a_col.reshape(nblk, SUBLANES, rt) - b_row[None], -jnp.inf)
                wts.append((cb3 * jnp.exp2(seg)).reshape(rt, rt).astype(BF16))
                ab.append(a_col)
                rows = slice((2 * pr + hh) * HEAD_DIM, (2 * pr + hh + 1) * HEAD_DIM)
                xw.append(xt[rows] * to_end[h:h + 1, :])
            a_cols += ab
            ea.append(jnp.exp2(jnp.where(low, ab[0], ab[1])))
            xb = xg[:, pr * LANES:(pr + 1) * LANES].astype(BF16)
            zero = jnp.zeros_like(xb)
            rhs = jnp.concatenate([jnp.where(low, xb, zero), jnp.where(low, zero, xb)], axis=0)
            ydiag.append(jnp.dot(jnp.concatenate(wts, axis=1), rhs, preferred_element_type=F32))
        ydiag = jnp.concatenate(ydiag, axis=1)
        ea = jnp.concatenate(ea, axis=1)
        xwt = jnp.concatenate(xw, axis=0).astype(BF16)

        yoff = []
        for s in range(bs):
            st = st_ref[s, k] if carry else s0_ref[s, k]
            yoff.append(lax.dot_general(cg[s * q:(s + 1) * q, :], st.astype(BF16),
                                        (((1,), (1,)), ((), ())), preferred_element_type=F32))
            bsel = bg if bs == 1 else jnp.where(seq_of_row == s, bg, jnp.zeros_like(bg))
            upd = jnp.dot(xwt, bsel, preferred_element_type=F32)
            last = (s + 1) * q - 1
            dec = jnp.concatenate(
                [jnp.broadcast_to(jnp.exp2(a_cols[h][last:last + 1, :]), (HEAD_DIM, N_STATE))
                 for h in range(HEADS_PER_GROUP)], axis=0)
            new = st * dec + upd
            if carry:
                st_ref[s, k] = new
            else:
                sout_ref[s, k] = new
        yoff = yoff[0] if bs == 1 else jnp.concatenate(yoff, axis=0)

        lanes = slice(k * GROUP_W, (k + 1) * GROUP_W)
        y = ydiag + yoff * ea + dsk_ref[:, lanes] * xg
        gz = y * sz_ref[:, lanes]
        ms = jnp.mean(gz * gz, axis=-1, keepdims=True)
        yn_ref[:, lanes] = (gz * lax.rsqrt(ms + EPS) * nw_ref[:, lanes]).astype(BF16)

    if carry:
        @pl.when(c == nc - 1)
        def _():
            sout_ref[...] = st_ref[...]


def _ssd(sz, xbc, a_t, dt_t, d_skip_x, norm_w, state0,
         *, n_seq, bs, q, nc, gps, shared_init, tile0=0, out_rows=None, zero_tail_tiles=0,
         yn_into=None):
    rt = bs * q
    assert rt == ROW_TILE and N_GROUPS % gps == 0
    rows = n_seq * q * nc if out_rows is None else out_rows
    otile0 = 0 if out_rows is None else tile0
    nsb = n_seq // bs
    n_pad = pl.cdiv(zero_tail_tiles, nc)
    gw, gn, nh = gps * GROUP_W, gps * N_STATE, gps * HEADS_PER_GROUP
    static = dict(bs=bs, q=q, nc=nc, gps=gps)
    kern = (functools.partial(_ssd_kernel, n_real=nsb, **static) if n_pad
            else functools.partial(_ssd_body, **static))
    bb, bc_ = D_INNER // gn, (D_INNER + N_GROUPS * N_STATE) // gn

    def real(s):
        return jnp.minimum(s, nsb - 1) if n_pad else s

    def tile(s, c):
        return jnp.where(s < nsb, s * nc + c, nsb * nc - 1) if n_pad else s * nc + c

    def otile(s, c):
        if not n_pad:
            return otile0 + s * nc + c
        tail = jnp.minimum((s - nsb) * nc + c, zero_tail_tiles - 1)
        return otile0 + jnp.where(s < nsb, s * nc + c, nsb * nc + tail)

    sidx = (lambda s: 0) if shared_init else real
    in_specs = [
        pl.BlockSpec((rt, gw), lambda s, g, c: (tile0 + tile(s, c), g)),
        pl.BlockSpec((rt, gw), lambda s, g, c: (tile0 + tile(s, c), g)),
        pl.BlockSpec((rt, gn), lambda s, g, c: (tile0 + tile(s, c), bb + g)),
        pl.BlockSpec((rt, gn), lambda s, g, c: (tile0 + tile(s, c), bc_ + g)),
        pl.BlockSpec((1, nh, rt), lambda s, g, c: (tile(s, c), g, 0)),
        pl.BlockSpec((1, nh, rt), lambda s, g, c: (tile(s, c), g, 0)),
        pl.BlockSpec((1, gw), lambda s, g, c: (0, g)),
        pl.BlockSpec((1, gw), lambda s, g, c: (0, g)),
        pl.BlockSpec((bs, gps, GROUP_W, N_STATE), lambda s, g, c: (sidx(s), g, 0, 0)),
    ]
    out_specs = [
        pl.BlockSpec((rt, gw), lambda s, g, c: (otile(s, c), g)),
        pl.BlockSpec((bs, gps, GROUP_W, N_STATE), lambda s, g, c: (real(s), g, 0, 0)),
    ]
    operands = [sz, xbc, xbc, xbc, a_t, dt_t, d_skip_x, norm_w, state0]
    aliases = {}
    if yn_into is not None:
        assert yn_into.shape == (rows, D_INNER)
        aliases = {len(operands): 0}
        in_specs.append(pl.BlockSpec(memory_space=pl.ANY))
        operands.append(yn_into)
    st_shape = (bs, gps, GROUP_W, N_STATE) if nc > 1 else (1, 1, SUBLANES, N_STATE)
    return pl.pallas_call(
        kern,
        grid=(nsb + n_pad, N_GROUPS // gps, nc),
        in_specs=in_specs,
        out_specs=out_specs,
        out_shape=[
            jax.ShapeDtypeStruct((rows, D_INNER), BF16),
            jax.ShapeDtypeStruct((n_seq, N_GROUPS, GROUP_W, N_STATE), F32),
        ],
        scratch_shapes=[pltpu.VMEM(st_shape, F32)],
        input_output_aliases=aliases,
        compiler_params=pltpu.CompilerParams(
            dimension_semantics=("arbitrary", "arbitrary", "arbitrary"),
            vmem_limit_bytes=VMEM_LIMIT),
        name="ssd",
    )(*operands)


def _sconv_kernel(hs_ref, hm_ref, wb_ref, wc_ref, wh_ref, wz_ref, cw_ref, prev_ref,
                  v_ref, newp_ref, news_ref,
                  w_scr, halo_p, halo_s, meta_u,
                  *, tm, width, n_prompt_tiles, tiles_per_seq, bs_sample, q_sample):
    i = pl.program_id(1)
    keep = SC_CONV_W - 1

    @pl.when(i == 0)
    def _():
        for k, w_ref in enumerate((wb_ref, wc_ref, wh_ref, wz_ref)):
            w_scr[:, k * width:(k + 1) * width] = w_ref[...].T.astype(BF16)
        rm = jnp.dot(hm_ref[...], w_scr[:, width:3 * width], preferred_element_type=F32)
        meta_u[0] = (rm[:, :width] * rm[:, width:])[META - keep:]

    r = jnp.dot(hs_ref[...], w_scr[...], preferred_element_type=F32)
    u = r[:, width:2 * width] * r[:, 2 * width:3 * width]

    def finish(uc):
        v_ref[...] = (r[:, :width] * uc * _silu(r[:, 3 * width:])).astype(BF16)

    @pl.when(i < n_prompt_tiles)
    def _():
        finish(_conv_rows(u, halo_p, meta_u, cw_ref, first=(i % tiles_per_seq) == 0,
                          bs=1, q=tm, carry=True))
        newp_ref[0] = u[tm - keep:]

    @pl.when(i >= n_prompt_tiles)
    def _():
        finish(_conv_rows(u, halo_s, prev_ref, cw_ref, first=i >= n_prompt_tiles,
                          bs=bs_sample, q=q_sample, carry=False))
        news_ref[...] = u.reshape(bs_sample, q_sample, width)[:, q_sample - keep:, :]


def _sconv(hs, hm, w_t, cw, prev_s, *, n_prompt, seq, n_seq_p, n_seq_s, q_sample,
           tm=PROJ_ROWS, width=SCONV_CHANNELS):
    rows = hs.shape[0]
    keep = SC_CONV_W - 1
    assert seq % tm == 0 and (rows - n_prompt) == tm and tm == n_seq_s * q_sample
    tiles_per_seq = seq // tm
    n_pt = n_prompt // tm
    kern = functools.partial(_sconv_kernel, tm=tm, width=width, n_prompt_tiles=n_pt,
                             tiles_per_seq=tiles_per_seq, bs_sample=n_seq_s, q_sample=q_sample)

    def w_rows(k):
        base = (W_SC + k * D_MODEL) // N_HEADS
        return lambda cbk, i: ((base + cbk * (width // N_HEADS)) * N_HEADS, 0)

    w_specs = [pl.BlockSpec((pl.Element(width), pl.Element(D_MODEL)), w_rows(k)) for k in range(4)]
    return pl.pallas_call(
        kern,
        grid=(D_MODEL // width, rows // tm),
        in_specs=[
            pl.BlockSpec((tm, D_MODEL), lambda cbk, i: (i, 0)),
            pl.BlockSpec((META, D_MODEL), lambda cbk, i: (0, 0)),
            *w_specs,
            pl.BlockSpec((SC_CONV_W, width), lambda cbk, i: (0, cbk)),
            pl.BlockSpec((n_seq_s, keep, width), lambda cbk, i: (0, 0, cbk)),
        ],
        out_specs=[
            pl.BlockSpec((tm, width), lambda cbk, i: (i, cbk)),
            pl.BlockSpec((1, keep, width),
                         lambda cbk, i: (jnp.minimum(i, n_pt - 1) // tiles_per_seq, 0, cbk)),
            pl.BlockSpec((n_seq_s, keep, width), lambda cbk, i: (0, 0, cbk)),
        ],
        out_shape=[
            jax.ShapeDtypeStruct((rows, D_MODEL), BF16),
            jax.ShapeDtypeStruct((n_seq_p, keep, D_MODEL), F32),
            jax.ShapeDtypeStruct((n_seq_s, keep, D_MODEL), F32),
        ],
        scratch_shapes=[
            pltpu.VMEM((D_MODEL, 4 * width), BF16),
            pltpu.VMEM((1, SUBLANES, width), F32),
            pltpu.VMEM((n_seq_s, SUBLANES, width), F32),
            pltpu.VMEM((1, keep, width), F32),
        ],
        compiler_params=pltpu.CompilerParams(
            dimension_semantics=("arbitrary", "arbitrary"),
            vmem_limit_bytes=VMEM_LIMIT),
        name="sconv",
    )(hs, hm, w_t, w_t, w_t, w_t, cw, prev_s)


def _merge_kernel(yn_ref, v_ref, ga_ref, gb_ref, wa_ref, wb_ref, o_ref, wa_scr, wb_scr):
    @pl.when(pl.program_id(1) == 0)
    def _():
        wa_scr[...] = wa_ref[...].astype(BF16)
        wb_scr[...] = wb_ref[...].astype(BF16)

    ya = jnp.dot(yn_ref[...], wa_scr[...], preferred_element_type=F32)
    yb = jnp.dot(v_ref[...], wb_scr[...], preferred_element_type=F32)
    o_ref[...] = (_sigmoid(ga_ref[...]) * ya + _sigmoid(gb_ref[...]) * yb).astype(BF16)


def _merge(yn, v, gates, wa, wb, *, tm=MERGE_ROWS, tn=MERGE_COLS):
    rows = yn.shape[0]
    return pl.pallas_call(
        _merge_kernel,
        grid=(D_MODEL // tn, rows // tm),
        in_specs=[
            pl.BlockSpec((tm, D_INNER), lambda j, i: (i, 0)),
            pl.BlockSpec((tm, D_MODEL), lambda j, i: (i, 0)),
            pl.BlockSpec((tm, tn), lambda j, i: (i, j)),
            pl.BlockSpec((tm, tn), lambda j, i: (i, D_MODEL // tn + j)),
            pl.BlockSpec((D_INNER, tn), lambda j, i: (0, j)),
            pl.BlockSpec((D_MODEL, tn), lambda j, i: (0, j)),
        ],
        out_specs=pl.BlockSpec((tm, tn), lambda j, i: (i, j)),
        out_shape=jax.ShapeDtypeStruct((rows, D_MODEL), BF16),
        scratch_shapes=[pltpu.VMEM((D_INNER, tn), BF16), pltpu.VMEM((D_MODEL, tn), BF16)],
        compiler_params=pltpu.CompilerParams(
            dimension_semantics=("arbitrary", "arbitrary"),
            vmem_limit_bytes=VMEM_LIMIT),
        name="merge",
    )(yn, v, gates, gates, wa, wb)


def _outproj_kernel(m_ref, x_ref, wo_ref, fw_ref, o_ref):
    y = x_ref[...] + jnp.dot(m_ref[...], wo_ref[...], preferred_element_type=F32)
    ms = jnp.mean(y * y, axis=-1, keepdims=True)
    o_ref[...] = y * lax.rsqrt(ms + EPS) * fw_ref[...]


def _outproj(m, x, wo, fw, *, row0, tm=OUTPROJ_ROWS):
    rows = x.shape[0]
    assert row0 % tm == 0 and rows % tm == 0
    t0 = row0 // tm
    return pl.pallas_call(
        _outproj_kernel,
        grid=(rows // tm,),
        in_specs=[
            pl.BlockSpec((tm, D_MODEL), lambda i: (t0 + i, 0)),
            pl.BlockSpec((tm, D_MODEL), lambda i: (i, 0)),
            pl.BlockSpec((D_MODEL, D_MODEL), lambda i: (0, 0), pipeline_mode=pl.Buffered(1)),
            pl.BlockSpec((1, D_MODEL), lambda i: (0, 0)),
        ],
        out_specs=pl.BlockSpec((tm, D_MODEL), lambda i: (i, 0)),
        out_shape=jax.ShapeDtypeStruct((rows, D_MODEL), F32),
        compiler_params=pltpu.CompilerParams(
            dimension_semantics=("arbitrary",),
            vmem_limit_bytes=VMEM_LIMIT),
        name="outproj",
    )(m, x, wo, fw)


def kernel(x_prompt, x_sample, state_ssd_conv, state_ssm, state_sconv, meta_tokens, norm_w,
           w_in, ssd_conv_w, ssd_conv_b, dt_bias, a_log, d_skip, ssd_norm_w, w_ssd_out,
           sconv_w, w_sconv_out, w_o, final_norm_w):
    bp, seq = x_prompt.shape[0], x_prompt.shape[1]
    bd, dec_seq = x_sample.shape[0], x_sample.shape[1]

    w_t = jnp.transpose(w_in[0])
    nw = norm_w[0].reshape(1, D_MODEL)
    fw = final_norm_w.reshape(1, D_MODEL)
    conv_w = ssd_conv_w[0]
    conv_b = ssd_conv_b[0].reshape(1, CONV_DIM)
    dtb = jnp.pad(dt_bias[0], (0, LANES - N_HEADS)).reshape(1, LANES)
    alog = jnp.pad(a_log[0], (0, LANES - N_HEADS)).reshape(1, LANES)
    dsk = jnp.repeat(d_skip[0], HEAD_DIM).reshape(1, D_INNER)
    gnw = ssd_norm_w[0].reshape(1, D_INNER)
    scw = sconv_w[0]

    xp = x_prompt.reshape(bp * seq, D_MODEL)
    xs = x_sample.reshape(bd * dec_seq, D_MODEL)
    n_p, n_s = bp * seq, bd * dec_seq
    streams = dict(n_prompt=n_p, seq=seq, n_seq_p=bp, n_seq_s=bd, q_sample=dec_seq)

    hs, hm = _norm(xp, xs, meta_tokens, nw)
    sz = _proj(hs, w_t, w_row0=0, ncols=D_INNER, silu=True)
    gates = _proj(hs, w_t, w_row0=W_GATE, ncols=2 * D_MODEL, silu=False)
    xbc, xbc_m, conv_p, conv_s = _xbc(hs, hm, w_t, conv_w, conv_b, state_ssd_conv[0], **streams)
    v_all, sc_p, sc_s = _sconv(hs, hm, w_t, scw, state_sconv[0], **streams)
    headscal = functools.partial(_headscal, w_t=w_t, dt_bias=dtb, a_log=alog)
    ssd = functools.partial(_ssd, d_skip_x=dsk, norm_w=gnw)

    at_m, dtt_m = headscal(hm, q=ROW_TILE, valid=META, tile0=0, ntiles=1)
    _, ssm_m = ssd(jnp.zeros((ROW_TILE, D_INNER), F32), xbc_m, at_m, dtt_m,
                   state0=jnp.zeros((1, N_GROUPS, GROUP_W, N_STATE), F32),
                   n_seq=1, bs=1, q=ROW_TILE, nc=1, gps=GROUPS_PER_STEP_META, shared_init=False)

    at_p, dtt_p = headscal(hs, q=ROW_TILE, valid=ROW_TILE, tile0=0, ntiles=n_p // ROW_TILE)
    yn, ssm_p = ssd(sz, xbc, at_p, dtt_p, state0=ssm_m, n_seq=bp, bs=1, q=ROW_TILE,
                    nc=seq // ROW_TILE, gps=GROUPS_PER_STEP_PROMPT, shared_init=True,
                    out_rows=n_p + n_s,
                    zero_tail_tiles=n_s // ROW_TILE)

    sbs = ROW_TILE // dec_seq
    at_s, dtt_s = headscal(hs, q=dec_seq, valid=ROW_TILE, tile0=n_p // ROW_TILE,
                           ntiles=n_s // ROW_TILE)
    yn, ssm_s = ssd(sz, xbc, at_s, dtt_s,
                    state0=state_ssm[0].reshape(bd, N_GROUPS, GROUP_W, N_STATE),
                    n_seq=bd, bs=sbs, q=dec_seq, nc=1, gps=GROUPS_PER_STEP_SAMPLE,
                    shared_init=False,
                    tile0=n_p // ROW_TILE, out_rows=n_p + n_s, yn_into=yn)

    merged = _merge(yn, v_all, gates, w_ssd_out[0], w_sconv_out[0])
    wo = w_o[0].astype(BF16)
    y_p = _outproj(merged, xp, wo, fw, row0=0)
    y_s = _outproj(merged, xs, wo, fw, row0=n_p)

    return (y_p.reshape(bp, seq, D_MODEL),
            y_s.reshape(bd, dec_seq, D_MODEL),
            conv_p[None],
            ssm_p.reshape(1, bp, N_HEADS, HEAD_DIM, N_STATE),
            sc_p[None],
            conv_s[None],
            ssm_s.reshape(1, bd, N_HEADS, HEAD_DIM, N_STATE),
            sc_s[None])
```

```python
import functools

import jax
import jax.numpy as jnp
from jax import lax
from jax.experimental import pallas as pl
from jax.experimental.pallas import tpu as pltpu

F32 = jnp.float32
BF16 = jnp.bfloat16

D_MODEL = 2048
D_INNER = 4096
N_HEADS = 64
HEAD_DIM = 64
N_STATE = 128
N_GROUPS = 8
GROUP_W = D_INNER // N_GROUPS
HEADS_PER_GROUP = N_HEADS // N_GROUPS
CONV_DIM = D_INNER + 2 * N_GROUPS * N_STATE
SSD_CONV_W = 4
SC_CONV_W = 3
META = 16
EPS = 1e-6
LOG2_E = 1.4426950408889634

LANES = 128
SUBLANES = 8
ROW_TILE = 128
W_DT = D_INNER + CONV_DIM
W_SC = W_DT + N_HEADS
W_GATE = W_SC + 4 * D_MODEL

VMEM_LIMIT = 52 * 1024 * 1024
NORM_ROWS = 512
PROJ_ROWS, PROJ_COLS = 1024, 1024
XBC_COLS = 512
SCONV_CHANNELS = 256
MERGE_ROWS, MERGE_COLS = 512, 512
OUTPROJ_ROWS = 512
HEADSCAL_TILES = 8
GROUPS_PER_STEP_PROMPT = N_GROUPS
GROUPS_PER_STEP_SAMPLE = 2
GROUPS_PER_STEP_META = 2


def _sigmoid(x):
    return 0.5 * (1.0 + jnp.tanh(0.5 * x))


def _silu(x):
    h = 0.5 * x
    return h * (1.0 + jnp.tanh(h))


def _rms_bf16(x, w):
    ms = jnp.mean(x * x, axis=-1, keepdims=True)
    return (x * lax.rsqrt(ms + EPS) * w).astype(BF16)


def _norm_kernel(xp_ref, xs_ref, xm_ref, nw_ref, hs_ref, hm_ref, *, n_prompt):
    i = pl.program_id(0)

    @pl.when(i < n_prompt)
    def _():
        hs_ref[...] = _rms_bf16(xp_ref[...], nw_ref[...])

    @pl.when(i >= n_prompt)
    def _():
        hs_ref[...] = _rms_bf16(xs_ref[...], nw_ref[...])

    @pl.when(i == 0)
    def _():
        hm_ref[:META, :] = _rms_bf16(xm_ref[...], nw_ref[...])
        hm_ref[META:, :] = jnp.zeros((ROW_TILE - META, D_MODEL), BF16)


def _norm(xp, xs, xm, norm_w, *, tm=NORM_ROWS):
    n_p, n_s = xp.shape[0] // tm, xs.shape[0] // tm
    kern = functools.partial(_norm_kernel, n_prompt=n_p)
    return pl.pallas_call(
        kern,
        grid=(n_p + n_s,),
        in_specs=[
            pl.BlockSpec((tm, D_MODEL), lambda i: (jnp.minimum(i, n_p - 1), 0)),
            pl.BlockSpec((tm, D_MODEL), lambda i: (jnp.maximum(i - n_p, 0), 0)),
            pl.BlockSpec((META, D_MODEL), lambda i: (0, 0)),
            pl.BlockSpec((1, D_MODEL), lambda i: (0, 0)),
        ],
        out_specs=[
            pl.BlockSpec((tm, D_MODEL), lambda i: (i, 0)),
            pl.BlockSpec((ROW_TILE, D_MODEL), lambda i: (0, 0)),
        ],
        out_shape=[
            jax.ShapeDtypeStruct((xp.shape[0] + xs.shape[0], D_MODEL), BF16),
            jax.ShapeDtypeStruct((ROW_TILE, D_MODEL), BF16),
        ],
        compiler_params=pltpu.CompilerParams(
            dimension_semantics=("arbitrary",), vmem_limit_bytes=VMEM_LIMIT),
        name="norm",
    )(xp, xs, xm, norm_w)


def _proj_kernel(hs_ref, wt_ref, o_ref, wb_ref, *, silu):
    @pl.when(pl.program_id(1) == 0)
    def _():
        wb_ref[...] = wt_ref[...].astype(BF16)

    r = lax.dot_general(hs_ref[...], wb_ref[...], (((1,), (1,)), ((), ())),
                        preferred_element_type=F32)
    o_ref[...] = _silu(r) if silu else r


def _proj(hs, w_t, *, w_row0, ncols, silu, tm=PROJ_ROWS, tn=PROJ_COLS):
    rows = hs.shape[0]
    assert w_row0 % N_HEADS == 0 and ncols % tn == 0 and tn % N_HEADS == 0

    def w_rows(j, i):
        return ((w_row0 // N_HEADS + j * (tn // N_HEADS)) * N_HEADS, 0)

    return pl.pallas_call(
        functools.partial(_proj_kernel, silu=silu),
        grid=(ncols // tn, rows // tm),
        in_specs=[
            pl.BlockSpec((tm, D_MODEL), lambda j, i: (i, 0)),
            pl.BlockSpec((pl.Element(tn), pl.Element(D_MODEL)), w_rows),
        ],
        out_specs=pl.BlockSpec((tm, tn), lambda j, i: (i, j)),
        out_shape=jax.ShapeDtypeStruct((rows, ncols), F32),
        scratch_shapes=[pltpu.VMEM((tn, D_MODEL), BF16)],
        compiler_params=pltpu.CompilerParams(
            dimension_semantics=("arbitrary", "arbitrary"),
            vmem_limit_bytes=VMEM_LIMIT),
        name="proj",
    )(hs, w_t)


def _conv_rows(x, halo_ref, prev_ref, w_ref, *, first, bs, q, carry):
    taps = w_ref.shape[0]
    rt, width = x.shape

    @pl.when(first)
    def _():
        halo_ref[:, SUBLANES - (taps - 1):, :] = prev_ref[...]

    prev = halo_ref[...]
    acc = None
    if bs == 1:
        row = lax.broadcasted_iota(jnp.int32, (SUBLANES, width), 0)
        for s in range(taps - 1, 0, -1):
            rolled = pltpu.roll(x, s, 0)
            head = jnp.where(row < s, pltpu.roll(prev[0], s, 0), rolled[:SUBLANES])
            term = jnp.concatenate([head, rolled[SUBLANES:]], axis=0) * w_ref[taps - 1 - s:taps - s, :]
            acc = term if acc is None else acc + term
        acc = acc + x * w_ref[taps - 1:taps, :]
        if carry:
            halo_ref[0] = x[rt - SUBLANES:, :]
        return acc
    assert q == SUBLANES and not carry
    x3 = x.reshape(bs, q, width)
    row = lax.broadcasted_iota(jnp.int32, x3.shape, 1)
    for s in range(taps - 1, 0, -1):
        shifted = jnp.where(row < s, pltpu.roll(prev, s, 1), pltpu.roll(x3, s, 1))
        term = shifted * w_ref[taps - 1 - s:taps - s, :]
        acc = term if acc is None else acc + term
    acc = acc + x3 * w_ref[taps - 1:taps, :]
    return acc.reshape(rt, width)


def _seg_cumsum(a, q):
    pos = lax.broadcasted_iota(jnp.int32, a.shape, 0) & (q - 1)
    s = 1
    while s < q:
        shifted = pltpu.roll(a, s, 0)
        a = a + jnp.where(pos >= s, shifted, 0.0)
        s *= 2
    return a


def _headscal_kernel(hs_ref, wdt_ref, dtb_ref, alog_ref, at_ref, bt_ref, *, q, valid, tps):
    dtr = lax.dot_general(hs_ref[...], wdt_ref[...].astype(BF16), (((1,), (1,)), ((), ())),
                          preferred_element_type=F32)
    dtv = jax.nn.softplus(dtr + dtb_ref[...])
    if valid < ROW_TILE:
        rows = lax.broadcasted_iota(jnp.int32, dtv.shape, 0)
        dtv = jnp.where(rows < valid, dtv, 0.0)
    acum = _seg_cumsum(dtv * (-jnp.exp(alog_ref[...])), q) * LOG2_E
    a_minus_logdt = acum - jnp.log(dtv) * LOG2_E
    for t in range(tps):
        at_ref[t] = acum[t * ROW_TILE:(t + 1) * ROW_TILE].T
        bt_ref[t] = a_minus_logdt[t * ROW_TILE:(t + 1) * ROW_TILE].T


def _headscal(hs, w_t, dt_bias, a_log, *, q, valid, tile0, ntiles):
    tps = min(ntiles, HEADSCAL_TILES)
    assert ntiles % tps == 0 and tile0 % tps == 0 and q <= ROW_TILE
    kern = functools.partial(_headscal_kernel, q=q, valid=valid, tps=tps)
    shape = jax.ShapeDtypeStruct((ntiles, LANES, ROW_TILE), F32)
    dt_blk = W_DT // LANES
    return pl.pallas_call(
        kern,
        grid=(ntiles // tps,),
        in_specs=[
            pl.BlockSpec((tps * ROW_TILE, D_MODEL), lambda t: (tile0 // tps + t, 0)),
            pl.BlockSpec((LANES, D_MODEL), lambda t: (dt_blk, 0)),
            pl.BlockSpec((1, LANES), lambda t: (0, 0)),
            pl.BlockSpec((1, LANES), lambda t: (0, 0)),
        ],
        out_specs=[
            pl.BlockSpec((tps, LANES, ROW_TILE), lambda t: (t, 0, 0)),
            pl.BlockSpec((tps, LANES, ROW_TILE), lambda t: (t, 0, 0)),
        ],
        out_shape=[shape, shape],
        compiler_params=pltpu.CompilerParams(dimension_semantics=("arbitrary",)),
        name="headscal",
    )(hs, w_t, dt_bias, a_log)


def _xbc_kernel(hs_ref, hm_ref, wt_ref, cw_ref, cb_ref, prev_ref,
                o_ref, om_ref, cnp_ref, cns_ref,
                wb_scr, halo_p, halo_s, halo_m, meta_prev, zero_prev,
                *, tm, n_prompt_tiles, tiles_per_seq, bs_sample, q_sample):
    i = pl.program_id(1)
    keep = SSD_CONV_W - 1
    tn = o_ref.shape[1]

    @pl.when(i == 0)
    def _():
        wb_scr[...] = wt_ref[...].T.astype(BF16)

    raw = jnp.dot(hs_ref[...], wb_scr[...], preferred_element_type=F32)

    def activate(conv):
        return _silu(conv + cb_ref[...])

    @pl.when(i == 0)
    def _():
        raw_m = jnp.dot(hm_ref[...], wb_scr[...], preferred_element_type=F32)
        meta_prev[0] = raw_m[META - keep:]
        zero_prev[...] = jnp.zeros(zero_prev.shape, F32)
        conv_m = _conv_rows(raw_m, halo_m, zero_prev, cw_ref, first=i == 0, bs=1, q=META,
                            carry=False)
        om_ref[:META, :] = activate(conv_m)
        om_ref[META:, :] = jnp.zeros((ROW_TILE - META, tn), F32)

    @pl.when(i < n_prompt_tiles)
    def _():
        o_ref[...] = activate(_conv_rows(raw, halo_p, meta_prev, cw_ref,
                                         first=(i % tiles_per_seq) == 0, bs=1, q=tm, carry=True))
        cnp_ref[0] = raw[tm - keep:]

    @pl.when(i >= n_prompt_tiles)
    def _():
        o_ref[...] = activate(_conv_rows(raw, halo_s, prev_ref, cw_ref, first=i >= n_prompt_tiles,
                                         bs=bs_sample, q=q_sample, carry=False))
        cns_ref[...] = raw.reshape(bs_sample, q_sample, tn)[:, q_sample - keep:, :]


def _xbc(hs, hm, w_t, cw, cb, prev_s, *, n_prompt, seq, n_seq_p, n_seq_s, q_sample,
         tm=PROJ_ROWS, tn=XBC_COLS):
    rows = hs.shape[0]
    keep = SSD_CONV_W - 1
    assert seq % tm == 0 and (rows - n_prompt) == tm and tm == n_seq_s * q_sample
    assert D_INNER % tn == 0 and CONV_DIM % tn == 0
    tiles_per_seq = seq // tm
    n_pt = n_prompt // tm
    kern = functools.partial(_xbc_kernel, tm=tm, n_prompt_tiles=n_pt, tiles_per_seq=tiles_per_seq,
                             bs_sample=n_seq_s, q_sample=q_sample)
    return pl.pallas_call(
        kern,
        grid=(CONV_DIM // tn, rows // tm),
        in_specs=[
            pl.BlockSpec((tm, D_MODEL), lambda j, i: (i, 0)),
            pl.BlockSpec((META, D_MODEL), lambda j, i: (0, 0)),
            pl.BlockSpec((tn, D_MODEL), lambda j, i: (D_INNER // tn + j, 0)),
            pl.BlockSpec((SSD_CONV_W, tn), lambda j, i: (0, j)),
            pl.BlockSpec((1, tn), lambda j, i: (0, j)),
            pl.BlockSpec((n_seq_s, keep, tn), lambda j, i: (0, 0, j)),
        ],
        out_specs=[
            pl.BlockSpec((tm, tn), lambda j, i: (i, j)),
            pl.BlockSpec((ROW_TILE, tn), lambda j, i: (0, j)),
            pl.BlockSpec((1, keep, tn),
                         lambda j, i: (jnp.minimum(i, n_pt - 1) // tiles_per_seq, 0, j)),
            pl.BlockSpec((n_seq_s, keep, tn), lambda j, i: (0, 0, j)),
        ],
        out_shape=[
            jax.ShapeDtypeStruct((rows, CONV_DIM), F32),
            jax.ShapeDtypeStruct((ROW_TILE, CONV_DIM), F32),
            jax.ShapeDtypeStruct((n_seq_p, keep, CONV_DIM), F32),
            jax.ShapeDtypeStruct((n_seq_s, keep, CONV_DIM), F32),
        ],
        scratch_shapes=[
            pltpu.VMEM((D_MODEL, tn), BF16),
            pltpu.VMEM((1, SUBLANES, tn), F32),
            pltpu.VMEM((n_seq_s, SUBLANES, tn), F32),
            pltpu.VMEM((1, SUBLANES, tn), F32),
            pltpu.VMEM((1, keep, tn), F32),
            pltpu.VMEM((1, keep, tn), F32),
        ],
        compiler_params=pltpu.CompilerParams(
            dimension_semantics=("arbitrary", "arbitrary"),
            vmem_limit_bytes=VMEM_LIMIT),
        name="xbc",
    )(hs, hm, w_t, cw, cb, prev_s)


def _ssd_kernel(*refs, n_real, **static):
    yn_ref = refs[-3]
    s = pl.program_id(0)

    @pl.when(s < n_real)
    def _():
        _ssd_body(*refs, **static)

    @pl.when(s >= n_real)
    def _():
        yn_ref[...] = jnp.zeros(yn_ref.shape, yn_ref.dtype)


def _ssd_body(sz_ref, x_ref, b_ref, c_ref, at_ref, bt_ref, dsk_ref, nw_ref, s0_ref,
              *rest, bs, q, nc, gps):
    yn_ref, sout_ref, st_ref = rest[-3:]
    rt = bs * q
    nh = gps * HEADS_PER_GROUP
    c = pl.program_id(2)
    carry = nc > 1

    if carry:
        @pl.when(c == 0)
        def _():
            st_ref[...] = s0_ref[...]

    xc = x_ref[...]
    bcb = b_ref[...].astype(BF16)
    ccb = c_ref[...].astype(BF16)

    a_t = at_ref[0]
    b_t = bt_ref[0]
    cols = jnp.concatenate([a_t, jnp.zeros((LANES - nh, rt), F32)], axis=0).T

    pos = lax.broadcasted_iota(jnp.int32, (nh, rt), 1) & (q - 1)
    a_end = a_t
    s = 1
    while s < q:
        a_end = jnp.where(pos + s < q, pltpu.roll(a_end, rt - s, 1), a_end)
        s *= 2
    to_end = jnp.exp2(a_end - b_t)

    nblk = rt // SUBLANES
    ri = lax.broadcasted_iota(jnp.int32, (nblk, SUBLANES, rt), 0) * SUBLANES + \
        lax.broadcasted_iota(jnp.int32, (nblk, SUBLANES, rt), 1)
    ci = lax.broadcasted_iota(jnp.int32, (nblk, SUBLANES, rt), 2)
    mask = (ri >= ci) & ((ri // q) == (ci // q))
    low = lax.broadcasted_iota(jnp.int32, (rt, LANES), 1) < HEAD_DIM
    seq_of_row = lax.broadcasted_iota(jnp.int32, (rt, N_STATE), 0) // q

    for k in range(gps):
        xg = xc[:, k * GROUP_W:(k + 1) * GROUP_W]
        bg = bcb[:, k * N_STATE:(k + 1) * N_STATE]
        cg = ccb[:, k * N_STATE:(k + 1) * N_STATE]
        cb = lax.dot_general(cg, bg, (((1,), (1,)), ((), ())), preferred_element_type=F32)
        cb3 = cb.reshape(nblk, SUBLANES, rt)
        xt = xg.T

        ydiag, ea, xw, a_cols = [], [], [], []
        for pr in range(HEADS_PER_GROUP // 2):
            wts, ab = [], []
            for hh in range(2):
                h = k * HEADS_PER_GROUP + 2 * pr + hh
                a_col = jnp.broadcast_to(cols[:, h:h + 1], (rt, LANES))
                b_row = jnp.broadcast_to(b_t[h:h + 1, :], (SUBLANES, rt))
                seg = jnp.where(mask, a_col.reshape(nblk, SUBLANES, rt) - b_row[None], -jnp.inf)
                wts.append((cb3 * jnp.exp2(seg)).reshape(rt, rt).astype(BF16))
                ab.append(a_col)
                rows = slice((2 * pr + hh) * HEAD_DIM, (2 * pr + hh + 1) * HEAD_DIM)
                xw.append(xt[rows] * to_end[h:h + 1, :])
            a_cols += ab
            ea.append(jnp.exp2(jnp.where(low, ab[0], ab[1])))
            xb = xg[:, pr * LANES:(pr + 1) * LANES].astype(BF16)
            zero = jnp.zeros_like(xb)
            rhs = jnp.concatenate([jnp.where(low, xb, zero), jnp.where(low, zero, xb)], axis=0)
            ydiag.append(jnp.dot(jnp.concatenate(wts, axis=1), rhs, preferred_element_type=F32))
        ydiag = jnp.concatenate(ydiag, axis=1)
        ea = jnp.concatenate(ea, axis=1)
        xwt = jnp.concatenate(xw, axis=0).astype(BF16)

        yoff = []
        for s in range(bs):
            st = st_ref[s, k] if carry else s0_ref[s, k]
            yoff.append(lax.dot_general(cg[s * q:(s + 1) * q, :], st.astype(BF16),
                                        (((1,), (1,)), ((), ())), preferred_element_type=F32))
            bsel = bg if bs == 1 else jnp.where(seq_of_row == s, bg, jnp.zeros_like(bg))
            upd = jnp.dot(xwt, bsel, preferred_element_type=F32)
            last = (s + 1) * q - 1
            dec = jnp.concatenate(
                [jnp.broadcast_to(jnp.exp2(a_cols[h][last:last + 1, :]), (HEAD_DIM, N_STATE))
                 for h in range(HEADS_PER_GROUP)], axis=0)
            new = st * dec + upd
            if carry:
                st_ref[s, k] = new
            else:
                sout_ref[s, k] = new
        yoff = yoff[0] if bs == 1 else jnp.concatenate(yoff, axis=0)

        lanes = slice(k * GROUP_W, (k + 1) * GROUP_W)
        y = ydiag + yoff * ea + dsk_ref[:, lanes] * xg
        gz = y * sz_ref[:, lanes]
        ms = jnp.mean(gz * gz, axis=-1, keepdims=True)
        yn_ref[:, lanes] = (gz * lax.rsqrt(ms + EPS) * nw_ref[:, lanes]).astype(BF16)

    if carry:
        @pl.when(c == nc - 1)
        def _():
            sout_ref[...] = st_ref[...]


def _ssd(sz, xbc, a_t, dt_t, d_skip_x, norm_w, state0,
         *, n_seq, bs, q, nc, gps, shared_init, tile0=0, out_rows=None, zero_tail_tiles=0,
         yn_into=None):
    rt = bs * q
    assert rt == ROW_TILE and N_GROUPS % gps == 0
    rows = n_seq * q * nc if out_rows is None else out_rows
    otile0 = 0 if out_rows is None else tile0
    nsb = n_seq // bs
    n_pad = pl.cdiv(zero_tail_tiles, nc)
    gw, gn, nh = gps * GROUP_W, gps * N_STATE, gps * HEADS_PER_GROUP
    static = dict(bs=bs, q=q, nc=nc, gps=gps)
    kern = (functools.partial(_ssd_kernel, n_real=nsb, **static) if n_pad
            else functools.partial(_ssd_body, **static))
    bb, bc_ = D_INNER // gn, (D_INNER + N_GROUPS * N_STATE) // gn

    def real(s):
        return jnp.minimum(s, nsb - 1) if n_pad else s

    def tile(s, c):
        return jnp.where(s < nsb, s * nc + c, nsb * nc - 1) if n_pad else s * nc + c

    def otile(s, c):
        if not n_pad:
            return otile0 + s * nc + c
        tail = jnp.minimum((s - nsb) * nc + c, zero_tail_tiles - 1)
        return otile0 + jnp.where(s < nsb, s * nc + c, nsb * nc + tail)

    sidx = (lambda s: 0) if shared_init else real
    in_specs = [
        pl.BlockSpec((rt, gw), lambda s, g, c: (tile0 + tile(s, c), g)),
        pl.BlockSpec((rt, gw), lambda s, g, c: (tile0 + tile(s, c), g)),
        pl.BlockSpec((rt, gn), lambda s, g, c: (tile0 + tile(s, c), bb + g)),
        pl.BlockSpec((rt, gn), lambda s, g, c: (tile0 + tile(s, c), bc_ + g)),
        pl.BlockSpec((1, nh, rt), lambda s, g, c: (tile(s, c), g, 0)),
        pl.BlockSpec((1, nh, rt), lambda s, g, c: (tile(s, c), g, 0)),
        pl.BlockSpec((1, gw), lambda s, g, c: (0, g)),
        pl.BlockSpec((1, gw), lambda s, g, c: (0, g)),
        pl.BlockSpec((bs, gps, GROUP_W, N_STATE), lambda s, g, c: (sidx(s), g, 0, 0)),
    ]
    out_specs = [
        pl.BlockSpec((rt, gw), lambda s, g, c: (otile(s, c), g)),
        pl.BlockSpec((bs, gps, GROUP_W, N_STATE), lambda s, g, c: (real(s), g, 0, 0)),
    ]
    operands = [sz, xbc, xbc, xbc, a_t, dt_t, d_skip_x, norm_w, state0]
    aliases = {}
    if yn_into is not None:
        assert yn_into.shape == (rows, D_INNER)
        aliases = {len(operands): 0}
        in_specs.append(pl.BlockSpec(memory_space=pl.ANY))
        operands.append(yn_into)
    st_shape = (bs, gps, GROUP_W, N_STATE) if nc > 1 else (1, 1, SUBLANES, N_STATE)
    return pl.pallas_call(
        kern,
        grid=(nsb + n_pad, N_GROUPS // gps, nc),
        in_specs=in_specs,
        out_specs=out_specs,
        out_shape=[
            jax.ShapeDtypeStruct((rows, D_INNER), BF16),
            jax.ShapeDtypeStruct((n_seq, N_GROUPS, GROUP_W, N_STATE), F32),
        ],
        scratch_shapes=[pltpu.VMEM(st_shape, F32)],
        input_output_aliases=aliases,
        compiler_params=pltpu.CompilerParams(
            dimension_semantics=("arbitrary", "arbitrary", "arbitrary"),
            vmem_limit_bytes=VMEM_LIMIT),
        name="ssd",
    )(*operands)


def _sconv_kernel(hs_ref, hm_ref, wb_ref, wc_ref, wh_ref, wz_ref, cw_ref, prev_ref,
                  v_ref, newp_ref, news_ref,
                  w_scr, halo_p, halo_s, meta_u,
                  *, tm, width, n_prompt_tiles, tiles_per_seq, bs_sample, q_sample):
    i = pl.program_id(1)
    keep = SC_CONV_W - 1

    @pl.when(i == 0)
    def _():
        for k, w_ref in enumerate((wb_ref, wc_ref, wh_ref, wz_ref)):
            w_scr[:, k * width:(k + 1) * width] = w_ref[...].T.astype(BF16)
        rm = jnp.dot(hm_ref[...], w_scr[:, width:3 * width], preferred_element_type=F32)
        meta_u[0] = (rm[:, :width] * rm[:, width:])[META - keep:]

    r = jnp.dot(hs_ref[...], w_scr[...], preferred_element_type=F32)
    u = r[:, width:2 * width] * r[:, 2 * width:3 * width]

    def finish(uc):
        v_ref[...] = (r[:, :width] * uc * _silu(r[:, 3 * width:])).astype(BF16)

    @pl.when(i < n_prompt_tiles)
    def _():
        finish(_conv_rows(u, halo_p, meta_u, cw_ref, first=(i % tiles_per_seq) == 0,
                          bs=1, q=tm, carry=True))
        newp_ref[0] = u[tm - keep:]

    @pl.when(i >= n_prompt_tiles)
    def _():
        finish(_conv_rows(u, halo_s, prev_ref, cw_ref, first=i >= n_prompt_tiles,
                          bs=bs_sample, q=q_sample, carry=False))
        news_ref[...] = u.reshape(bs_sample, q_sample, width)[:, q_sample - keep:, :]


def _sconv(hs, hm, w_t, cw, prev_s, *, n_prompt, seq, n_seq_p, n_seq_s, q_sample,
           tm=PROJ_ROWS, width=SCONV_CHANNELS):
    rows = hs.shape[0]
    keep = SC_CONV_W - 1
    assert seq % tm == 0 and (rows - n_prompt) == tm and tm == n_seq_s * q_sample
    tiles_per_seq = seq // tm
    n_pt = n_prompt // tm
    kern = functools.partial(_sconv_kernel, tm=tm, width=width, n_prompt_tiles=n_pt,
                             tiles_per_seq=tiles_per_seq, bs_sample=n_seq_s, q_sample=q_sample)

    def w_rows(k):
        base = (W_SC + k * D_MODEL) // N_HEADS
        return lambda cbk, i: ((base + cbk * (width // N_HEADS)) * N_HEADS, 0)

    w_specs = [pl.BlockSpec((pl.Element(width), pl.Element(D_MODEL)), w_rows(k)) for k in range(4)]
    return pl.pallas_call(
        kern,
        grid=(D_MODEL // width, rows // tm),
        in_specs=[
            pl.BlockSpec((tm, D_MODEL), lambda cbk, i: (i, 0)),
            pl.BlockSpec((META, D_MODEL), lambda cbk, i: (0, 0)),
            *w_specs,
            pl.BlockSpec((SC_CONV_W, width), lambda cbk, i: (0, cbk)),
            pl.BlockSpec((n_seq_s, keep, width), lambda cbk, i: (0, 0, cbk)),
        ],
        out_specs=[
            pl.BlockSpec((tm, width), lambda cbk, i: (i, cbk)),
            pl.BlockSpec((1, keep, width),
                         lambda cbk, i: (jnp.minimum(i, n_pt - 1) // tiles_per_seq, 0, cbk)),
            pl.BlockSpec((n_seq_s, keep, width), lambda cbk, i: (0, 0, cbk)),
        ],
        out_shape=[
            jax.ShapeDtypeStruct((rows, D_MODEL), BF16),
            jax.ShapeDtypeStruct((n_seq_p, keep, D_MODEL), F32),
            jax.ShapeDtypeStruct((n_seq_s, keep, D_MODEL), F32),
        ],
        scratch_shapes=[
            pltpu.VMEM((D_MODEL, 4 * width), BF16),
            pltpu.VMEM((1, SUBLANES, width), F32),
            pltpu.VMEM((n_seq_s, SUBLANES, width), F32),
            pltpu.VMEM((1, keep, width), F32),
        ],
        compiler_params=pltpu.CompilerParams(
            dimension_semantics=("arbitrary", "arbitrary"),
            vmem_limit_bytes=VMEM_LIMIT),
        name="sconv",
    )(hs, hm, w_t, w_t, w_t, w_t, cw, prev_s)


def _merge_kernel(yn_ref, v_ref, ga_ref, gb_ref, wa_ref, wb_ref, o_ref, wa_scr, wb_scr):
    @pl.when(pl.program_id(1) == 0)
    def _():
        wa_scr[...] = wa_ref[...].astype(BF16)
        wb_scr[...] = wb_ref[...].astype(BF16)

    ya = jnp.dot(yn_ref[...], wa_scr[...], preferred_element_type=F32)
    yb = jnp.dot(v_ref[...], wb_scr[...], preferred_element_type=F32)
    o_ref[...] = (_sigmoid(ga_ref[...]) * ya + _sigmoid(gb_ref[...]) * yb).astype(BF16)


def _merge(yn, v, gates, wa, wb, *, tm=MERGE_ROWS, tn=MERGE_COLS):
    rows = yn.shape[0]
    return pl.pallas_call(
        _merge_kernel,
        grid=(D_MODEL // tn, rows // tm),
        in_specs=[
            pl.BlockSpec((tm, D_INNER), lambda j, i: (i, 0)),
            pl.BlockSpec((tm, D_MODEL), lambda j, i: (i, 0)),
            pl.BlockSpec((tm, tn), lambda j, i: (i, j)),
            pl.BlockSpec((tm, tn), lambda j, i: (i, D_MODEL // tn + j)),
            pl.BlockSpec((D_INNER, tn), lambda j, i: (0, j)),
            pl.BlockSpec((D_MODEL, tn), lambda j, i: (0, j)),
        ],
        out_specs=pl.BlockSpec((tm, tn), lambda j, i: (i, j)),
        out_shape=jax.ShapeDtypeStruct((rows, D_MODEL), BF16),
        scratch_shapes=[pltpu.VMEM((D_INNER, tn), BF16), pltpu.VMEM((D_MODEL, tn), BF16)],
        compiler_params=pltpu.CompilerParams(
            dimension_semantics=("arbitrary", "arbitrary"),
            vmem_limit_bytes=VMEM_LIMIT),
        name="merge",
    )(yn, v, gates, gates, wa, wb)


def _outproj_kernel(m_ref, x_ref, wo_ref, fw_ref, o_ref):
    y = x_ref[...] + jnp.dot(m_ref[...], wo_ref[...], preferred_element_type=F32)
    ms = jnp.mean(y * y, axis=-1, keepdims=True)
    o_ref[...] = y * lax.rsqrt(ms + EPS) * fw_ref[...]


def _outproj(m, x, wo, fw, *, row0, tm=OUTPROJ_ROWS):
    rows = x.shape[0]
    assert row0 % tm == 0 and rows % tm == 0
    t0 = row0 // tm
    return pl.pallas_call(
        _outproj_kernel,
        grid=(rows // tm,),
        in_specs=[
            pl.BlockSpec((tm, D_MODEL), lambda i: (t0 + i, 0)),
            pl.BlockSpec((tm, D_MODEL), lambda i: (i, 0)),
            pl.BlockSpec((D_MODEL, D_MODEL), lambda i: (0, 0), pipeline_mode=pl.Buffered(1)),
            pl.BlockSpec((1, D_MODEL), lambda i: (0, 0)),
        ],
        out_specs=pl.BlockSpec((tm, D_MODEL), lambda i: (i, 0)),
        out_shape=jax.ShapeDtypeStruct((rows, D_MODEL), F32),
        compiler_params=pltpu.CompilerParams(
            dimension_semantics=("arbitrary",),
            vmem_limit_bytes=VMEM_LIMIT),
        name="outproj",
    )(m, x, wo, fw)


def kernel(x_prompt, x_sample, state_ssd_conv, state_ssm, state_sconv, meta_tokens, norm_w,
           w_in, ssd_conv_w, ssd_conv_b, dt_bias, a_log, d_skip, ssd_norm_w, w_ssd_out,
           sconv_w, w_sconv_out, w_o, final_norm_w):
    bp, seq = x_prompt.shape[0], x_prompt.shape[1]
    bd, dec_seq = x_sample.shape[0], x_sample.shape[1]

    w_t = jnp.transpose(w_in[0])
    nw = norm_w[0].reshape(1, D_MODEL)
    fw = final_norm_w.reshape(1, D_MODEL)
    conv_w = ssd_conv_w[0]
    conv_b = ssd_conv_b[0].reshape(1, CONV_DIM)
    dtb = jnp.pad(dt_bias[0], (0, LANES - N_HEADS)).reshape(1, LANES)
    alog = jnp.pad(a_log[0], (0, LANES - N_HEADS)).reshape(1, LANES)
    dsk = jnp.repeat(d_skip[0], HEAD_DIM).reshape(1, D_INNER)
    gnw = ssd_norm_w[0].reshape(1, D_INNER)
    scw = sconv_w[0]

    xp = x_prompt.reshape(bp * seq, D_MODEL)
    xs = x_sample.reshape(bd * dec_seq, D_MODEL)
    n_p, n_s = bp * seq, bd * dec_seq
    streams = dict(n_prompt=n_p, seq=seq, n_seq_p=bp, n_seq_s=bd, q_sample=dec_seq)

    hs, hm = _norm(xp, xs, meta_tokens, nw)
    sz = _proj(hs, w_t, w_row0=0, ncols=D_INNER, silu=True)
    gates = _proj(hs, w_t, w_row0=W_GATE, ncols=2 * D_MODEL, silu=False)
    xbc, xbc_m, conv_p, conv_s = _xbc(hs, hm, w_t, conv_w, conv_b, state_ssd_conv[0], **streams)
    v_all, sc_p, sc_s = _sconv(hs, hm, w_t, scw, state_sconv[0], **streams)
    headscal = functools.partial(_headscal, w_t=w_t, dt_bias=dtb, a_log=alog)
    ssd = functools.partial(_ssd, d_skip_x=dsk, norm_w=gnw)

    at_m, dtt_m = headscal(hm, q=ROW_TILE, valid=META, tile0=0, ntiles=1)
    _, ssm_m = ssd(jnp.zeros((ROW_TILE, D_INNER), F32), xbc_m, at_m, dtt_m,
                   state0=jnp.zeros((1, N_GROUPS, GROUP_W, N_STATE), F32),
                   n_seq=1, bs=1, q=ROW_TILE, nc=1, gps=GROUPS_PER_STEP_META, shared_init=False)

    at_p, dtt_p = headscal(hs, q=ROW_TILE, valid=ROW_TILE, tile0=0, ntiles=n_p // ROW_TILE)
    yn, ssm_p = ssd(sz, xbc, at_p, dtt_p, state0=ssm_m, n_seq=bp, bs=1, q=ROW_TILE,
                    nc=seq // ROW_TILE, gps=GROUPS_PER_STEP_PROMPT, shared_init=True,
                    out_rows=n_p + n_s,
                    zero_tail_tiles=n_s // ROW_TILE)

    sbs = ROW_TILE // dec_seq
    at_s, dtt_s = headscal(hs, q=dec_seq, valid=ROW_TILE, tile0=n_p // ROW_TILE,
                           ntiles=n_s // ROW_TILE)
    yn, ssm_s = ssd(sz, xbc, at_s, dtt_s,
                    state0=state_ssm[0].reshape(bd, N_GROUPS, GROUP_W, N_STATE),
                    n_seq=bd, bs=sbs, q=dec_seq, nc=1, gps=GROUPS_PER_STEP_SAMPLE,
                    shared_init=False,
                    tile0=n_p // ROW_TILE, out_rows=n_p + n_s, yn_into=yn)

    merged = _merge(yn, v_all, gates, w_ssd_out[0], w_sconv_out[0])
    wo = w_o[0].astype(BF16)
    y_p = _outproj(merged, xp, wo, fw, row0=0)
    y_s = _outproj(merged, xs, wo, fw, row0=n_p)

    return (y_p.reshape(bp, seq, D_MODEL),
            y_s.reshape(bd, dec_seq, D_MODEL),
            conv_p[None],
            ssm_p.reshape(1, bp, N_HEADS, HEAD_DIM, N_STATE),
            sc_p[None],
            conv_s[None],
            ssm_s.reshape(1, bd, N_HEADS, HEAD_DIM, N_STATE),
            sc_s[None])
```

```python
import functools

import jax
import jax.numpy as jnp
from jax import lax
from jax.experimental import pallas as pl
from jax.experimental.pallas import tpu as pltpu

F32 = jnp.float32
BF16 = jnp.bfloat16

D_MODEL = 2048
D_INNER = 4096
N_HEADS = 64
HEAD_DIM = 64
N_STATE = 128
N_GROUPS = 8
GROUP_W = D_INNER // N_GROUPS
HEADS_PER_GROUP = N_HEADS // N_GROUPS
CONV_DIM = D_INNER + 2 * N_GROUPS * N_STATE
SSD_CONV_W = 4
SC_CONV_W = 3
META = 16
EPS = 1e-6
LOG2_E = 1.4426950408889634

LANES = 128
SUBLANES = 8
ROW_TILE = 128
W_DT = D_INNER + CONV_DIM
W_SC = W_DT + N_HEADS
W_GATE = W_SC + 4 * D_MODEL

VMEM_LIMIT = 52 * 1024 * 1024
NORM_ROWS = 512
PROJ_ROWS, PROJ_COLS = 1024, 1024
XBC_COLS = 512
SCONV_CHANNELS = 256
MERGE_ROWS, MERGE_COLS = 512, 512
OUTPROJ_ROWS = 512
HEADSCAL_TILES = 8
GROUPS_PER_STEP_PROMPT = N_GROUPS
CHUNKS_PER_STEP_PROMPT = 2
GROUPS_PER_STEP_SAMPLE = 2
GROUPS_PER_STEP_META = 2


def _sigmoid(x):
    return 0.5 * (1.0 + jnp.tanh(0.5 * x))


def _silu(x):
    h = 0.5 * x
    return h * (1.0 + jnp.tanh(h))


def _rms_bf16(x, w):
    ms = jnp.mean(x * x, axis=-1, keepdims=True)
    return (x * lax.rsqrt(ms + EPS) * w).astype(BF16)


def _norm_kernel(xp_ref, xs_ref, xm_ref, nw_ref, hs_ref, hm_ref, *, n_prompt):
    i = pl.program_id(0)

    @pl.when(i < n_prompt)
    def _():
        hs_ref[...] = _rms_bf16(xp_ref[...], nw_ref[...])

    @pl.when(i >= n_prompt)
    def _():
        hs_ref[...] = _rms_bf16(xs_ref[...], nw_ref[...])

    @pl.when(i == 0)
    def _():
        hm_ref[:META, :] = _rms_bf16(xm_ref[...], nw_ref[...])
        hm_ref[META:, :] = jnp.zeros((ROW_TILE - META, D_MODEL), BF16)


def _norm(xp, xs, xm, norm_w, *, tm=NORM_ROWS):
    n_p, n_s = xp.shape[0] // tm, xs.shape[0] // tm
    kern = functools.partial(_norm_kernel, n_prompt=n_p)
    return pl.pallas_call(
        kern,
        grid=(n_p + n_s,),
        in_specs=[
            pl.BlockSpec((tm, D_MODEL), lambda i: (jnp.minimum(i, n_p - 1), 0)),
            pl.BlockSpec((tm, D_MODEL), lambda i: (jnp.maximum(i - n_p, 0), 0)),
            pl.BlockSpec((META, D_MODEL), lambda i: (0, 0)),
            pl.BlockSpec((1, D_MODEL), lambda i: (0, 0)),
        ],
        out_specs=[
            pl.BlockSpec((tm, D_MODEL), lambda i: (i, 0)),
            pl.BlockSpec((ROW_TILE, D_MODEL), lambda i: (0, 0)),
        ],
        out_shape=[
            jax.ShapeDtypeStruct((xp.shape[0] + xs.shape[0], D_MODEL), BF16),
            jax.ShapeDtypeStruct((ROW_TILE, D_MODEL), BF16),
        ],
        compiler_params=pltpu.CompilerParams(
            dimension_semantics=("arbitrary",), vmem_limit_bytes=VMEM_LIMIT),
        name="norm",
    )(xp, xs, xm, norm_w)


def _proj_kernel(hs_ref, wt_ref, o_ref, wb_ref, *, silu):
    @pl.when(pl.program_id(1) == 0)
    def _():
        wb_ref[...] = wt_ref[...].T.astype(BF16)

    r = jnp.dot(hs_ref[...], wb_ref[...], preferred_element_type=F32)
    o_ref[...] = _silu(r) if silu else r


def _proj(hs, w_t, *, w_row0, ncols, silu, tm=PROJ_ROWS, tn=PROJ_COLS):
    rows = hs.shape[0]
    assert w_row0 % N_HEADS == 0 and ncols % tn == 0 and tn % N_HEADS == 0

    def w_rows(j, i):
        return ((w_row0 // N_HEADS + j * (tn // N_HEADS)) * N_HEADS, 0)

    return pl.pallas_call(
        functools.partial(_proj_kernel, silu=silu),
        grid=(ncols // tn, rows // tm),
        in_specs=[
            pl.BlockSpec((tm, D_MODEL), lambda j, i: (i, 0)),
            pl.BlockSpec((pl.Element(tn), pl.Element(D_MODEL)), w_rows),
        ],
        out_specs=pl.BlockSpec((tm, tn), lambda j, i: (i, j)),
        out_shape=jax.ShapeDtypeStruct((rows, ncols), F32),
        scratch_shapes=[pltpu.VMEM((D_MODEL, tn), BF16)],
        compiler_params=pltpu.CompilerParams(
            dimension_semantics=("arbitrary", "arbitrary"),
            vmem_limit_bytes=VMEM_LIMIT),
        name="proj",
    )(hs, w_t)


def _conv_rows(x, halo_ref, prev_ref, w_ref, *, first, bs, q, carry):
    taps = w_ref.shape[0]
    rt, width = x.shape

    @pl.when(first)
    def _():
        halo_ref[:, SUBLANES - (taps - 1):, :] = prev_ref[...]

    prev = halo_ref[...]
    acc = None
    if bs == 1:
        row = lax.broadcasted_iota(jnp.int32, (SUBLANES, width), 0)
        for s in range(taps - 1, 0, -1):
            rolled = pltpu.roll(x, s, 0)
            head = jnp.where(row < s, pltpu.roll(prev[0], s, 0), rolled[:SUBLANES])
            term = jnp.concatenate([head, rolled[SUBLANES:]], axis=0) * w_ref[taps - 1 - s:taps - s, :]
            acc = term if acc is None else acc + term
        acc = acc + x * w_ref[taps - 1:taps, :]
        if carry:
            halo_ref[0] = x[rt - SUBLANES:, :]
        return acc
    assert q == SUBLANES and not carry
    x3 = x.reshape(bs, q, width)
    row = lax.broadcasted_iota(jnp.int32, x3.shape, 1)
    for s in range(taps - 1, 0, -1):
        shifted = jnp.where(row < s, pltpu.roll(prev, s, 1), pltpu.roll(x3, s, 1))
        term = shifted * w_ref[taps - 1 - s:taps - s, :]
        acc = term if acc is None else acc + term
    acc = acc + x3 * w_ref[taps - 1:taps, :]
    return acc.reshape(rt, width)


def _seg_cumsum(a, q):
    pos = lax.broadcasted_iota(jnp.int32, a.shape, 0) & (q - 1)
    s = 1
    while s < q:
        shifted = pltpu.roll(a, s, 0)
        a = a + jnp.where(pos >= s, shifted, 0.0)
        s *= 2
    return a


def _headscal_kernel(hs_ref, wdt_ref, dtb_ref, alog_ref, at_ref, bt_ref, *, q, valid, tps):
    dtr = lax.dot_general(hs_ref[...], wdt_ref[...].astype(BF16), (((1,), (1,)), ((), ())),
                          preferred_element_type=F32)
    dtv = jax.nn.softplus(dtr + dtb_ref[...])
    if valid < ROW_TILE:
        rows = lax.broadcasted_iota(jnp.int32, dtv.shape, 0)
        dtv = jnp.where(rows < valid, dtv, 0.0)
    acum = _seg_cumsum(dtv * (-jnp.exp(alog_ref[...])), q) * LOG2_E
    a_minus_logdt = acum - jnp.log(dtv) * LOG2_E
    for t in range(tps):
        at_ref[t] = acum[t * ROW_TILE:(t + 1) * ROW_TILE].T
        bt_ref[t] = a_minus_logdt[t * ROW_TILE:(t + 1) * ROW_TILE].T


def _headscal(hs, w_t, dt_bias, a_log, *, q, valid, tile0, ntiles):
    tps = min(ntiles, HEADSCAL_TILES)
    assert ntiles % tps == 0 and tile0 % tps == 0 and q <= ROW_TILE
    kern = functools.partial(_headscal_kernel, q=q, valid=valid, tps=tps)
    shape = jax.ShapeDtypeStruct((ntiles, LANES, ROW_TILE), F32)
    dt_blk = W_DT // LANES
    return pl.pallas_call(
        kern,
        grid=(ntiles // tps,),
        in_specs=[
            pl.BlockSpec((tps * ROW_TILE, D_MODEL), lambda t: (tile0 // tps + t, 0)),
            pl.BlockSpec((LANES, D_MODEL), lambda t: (dt_blk, 0)),
            pl.BlockSpec((1, LANES), lambda t: (0, 0)),
            pl.BlockSpec((1, LANES), lambda t: (0, 0)),
        ],
        out_specs=[
            pl.BlockSpec((tps, LANES, ROW_TILE), lambda t: (t, 0, 0)),
            pl.BlockSpec((tps, LANES, ROW_TILE), lambda t: (t, 0, 0)),
        ],
        out_shape=[shape, shape],
        compiler_params=pltpu.CompilerParams(dimension_semantics=("arbitrary",)),
        name="headscal",
    )(hs, w_t, dt_bias, a_log)


def _xbc_kernel(hs_ref, hm_ref, wt_ref, cw_ref, cb_ref, prev_ref,
                o_ref, om_ref, cnp_ref, cns_ref,
                wb_scr, halo_p, halo_s, halo_m, meta_prev, zero_prev,
                *, tm, n_prompt_tiles, tiles_per_seq, bs_sample, q_sample):
    i = pl.program_id(1)
    keep = SSD_CONV_W - 1
    tn = o_ref.shape[1]

    @pl.when(i == 0)
    def _():
        wb_scr[...] = wt_ref[...].T.astype(BF16)

    raw = jnp.dot(hs_ref[...], wb_scr[...], preferred_element_type=F32)

    def activate(conv):
        return _silu(conv + cb_ref[...])

    @pl.when(i == 0)
    def _():
        raw_m = jnp.dot(hm_ref[...], wb_scr[...], preferred_element_type=F32)
        meta_prev[0] = raw_m[META - keep:]
        zero_prev[...] = jnp.zeros(zero_prev.shape, F32)
        conv_m = _conv_rows(raw_m, halo_m, zero_prev, cw_ref, first=i == 0, bs=1, q=META,
                            carry=False)
        om_ref[:META, :] = activate(conv_m)
        om_ref[META:, :] = jnp.zeros((ROW_TILE - META, tn), F32)

    @pl.when(i < n_prompt_tiles)
    def _():
        o_ref[...] = activate(_conv_rows(raw, halo_p, meta_prev, cw_ref,
                                         first=(i % tiles_per_seq) == 0, bs=1, q=tm, carry=True))
        cnp_ref[0] = raw[tm - keep:]

    @pl.when(i >= n_prompt_tiles)
    def _():
        o_ref[...] = activate(_conv_rows(raw, halo_s, prev_ref, cw_ref, first=i >= n_prompt_tiles,
                                         bs=bs_sample, q=q_sample, carry=False))
        cns_ref[...] = raw.reshape(bs_sample, q_sample, tn)[:, q_sample - keep:, :]


def _xbc(hs, hm, w_t, cw, cb, prev_s, *, n_prompt, seq, n_seq_p, n_seq_s, q_sample,
         tm=PROJ_ROWS, tn=XBC_COLS):
    rows = hs.shape[0]
    keep = SSD_CONV_W - 1
    assert seq % tm == 0 and (rows - n_prompt) == tm and tm == n_seq_s * q_sample
    assert D_INNER % tn == 0 and CONV_DIM % tn == 0
    tiles_per_seq = seq // tm
    n_pt = n_prompt // tm
    kern = functools.partial(_xbc_kernel, tm=tm, n_prompt_tiles=n_pt, tiles_per_seq=tiles_per_seq,
                             bs_sample=n_seq_s, q_sample=q_sample)
    return pl.pallas_call(
        kern,
        grid=(CONV_DIM // tn, rows // tm),
        in_specs=[
            pl.BlockSpec((tm, D_MODEL), lambda j, i: (i, 0)),
            pl.BlockSpec((META, D_MODEL), lambda j, i: (0, 0)),
            pl.BlockSpec((tn, D_MODEL), lambda j, i: (D_INNER // tn + j, 0)),
            pl.BlockSpec((SSD_CONV_W, tn), lambda j, i: (0, j)),
            pl.BlockSpec((1, tn), lambda j, i: (0, j)),
            pl.BlockSpec((n_seq_s, keep, tn), lambda j, i: (0, 0, j)),
        ],
        out_specs=[
            pl.BlockSpec((tm, tn), lambda j, i: (i, j)),
            pl.BlockSpec((ROW_TILE, tn), lambda j, i: (0, j)),
            pl.BlockSpec((1, keep, tn),
                         lambda j, i: (jnp.minimum(i, n_pt - 1) // tiles_per_seq, 0, j)),
            pl.BlockSpec((n_seq_s, keep, tn), lambda j, i: (0, 0, j)),
        ],
        out_shape=[
            jax.ShapeDtypeStruct((rows, CONV_DIM), F32),
            jax.ShapeDtypeStruct((ROW_TILE, CONV_DIM), F32),
            jax.ShapeDtypeStruct((n_seq_p, keep, CONV_DIM), F32),
            jax.ShapeDtypeStruct((n_seq_s, keep, CONV_DIM), F32),
        ],
        scratch_shapes=[
            pltpu.VMEM((D_MODEL, tn), BF16),
            pltpu.VMEM((1, SUBLANES, tn), F32),
            pltpu.VMEM((n_seq_s, SUBLANES, tn), F32),
            pltpu.VMEM((1, SUBLANES, tn), F32),
            pltpu.VMEM((1, keep, tn), F32),
            pltpu.VMEM((1, keep, tn), F32),
        ],
        compiler_params=pltpu.CompilerParams(
            dimension_semantics=("arbitrary", "arbitrary"),
            vmem_limit_bytes=VMEM_LIMIT),
        name="xbc",
    )(hs, hm, w_t, cw, cb, prev_s)


def _ssd_kernel(*refs, n_real, **static):
    yn_ref = refs[-3]
    s = pl.program_id(0)

    @pl.when(s < n_real)
    def _():
        _ssd_body(*refs, **static)

    @pl.when(s >= n_real)
    def _():
        yn_ref[...] = jnp.zeros(yn_ref.shape, yn_ref.dtype)


def _ssd_body(sz_ref, x_ref, b_ref, c_ref, at_ref, bt_ref, dsk_ref, nw_ref, s0_ref,
              *rest, bs, q, nc, gps, cps):
    yn_ref, sout_ref, st_ref = rest[-3:]
    rt = bs * q
    c = pl.program_id(2)
    carry = nc * cps > 1

    if carry:
        @pl.when(c == 0)
        def _():
            st_ref[...] = s0_ref[...]

    for ch in range(cps):
        rows = pl.ds(ch * rt, rt)
        _ssd_chunk(sz_ref.at[rows], x_ref.at[rows], b_ref.at[rows], c_ref.at[rows],
                   at_ref.at[ch], bt_ref.at[ch], dsk_ref, nw_ref, s0_ref,
                   yn_ref.at[rows], sout_ref, st_ref, bs=bs, q=q, gps=gps, carry=carry)

    if carry:
        @pl.when(c == nc - 1)
        def _():
            sout_ref[...] = st_ref[...]


def _ssd_chunk(sz_ref, x_ref, b_ref, c_ref, at_ref, bt_ref, dsk_ref, nw_ref, s0_ref,
               yn_ref, sout_ref, st_ref, *, bs, q, gps, carry):
    rt = bs * q
    nh = gps * HEADS_PER_GROUP

    xc = x_ref[...]
    bcb = b_ref[...].astype(BF16)
    ccb = c_ref[...].astype(BF16)

    a_t = at_ref[...]
    b_t = bt_ref[...]
    cols = jnp.concatenate([a_t, jnp.zeros((LANES - nh, rt), F32)], axis=0).T

    pos = lax.broadcasted_iota(jnp.int32, (nh, rt), 1) & (q - 1)
    a_end = a_t
    s = 1
    while s < q:
        a_end = jnp.where(pos + s < q, pltpu.roll(a_end, rt - s, 1), a_end)
        s *= 2
    to_end = jnp.exp2(a_end - b_t)

    nblk = rt // SUBLANES
    ri = lax.broadcasted_iota(jnp.int32, (nblk, SUBLANES, rt), 0) * SUBLANES + \
        lax.broadcasted_iota(jnp.int32, (nblk, SUBLANES, rt), 1)
    ci = lax.broadcasted_iota(jnp.int32, (nblk, SUBLANES, rt), 2)
    mask = (ri >= ci) & ((ri // q) == (ci // q))
    low = lax.broadcasted_iota(jnp.int32, (rt, LANES), 1) < HEAD_DIM
    seq_of_row = lax.broadcasted_iota(jnp.int32, (rt, N_STATE), 0) // q

    for k in range(gps):
        xg = xc[:, k * GROUP_W:(k + 1) * GROUP_W]
        bg = bcb[:, k * N_STATE:(k + 1) * N_STATE]
        cg = ccb[:, k * N_STATE:(k + 1) * N_STATE]
        cb = lax.dot_general(cg, bg, (((1,), (1,)), ((), ())), preferred_element_type=F32)
        cb3 = cb.reshape(nblk, SUBLANES, rt)
        xt = xg.T

        ydiag, ea, xw, a_cols = [], [], [], []
        for pr in range(HEADS_PER_GROUP // 2):
            wts, ab = [], []
            for hh in range(2):
                h = k * HEADS_PER_GROUP + 2 * pr + hh
                a_col = jnp.broadcast_to(cols[:, h:h + 1], (rt, LANES))
                b_row = jnp.broadcast_to(b_t[h:h + 1, :], (SUBLANES, rt))
                seg = jnp.where(mask, a_col.reshape(nblk, SUBLANES, rt) - b_row[None], -jnp.inf)
                wts.append((cb3 * jnp.exp2(seg)).reshape(rt, rt).astype(BF16))
                ab.append(a_col)
                rows = slice((2 * pr + hh) * HEAD_DIM, (2 * pr + hh + 1) * HEAD_DIM)
                xw.append(xt[rows] * to_end[h:h + 1, :])
            a_cols += ab
            ea.append(jnp.exp2(jnp.where(low, ab[0], ab[1])))
            xb = xg[:, pr * LANES:(pr + 1) * LANES].astype(BF16)
            zero = jnp.zeros_like(xb)
            rhs = jnp.concatenate([jnp.where(low, xb, zero), jnp.where(low, zero, xb)], axis=0)
            ydiag.append(jnp.dot(jnp.concatenate(wts, axis=1), rhs, preferred_element_type=F32))
        ydiag = jnp.concatenate(ydiag, axis=1)
        ea = jnp.concatenate(ea, axis=1)
        xwt = jnp.concatenate(xw, axis=0).astype(BF16)

        yoff = []
        for s in range(bs):
            st = st_ref[s, k] if carry else s0_ref[s, k]
            yoff.append(lax.dot_general(cg[s * q:(s + 1) * q, :], st.astype(BF16),
                                        (((1,), (1,)), ((), ())), preferred_element_type=F32))
            bsel = bg if bs == 1 else jnp.where(seq_of_row == s, bg, jnp.zeros_like(bg))
            upd = jnp.dot(xwt, bsel, preferred_element_type=F32)
            last = (s + 1) * q - 1
            dec = jnp.concatenate(
                [jnp.broadcast_to(jnp.exp2(a_cols[h][last:last + 1, :]), (HEAD_DIM, N_STATE))
                 for h in range(HEADS_PER_GROUP)], axis=0)
            new = st * dec + upd
            if carry:
                st_ref[s, k] = new
            else:
                sout_ref[s, k] = new
        yoff = yoff[0] if bs == 1 else jnp.concatenate(yoff, axis=0)

        lanes = slice(k * GROUP_W, (k + 1) * GROUP_W)
        y = ydiag + yoff * ea + dsk_ref[:, lanes] * xg
        gz = y * sz_ref[:, lanes]
        ms = jnp.mean(gz * gz, axis=-1, keepdims=True)
        yn_ref[:, lanes] = (gz * lax.rsqrt(ms + EPS) * nw_ref[:, lanes]).astype(BF16)


def _ssd(sz, xbc, a_t, dt_t, d_skip_x, norm_w, state0,
         *, n_seq, bs, q, nc, gps, shared_init, cps=1, tile0=0, out_rows=None,
         zero_tail_tiles=0, yn_into=None):
    rt = bs * q
    assert rt == ROW_TILE and N_GROUPS % gps == 0
    assert nc % cps == 0 and tile0 % cps == 0 and zero_tail_tiles % cps == 0
    rows = n_seq * q * nc if out_rows is None else out_rows
    nc, tile0, zero_tail_tiles = nc // cps, tile0 // cps, zero_tail_tiles // cps
    otile0 = 0 if out_rows is None else tile0
    nsb = n_seq // bs
    n_pad = pl.cdiv(zero_tail_tiles, nc)
    gw, gn, nh = gps * GROUP_W, gps * N_STATE, gps * HEADS_PER_GROUP
    static = dict(bs=bs, q=q, nc=nc, gps=gps, cps=cps)
    kern = (functools.partial(_ssd_kernel, n_real=nsb, **static) if n_pad
            else functools.partial(_ssd_body, **static))
    bb, bc_ = D_INNER // gn, (D_INNER + N_GROUPS * N_STATE) // gn

    def real(s):
        return jnp.minimum(s, nsb - 1) if n_pad else s

    def tile(s, c):
        return jnp.where(s < nsb, s * nc + c, nsb * nc - 1) if n_pad else s * nc + c

    def otile(s, c):
        if not n_pad:
            return otile0 + s * nc + c
        tail = jnp.minimum((s - nsb) * nc + c, zero_tail_tiles - 1)
        return otile0 + jnp.where(s < nsb, s * nc + c, nsb * nc + tail)

    sidx = (lambda s: 0) if shared_init else real
    in_specs = [
        pl.BlockSpec((cps * rt, gw), lambda s, g, c: (tile0 + tile(s, c), g)),
        pl.BlockSpec((cps * rt, gw), lambda s, g, c: (tile0 + tile(s, c), g)),
        pl.BlockSpec((cps * rt, gn), lambda s, g, c: (tile0 + tile(s, c), bb + g)),
        pl.BlockSpec((cps * rt, gn), lambda s, g, c: (tile0 + tile(s, c), bc_ + g)),
        pl.BlockSpec((cps, nh, rt), lambda s, g, c: (tile(s, c), g, 0)),
        pl.BlockSpec((cps, nh, rt), lambda s, g, c: (tile(s, c), g, 0)),
        pl.BlockSpec((1, gw), lambda s, g, c: (0, g)),
        pl.BlockSpec((1, gw), lambda s, g, c: (0, g)),
        pl.BlockSpec((bs, gps, GROUP_W, N_STATE), lambda s, g, c: (sidx(s), g, 0, 0)),
    ]
    out_specs = [
        pl.BlockSpec((cps * rt, gw), lambda s, g, c: (otile(s, c), g)),
        pl.BlockSpec((bs, gps, GROUP_W, N_STATE), lambda s, g, c: (real(s), g, 0, 0)),
    ]
    operands = [sz, xbc, xbc, xbc, a_t, dt_t, d_skip_x, norm_w, state0]
    aliases = {}
    if yn_into is not None:
        assert yn_into.shape == (rows, D_INNER)
        aliases = {len(operands): 0}
        in_specs.append(pl.BlockSpec(memory_space=pl.ANY))
        operands.append(yn_into)
    st_shape = (bs, gps, GROUP_W, N_STATE) if nc * cps > 1 else (1, 1, SUBLANES, N_STATE)
    return pl.pallas_call(
        kern,
        grid=(nsb + n_pad, N_GROUPS // gps, nc),
        in_specs=in_specs,
        out_specs=out_specs,
        out_shape=[
            jax.ShapeDtypeStruct((rows, D_INNER), BF16),
            jax.ShapeDtypeStruct((n_seq, N_GROUPS, GROUP_W, N_STATE), F32),
        ],
        scratch_shapes=[pltpu.VMEM(st_shape, F32)],
        input_output_aliases=aliases,
        compiler_params=pltpu.CompilerParams(
            dimension_semantics=("arbitrary", "arbitrary", "arbitrary"),
            vmem_limit_bytes=VMEM_LIMIT),
        name="ssd",
    )(*operands)


def _sconv_kernel(hs_ref, hm_ref, wb_ref, wc_ref, wh_ref, wz_ref, cw_ref, prev_ref,
                  v_ref, newp_ref, news_ref,
                  w_scr, halo_p, halo_s, meta_u,
                  *, tm, width, n_prompt_tiles, tiles_per_seq, bs_sample, q_sample):
    i = pl.program_id(1)
    keep = SC_CONV_W - 1

    @pl.when(i == 0)
    def _():
        for k, w_ref in enumerate((wb_ref, wc_ref, wh_ref, wz_ref)):
            w_scr[:, k * width:(k + 1) * width] = w_ref[...].T.astype(BF16)
        rm = jnp.dot(hm_ref[...], w_scr[:, width:3 * width], preferred_element_type=F32)
        meta_u[0] = (rm[:, :width] * rm[:, width:])[META - keep:]

    r = jnp.dot(hs_ref[...], w_scr[...], preferred_element_type=F32)
    u = r[:, width:2 * width] * r[:, 2 * width:3 * width]

    def finish(uc):
        v_ref[...] = (r[:, :width] * uc * _silu(r[:, 3 * width:])).astype(BF16)

    @pl.when(i < n_prompt_tiles)
    def _():
        finish(_conv_rows(u, halo_p, meta_u, cw_ref, first=(i % tiles_per_seq) == 0,
                          bs=1, q=tm, carry=True))
        newp_ref[0] = u[tm - keep:]

    @pl.when(i >= n_prompt_tiles)
    def _():
        finish(_conv_rows(u, halo_s, prev_ref, cw_ref, first=i >= n_prompt_tiles,
                          bs=bs_sample, q=q_sample, carry=False))
        news_ref[...] = u.reshape(bs_sample, q_sample, width)[:, q_sample - keep:, :]


def _sconv(hs, hm, w_t, cw, prev_s, *, n_prompt, seq, n_seq_p, n_seq_s, q_sample,
           tm=PROJ_ROWS, width=SCONV_CHANNELS):
    rows = hs.shape[0]
    keep = SC_CONV_W - 1
    assert seq % tm == 0 and (rows - n_prompt) == tm and tm == n_seq_s * q_sample
    tiles_per_seq = seq // tm
    n_pt = n_prompt // tm
    kern = functools.partial(_sconv_kernel, tm=tm, width=width, n_prompt_tiles=n_pt,
                             tiles_per_seq=tiles_per_seq, bs_sample=n_seq_s, q_sample=q_sample)

    def w_rows(k):
        base = (W_SC + k * D_MODEL) // N_HEADS
        return lambda cbk, i: ((base + cbk * (width // N_HEADS)) * N_HEADS, 0)

    w_specs = [pl.BlockSpec((pl.Element(width), pl.Element(D_MODEL)), w_rows(k)) for k in range(4)]
    return pl.pallas_call(
        kern,
        grid=(D_MODEL // width, rows // tm),
        in_specs=[
            pl.BlockSpec((tm, D_MODEL), lambda cbk, i: (i, 0)),
            pl.BlockSpec((META, D_MODEL), lambda cbk, i: (0, 0)),
            *w_specs,
            pl.BlockSpec((SC_CONV_W, width), lambda cbk, i: (0, cbk)),
            pl.BlockSpec((n_seq_s, keep, width), lambda cbk, i: (0, 0, cbk)),
        ],
        out_specs=[
            pl.BlockSpec((tm, width), lambda cbk, i: (i, cbk)),
            pl.BlockSpec((1, keep, width),
                         lambda cbk, i: (jnp.minimum(i, n_pt - 1) // tiles_per_seq, 0, cbk)),
            pl.BlockSpec((n_seq_s, keep, width), lambda cbk, i: (0, 0, cbk)),
        ],
        out_shape=[
            jax.ShapeDtypeStruct((rows, D_MODEL), BF16),
            jax.ShapeDtypeStruct((n_seq_p, keep, D_MODEL), F32),
            jax.ShapeDtypeStruct((n_seq_s, keep, D_MODEL), F32),
        ],
        scratch_shapes=[
            pltpu.VMEM((D_MODEL, 4 * width), BF16),
            pltpu.VMEM((1, SUBLANES, width), F32),
            pltpu.VMEM((n_seq_s, SUBLANES, width), F32),
            pltpu.VMEM((1, keep, width), F32),
        ],
        compiler_params=pltpu.CompilerParams(
            dimension_semantics=("arbitrary", "arbitrary"),
            vmem_limit_bytes=VMEM_LIMIT),
        name="sconv",
    )(hs, hm, w_t, w_t, w_t, w_t, cw, prev_s)


def _merge_kernel(yn_ref, v_ref, ga_ref, gb_ref, wa_ref, wb_ref, o_ref, wa_scr, wb_scr):
    @pl.when(pl.program_id(1) == 0)
    def _():
        wa_scr[...] = wa_ref[...].astype(BF16)
        wb_scr[...] = wb_ref[...].astype(BF16)

    ya = jnp.dot(yn_ref[...], wa_scr[...], preferred_element_type=F32)
    yb = jnp.dot(v_ref[...], wb_scr[...], preferred_element_type=F32)
    o_ref[...] = (_sigmoid(ga_ref[...]) * ya + _sigmoid(gb_ref[...]) * yb).astype(BF16)


def _merge(yn, v, gates, wa, wb, *, tm=MERGE_ROWS, tn=MERGE_COLS):
    rows = yn.shape[0]
    return pl.pallas_call(
        _merge_kernel,
        grid=(D_MODEL // tn, rows // tm),
        in_specs=[
            pl.BlockSpec((tm, D_INNER), lambda j, i: (i, 0)),
            pl.BlockSpec((tm, D_MODEL), lambda j, i: (i, 0)),
            pl.BlockSpec((tm, tn), lambda j, i: (i, j)),
            pl.BlockSpec((tm, tn), lambda j, i: (i, D_MODEL // tn + j)),
            pl.BlockSpec((D_INNER, tn), lambda j, i: (0, j)),
            pl.BlockSpec((D_MODEL, tn), lambda j, i: (0, j)),
        ],
        out_specs=pl.BlockSpec((tm, tn), lambda j, i: (i, j)),
        out_shape=jax.ShapeDtypeStruct((rows, D_MODEL), BF16),
        scratch_shapes=[pltpu.VMEM((D_INNER, tn), BF16), pltpu.VMEM((D_MODEL, tn), BF16)],
        compiler_params=pltpu.CompilerParams(
            dimension_semantics=("arbitrary", "arbitrary"),
            vmem_limit_bytes=VMEM_LIMIT),
        name="merge",
    )(yn, v, gates, gates, wa, wb)


def _outproj_kernel(m_ref, x_ref, wo_ref, fw_ref, o_ref):
    y = x_ref[...] + jnp.dot(m_ref[...], wo_ref[...], preferred_element_type=F32)
    ms = jnp.mean(y * y, axis=-1, keepdims=True)
    o_ref[...] = y * lax.rsqrt(ms + EPS) * fw_ref[...]


def _outproj(m, x, wo, fw, *, row0, tm=OUTPROJ_ROWS):
    rows = x.shape[0]
    assert row0 % tm == 0 and rows % tm == 0
    t0 = row0 // tm
    return pl.pallas_call(
        _outproj_kernel,
        grid=(rows // tm,),
        in_specs=[
            pl.BlockSpec((tm, D_MODEL), lambda i: (t0 + i, 0)),
            pl.BlockSpec((tm, D_MODEL), lambda i: (i, 0)),
            pl.BlockSpec((D_MODEL, D_MODEL), lambda i: (0, 0), pipeline_mode=pl.Buffered(1)),
            pl.BlockSpec((1, D_MODEL), lambda i: (0, 0)),
        ],
        out_specs=pl.BlockSpec((tm, D_MODEL), lambda i: (i, 0)),
        out_shape=jax.ShapeDtypeStruct((rows, D_MODEL), F32),
        compiler_params=pltpu.CompilerParams(
            dimension_semantics=("arbitrary",),
            vmem_limit_bytes=VMEM_LIMIT),
        name="outproj",
    )(m, x, wo, fw)


def kernel(x_prompt, x_sample, state_ssd_conv, state_ssm, state_sconv, meta_tokens, norm_w,
           w_in, ssd_conv_w, ssd_conv_b, dt_bias, a_log, d_skip, ssd_norm_w, w_ssd_out,
           sconv_w, w_sconv_out, w_o, final_norm_w):
    bp, seq = x_prompt.shape[0], x_prompt.shape[1]
    bd, dec_seq = x_sample.shape[0], x_sample.shape[1]

    w_t = jnp.transpose(w_in[0])
    nw = norm_w[0].reshape(1, D_MODEL)
    fw = final_norm_w.reshape(1, D_MODEL)
    conv_w = ssd_conv_w[0]
    conv_b = ssd_conv_b[0].reshape(1, CONV_DIM)
    dtb = jnp.pad(dt_bias[0], (0, LANES - N_HEADS)).reshape(1, LANES)
    alog = jnp.pad(a_log[0], (0, LANES - N_HEADS)).reshape(1, LANES)
    dsk = jnp.repeat(d_skip[0], HEAD_DIM).reshape(1, D_INNER)
    gnw = ssd_norm_w[0].reshape(1, D_INNER)
    scw = sconv_w[0]

    xp = x_prompt.reshape(bp * seq, D_MODEL)
    xs = x_sample.reshape(bd * dec_seq, D_MODEL)
    n_p, n_s = bp * seq, bd * dec_seq
    streams = dict(n_prompt=n_p, seq=seq, n_seq_p=bp, n_seq_s=bd, q_sample=dec_seq)

    hs, hm = _norm(xp, xs, meta_tokens, nw)
    sz = _proj(hs, w_t, w_row0=0, ncols=D_INNER, silu=True)
    gates = _proj(hs, w_t, w_row0=W_GATE, ncols=2 * D_MODEL, silu=False)
    xbc, xbc_m, conv_p, conv_s = _xbc(hs, hm, w_t, conv_w, conv_b, state_ssd_conv[0], **streams)
    v_all, sc_p, sc_s = _sconv(hs, hm, w_t, scw, state_sconv[0], **streams)
    headscal = functools.partial(_headscal, w_t=w_t, dt_bias=dtb, a_log=alog)
    ssd = functools.partial(_ssd, d_skip_x=dsk, norm_w=gnw)

    at_m, dtt_m = headscal(hm, q=ROW_TILE, valid=META, tile0=0, ntiles=1)
    _, ssm_m = ssd(jnp.zeros((ROW_TILE, D_INNER), F32), xbc_m, at_m, dtt_m,
                   state0=jnp.zeros((1, N_GROUPS, GROUP_W, N_STATE), F32),
                   n_seq=1, bs=1, q=ROW_TILE, nc=1, gps=GROUPS_PER_STEP_META, shared_init=False)

    at_p, dtt_p = headscal(hs, q=ROW_TILE, valid=ROW_TILE, tile0=0, ntiles=n_p // ROW_TILE)
    yn, ssm_p = ssd(sz, xbc, at_p, dtt_p, state0=ssm_m, n_seq=bp, bs=1, q=ROW_TILE,
                    nc=seq // ROW_TILE, gps=GROUPS_PER_STEP_PROMPT, cps=CHUNKS_PER_STEP_PROMPT,
                    shared_init=True, out_rows=n_p + n_s,
                    zero_tail_tiles=n_s // ROW_TILE)

    sbs = ROW_TILE // dec_seq
    at_s, dtt_s = headscal(hs, q=dec_seq, valid=ROW_TILE, tile0=n_p // ROW_TILE,
                           ntiles=n_s // ROW_TILE)
    yn, ssm_s = ssd(sz, xbc, at_s, dtt_s,
                    state0=state_ssm[0].reshape(bd, N_GROUPS, GROUP_W, N_STATE),
                    n_seq=bd, bs=sbs, q=dec_seq, nc=1, gps=GROUPS_PER_STEP_SAMPLE,
                    shared_init=False,
                    tile0=n_p // ROW_TILE, out_rows=n_p + n_s, yn_into=yn)

    merged = _merge(yn, v_all, gates, w_ssd_out[0], w_sconv_out[0])
    wo = w_o[0].astype(BF16)
    y_p = _outproj(merged, xp, wo, fw, row0=0)
    y_s = _outproj(merged, xs, wo, fw, row0=n_p)

    return (y_p.reshape(bp, seq, D_MODEL),
            y_s.reshape(bd, dec_seq, D_MODEL),
            conv_p[None],
            ssm_p.reshape(1, bp, N_HEADS, HEAD_DIM, N_STATE),
            sc_p[None],
            conv_s[None],
            ssm_s.reshape(1, bd, N_HEADS, HEAD_DIM, N_STATE),
            sc_s[None])
```

```python
import functools

import jax
import jax.numpy as jnp
from jax import lax
from jax.experimental import pallas as pl
from jax.experimental.pallas import tpu as pltpu

F32 = jnp.float32
BF16 = jnp.bfloat16

D_MODEL = 2048
D_INNER = 4096
N_HEADS = 64
HEAD_DIM = 64
N_STATE = 128
N_GROUPS = 8
GROUP_W = D_INNER // N_GROUPS
HEADS_PER_GROUP = N_HEADS // N_GROUPS
CONV_DIM = D_INNER + 2 * N_GROUPS * N_STATE
SSD_CONV_W = 4
SC_CONV_W = 3
META = 16
EPS = 1e-6
LOG2_E = 1.4426950408889634

LANES = 128
SUBLANES = 8
ROW_TILE = 128
W_DT = D_INNER + CONV_DIM
W_SC = W_DT + N_HEADS
W_GATE = W_SC + 4 * D_MODEL

VMEM_LIMIT = 52 * 1024 * 1024
NORM_ROWS = 512
PROJ_ROWS, PROJ_COLS = 1024, 1024
XBC_COLS = 512
SCONV_CHANNELS = 256
MERGE_ROWS, MERGE_COLS = 512, 512
OUTPROJ_ROWS = 512
HEADSCAL_TILES = 8
GROUPS_PER_STEP_PROMPT = N_GROUPS
GROUPS_PER_STEP_SAMPLE = 2
GROUPS_PER_STEP_META = 2


def _sigmoid(x):
    return 0.5 * (1.0 + jnp.tanh(0.5 * x))


def _silu(x):
    h = 0.5 * x
    return h * (1.0 + jnp.tanh(h))


def _rms_bf16(x, w):
    ms = jnp.mean(x * x, axis=-1, keepdims=True)
    return (x * lax.rsqrt(ms + EPS) * w).astype(BF16)


def _norm_kernel(xp_ref, xs_ref, xm_ref, nw_ref, wdt_ref, hs_ref, hm_ref, dt_ref, dtm_ref,
                 wdt_scr, *, n_prompt):
    i = pl.program_id(0)

    def dt_proj(h):
        return lax.dot_general(h, wdt_scr[...], (((1,), (1,)), ((), ())),
                               preferred_element_type=F32)

    @pl.when(i == 0)
    def _():
        wdt_scr[...] = wdt_ref[...].astype(BF16)
        hm_ref[:META, :] = _rms_bf16(xm_ref[...], nw_ref[...])
        hm_ref[META:, :] = jnp.zeros((ROW_TILE - META, D_MODEL), BF16)
        dtm_ref[...] = dt_proj(hm_ref[...])

    @pl.when(i < n_prompt)
    def _():
        hs_ref[...] = _rms_bf16(xp_ref[...], nw_ref[...])

    @pl.when(i >= n_prompt)
    def _():
        hs_ref[...] = _rms_bf16(xs_ref[...], nw_ref[...])

    dt_ref[...] = dt_proj(hs_ref[...])


def _norm(xp, xs, xm, norm_w, w_t, *, tm=NORM_ROWS):
    n_p, n_s = xp.shape[0] // tm, xs.shape[0] // tm
    rows = xp.shape[0] + xs.shape[0]
    kern = functools.partial(_norm_kernel, n_prompt=n_p)
    return pl.pallas_call(
        kern,
        grid=(n_p + n_s,),
        in_specs=[
            pl.BlockSpec((tm, D_MODEL), lambda i: (jnp.minimum(i, n_p - 1), 0)),
            pl.BlockSpec((tm, D_MODEL), lambda i: (jnp.maximum(i - n_p, 0), 0)),
            pl.BlockSpec((META, D_MODEL), lambda i: (0, 0)),
            pl.BlockSpec((1, D_MODEL), lambda i: (0, 0)),
            pl.BlockSpec((LANES, D_MODEL), lambda i: (W_DT // LANES, 0)),
        ],
        out_specs=[
            pl.BlockSpec((tm, D_MODEL), lambda i: (i, 0)),
            pl.BlockSpec((ROW_TILE, D_MODEL), lambda i: (0, 0)),
            pl.BlockSpec((tm, LANES), lambda i: (i, 0)),
            pl.BlockSpec((ROW_TILE, LANES), lambda i: (0, 0)),
        ],
        out_shape=[
            jax.ShapeDtypeStruct((rows, D_MODEL), BF16),
            jax.ShapeDtypeStruct((ROW_TILE, D_MODEL), BF16),
            jax.ShapeDtypeStruct((rows, LANES), F32),
            jax.ShapeDtypeStruct((ROW_TILE, LANES), F32),
        ],
        scratch_shapes=[pltpu.VMEM((LANES, D_MODEL), BF16)],
        compiler_params=pltpu.CompilerParams(
            dimension_semantics=("arbitrary",), vmem_limit_bytes=VMEM_LIMIT),
        name="norm",
    )(xp, xs, xm, norm_w, w_t)


def _proj_kernel(hs_ref, wt_ref, o_ref, wb_ref, *, silu):
    @pl.when(pl.program_id(1) == 0)
    def _():
        wb_ref[...] = wt_ref[...].T.astype(BF16)

    r = jnp.dot(hs_ref[...], wb_ref[...], preferred_element_type=F32)
    o_ref[...] = _silu(r) if silu else r


def _proj(hs, w_t, *, w_row0, ncols, silu, tm=PROJ_ROWS, tn=PROJ_COLS):
    rows = hs.shape[0]
    assert w_row0 % N_HEADS == 0 and ncols % tn == 0 and tn % N_HEADS == 0

    def w_rows(j, i):
        return ((w_row0 // N_HEADS + j * (tn // N_HEADS)) * N_HEADS, 0)

    return pl.pallas_call(
        functools.partial(_proj_kernel, silu=silu),
        grid=(ncols // tn, rows // tm),
        in_specs=[
            pl.BlockSpec((tm, D_MODEL), lambda j, i: (i, 0)),
            pl.BlockSpec((pl.Element(tn), pl.Element(D_MODEL)), w_rows),
        ],
        out_specs=pl.BlockSpec((tm, tn), lambda j, i: (i, j)),
        out_shape=jax.ShapeDtypeStruct((rows, ncols), F32),
        scratch_shapes=[pltpu.VMEM((D_MODEL, tn), BF16)],
        compiler_params=pltpu.CompilerParams(
            dimension_semantics=("arbitrary", "arbitrary"),
            vmem_limit_bytes=VMEM_LIMIT),
        name="proj",
    )(hs, w_t)


def _conv_rows(x, halo_ref, prev_ref, w_ref, *, first, bs, q, carry):
    taps = w_ref.shape[0]
    rt, width = x.shape

    @pl.when(first)
    def _():
        halo_ref[:, SUBLANES - (taps - 1):, :] = prev_ref[...]

    prev = halo_ref[...]
    acc = None
    if bs == 1:
        row = lax.broadcasted_iota(jnp.int32, (SUBLANES, width), 0)
        for s in range(taps - 1, 0, -1):
            rolled = pltpu.roll(x, s, 0)
            head = jnp.where(row < s, pltpu.roll(prev[0], s, 0), rolled[:SUBLANES])
            term = jnp.concatenate([head, rolled[SUBLANES:]], axis=0) * w_ref[taps - 1 - s:taps - s, :]
            acc = term if acc is None else acc + term
        acc = acc + x * w_ref[taps - 1:taps, :]
        if carry:
            halo_ref[0] = x[rt - SUBLANES:, :]
        return acc
    assert q == SUBLANES and not carry
    x3 = x.reshape(bs, q, width)
    row = lax.broadcasted_iota(jnp.int32, x3.shape, 1)
    for s in range(taps - 1, 0, -1):
        shifted = jnp.where(row < s, pltpu.roll(prev, s, 1), pltpu.roll(x3, s, 1))
        term = shifted * w_ref[taps - 1 - s:taps - s, :]
        acc = term if acc is None else acc + term
    acc = acc + x3 * w_ref[taps - 1:taps, :]
    return acc.reshape(rt, width)


def _seg_cumsum(a, q):
    pos = lax.broadcasted_iota(jnp.int32, a.shape, 0) & (q - 1)
    s = 1
    while s < q:
        shifted = pltpu.roll(a, s, 0)
        a = a + jnp.where(pos >= s, shifted, 0.0)
        s *= 2
    return a


def _headscal_kernel(dtr_ref, dtb_ref, alog_ref, at_ref, bt_ref, *, q, valid, tps):
    dtv = jax.nn.softplus(dtr_ref[...] + dtb_ref[...])
    if valid < ROW_TILE:
        rows = lax.broadcasted_iota(jnp.int32, dtv.shape, 0)
        dtv = jnp.where(rows < valid, dtv, 0.0)
    acum = _seg_cumsum(dtv * (-jnp.exp(alog_ref[...])), q) * LOG2_E
    a_minus_logdt = acum - jnp.log(dtv) * LOG2_E
    for t in range(tps):
        at_ref[t] = acum[t * ROW_TILE:(t + 1) * ROW_TILE].T
        bt_ref[t] = a_minus_logdt[t * ROW_TILE:(t + 1) * ROW_TILE].T


def _headscal(dt_raw, dt_bias, a_log, *, q, valid, tile0, ntiles):
    tps = min(ntiles, HEADSCAL_TILES)
    assert ntiles % tps == 0 and tile0 % tps == 0 and q <= ROW_TILE
    kern = functools.partial(_headscal_kernel, q=q, valid=valid, tps=tps)
    shape = jax.ShapeDtypeStruct((ntiles, LANES, ROW_TILE), F32)
    return pl.pallas_call(
        kern,
        grid=(ntiles // tps,),
        in_specs=[
            pl.BlockSpec((tps * ROW_TILE, LANES), lambda t: (tile0 // tps + t, 0)),
            pl.BlockSpec((1, LANES), lambda t: (0, 0)),
            pl.BlockSpec((1, LANES), lambda t: (0, 0)),
        ],
        out_specs=[
            pl.BlockSpec((tps, LANES, ROW_TILE), lambda t: (t, 0, 0)),
            pl.BlockSpec((tps, LANES, ROW_TILE), lambda t: (t, 0, 0)),
        ],
        out_shape=[shape, shape],
        compiler_params=pltpu.CompilerParams(dimension_semantics=("arbitrary",)),
        name="headscal",
    )(dt_raw, dt_bias, a_log)


def _xbc_kernel(hs_ref, hm_ref, wt_ref, cw_ref, cb_ref, prev_ref,
                o_ref, om_ref, cnp_ref, cns_ref,
                wb_scr, halo_p, halo_s, halo_m, meta_prev, zero_prev,
                *, tm, n_prompt_tiles, tiles_per_seq, bs_sample, q_sample):
    i = pl.program_id(1)
    keep = SSD_CONV_W - 1
    tn = o_ref.shape[1]

    @pl.when(i == 0)
    def _():
        wb_scr[...] = wt_ref[...].T.astype(BF16)

    raw = jnp.dot(hs_ref[...], wb_scr[...], preferred_element_type=F32)

    def activate(conv):
        return _silu(conv + cb_ref[...])

    @pl.when(i == 0)
    def _():
        raw_m = jnp.dot(hm_ref[...], wb_scr[...], preferred_element_type=F32)
        meta_prev[0] = raw_m[META - keep:]
        zero_prev[...] = jnp.zeros(zero_prev.shape, F32)
        conv_m = _conv_rows(raw_m, halo_m, zero_prev, cw_ref, first=i == 0, bs=1, q=META,
                            carry=False)
        om_ref[:META, :] = activate(conv_m)
        om_ref[META:, :] = jnp.zeros((ROW_TILE - META, tn), F32)

    @pl.when(i < n_prompt_tiles)
    def _():
        o_ref[...] = activate(_conv_rows(raw, halo_p, meta_prev, cw_ref,
                                         first=(i % tiles_per_seq) == 0, bs=1, q=tm, carry=True))
        cnp_ref[0] = raw[tm - keep:]

    @pl.when(i >= n_prompt_tiles)
    def _():
        o_ref[...] = activate(_conv_rows(raw, halo_s, prev_ref, cw_ref, first=i >= n_prompt_tiles,
                                         bs=bs_sample, q=q_sample, carry=False))
        cns_ref[...] = raw.reshape(bs_sample, q_sample, tn)[:, q_sample - keep:, :]


def _xbc(hs, hm, w_t, cw, cb, prev_s, *, n_prompt, seq, n_seq_p, n_seq_s, q_sample,
         tm=PROJ_ROWS, tn=XBC_COLS):
    rows = hs.shape[0]
    keep = SSD_CONV_W - 1
    assert seq % tm == 0 and (rows - n_prompt) == tm and tm == n_seq_s * q_sample
    assert D_INNER % tn == 0 and CONV_DIM % tn == 0
    tiles_per_seq = seq // tm
    n_pt = n_prompt // tm
    kern = functools.partial(_xbc_kernel, tm=tm, n_prompt_tiles=n_pt, tiles_per_seq=tiles_per_seq,
                             bs_sample=n_seq_s, q_sample=q_sample)
    return pl.pallas_call(
        kern,
        grid=(CONV_DIM // tn, rows // tm),
        in_specs=[
            pl.BlockSpec((tm, D_MODEL), lambda j, i: (i, 0)),
            pl.BlockSpec((META, D_MODEL), lambda j, i: (0, 0)),
            pl.BlockSpec((tn, D_MODEL), lambda j, i: (D_INNER // tn + j, 0)),
            pl.BlockSpec((SSD_CONV_W, tn), lambda j, i: (0, j)),
            pl.BlockSpec((1, tn), lambda j, i: (0, j)),
            pl.BlockSpec((n_seq_s, keep, tn), lambda j, i: (0, 0, j)),
        ],
        out_specs=[
            pl.BlockSpec((tm, tn), lambda j, i: (i, j)),
            pl.BlockSpec((ROW_TILE, tn), lambda j, i: (0, j)),
            pl.BlockSpec((1, keep, tn),
                         lambda j, i: (jnp.minimum(i, n_pt - 1) // tiles_per_seq, 0, j)),
            pl.BlockSpec((n_seq_s, keep, tn), lambda j, i: (0, 0, j)),
        ],
        out_shape=[
            jax.ShapeDtypeStruct((rows, CONV_DIM), F32),
            jax.ShapeDtypeStruct((ROW_TILE, CONV_DIM), F32),
            jax.ShapeDtypeStruct((n_seq_p, keep, CONV_DIM), F32),
            jax.ShapeDtypeStruct((n_seq_s, keep, CONV_DIM), F32),
        ],
        scratch_shapes=[
            pltpu.VMEM((D_MODEL, tn), BF16),
            pltpu.VMEM((1, SUBLANES, tn), F32),
            pltpu.VMEM((n_seq_s, SUBLANES, tn), F32),
            pltpu.VMEM((1, SUBLANES, tn), F32),
            pltpu.VMEM((1, keep, tn), F32),
            pltpu.VMEM((1, keep, tn), F32),
        ],
        compiler_params=pltpu.CompilerParams(
            dimension_semantics=("arbitrary", "arbitrary"),
            vmem_limit_bytes=VMEM_LIMIT),
        name="xbc",
    )(hs, hm, w_t, cw, cb, prev_s)


def _ssd_kernel(*refs, n_real, **static):
    yn_ref = refs[-3]
    s = pl.program_id(0)

    @pl.when(s < n_real)
    def _():
        _ssd_body(*refs, **static)

    @pl.when(s >= n_real)
    def _():
        yn_ref[...] = jnp.zeros(yn_ref.shape, yn_ref.dtype)


def _ssd_body(sz_ref, x_ref, b_ref, c_ref, at_ref, bt_ref, dsk_ref, nw_ref, s0_ref,
              *rest, bs, q, nc, gps):
    yn_ref, sout_ref, st_ref = rest[-3:]
    rt = bs * q
    nh = gps * HEADS_PER_GROUP
    c = pl.program_id(2)
    carry = nc > 1

    if carry:
        @pl.when(c == 0)
        def _():
            st_ref[...] = s0_ref[...]

    xc = x_ref[...]
    bcb = b_ref[...].astype(BF16)
    ccb = c_ref[...].astype(BF16)

    a_t = at_ref[0]
    b_t = bt_ref[0]
    cols = jnp.concatenate([a_t, jnp.zeros((LANES - nh, rt), F32)], axis=0).T

    pos = lax.broadcasted_iota(jnp.int32, (nh, rt), 1) & (q - 1)
    a_end = a_t
    s = 1
    while s < q:
        a_end = jnp.where(pos + s < q, pltpu.roll(a_end, rt - s, 1), a_end)
        s *= 2
    to_end = jnp.exp2(a_end - b_t)

    nblk = rt // SUBLANES
    ri = lax.broadcasted_iota(jnp.int32, (nblk, SUBLANES, rt), 0) * SUBLANES + \
        lax.broadcasted_iota(jnp.int32, (nblk, SUBLANES, rt), 1)
    ci = lax.broadcasted_iota(jnp.int32, (nblk, SUBLANES, rt), 2)
    mask = (ri >= ci) & ((ri // q) == (ci // q))
    low = lax.broadcasted_iota(jnp.int32, (rt, LANES), 1) < HEAD_DIM
    seq_of_row = lax.broadcasted_iota(jnp.int32, (rt, N_STATE), 0) // q

    for k in range(gps):
        xg = xc[:, k * GROUP_W:(k + 1) * GROUP_W]
        bg = bcb[:, k * N_STATE:(k + 1) * N_STATE]
        cg = ccb[:, k * N_STATE:(k + 1) * N_STATE]
        cb = lax.dot_general(cg, bg, (((1,), (1,)), ((), ())), preferred_element_type=F32)
        cb3 = cb.reshape(nblk, SUBLANES, rt)
        xt = xg.T

        ydiag, ea, xw, a_cols = [], [], [], []
        for pr in range(HEADS_PER_GROUP // 2):
            wts, ab = [], []
            for hh in range(2):
                h = k * HEADS_PER_GROUP + 2 * pr + hh
                a_col = jnp.broadcast_to(cols[:, h:h + 1], (rt, LANES))
                b_row = jnp.broadcast_to(b_t[h:h + 1, :], (SUBLANES, rt))
                seg = jnp.where(mask, a_col.reshape(nblk, SUBLANES, rt) - b_row[None], -jnp.inf)
                wts.append((cb3 * jnp.exp2(seg)).reshape(rt, rt).astype(BF16))
                ab.append(a_col)
                rows = slice((2 * pr + hh) * HEAD_DIM, (2 * pr + hh + 1) * HEAD_DIM)
                xw.append(xt[rows] * to_end[h:h + 1, :])
            a_cols += ab
            ea.append(jnp.exp2(jnp.where(low, ab[0], ab[1])))
            xb = xg[:, pr * LANES:(pr + 1) * LANES].astype(BF16)
            zero = jnp.zeros_like(xb)
            rhs = jnp.concatenate([jnp.where(low, xb, zero), jnp.where(low, zero, xb)], axis=0)
            ydiag.append(jnp.dot(jnp.concatenate(wts, axis=1), rhs, preferred_element_type=F32))
        ydiag = jnp.concatenate(ydiag, axis=1)
        ea = jnp.concatenate(ea, axis=1)
        xwt = jnp.concatenate(xw, axis=0).astype(BF16)

        yoff = []
        for s in range(bs):
            st = st_ref[s, k] if carry else s0_ref[s, k]
            yoff.append(lax.dot_general(cg[s * q:(s + 1) * q, :], st.astype(BF16),
                                        (((1,), (1,)), ((), ())), preferred_element_type=F32))
            bsel = bg if bs == 1 else jnp.where(seq_of_row == s, bg, jnp.zeros_like(bg))
            upd = jnp.dot(xwt, bsel, preferred_element_type=F32)
            last = (s + 1) * q - 1
            dec = jnp.concatenate(
                [jnp.broadcast_to(jnp.exp2(a_cols[h][last:last + 1, :]), (HEAD_DIM, N_STATE))
                 for h in range(HEADS_PER_GROUP)], axis=0)
            new = st * dec + upd
            if carry:
                st_ref[s, k] = new
            else:
                sout_ref[s, k] = new
        yoff = yoff[0] if bs == 1 else jnp.concatenate(yoff, axis=0)

        lanes = slice(k * GROUP_W, (k + 1) * GROUP_W)
        y = ydiag + yoff * ea + dsk_ref[:, lanes] * xg
        gz = y * sz_ref[:, lanes]
        ms = jnp.mean(gz * gz, axis=-1, keepdims=True)
        yn_ref[:, lanes] = (gz * lax.rsqrt(ms + EPS) * nw_ref[:, lanes]).astype(BF16)

    if carry:
        @pl.when(c == nc - 1)
        def _():
            sout_ref[...] = st_ref[...]


def _ssd(sz, xbc, a_t, dt_t, d_skip_x, norm_w, state0,
         *, n_seq, bs, q, nc, gps, shared_init, tile0=0, out_rows=None, zero_tail_tiles=0,
         yn_into=None):
    rt = bs * q
    assert rt == ROW_TILE and N_GROUPS % gps == 0
    rows = n_seq * q * nc if out_rows is None else out_rows
    otile0 = 0 if out_rows is None else tile0
    nsb = n_seq // bs
    n_pad = pl.cdiv(zero_tail_tiles, nc)
    gw, gn, nh = gps * GROUP_W, gps * N_STATE, gps * HEADS_PER_GROUP
    static = dict(bs=bs, q=q, nc=nc, gps=gps)
    kern = (functools.partial(_ssd_kernel, n_real=nsb, **static) if n_pad
            else functools.partial(_ssd_body, **static))
    bb, bc_ = D_INNER // gn, (D_INNER + N_GROUPS * N_STATE) // gn

    def real(s):
        return jnp.minimum(s, nsb - 1) if n_pad else s

    def tile(s, c):
        return jnp.where(s < nsb, s * nc + c, nsb * nc - 1) if n_pad else s * nc + c

    def otile(s, c):
        if not n_pad:
            return otile0 + s * nc + c
        tail = jnp.minimum((s - nsb) * nc + c, zero_tail_tiles - 1)
        return otile0 + jnp.where(s < nsb, s * nc + c, nsb * nc + tail)

    sidx = (lambda s: 0) if shared_init else real
    in_specs = [
        pl.BlockSpec((rt, gw), lambda s, g, c: (tile0 + tile(s, c), g)),
        pl.BlockSpec((rt, gw), lambda s, g, c: (tile0 + tile(s, c), g)),
        pl.BlockSpec((rt, gn), lambda s, g, c: (tile0 + tile(s, c), bb + g)),
        pl.BlockSpec((rt, gn), lambda s, g, c: (tile0 + tile(s, c), bc_ + g)),
        pl.BlockSpec((1, nh, rt), lambda s, g, c: (tile(s, c), g, 0)),
        pl.BlockSpec((1, nh, rt), lambda s, g, c: (tile(s, c), g, 0)),
        pl.BlockSpec((1, gw), lambda s, g, c: (0, g)),
        pl.BlockSpec((1, gw), lambda s, g, c: (0, g)),
        pl.BlockSpec((bs, gps, GROUP_W, N_STATE), lambda s, g, c: (sidx(s), g, 0, 0)),
    ]
    out_specs = [
        pl.BlockSpec((rt, gw), lambda s, g, c: (otile(s, c), g)),
        pl.BlockSpec((bs, gps, GROUP_W, N_STATE), lambda s, g, c: (real(s), g, 0, 0)),
    ]
    operands = [sz, xbc, xbc, xbc, a_t, dt_t, d_skip_x, norm_w, state0]
    aliases = {}
    if yn_into is not None:
        assert yn_into.shape == (rows, D_INNER)
        aliases = {len(operands): 0}
        in_specs.append(pl.BlockSpec(memory_space=pl.ANY))
        operands.append(yn_into)
    st_shape = (bs, gps, GROUP_W, N_STATE) if nc > 1 else (1, 1, SUBLANES, N_STATE)
    return pl.pallas_call(
        kern,
        grid=(nsb + n_pad, N_GROUPS // gps, nc),
        in_specs=in_specs,
        out_specs=out_specs,
        out_shape=[
            jax.ShapeDtypeStruct((rows, D_INNER), BF16),
            jax.ShapeDtypeStruct((n_seq, N_GROUPS, GROUP_W, N_STATE), F32),
        ],
        scratch_shapes=[pltpu.VMEM(st_shape, F32)],
        input_output_aliases=aliases,
        compiler_params=pltpu.CompilerParams(
            dimension_semantics=("arbitrary", "arbitrary", "arbitrary"),
            vmem_limit_bytes=VMEM_LIMIT),
        name="ssd",
    )(*operands)


def _sconv_kernel(hs_ref, hm_ref, wb_ref, wc_ref, wh_ref, wz_ref, cw_ref, prev_ref,
                  v_ref, newp_ref, news_ref,
                  w_scr, halo_p, halo_s, meta_u,
                  *, tm, width, n_prompt_tiles, tiles_per_seq, bs_sample, q_sample):
    i = pl.program_id(1)
    keep = SC_CONV_W - 1

    @pl.when(i == 0)
    def _():
        for k, w_ref in enumerate((wb_ref, wc_ref, wh_ref, wz_ref)):
            w_scr[:, k * width:(k + 1) * width] = w_ref[...].T.astype(BF16)
        rm = jnp.dot(hm_ref[...], w_scr[:, width:3 * width], preferred_element_type=F32)
        meta_u[0] = (rm[:, :width] * rm[:, width:])[META - keep:]

    r = jnp.dot(hs_ref[...], w_scr[...], preferred_element_type=F32)
    u = r[:, width:2 * width] * r[:, 2 * width:3 * width]

    def finish(uc):
        v_ref[...] = (r[:, :width] * uc * _silu(r[:, 3 * width:])).astype(BF16)

    @pl.when(i < n_prompt_tiles)
    def _():
        finish(_conv_rows(u, halo_p, meta_u, cw_ref, first=(i % tiles_per_seq) == 0,
                          bs=1, q=tm, carry=True))
        newp_ref[0] = u[tm - keep:]

    @pl.when(i >= n_prompt_tiles)
    def _():
        finish(_conv_rows(u, halo_s, prev_ref, cw_ref, first=i >= n_prompt_tiles,
                          bs=bs_sample, q=q_sample, carry=False))
        news_ref[...] = u.reshape(bs_sample, q_sample, width)[:, q_sample - keep:, :]


def _sconv(hs, hm, w_t, cw, prev_s, *, n_prompt, seq, n_seq_p, n_seq_s, q_sample,
           tm=PROJ_ROWS, width=SCONV_CHANNELS):
    rows = hs.shape[0]
    keep = SC_CONV_W - 1
    assert seq % tm == 0 and (rows - n_prompt) == tm and tm == n_seq_s * q_sample
    tiles_per_seq = seq // tm
    n_pt = n_prompt // tm
    kern = functools.partial(_sconv_kernel, tm=tm, width=width, n_prompt_tiles=n_pt,
                             tiles_per_seq=tiles_per_seq, bs_sample=n_seq_s, q_sample=q_sample)

    def w_rows(k):
        base = (W_SC + k * D_MODEL) // N_HEADS
        return lambda cbk, i: ((base + cbk * (width // N_HEADS)) * N_HEADS, 0)

    w_specs = [pl.BlockSpec((pl.Element(width), pl.Element(D_MODEL)), w_rows(k)) for k in range(4)]
    return pl.pallas_call(
        kern,
        grid=(D_MODEL // width, rows // tm),
        in_specs=[
            pl.BlockSpec((tm, D_MODEL), lambda cbk, i: (i, 0)),
            pl.BlockSpec((META, D_MODEL), lambda cbk, i: (0, 0)),
            *w_specs,
            pl.BlockSpec((SC_CONV_W, width), lambda cbk, i: (0, cbk)),
            pl.BlockSpec((n_seq_s, keep, width), lambda cbk, i: (0, 0, cbk)),
        ],
        out_specs=[
            pl.BlockSpec((tm, width), lambda cbk, i: (i, cbk)),
            pl.BlockSpec((1, keep, width),
                         lambda cbk, i: (jnp.minimum(i, n_pt - 1) // tiles_per_seq, 0, cbk)),
            pl.BlockSpec((n_seq_s, keep, width), lambda cbk, i: (0, 0, cbk)),
        ],
        out_shape=[
            jax.ShapeDtypeStruct((rows, D_MODEL), BF16),
            jax.ShapeDtypeStruct((n_seq_p, keep, D_MODEL), F32),
            jax.ShapeDtypeStruct((n_seq_s, keep, D_MODEL), F32),
        ],
        scratch_shapes=[
            pltpu.VMEM((D_MODEL, 4 * width), BF16),
            pltpu.VMEM((1, SUBLANES, width), F32),
            pltpu.VMEM((n_seq_s, SUBLANES, width), F32),
            pltpu.VMEM((1, keep, width), F32),
        ],
        compiler_params=pltpu.CompilerParams(
            dimension_semantics=("arbitrary", "arbitrary"),
            vmem_limit_bytes=VMEM_LIMIT),
        name="sconv",
    )(hs, hm, w_t, w_t, w_t, w_t, cw, prev_s)


def _merge_kernel(yn_ref, v_ref, ga_ref, gb_ref, wa_ref, wb_ref, o_ref, wa_scr, wb_scr):
    @pl.when(pl.program_id(1) == 0)
    def _():
        wa_scr[...] = wa_ref[...].astype(BF16)
        wb_scr[...] = wb_ref[...].astype(BF16)

    ya = jnp.dot(yn_ref[...], wa_scr[...], preferred_element_type=F32)
    yb = jnp.dot(v_ref[...], wb_scr[...], preferred_element_type=F32)
    o_ref[...] = (_sigmoid(ga_ref[...]) * ya + _sigmoid(gb_ref[...]) * yb).astype(BF16)


def _merge(yn, v, gates, wa, wb, *, tm=MERGE_ROWS, tn=MERGE_COLS):
    rows = yn.shape[0]
    return pl.pallas_call(
        _merge_kernel,
        grid=(D_MODEL // tn, rows // tm),
        in_specs=[
            pl.BlockSpec((tm, D_INNER), lambda j, i: (i, 0)),
            pl.BlockSpec((tm, D_MODEL), lambda j, i: (i, 0)),
            pl.BlockSpec((tm, tn), lambda j, i: (i, j)),
            pl.BlockSpec((tm, tn), lambda j, i: (i, D_MODEL // tn + j)),
            pl.BlockSpec((D_INNER, tn), lambda j, i: (0, j)),
            pl.BlockSpec((D_MODEL, tn), lambda j, i: (0, j)),
        ],
        out_specs=pl.BlockSpec((tm, tn), lambda j, i: (i, j)),
        out_shape=jax.ShapeDtypeStruct((rows, D_MODEL), BF16),
        scratch_shapes=[pltpu.VMEM((D_INNER, tn), BF16), pltpu.VMEM((D_MODEL, tn), BF16)],
        compiler_params=pltpu.CompilerParams(
            dimension_semantics=("arbitrary", "arbitrary"),
            vmem_limit_bytes=VMEM_LIMIT),
        name="merge",
    )(yn, v, gates, gates, wa, wb)


def _outproj_kernel(m_ref, x_ref, wo_ref, fw_ref, o_ref):
    y = x_ref[...] + jnp.dot(m_ref[...], wo_ref[...], preferred_element_type=F32)
    ms = jnp.mean(y * y, axis=-1, keepdims=True)
    o_ref[...] = y * lax.rsqrt(ms + EPS) * fw_ref[...]


def _outproj(m, x, wo, fw, *, row0, tm=OUTPROJ_ROWS):
    rows = x.shape[0]
    assert row0 % tm == 0 and rows % tm == 0
    t0 = row0 // tm
    return pl.pallas_call(
        _outproj_kernel,
        grid=(rows // tm,),
        in_specs=[
            pl.BlockSpec((tm, D_MODEL), lambda i: (t0 + i, 0)),
            pl.BlockSpec((tm, D_MODEL), lambda i: (i, 0)),
            pl.BlockSpec((D_MODEL, D_MODEL), lambda i: (0, 0), pipeline_mode=pl.Buffered(1)),
            pl.BlockSpec((1, D_MODEL), lambda i: (0, 0)),
        ],
        out_specs=pl.BlockSpec((tm, D_MODEL), lambda i: (i, 0)),
        out_shape=jax.ShapeDtypeStruct((rows, D_MODEL), F32),
        compiler_params=pltpu.CompilerParams(
            dimension_semantics=("arbitrary",),
            vmem_limit_bytes=VMEM_LIMIT),
        name="outproj",
    )(m, x, wo, fw)


def kernel(x_prompt, x_sample, state_ssd_conv, state_ssm, state_sconv, meta_tokens, norm_w,
           w_in, ssd_conv_w, ssd_conv_b, dt_bias, a_log, d_skip, ssd_norm_w, w_ssd_out,
           sconv_w, w_sconv_out, w_o, final_norm_w):
    bp, seq = x_prompt.shape[0], x_prompt.shape[1]
    bd, dec_seq = x_sample.shape[0], x_sample.shape[1]

    w_t = jnp.transpose(w_in[0])
    nw = norm_w[0].reshape(1, D_MODEL)
    fw = final_norm_w.reshape(1, D_MODEL)
    conv_w = ssd_conv_w[0]
    conv_b = ssd_conv_b[0].reshape(1, CONV_DIM)
    dtb = jnp.pad(dt_bias[0], (0, LANES - N_HEADS)).reshape(1, LANES)
    alog = jnp.pad(a_log[0], (0, LANES - N_HEADS)).reshape(1, LANES)
    dsk = jnp.repeat(d_skip[0], HEAD_DIM).reshape(1, D_INNER)
    gnw = ssd_norm_w[0].reshape(1, D_INNER)
    scw = sconv_w[0]

    xp = x_prompt.reshape(bp * seq, D_MODEL)
    xs = x_sample.reshape(bd * dec_seq, D_MODEL)
    n_p, n_s = bp * seq, bd * dec_seq
    streams = dict(n_prompt=n_p, seq=seq, n_seq_p=bp, n_seq_s=bd, q_sample=dec_seq)

    hs, hm, dt_raw, dt_raw_m = _norm(xp, xs, meta_tokens, nw, w_t)
    sz = _proj(hs, w_t, w_row0=0, ncols=D_INNER, silu=True)
    gates = _proj(hs, w_t, w_row0=W_GATE, ncols=2 * D_MODEL, silu=False)
    xbc, xbc_m, conv_p, conv_s = _xbc(hs, hm, w_t, conv_w, conv_b, state_ssd_conv[0], **streams)
    v_all, sc_p, sc_s = _sconv(hs, hm, w_t, scw, state_sconv[0], **streams)
    headscal = functools.partial(_headscal, dt_bias=dtb, a_log=alog)
    ssd = functools.partial(_ssd, d_skip_x=dsk, norm_w=gnw)

    at_m, dtt_m = headscal(dt_raw_m, q=ROW_TILE, valid=META, tile0=0, ntiles=1)
    _, ssm_m = ssd(jnp.zeros((ROW_TILE, D_INNER), F32), xbc_m, at_m, dtt_m,
                   state0=jnp.zeros((1, N_GROUPS, GROUP_W, N_STATE), F32),
                   n_seq=1, bs=1, q=ROW_TILE, nc=1, gps=GROUPS_PER_STEP_META, shared_init=False)

    at_p, dtt_p = headscal(dt_raw, q=ROW_TILE, valid=ROW_TILE, tile0=0, ntiles=n_p // ROW_TILE)
    yn, ssm_p = ssd(sz, xbc, at_p, dtt_p, state0=ssm_m, n_seq=bp, bs=1, q=ROW_TILE,
                    nc=seq // ROW_TILE, gps=GROUPS_PER_STEP_PROMPT, shared_init=True,
                    out_rows=n_p + n_s,
                    zero_tail_tiles=n_s // ROW_TILE)

    sbs = ROW_TILE // dec_seq
    at_s, dtt_s = headscal(dt_raw, q=dec_seq, valid=ROW_TILE, tile0=n_p // ROW_TILE,
                           ntiles=n_s // ROW_TILE)
    yn, ssm_s = ssd(sz, xbc, at_s, dtt_s,
                    state0=state_ssm[0].reshape(bd, N_GROUPS, GROUP_W, N_STATE),
                    n_seq=bd, bs=sbs, q=dec_seq, nc=1, gps=GROUPS_PER_STEP_SAMPLE,
                    shared_init=False,
                    tile0=n_p // ROW_TILE, out_rows=n_p + n_s, yn_into=yn)

    merged = _merge(yn, v_all, gates, w_ssd_out[0], w_sconv_out[0])
    wo = w_o[0].astype(BF16)
    y_p = _outproj(merged, xp, wo, fw, row0=0)
    y_s = _outproj(merged, xs, wo, fw, row0=n_p)

    return (y_p.reshape(bp, seq, D_MODEL),
            y_s.reshape(bd, dec_seq, D_MODEL),
            conv_p[None],
            ssm_p.reshape(1, bp, N_HEADS, HEAD_DIM, N_STATE),
            sc_p[None],
            conv_s[None],
            ssm_s.reshape(1, bd, N_HEADS, HEAD_DIM, N_STATE),
            sc_s[None])
```

```python
import functools

import jax
import jax.numpy as jnp
from jax import lax
from jax.experimental import pallas as pl
from jax.experimental.pallas import tpu as pltpu

F32 = jnp.float32
BF16 = jnp.bfloat16

D_MODEL = 2048
D_INNER = 4096
N_HEADS = 64
HEAD_DIM = 64
N_STATE = 128
N_GROUPS = 8
GROUP_W = D_INNER // N_GROUPS
HEADS_PER_GROUP = N_HEADS // N_GROUPS
CONV_DIM = D_INNER + 2 * N_GROUPS * N_STATE
SSD_CONV_W = 4
SC_CONV_W = 3
META = 16
EPS = 1e-6
LOG2_E = 1.4426950408889634

LANES = 128
SUBLANES = 8
ROW_TILE = 128
W_DT = D_INNER + CONV_DIM
W_SC = W_DT + N_HEADS
W_GATE = W_SC + 4 * D_MODEL

VMEM_LIMIT = 52 * 1024 * 1024
NORM_ROWS = 512
PROJ_ROWS, PROJ_COLS = 1024, 1024
XBC_COLS = 512
SCONV_CHANNELS = 256
MERGE_ROWS, MERGE_COLS = 512, 512
OUTPROJ_ROWS = 512
HEADSCAL_TILES = 8
GROUPS_PER_STEP_PROMPT = N_GROUPS
GROUPS_PER_STEP_SAMPLE = 2
GROUPS_PER_STEP_META = 2


def _sigmoid(x):
    return 0.5 * (1.0 + jnp.tanh(0.5 * x))


def _silu(x):
    h = 0.5 * x
    return h * (1.0 + jnp.tanh(h))


def _rms_bf16(x, w):
    ms = jnp.mean(x * x, axis=-1, keepdims=True)
    return (x * lax.rsqrt(ms + EPS) * w).astype(BF16)


def _norm_kernel(xp_ref, xs_ref, xm_ref, nw_ref, wdt_ref, hs_ref, hm_ref, dt_ref, dtm_ref,
                 wdt_scr, *, n_prompt):
    i = pl.program_id(0)

    def dt_proj(h):
        return lax.dot_general(h, wdt_scr[...], (((1,), (1,)), ((), ())),
                               preferred_element_type=F32)

    @pl.when(i == 0)
    def _():
        wdt_scr[...] = wdt_ref[...].astype(BF16)
        hm_ref[:META, :] = _rms_bf16(xm_ref[...], nw_ref[...])
        hm_ref[META:, :] = jnp.zeros((ROW_TILE - META, D_MODEL), BF16)
        dtm_ref[...] = dt_proj(hm_ref[...])

    @pl.when(i < n_prompt)
    def _():
        hs_ref[...] = _rms_bf16(xp_ref[...], nw_ref[...])

    @pl.when(i >= n_prompt)
    def _():
        hs_ref[...] = _rms_bf16(xs_ref[...], nw_ref[...])

    dt_ref[...] = dt_proj(hs_ref[...])


def _norm(xp, xs, xm, norm_w, w_t, *, tm=NORM_ROWS):
    n_p, n_s = xp.shape[0] // tm, xs.shape[0] // tm
    rows = xp.shape[0] + xs.shape[0]
    kern = functools.partial(_norm_kernel, n_prompt=n_p)
    return pl.pallas_call(
        kern,
        grid=(n_p + n_s,),
        in_specs=[
            pl.BlockSpec((tm, D_MODEL), lambda i: (jnp.minimum(i, n_p - 1), 0)),
            pl.BlockSpec((tm, D_MODEL), lambda i: (jnp.maximum(i - n_p, 0), 0)),
            pl.BlockSpec((META, D_MODEL), lambda i: (0, 0)),
            pl.BlockSpec((1, D_MODEL), lambda i: (0, 0)),
            pl.BlockSpec((LANES, D_MODEL), lambda i: (W_DT // LANES, 0)),
        ],
        out_specs=[
            pl.BlockSpec((tm, D_MODEL), lambda i: (i, 0)),
            pl.BlockSpec((ROW_TILE, D_MODEL), lambda i: (0, 0)),
            pl.BlockSpec((tm, LANES), lambda i: (i, 0)),
            pl.BlockSpec((ROW_TILE, LANES), lambda i: (0, 0)),
        ],
        out_shape=[
            jax.ShapeDtypeStruct((rows, D_MODEL), BF16),
            jax.ShapeDtypeStruct((ROW_TILE, D_MODEL), BF16),
            jax.ShapeDtypeStruct((rows, LANES), F32),
            jax.ShapeDtypeStruct((ROW_TILE, LANES), F32),
        ],
        scratch_shapes=[pltpu.VMEM((LANES, D_MODEL), BF16)],
        compiler_params=pltpu.CompilerParams(
            dimension_semantics=("arbitrary",), vmem_limit_bytes=VMEM_LIMIT),
        name="norm",
    )(xp, xs, xm, norm_w, w_t)


def _proj_kernel(hs_ref, wt_ref, o_ref, wb_ref, *, silu):
    @pl.when(pl.program_id(1) == 0)
    def _():
        wb_ref[...] = wt_ref[...].T.astype(BF16)

    r = jnp.dot(hs_ref[...], wb_ref[...], preferred_element_type=F32)
    o_ref[...] = _silu(r) if silu else r


def _proj(hs, w_t, *, w_row0, ncols, silu, tm=PROJ_ROWS, tn=PROJ_COLS):
    rows = hs.shape[0]
    assert w_row0 % N_HEADS == 0 and ncols % tn == 0 and tn % N_HEADS == 0

    def w_rows(j, i):
        return ((w_row0 // N_HEADS + j * (tn // N_HEADS)) * N_HEADS, 0)

    return pl.pallas_call(
        functools.partial(_proj_kernel, silu=silu),
        grid=(ncols // tn, rows // tm),
        in_specs=[
            pl.BlockSpec((tm, D_MODEL), lambda j, i: (i, 0)),
            pl.BlockSpec((pl.Element(tn), pl.Element(D_MODEL)), w_rows),
        ],
        out_specs=pl.BlockSpec((tm, tn), lambda j, i: (i, j)),
        out_shape=jax.ShapeDtypeStruct((rows, ncols), F32),
        scratch_shapes=[pltpu.VMEM((D_MODEL, tn), BF16)],
        compiler_params=pltpu.CompilerParams(
            dimension_semantics=("arbitrary", "arbitrary"),
            vmem_limit_bytes=VMEM_LIMIT),
        name="proj",
    )(hs, w_t)


def _conv_rows(x, halo_ref, prev_ref, w_ref, *, first, bs, q, carry):
    taps = w_ref.shape[0]
    rt, width = x.shape

    @pl.when(first)
    def _():
        halo_ref[:, SUBLANES - (taps - 1):, :] = prev_ref[...]

    prev = halo_ref[...]
    acc = None
    if bs == 1:
        row = lax.broadcasted_iota(jnp.int32, (SUBLANES, width), 0)
        for s in range(taps - 1, 0, -1):
            rolled = pltpu.roll(x, s, 0)
            head = jnp.where(row < s, pltpu.roll(prev[0], s, 0), rolled[:SUBLANES])
            term = jnp.concatenate([head, rolled[SUBLANES:]], axis=0) * w_ref[taps - 1 - s:taps - s, :]
            acc = term if acc is None else acc + term
        acc = acc + x * w_ref[taps - 1:taps, :]
        if carry:
            halo_ref[0] = x[rt - SUBLANES:, :]
        return acc
    assert q == SUBLANES and not carry
    x3 = x.reshape(bs, q, width)
    row = lax.broadcasted_iota(jnp.int32, x3.shape, 1)
    for s in range(taps - 1, 0, -1):
        shifted = jnp.where(row < s, pltpu.roll(prev, s, 1), pltpu.roll(x3, s, 1))
        term = shifted * w_ref[taps - 1 - s:taps - s, :]
        acc = term if acc is None else acc + term
    acc = acc + x3 * w_ref[taps - 1:taps, :]
    return acc.reshape(rt, width)


def _seg_cumsum(a, q):
    pos = lax.broadcasted_iota(jnp.int32, a.shape, 0) & (q - 1)
    s = 1
    while s < q:
        shifted = pltpu.roll(a, s, 0)
        a = a + jnp.where(pos >= s, shifted, 0.0)
        s *= 2
    return a


def _headscal_kernel(dtr_ref, dtb_ref, alog_ref, at_ref, bt_ref, *, q, valid, tps):
    dtv = jax.nn.softplus(dtr_ref[...] + dtb_ref[...])
    if valid < ROW_TILE:
        rows = lax.broadcasted_iota(jnp.int32, dtv.shape, 0)
        dtv = jnp.where(rows < valid, dtv, 0.0)
    acum = _seg_cumsum(dtv * (-jnp.exp(alog_ref[...])), q) * LOG2_E
    a_minus_logdt = acum - jnp.log(dtv) * LOG2_E
    for t in range(tps):
        at_ref[t] = acum[t * ROW_TILE:(t + 1) * ROW_TILE].T
        bt_ref[t] = a_minus_logdt[t * ROW_TILE:(t + 1) * ROW_TILE].T


def _headscal(dt_raw, dt_bias, a_log, *, q, valid, tile0, ntiles):
    tps = min(ntiles, HEADSCAL_TILES)
    assert ntiles % tps == 0 and tile0 % tps == 0 and q <= ROW_TILE
    kern = functools.partial(_headscal_kernel, q=q, valid=valid, tps=tps)
    shape = jax.ShapeDtypeStruct((ntiles, LANES, ROW_TILE), F32)
    return pl.pallas_call(
        kern,
        grid=(ntiles // tps,),
        in_specs=[
            pl.BlockSpec((tps * ROW_TILE, LANES), lambda t: (tile0 // tps + t, 0)),
            pl.BlockSpec((1, LANES), lambda t: (0, 0)),
            pl.BlockSpec((1, LANES), lambda t: (0, 0)),
        ],
        out_specs=[
            pl.BlockSpec((tps, LANES, ROW_TILE), lambda t: (t, 0, 0)),
            pl.BlockSpec((tps, LANES, ROW_TILE), lambda t: (t, 0, 0)),
        ],
        out_shape=[shape, shape],
        compiler_params=pltpu.CompilerParams(dimension_semantics=("arbitrary",)),
        name="headscal",
    )(dt_raw, dt_bias, a_log)


def _xbc_kernel(hs_ref, hm_ref, wt_ref, cw_ref, cb_ref, prev_ref,
                o_ref, om_ref, cnp_ref, cns_ref,
                wb_scr, halo_p, halo_s, halo_m, meta_prev, zero_prev,
                *, tm, n_prompt_tiles, tiles_per_seq, bs_sample, q_sample):
    i = pl.program_id(1)
    keep = SSD_CONV_W - 1
    tn = o_ref.shape[1]

    @pl.when(i == 0)
    def _():
        wb_scr[...] = wt_ref[...].T.astype(BF16)

    raw = jnp.dot(hs_ref[...], wb_scr[...], preferred_element_type=F32)

    def activate(conv):
        return _silu(conv + cb_ref[...])

    @pl.when(i == 0)
    def _():
        raw_m = jnp.dot(hm_ref[...], wb_scr[...], preferred_element_type=F32)
        meta_prev[0] = raw_m[META - keep:]
        zero_prev[...] = jnp.zeros(zero_prev.shape, F32)
        conv_m = _conv_rows(raw_m, halo_m, zero_prev, cw_ref, first=i == 0, bs=1, q=META,
                            carry=False)
        om_ref[:META, :] = activate(conv_m)
        om_ref[META:, :] = jnp.zeros((ROW_TILE - META, tn), F32)

    @pl.when(i < n_prompt_tiles)
    def _():
        o_ref[...] = activate(_conv_rows(raw, halo_p, meta_prev, cw_ref,
                                         first=(i % tiles_per_seq) == 0, bs=1, q=tm, carry=True))
        cnp_ref[0] = raw[tm - keep:]

    @pl.when(i >= n_prompt_tiles)
    def _():
        o_ref[...] = activate(_conv_rows(raw, halo_s, prev_ref, cw_ref, first=i >= n_prompt_tiles,
                                         bs=bs_sample, q=q_sample, carry=False))
        cns_ref[...] = raw.reshape(bs_sample, q_sample, tn)[:, q_sample - keep:, :]


def _xbc(hs, hm, w_t, cw, cb, prev_s, *, n_prompt, seq, n_seq_p, n_seq_s, q_sample,
         tm=PROJ_ROWS, tn=XBC_COLS):
    rows = hs.shape[0]
    keep = SSD_CONV_W - 1
    assert seq % tm == 0 and (rows - n_prompt) == tm and tm == n_seq_s * q_sample
    assert D_INNER % tn == 0 and CONV_DIM % tn == 0
    tiles_per_seq = seq // tm
    n_pt = n_prompt // tm
    kern = functools.partial(_xbc_kernel, tm=tm, n_prompt_tiles=n_pt, tiles_per_seq=tiles_per_seq,
                             bs_sample=n_seq_s, q_sample=q_sample)
    return pl.pallas_call(
        kern,
        grid=(CONV_DIM // tn, rows // tm),
        in_specs=[
            pl.BlockSpec((tm, D_MODEL), lambda j, i: (i, 0)),
            pl.BlockSpec((META, D_MODEL), lambda j, i: (0, 0)),
            pl.BlockSpec((tn, D_MODEL), lambda j, i: (D_INNER // tn + j, 0)),
            pl.BlockSpec((SSD_CONV_W, tn), lambda j, i: (0, j)),
            pl.BlockSpec((1, tn), lambda j, i: (0, j)),
            pl.BlockSpec((n_seq_s, keep, tn), lambda j, i: (0, 0, j)),
        ],
        out_specs=[
            pl.BlockSpec((tm, tn), lambda j, i: (i, j)),
            pl.BlockSpec((ROW_TILE, tn), lambda j, i: (0, j)),
            pl.BlockSpec((1, keep, tn),
                         lambda j, i: (jnp.minimum(i, n_pt - 1) // tiles_per_seq, 0, j)),
            pl.BlockSpec((n_seq_s, keep, tn), lambda j, i: (0, 0, j)),
        ],
        out_shape=[
            jax.ShapeDtypeStruct((rows, CONV_DIM), F32),
            jax.ShapeDtypeStruct((ROW_TILE, CONV_DIM), F32),
            jax.ShapeDtypeStruct((n_seq_p, keep, CONV_DIM), F32),
            jax.ShapeDtypeStruct((n_seq_s, keep, CONV_DIM), F32),
        ],
        scratch_shapes=[
            pltpu.VMEM((D_MODEL, tn), BF16),
            pltpu.VMEM((1, SUBLANES, tn), F32),
            pltpu.VMEM((n_seq_s, SUBLANES, tn), F32),
            pltpu.VMEM((1, SUBLANES, tn), F32),
            pltpu.VMEM((1, keep, tn), F32),
            pltpu.VMEM((1, keep, tn), F32),
        ],
        compiler_params=pltpu.CompilerParams(
            dimension_semantics=("arbitrary", "arbitrary"),
            vmem_limit_bytes=VMEM_LIMIT),
        name="xbc",
    )(hs, hm, w_t, cw, cb, prev_s)


def _ssd_kernel(*refs, n_real, **static):
    yn_ref = refs[-3]
    s = pl.program_id(0)

    @pl.when(s < n_real)
    def _():
        _ssd_body(*refs, **static)

    @pl.when(s >= n_real)
    def _():
        yn_ref[...] = jnp.zeros(yn_ref.shape, yn_ref.dtype)


def _ssd_body(sz_ref, x_ref, b_ref, c_ref, at_ref, bt_ref, dsk_ref, nw_ref, s0_ref,
              *rest, bs, q, nc, gps):
    yn_ref, sout_ref, st_ref = rest[-3:]
    rt = bs * q
    nh = gps * HEADS_PER_GROUP
    c = pl.program_id(2)
    carry = nc > 1

    if carry:
        @pl.when(c == 0)
        def _():
            st_ref[...] = s0_ref[...]

    xc = x_ref[...]
    bcb = b_ref[...].astype(BF16)
    ccb = c_ref[...].astype(BF16)

    a_t = at_ref[0]
    b_t = bt_ref[0]
    cols = jnp.concatenate([a_t, jnp.zeros((LANES - nh, rt), F32)], axis=0).T

    if bs == 1:
        a_end = jnp.broadcast_to(a_t[:, rt - 1:rt], (nh, rt))
    else:
        pos = lax.broadcasted_iota(jnp.int32, (nh, rt), 1) & (q - 1)
        a_end = a_t
        s = 1
        while s < q:
            a_end = jnp.where(pos + s < q, pltpu.roll(a_end, rt - s, 1), a_end)
            s *= 2
    to_end = jnp.exp2(a_end - b_t)

    nblk = rt // SUBLANES
    ri = lax.broadcasted_iota(jnp.int32, (nblk, SUBLANES, rt), 0) * SUBLANES + \
        lax.broadcasted_iota(jnp.int32, (nblk, SUBLANES, rt), 1)
    ci = lax.broadcasted_iota(jnp.int32, (nblk, SUBLANES, rt), 2)
    mask = (ri >= ci) & ((ri // q) == (ci // q))
    low = lax.broadcasted_iota(jnp.int32, (rt, LANES), 1) < HEAD_DIM
    seq_of_row = lax.broadcasted_iota(jnp.int32, (rt, N_STATE), 0) // q

    for k in range(gps):
        xg = xc[:, k * GROUP_W:(k + 1) * GROUP_W]
        bg = bcb[:, k * N_STATE:(k + 1) * N_STATE]
        cg = ccb[:, k * N_STATE:(k + 1) * N_STATE]
        cb = lax.dot_general(cg, bg, (((1,), (1,)), ((), ())), preferred_element_type=F32)
        cb3 = cb.reshape(nblk, SUBLANES, rt)
        xt = xg.T

        ydiag, ea, xw, a_cols = [], [], [], []
        for pr in range(HEADS_PER_GROUP // 2):
            wts, ab = [], []
            for hh in range(2):
                h = k * HEADS_PER_GROUP + 2 * pr + hh
                a_col = jnp.broadcast_to(cols[:, h:h + 1], (rt, LANES))
                b_row = jnp.broadcast_to(b_t[h:h + 1, :], (SUBLANES, rt))
                seg = jnp.where(mask, a_col.reshape(nblk, SUBLANES, rt) - b_row[None], -jnp.inf)
                wts.append((cb3 * jnp.exp2(seg)).reshape(rt, rt).astype(BF16))
                ab.append(a_col)
                rows = slice((2 * pr + hh) * HEAD_DIM, (2 * pr + hh + 1) * HEAD_DIM)
                xw.append(xt[rows] * to_end[h:h + 1, :])
            a_cols += ab
            ea.append(jnp.exp2(jnp.where(low, ab[0], ab[1])))
            xb = xg[:, pr * LANES:(pr + 1) * LANES].astype(BF16)
            zero = jnp.zeros_like(xb)
            rhs = jnp.concatenate([jnp.where(low, xb, zero), jnp.where(low, zero, xb)], axis=0)
            ydiag.append(jnp.dot(jnp.concatenate(wts, axis=1), rhs, preferred_element_type=F32))
        ydiag = jnp.concatenate(ydiag, axis=1)
        ea = jnp.concatenate(ea, axis=1)
        xwt = jnp.concatenate(xw, axis=0).astype(BF16)

        yoff = []
        for s in range(bs):
            st = st_ref[s, k] if carry else s0_ref[s, k]
            yoff.append(lax.dot_general(cg[s * q:(s + 1) * q, :], st.astype(BF16),
                                        (((1,), (1,)), ((), ())), preferred_element_type=F32))
            bsel = bg if bs == 1 else jnp.where(seq_of_row == s, bg, jnp.zeros_like(bg))
            upd = jnp.dot(xwt, bsel, preferred_element_type=F32)
            last = (s + 1) * q - 1
            dec = jnp.concatenate(
                [jnp.broadcast_to(jnp.exp2(a_cols[h][last:last + 1, :]), (HEAD_DIM, N_STATE))
                 for h in range(HEADS_PER_GROUP)], axis=0)
            new = st * dec + upd
            if carry:
                st_ref[s, k] = new
            else:
                sout_ref[s, k] = new
        yoff = yoff[0] if bs == 1 else jnp.concatenate(yoff, axis=0)

        lanes = slice(k * GROUP_W, (k + 1) * GROUP_W)
        y = ydiag + yoff * ea + dsk_ref[:, lanes] * xg
        gz = y * sz_ref[:, lanes]
        ms = jnp.mean(gz * gz, axis=-1, keepdims=True)
        yn_ref[:, lanes] = (gz * lax.rsqrt(ms + EPS) * nw_ref[:, lanes]).astype(BF16)

    if carry:
        @pl.when(c == nc - 1)
        def _():
            sout_ref[...] = st_ref[...]


def _ssd(sz, xbc, a_t, dt_t, d_skip_x, norm_w, state0,
         *, n_seq, bs, q, nc, gps, shared_init, tile0=0, out_rows=None, zero_tail_tiles=0,
         yn_into=None):
    rt = bs * q
    assert rt == ROW_TILE and N_GROUPS % gps == 0
    rows = n_seq * q * nc if out_rows is None else out_rows
    otile0 = 0 if out_rows is None else tile0
    nsb = n_seq // bs
    n_pad = pl.cdiv(zero_tail_tiles, nc)
    gw, gn, nh = gps * GROUP_W, gps * N_STATE, gps * HEADS_PER_GROUP
    static = dict(bs=bs, q=q, nc=nc, gps=gps)
    kern = (functools.partial(_ssd_kernel, n_real=nsb, **static) if n_pad
            else functools.partial(_ssd_body, **static))
    bb, bc_ = D_INNER // gn, (D_INNER + N_GROUPS * N_STATE) // gn

    def real(s):
        return jnp.minimum(s, nsb - 1) if n_pad else s

    def tile(s, c):
        return jnp.where(s < nsb, s * nc + c, nsb * nc - 1) if n_pad else s * nc + c

    def otile(s, c):
        if not n_pad:
            return otile0 + s * nc + c
        tail = jnp.minimum((s - nsb) * nc + c, zero_tail_tiles - 1)
        return otile0 + jnp.where(s < nsb, s * nc + c, nsb * nc + tail)

    sidx = (lambda s: 0) if shared_init else real
    in_specs = [
        pl.BlockSpec((rt, gw), lambda s, g, c: (tile0 + tile(s, c), g)),
        pl.BlockSpec((rt, gw), lambda s, g, c: (tile0 + tile(s, c), g)),
        pl.BlockSpec((rt, gn), lambda s, g, c: (tile0 + tile(s, c), bb + g)),
        pl.BlockSpec((rt, gn), lambda s, g, c: (tile0 + tile(s, c), bc_ + g)),
        pl.BlockSpec((1, nh, rt), lambda s, g, c: (tile(s, c), g, 0)),
        pl.BlockSpec((1, nh, rt), lambda s, g, c: (tile(s, c), g, 0)),
        pl.BlockSpec((1, gw), lambda s, g, c: (0, g)),
        pl.BlockSpec((1, gw), lambda s, g, c: (0, g)),
        pl.BlockSpec((bs, gps, GROUP_W, N_STATE), lambda s, g, c: (sidx(s), g, 0, 0)),
    ]
    out_specs = [
        pl.BlockSpec((rt, gw), lambda s, g, c: (otile(s, c), g)),
        pl.BlockSpec((bs, gps, GROUP_W, N_STATE), lambda s, g, c: (real(s), g, 0, 0)),
    ]
    operands = [sz, xbc, xbc, xbc, a_t, dt_t, d_skip_x, norm_w, state0]
    aliases = {}
    if yn_into is not None:
        assert yn_into.shape == (rows, D_INNER)
        aliases = {len(operands): 0}
        in_specs.append(pl.BlockSpec(memory_space=pl.ANY))
        operands.append(yn_into)
    st_shape = (bs, gps, GROUP_W, N_STATE) if nc > 1 else (1, 1, SUBLANES, N_STATE)
    return pl.pallas_call(
        kern,
        grid=(nsb + n_pad, N_GROUPS // gps, nc),
        in_specs=in_specs,
        out_specs=out_specs,
        out_shape=[
            jax.ShapeDtypeStruct((rows, D_INNER), BF16),
            jax.ShapeDtypeStruct((n_seq, N_GROUPS, GROUP_W, N_STATE), F32),
        ],
        scratch_shapes=[pltpu.VMEM(st_shape, F32)],
        input_output_aliases=aliases,
        compiler_params=pltpu.CompilerParams(
            dimension_semantics=("arbitrary", "arbitrary", "arbitrary"),
            vmem_limit_bytes=VMEM_LIMIT),
        name="ssd",
    )(*operands)


def _sconv_kernel(hs_ref, hm_ref, wb_ref, wc_ref, wh_ref, wz_ref, cw_ref, prev_ref,
                  v_ref, newp_ref, news_ref,
                  w_scr, halo_p, halo_s, meta_u,
                  *, tm, width, n_prompt_tiles, tiles_per_seq, bs_sample, q_sample):
    i = pl.program_id(1)
    keep = SC_CONV_W - 1

    @pl.when(i == 0)
    def _():
        for k, w_ref in enumerate((wb_ref, wc_ref, wh_ref, wz_ref)):
            w_scr[:, k * width:(k + 1) * width] = w_ref[...].T.astype(BF16)
        rm = jnp.dot(hm_ref[...], w_scr[:, width:3 * width], preferred_element_type=F32)
        meta_u[0] = (rm[:, :width] * rm[:, width:])[META - keep:]

    r = jnp.dot(hs_ref[...], w_scr[...], preferred_element_type=F32)
    u = r[:, width:2 * width] * r[:, 2 * width:3 * width]

    def finish(uc):
        v_ref[...] = (r[:, :width] * uc * _silu(r[:, 3 * width:])).astype(BF16)

    @pl.when(i < n_prompt_tiles)
    def _():
        finish(_conv_rows(u, halo_p, meta_u, cw_ref, first=(i % tiles_per_seq) == 0,
                          bs=1, q=tm, carry=True))
        newp_ref[0] = u[tm - keep:]

    @pl.when(i >= n_prompt_tiles)
    def _():
        finish(_conv_rows(u, halo_s, prev_ref, cw_ref, first=i >= n_prompt_tiles,
                          bs=bs_sample, q=q_sample, carry=False))
        news_ref[...] = u.reshape(bs_sample, q_sample, width)[:, q_sample - keep:, :]


def _sconv(hs, hm, w_t, cw, prev_s, *, n_prompt, seq, n_seq_p, n_seq_s, q_sample,
           tm=PROJ_ROWS, width=SCONV_CHANNELS):
    rows = hs.shape[0]
    keep = SC_CONV_W - 1
    assert seq % tm == 0 and (rows - n_prompt) == tm and tm == n_seq_s * q_sample
    tiles_per_seq = seq // tm
    n_pt = n_prompt // tm
    kern = functools.partial(_sconv_kernel, tm=tm, width=width, n_prompt_tiles=n_pt,
                             tiles_per_seq=tiles_per_seq, bs_sample=n_seq_s, q_sample=q_sample)

    def w_rows(k):
        base = (W_SC + k * D_MODEL) // N_HEADS
        return lambda cbk, i: ((base + cbk * (width // N_HEADS)) * N_HEADS, 0)

    w_specs = [pl.BlockSpec((pl.Element(width), pl.Element(D_MODEL)), w_rows(k)) for k in range(4)]
    return pl.pallas_call(
        kern,
        grid=(D_MODEL // width, rows // tm),
        in_specs=[
            pl.BlockSpec((tm, D_MODEL), lambda cbk, i: (i, 0)),
            pl.BlockSpec((META, D_MODEL), lambda cbk, i: (0, 0)),
            *w_specs,
            pl.BlockSpec((SC_CONV_W, width), lambda cbk, i: (0, cbk)),
            pl.BlockSpec((n_seq_s, keep, width), lambda cbk, i: (0, 0, cbk)),
        ],
        out_specs=[
            pl.BlockSpec((tm, width), lambda cbk, i: (i, cbk)),
            pl.BlockSpec((1, keep, width),
                         lambda cbk, i: (jnp.minimum(i, n_pt - 1) // tiles_per_seq, 0, cbk)),
            pl.BlockSpec((n_seq_s, keep, width), lambda cbk, i: (0, 0, cbk)),
        ],
        out_shape=[
            jax.ShapeDtypeStruct((rows, D_MODEL), BF16),
            jax.ShapeDtypeStruct((n_seq_p, keep, D_MODEL), F32),
            jax.ShapeDtypeStruct((n_seq_s, keep, D_MODEL), F32),
        ],
        scratch_shapes=[
            pltpu.VMEM((D_MODEL, 4 * width), BF16),
            pltpu.VMEM((1, SUBLANES, width), F32),
            pltpu.VMEM((n_seq_s, SUBLANES, width), F32),
            pltpu.VMEM((1, keep, width), F32),
        ],
        compiler_params=pltpu.CompilerParams(
            dimension_semantics=("arbitrary", "arbitrary"),
            vmem_limit_bytes=VMEM_LIMIT),
        name="sconv",
    )(hs, hm, w_t, w_t, w_t, w_t, cw, prev_s)


def _merge_kernel(yn_ref, v_ref, ga_ref, gb_ref, wa_ref, wb_ref, o_ref, wa_scr, wb_scr):
    @pl.when(pl.program_id(1) == 0)
    def _():
        wa_scr[...] = wa_ref[...].astype(BF16)
        wb_scr[...] = wb_ref[...].astype(BF16)

    ya = jnp.dot(yn_ref[...], wa_scr[...], preferred_element_type=F32)
    yb = jnp.dot(v_ref[...], wb_scr[...], preferred_element_type=F32)
    o_ref[...] = (_sigmoid(ga_ref[...]) * ya + _sigmoid(gb_ref[...]) * yb).astype(BF16)


def _merge(yn, v, gates, wa, wb, *, tm=MERGE_ROWS, tn=MERGE_COLS):
    rows = yn.shape[0]
    return pl.pallas_call(
        _merge_kernel,
        grid=(D_MODEL // tn, rows // tm),
        in_specs=[
            pl.BlockSpec((tm, D_INNER), lambda j, i: (i, 0)),
            pl.BlockSpec((tm, D_MODEL), lambda j, i: (i, 0)),
            pl.BlockSpec((tm, tn), lambda j, i: (i, j)),
            pl.BlockSpec((tm, tn), lambda j, i: (i, D_MODEL // tn + j)),
            pl.BlockSpec((D_INNER, tn), lambda j, i: (0, j)),
            pl.BlockSpec((D_MODEL, tn), lambda j, i: (0, j)),
        ],
        out_specs=pl.BlockSpec((tm, tn), lambda j, i: (i, j)),
        out_shape=jax.ShapeDtypeStruct((rows, D_MODEL), BF16),
        scratch_shapes=[pltpu.VMEM((D_INNER, tn), BF16), pltpu.VMEM((D_MODEL, tn), BF16)],
        compiler_params=pltpu.CompilerParams(
            dimension_semantics=("arbitrary", "arbitrary"),
            vmem_limit_bytes=VMEM_LIMIT),
        name="merge",
    )(yn, v, gates, gates, wa, wb)


def _outproj_kernel(m_ref, x_ref, wo_ref, fw_ref, o_ref):
    y = x_ref[...] + jnp.dot(m_ref[...], wo_ref[...], preferred_element_type=F32)
    ms = jnp.mean(y * y, axis=-1, keepdims=True)
    o_ref[...] = y * lax.rsqrt(ms + EPS) * fw_ref[...]


def _outproj(m, x, wo, fw, *, row0, tm=OUTPROJ_ROWS):
    rows = x.shape[0]
    assert row0 % tm == 0 and rows % tm == 0
    t0 = row0 // tm
    return pl.pallas_call(
        _outproj_kernel,
        grid=(rows // tm,),
        in_specs=[
            pl.BlockSpec((tm, D_MODEL), lambda i: (t0 + i, 0)),
            pl.BlockSpec((tm, D_MODEL), lambda i: (i, 0)),
            pl.BlockSpec((D_MODEL, D_MODEL), lambda i: (0, 0), pipeline_mode=pl.Buffered(1)),
            pl.BlockSpec((1, D_MODEL), lambda i: (0, 0)),
        ],
        out_specs=pl.BlockSpec((tm, D_MODEL), lambda i: (i, 0)),
        out_shape=jax.ShapeDtypeStruct((rows, D_MODEL), F32),
        compiler_params=pltpu.CompilerParams(
            dimension_semantics=("arbitrary",),
            vmem_limit_bytes=VMEM_LIMIT),
        name="outproj",
    )(m, x, wo, fw)


def kernel(x_prompt, x_sample, state_ssd_conv, state_ssm, state_sconv, meta_tokens, norm_w,
           w_in, ssd_conv_w, ssd_conv_b, dt_bias, a_log, d_skip, ssd_norm_w, w_ssd_out,
           sconv_w, w_sconv_out, w_o, final_norm_w):
    bp, seq = x_prompt.shape[0], x_prompt.shape[1]
    bd, dec_seq = x_sample.shape[0], x_sample.shape[1]

    w_t = jnp.transpose(w_in[0])
    nw = norm_w[0].reshape(1, D_MODEL)
    fw = final_norm_w.reshape(1, D_MODEL)
    conv_w = ssd_conv_w[0]
    conv_b = ssd_conv_b[0].reshape(1, CONV_DIM)
    dtb = jnp.pad(dt_bias[0], (0, LANES - N_HEADS)).reshape(1, LANES)
    alog = jnp.pad(a_log[0], (0, LANES - N_HEADS)).reshape(1, LANES)
    dsk = jnp.repeat(d_skip[0], HEAD_DIM).reshape(1, D_INNER)
    gnw = ssd_norm_w[0].reshape(1, D_INNER)
    scw = sconv_w[0]

    xp = x_prompt.reshape(bp * seq, D_MODEL)
    xs = x_sample.reshape(bd * dec_seq, D_MODEL)
    n_p, n_s = bp * seq, bd * dec_seq
    streams = dict(n_prompt=n_p, seq=seq, n_seq_p=bp, n_seq_s=bd, q_sample=dec_seq)

    hs, hm, dt_raw, dt_raw_m = _norm(xp, xs, meta_tokens, nw, w_t)
    sz = _proj(hs, w_t, w_row0=0, ncols=D_INNER, silu=True)
    gates = _proj(hs, w_t, w_row0=W_GATE, ncols=2 * D_MODEL, silu=False)
    xbc, xbc_m, conv_p, conv_s = _xbc(hs, hm, w_t, conv_w, conv_b, state_ssd_conv[0], **streams)
    v_all, sc_p, sc_s = _sconv(hs, hm, w_t, scw, state_sconv[0], **streams)
    headscal = functools.partial(_headscal, dt_bias=dtb, a_log=alog)
    ssd = functools.partial(_ssd, d_skip_x=dsk, norm_w=gnw)

    at_m, dtt_m = headscal(dt_raw_m, q=ROW_TILE, valid=META, tile0=0, ntiles=1)
    _, ssm_m = ssd(jnp.zeros((ROW_TILE, D_INNER), F32), xbc_m, at_m, dtt_m,
                   state0=jnp.zeros((1, N_GROUPS, GROUP_W, N_STATE), F32),
                   n_seq=1, bs=1, q=ROW_TILE, nc=1, gps=GROUPS_PER_STEP_META, shared_init=False)

    at_p, dtt_p = headscal(dt_raw, q=ROW_TILE, valid=ROW_TILE, tile0=0, ntiles=n_p // ROW_TILE)
    yn, ssm_p = ssd(sz, xbc, at_p, dtt_p, state0=ssm_m, n_seq=bp, bs=1, q=ROW_TILE,
                    nc=seq // ROW_TILE, gps=GROUPS_PER_STEP_PROMPT, shared_init=True,
                    out_rows=n_p + n_s,
                    zero_tail_tiles=n_s // ROW_TILE)

    sbs = ROW_TILE // dec_seq
    at_s, dtt_s = headscal(dt_raw, q=dec_seq, valid=ROW_TILE, tile0=n_p // ROW_TILE,
                           ntiles=n_s // ROW_TILE)
    yn, ssm_s = ssd(sz, xbc, at_s, dtt_s,
                    state0=state_ssm[0].reshape(bd, N_GROUPS, GROUP_W, N_STATE),
                    n_seq=bd, bs=sbs, q=dec_seq, nc=1, gps=GROUPS_PER_STEP_SAMPLE,
                    shared_init=False,
                    tile0=n_p // ROW_TILE, out_rows=n_p + n_s, yn_into=yn)

    merged = _merge(yn, v_all, gates, w_ssd_out[0], w_sconv_out[0])
    wo = w_o[0].astype(BF16)
    y_p = _outproj(merged, xp, wo, fw, row0=0)
    y_s = _outproj(merged, xs, wo, fw, row0=n_p)

    return (y_p.reshape(bp, seq, D_MODEL),
            y_s.reshape(bd, dec_seq, D_MODEL),
            conv_p[None],
            ssm_p.reshape(1, bp, N_HEADS, HEAD_DIM, N_STATE),
            sc_p[None],
            conv_s[None],
            ssm_s.reshape(1, bd, N_HEADS, HEAD_DIM, N_STATE),
            sc_s[None])
```

```python
import functools

import jax
import jax.numpy as jnp
from jax import lax
from jax.experimental import pallas as pl
from jax.experimental.pallas import tpu as pltpu

F32 = jnp.float32
BF16 = jnp.bfloat16

D_MODEL = 2048
D_INNER = 4096
N_HEADS = 64
HEAD_DIM = 64
N_STATE = 128
N_GROUPS = 8
GROUP_W = D_INNER // N_GROUPS
HEADS_PER_GROUP = N_HEADS // N_GROUPS
CONV_DIM = D_INNER + 2 * N_GROUPS * N_STATE
SSD_CONV_W = 4
SC_CONV_W = 3
META = 16
EPS = 1e-6
LOG2_E = 1.4426950408889634

LANES = 128
SUBLANES = 8
ROW_TILE = 128
W_DT = D_INNER + CONV_DIM
W_SC = W_DT + N_HEADS
W_GATE = W_SC + 4 * D_MODEL

VMEM_LIMIT = 52 * 1024 * 1024
NORM_ROWS = 512
PROJ_ROWS, PROJ_COLS = 1024, 1024
XBC_COLS = 512
SCONV_CHANNELS = 256
MERGE_ROWS, MERGE_COLS = 512, 512
OUTPROJ_ROWS = 512
HEADSCAL_TILES = 8
GROUPS_PER_STEP_PROMPT = N_GROUPS
GROUPS_PER_STEP_SAMPLE = 2
GROUPS_PER_STEP_META = 2


def _sigmoid(x):
    return 0.5 * (1.0 + jnp.tanh(0.5 * x))


def _silu_of_twice(h):
    return h * (1.0 + jnp.tanh(h))


def _rms_bf16(x, w):
    ms = jnp.mean(x * x, axis=-1, keepdims=True)
    return (x * lax.rsqrt(ms + EPS) * w).astype(BF16)


def _norm_kernel(xp_ref, xs_ref, xm_ref, nw_ref, wdt_ref, hs_ref, hm_ref, dt_ref, dtm_ref,
                 wdt_scr, *, n_prompt):
    i = pl.program_id(0)

    def dt_proj(h):
        return lax.dot_general(h, wdt_scr[...], (((1,), (1,)), ((), ())),
                               preferred_element_type=F32)

    @pl.when(i == 0)
    def _():
        wdt_scr[...] = wdt_ref[...].astype(BF16)
        hm_ref[:META, :] = _rms_bf16(xm_ref[...], nw_ref[...])
        hm_ref[META:, :] = jnp.zeros((ROW_TILE - META, D_MODEL), BF16)
        dtm_ref[...] = dt_proj(hm_ref[...])

    @pl.when(i < n_prompt)
    def _():
        hs_ref[...] = _rms_bf16(xp_ref[...], nw_ref[...])

    @pl.when(i >= n_prompt)
    def _():
        hs_ref[...] = _rms_bf16(xs_ref[...], nw_ref[...])

    dt_ref[...] = dt_proj(hs_ref[...])


def _norm(xp, xs, xm, norm_w, w_t, *, tm=NORM_ROWS):
    n_p, n_s = xp.shape[0] // tm, xs.shape[0] // tm
    rows = xp.shape[0] + xs.shape[0]
    kern = functools.partial(_norm_kernel, n_prompt=n_p)
    return pl.pallas_call(
        kern,
        grid=(n_p + n_s,),
        in_specs=[
            pl.BlockSpec((tm, D_MODEL), lambda i: (jnp.minimum(i, n_p - 1), 0)),
            pl.BlockSpec((tm, D_MODEL), lambda i: (jnp.maximum(i - n_p, 0), 0)),
            pl.BlockSpec((META, D_MODEL), lambda i: (0, 0)),
            pl.BlockSpec((1, D_MODEL), lambda i: (0, 0)),
            pl.BlockSpec((LANES, D_MODEL), lambda i: (W_DT // LANES, 0)),
        ],
        out_specs=[
            pl.BlockSpec((tm, D_MODEL), lambda i: (i, 0)),
            pl.BlockSpec((ROW_TILE, D_MODEL), lambda i: (0, 0)),
            pl.BlockSpec((tm, LANES), lambda i: (i, 0)),
            pl.BlockSpec((ROW_TILE, LANES), lambda i: (0, 0)),
        ],
        out_shape=[
            jax.ShapeDtypeStruct((rows, D_MODEL), BF16),
            jax.ShapeDtypeStruct((ROW_TILE, D_MODEL), BF16),
            jax.ShapeDtypeStruct((rows, LANES), F32),
            jax.ShapeDtypeStruct((ROW_TILE, LANES), F32),
        ],
        scratch_shapes=[pltpu.VMEM((LANES, D_MODEL), BF16)],
        compiler_params=pltpu.CompilerParams(
            dimension_semantics=("arbitrary",), vmem_limit_bytes=VMEM_LIMIT),
        name="norm",
    )(xp, xs, xm, norm_w, w_t)


def _proj_kernel(hs_ref, wt_ref, o_ref, wb_ref, *, silu):
    @pl.when(pl.program_id(1) == 0)
    def _():
        w = wt_ref[...] * 0.5 if silu else wt_ref[...]
        wb_ref[...] = w.T.astype(BF16)

    r = jnp.dot(hs_ref[...], wb_ref[...], preferred_element_type=F32)
    o_ref[...] = _silu_of_twice(r) if silu else r


def _proj(hs, w_t, *, w_row0, ncols, silu, tm=PROJ_ROWS, tn=PROJ_COLS):
    rows = hs.shape[0]
    assert w_row0 % N_HEADS == 0 and ncols % tn == 0 and tn % N_HEADS == 0

    def w_rows(j, i):
        return ((w_row0 // N_HEADS + j * (tn // N_HEADS)) * N_HEADS, 0)

    return pl.pallas_call(
        functools.partial(_proj_kernel, silu=silu),
        grid=(ncols // tn, rows // tm),
        in_specs=[
            pl.BlockSpec((tm, D_MODEL), lambda j, i: (i, 0)),
            pl.BlockSpec((pl.Element(tn), pl.Element(D_MODEL)), w_rows),
        ],
        out_specs=pl.BlockSpec((tm, tn), lambda j, i: (i, j)),
        out_shape=jax.ShapeDtypeStruct((rows, ncols), F32),
        scratch_shapes=[pltpu.VMEM((D_MODEL, tn), BF16)],
        compiler_params=pltpu.CompilerParams(
            dimension_semantics=("arbitrary", "arbitrary"),
            vmem_limit_bytes=VMEM_LIMIT),
        name="proj",
    )(hs, w_t)


def _conv_rows(x, halo_ref, prev_ref, w_ref, *, first, bs, q, carry):
    taps = w_ref.shape[0]
    rt, width = x.shape

    @pl.when(first)
    def _():
        halo_ref[:, SUBLANES - (taps - 1):, :] = prev_ref[...]

    prev = halo_ref[...]
    acc = None
    if bs == 1:
        row = lax.broadcasted_iota(jnp.int32, (SUBLANES, width), 0)
        for s in range(taps - 1, 0, -1):
            rolled = pltpu.roll(x, s, 0)
            head = jnp.where(row < s, pltpu.roll(prev[0], s, 0), rolled[:SUBLANES])
            term = jnp.concatenate([head, rolled[SUBLANES:]], axis=0) * w_ref[taps - 1 - s:taps - s, :]
            acc = term if acc is None else acc + term
        acc = acc + x * w_ref[taps - 1:taps, :]
        if carry:
            halo_ref[0] = x[rt - SUBLANES:, :]
        return acc
    assert q == SUBLANES and not carry
    x3 = x.reshape(bs, q, width)
    row = lax.broadcasted_iota(jnp.int32, x3.shape, 1)
    for s in range(taps - 1, 0, -1):
        shifted = jnp.where(row < s, pltpu.roll(prev, s, 1), pltpu.roll(x3, s, 1))
        term = shifted * w_ref[taps - 1 - s:taps - s, :]
        acc = term if acc is None else acc + term
    acc = acc + x3 * w_ref[taps - 1:taps, :]
    return acc.reshape(rt, width)


def _seg_cumsum(a, q):
    pos = lax.broadcasted_iota(jnp.int32, a.shape, 0) & (q - 1)
    s = 1
    while s < q:
        shifted = pltpu.roll(a, s, 0)
        a = a + jnp.where(pos >= s, shifted, 0.0)
        s *= 2
    return a


def _headscal_kernel(dtr_ref, dtb_ref, alog_ref, at_ref, bt_ref, *, q, valid, tps):
    dtv = jax.nn.softplus(dtr_ref[...] + dtb_ref[...])
    if valid < ROW_TILE:
        rows = lax.broadcasted_iota(jnp.int32, dtv.shape, 0)
        dtv = jnp.where(rows < valid, dtv, 0.0)
    acum = _seg_cumsum(dtv * (-jnp.exp(alog_ref[...])), q) * LOG2_E
    a_minus_logdt = acum - jnp.log(dtv) * LOG2_E
    for t in range(tps):
        at_ref[t] = acum[t * ROW_TILE:(t + 1) * ROW_TILE].T
        bt_ref[t] = a_minus_logdt[t * ROW_TILE:(t + 1) * ROW_TILE].T


def _headscal(dt_raw, dt_bias, a_log, *, q, valid, tile0, ntiles):
    tps = min(ntiles, HEADSCAL_TILES)
    assert ntiles % tps == 0 and tile0 % tps == 0 and q <= ROW_TILE
    kern = functools.partial(_headscal_kernel, q=q, valid=valid, tps=tps)
    shape = jax.ShapeDtypeStruct((ntiles, LANES, ROW_TILE), F32)
    return pl.pallas_call(
        kern,
        grid=(ntiles // tps,),
        in_specs=[
            pl.BlockSpec((tps * ROW_TILE, LANES), lambda t: (tile0 // tps + t, 0)),
            pl.BlockSpec((1, LANES), lambda t: (0, 0)),
            pl.BlockSpec((1, LANES), lambda t: (0, 0)),
        ],
        out_specs=[
            pl.BlockSpec((tps, LANES, ROW_TILE), lambda t: (t, 0, 0)),
            pl.BlockSpec((tps, LANES, ROW_TILE), lambda t: (t, 0, 0)),
        ],
        out_shape=[shape, shape],
        compiler_params=pltpu.CompilerParams(dimension_semantics=("arbitrary",)),
        name="headscal",
    )(dt_raw, dt_bias, a_log)


def _xbc_kernel(hs_ref, hm_ref, wt_ref, cw_ref, cb_ref, prev_ref,
                o_ref, om_ref, cnp_ref, cns_ref,
                wb_scr, halo_p, halo_s, halo_m, meta_prev, zero_prev,
                *, tm, n_prompt_tiles, tiles_per_seq, bs_sample, q_sample):
    i = pl.program_id(1)
    keep = SSD_CONV_W - 1
    tn = o_ref.shape[1]

    @pl.when(i == 0)
    def _():
        wb_scr[...] = wt_ref[...].T.astype(BF16)

    raw = jnp.dot(hs_ref[...], wb_scr[...], preferred_element_type=F32)

    def activate(conv):
        return _silu_of_twice(conv + cb_ref[...])

    @pl.when(i == 0)
    def _():
        raw_m = jnp.dot(hm_ref[...], wb_scr[...], preferred_element_type=F32)
        meta_prev[0] = raw_m[META - keep:]
        zero_prev[...] = jnp.zeros(zero_prev.shape, F32)
        conv_m = _conv_rows(raw_m, halo_m, zero_prev, cw_ref, first=i == 0, bs=1, q=META,
                            carry=False)
        om_ref[:META, :] = activate(conv_m)
        om_ref[META:, :] = jnp.zeros((ROW_TILE - META, tn), F32)

    @pl.when(i < n_prompt_tiles)
    def _():
        o_ref[...] = activate(_conv_rows(raw, halo_p, meta_prev, cw_ref,
                                         first=(i % tiles_per_seq) == 0, bs=1, q=tm, carry=True))
        cnp_ref[0] = raw[tm - keep:]

    @pl.when(i >= n_prompt_tiles)
    def _():
        o_ref[...] = activate(_conv_rows(raw, halo_s, prev_ref, cw_ref, first=i >= n_prompt_tiles,
                                         bs=bs_sample, q=q_sample, carry=False))
        cns_ref[...] = raw.reshape(bs_sample, q_sample, tn)[:, q_sample - keep:, :]


def _xbc(hs, hm, w_t, cw, cb, prev_s, *, n_prompt, seq, n_seq_p, n_seq_s, q_sample,
         tm=PROJ_ROWS, tn=XBC_COLS):
    rows = hs.shape[0]
    keep = SSD_CONV_W - 1
    assert seq % tm == 0 and (rows - n_prompt) == tm and tm == n_seq_s * q_sample
    assert D_INNER % tn == 0 and CONV_DIM % tn == 0
    tiles_per_seq = seq // tm
    n_pt = n_prompt // tm
    kern = functools.partial(_xbc_kernel, tm=tm, n_prompt_tiles=n_pt, tiles_per_seq=tiles_per_seq,
                             bs_sample=n_seq_s, q_sample=q_sample)
    return pl.pallas_call(
        kern,
        grid=(CONV_DIM // tn, rows // tm),
        in_specs=[
            pl.BlockSpec((tm, D_MODEL), lambda j, i: (i, 0)),
            pl.BlockSpec((META, D_MODEL), lambda j, i: (0, 0)),
            pl.BlockSpec((tn, D_MODEL), lambda j, i: (D_INNER // tn + j, 0)),
            pl.BlockSpec((SSD_CONV_W, tn), lambda j, i: (0, j)),
            pl.BlockSpec((1, tn), lambda j, i: (0, j)),
            pl.BlockSpec((n_seq_s, keep, tn), lambda j, i: (0, 0, j)),
        ],
        out_specs=[
            pl.BlockSpec((tm, tn), lambda j, i: (i, j)),
            pl.BlockSpec((ROW_TILE, tn), lambda j, i: (0, j)),
            pl.BlockSpec((1, keep, tn),
                         lambda j, i: (jnp.minimum(i, n_pt - 1) // tiles_per_seq, 0, j)),
            pl.BlockSpec((n_seq_s, keep, tn), lambda j, i: (0, 0, j)),
        ],
        out_shape=[
            jax.ShapeDtypeStruct((rows, CONV_DIM), F32),
            jax.ShapeDtypeStruct((ROW_TILE, CONV_DIM), F32),
            jax.ShapeDtypeStruct((n_seq_p, keep, CONV_DIM), F32),
            jax.ShapeDtypeStruct((n_seq_s, keep, CONV_DIM), F32),
        ],
        scratch_shapes=[
            pltpu.VMEM((D_MODEL, tn), BF16),
            pltpu.VMEM((1, SUBLANES, tn), F32),
            pltpu.VMEM((n_seq_s, SUBLANES, tn), F32),
            pltpu.VMEM((1, SUBLANES, tn), F32),
            pltpu.VMEM((1, keep, tn), F32),
            pltpu.VMEM((1, keep, tn), F32),
        ],
        compiler_params=pltpu.CompilerParams(
            dimension_semantics=("arbitrary", "arbitrary"),
            vmem_limit_bytes=VMEM_LIMIT),
        name="xbc",
    )(hs, hm, w_t, cw, cb, prev_s)


def _ssd_kernel(*refs, n_real, **static):
    yn_ref = refs[-3]
    s = pl.program_id(0)

    @pl.when(s < n_real)
    def _():
        _ssd_body(*refs, **static)

    @pl.when(s >= n_real)
    def _():
        yn_ref[...] = jnp.zeros(yn_ref.shape, yn_ref.dtype)


def _ssd_body(sz_ref, x_ref, b_ref, c_ref, at_ref, bt_ref, dsk_ref, nw_ref, s0_ref,
              *rest, bs, q, nc, gps):
    yn_ref, sout_ref, st_ref = rest[-3:]
    rt = bs * q
    nh = gps * HEADS_PER_GROUP
    c = pl.program_id(2)
    carry = nc > 1

    if carry:
        @pl.when(c == 0)
        def _():
            st_ref[...] = s0_ref[...]

    xc = x_ref[...]
    bcb = b_ref[...].astype(BF16)
    ccb = c_ref[...].astype(BF16)

    a_t = at_ref[0]
    b_t = bt_ref[0]
    cols = jnp.concatenate([a_t, jnp.zeros((LANES - nh, rt), F32)], axis=0).T

    if bs == 1:
        a_end = jnp.broadcast_to(a_t[:, rt - 1:rt], (nh, rt))
    else:
        pos = lax.broadcasted_iota(jnp.int32, (nh, rt), 1) & (q - 1)
        a_end = a_t
        s = 1
        while s < q:
            a_end = jnp.where(pos + s < q, pltpu.roll(a_end, rt - s, 1), a_end)
            s *= 2
    to_end = jnp.exp2(a_end - b_t)

    nblk = rt // SUBLANES
    ri = lax.broadcasted_iota(jnp.int32, (nblk, SUBLANES, rt), 0) * SUBLANES + \
        lax.broadcasted_iota(jnp.int32, (nblk, SUBLANES, rt), 1)
    ci = lax.broadcasted_iota(jnp.int32, (nblk, SUBLANES, rt), 2)
    mask = (ri >= ci) & ((ri // q) == (ci // q))
    low = lax.broadcasted_iota(jnp.int32, (rt, LANES), 1) < HEAD_DIM
    seq_of_row = lax.broadcasted_iota(jnp.int32, (rt, N_STATE), 0) // q

    for k in range(gps):
        xg = xc[:, k * GROUP_W:(k + 1) * GROUP_W]
        bg = bcb[:, k * N_STATE:(k + 1) * N_STATE]
        cg = ccb[:, k * N_STATE:(k + 1) * N_STATE]
        cb = lax.dot_general(cg, bg, (((1,), (1,)), ((), ())), preferred_element_type=F32)
        cb3 = cb.reshape(nblk, SUBLANES, rt)
        xt = xg.T

        ydiag, ea, xw, a_cols = [], [], [], []
        for pr in range(HEADS_PER_GROUP // 2):
            wts, ab = [], []
            for hh in range(2):
                h = k * HEADS_PER_GROUP + 2 * pr + hh
                a_col = jnp.broadcast_to(cols[:, h:h + 1], (rt, LANES))
                b_row = jnp.broadcast_to(b_t[h:h + 1, :], (SUBLANES, rt))
                seg = jnp.where(mask, a_col.reshape(nblk, SUBLANES, rt) - b_row[None], -jnp.inf)
                wts.append((cb3 * jnp.exp2(seg)).reshape(rt, rt).astype(BF16))
                ab.append(a_col)
                rows = slice((2 * pr + hh) * HEAD_DIM, (2 * pr + hh + 1) * HEAD_DIM)
                xw.append(xt[rows] * to_end[h:h + 1, :])
            a_cols += ab
            ea.append(jnp.exp2(jnp.where(low, ab[0], ab[1])))
            xb = xg[:, pr * LANES:(pr + 1) * LANES].astype(BF16)
            zero = jnp.zeros_like(xb)
            rhs = jnp.concatenate([jnp.where(low, xb, zero), jnp.where(low, zero, xb)], axis=0)
            ydiag.append(jnp.dot(jnp.concatenate(wts, axis=1), rhs, preferred_element_type=F32))
        ydiag = jnp.concatenate(ydiag, axis=1)
        ea = jnp.concatenate(ea, axis=1)
        xwt = jnp.concatenate(xw, axis=0).astype(BF16)

        yoff = []
        for s in range(bs):
            st = st_ref[s, k] if carry else s0_ref[s, k]
            yoff.append(lax.dot_general(cg[s * q:(s + 1) * q, :], st.astype(BF16),
                                        (((1,), (1,)), ((), ())), preferred_element_type=F32))
            bsel = bg if bs == 1 else jnp.where(seq_of_row == s, bg, jnp.zeros_like(bg))
            upd = jnp.dot(xwt, bsel, preferred_element_type=F32)
            last = (s + 1) * q - 1
            dec = jnp.concatenate(
                [jnp.broadcast_to(jnp.exp2(a_cols[h][last:last + 1, :]), (HEAD_DIM, N_STATE))
                 for h in range(HEADS_PER_GROUP)], axis=0)
            new = st * dec + upd
            if carry:
                st_ref[s, k] = new
            else:
                sout_ref[s, k] = new
        yoff = yoff[0] if bs == 1 else jnp.concatenate(yoff, axis=0)

        lanes = slice(k * GROUP_W, (k + 1) * GROUP_W)
        y = ydiag + yoff * ea + dsk_ref[:, lanes] * xg
        gz = y * sz_ref[:, lanes]
        ms = jnp.mean(gz * gz, axis=-1, keepdims=True)
        yn_ref[:, lanes] = (gz * lax.rsqrt(ms + EPS) * nw_ref[:, lanes]).astype(BF16)

    if carry:
        @pl.when(c == nc - 1)
        def _():
            sout_ref[...] = st_ref[...]


def _ssd(sz, xbc, a_t, dt_t, d_skip_x, norm_w, state0,
         *, n_seq, bs, q, nc, gps, shared_init, tile0=0, out_rows=None, zero_tail_tiles=0,
         yn_into=None):
    rt = bs * q
    assert rt == ROW_TILE and N_GROUPS % gps == 0
    rows = n_seq * q * nc if out_rows is None else out_rows
    otile0 = 0 if out_rows is None else tile0
    nsb = n_seq // bs
    n_pad = pl.cdiv(zero_tail_tiles, nc)
    gw, gn, nh = gps * GROUP_W, gps * N_STATE, gps * HEADS_PER_GROUP
    static = dict(bs=bs, q=q, nc=nc, gps=gps)
    kern = (functools.partial(_ssd_kernel, n_real=nsb, **static) if n_pad
            else functools.partial(_ssd_body, **static))
    bb, bc_ = D_INNER // gn, (D_INNER + N_GROUPS * N_STATE) // gn

    def real(s):
        return jnp.minimum(s, nsb - 1) if n_pad else s

    def tile(s, c):
        return jnp.where(s < nsb, s * nc + c, nsb * nc - 1) if n_pad else s * nc + c

    def otile(s, c):
        if not n_pad:
            return otile0 + s * nc + c
        tail = jnp.minimum((s - nsb) * nc + c, zero_tail_tiles - 1)
        return otile0 + jnp.where(s < nsb, s * nc + c, nsb * nc + tail)

    sidx = (lambda s: 0) if shared_init else real
    in_specs = [
        pl.BlockSpec((rt, gw), lambda s, g, c: (tile0 + tile(s, c), g)),
        pl.BlockSpec((rt, gw), lambda s, g, c: (tile0 + tile(s, c), g)),
        pl.BlockSpec((rt, gn), lambda s, g, c: (tile0 + tile(s, c), bb + g)),
        pl.BlockSpec((rt, gn), lambda s, g, c: (tile0 + tile(s, c), bc_ + g)),
        pl.BlockSpec((1, nh, rt), lambda s, g, c: (tile(s, c), g, 0)),
        pl.BlockSpec((1, nh, rt), lambda s, g, c: (tile(s, c), g, 0)),
        pl.BlockSpec((1, gw), lambda s, g, c: (0, g)),
        pl.BlockSpec((1, gw), lambda s, g, c: (0, g)),
        pl.BlockSpec((bs, gps, GROUP_W, N_STATE), lambda s, g, c: (sidx(s), g, 0, 0)),
    ]
    out_specs = [
        pl.BlockSpec((rt, gw), lambda s, g, c: (otile(s, c), g)),
        pl.BlockSpec((bs, gps, GROUP_W, N_STATE), lambda s, g, c: (real(s), g, 0, 0)),
    ]
    operands = [sz, xbc, xbc, xbc, a_t, dt_t, d_skip_x, norm_w, state0]
    aliases = {}
    if yn_into is not None:
        assert yn_into.shape == (rows, D_INNER)
        aliases = {len(operands): 0}
        in_specs.append(pl.BlockSpec(memory_space=pl.ANY))
        operands.append(yn_into)
    st_shape = (bs, gps, GROUP_W, N_STATE) if nc > 1 else (1, 1, SUBLANES, N_STATE)
    return pl.pallas_call(
        kern,
        grid=(nsb + n_pad, N_GROUPS // gps, nc),
        in_specs=in_specs,
        out_specs=out_specs,
        out_shape=[
            jax.ShapeDtypeStruct((rows, D_INNER), BF16),
            jax.ShapeDtypeStruct((n_seq, N_GROUPS, GROUP_W, N_STATE), F32),
        ],
        scratch_shapes=[pltpu.VMEM(st_shape, F32)],
        input_output_aliases=aliases,
        compiler_params=pltpu.CompilerParams(
            dimension_semantics=("arbitrary", "arbitrary", "arbitrary"),
            vmem_limit_bytes=VMEM_LIMIT),
        name="ssd",
    )(*operands)


def _sconv_kernel(hs_ref, hm_ref, wb_ref, wc_ref, wh_ref, wz_ref, cw_ref, prev_ref,
                  v_ref, newp_ref, news_ref,
                  w_scr, halo_p, halo_s, meta_u,
                  *, tm, width, n_prompt_tiles, tiles_per_seq, bs_sample, q_sample):
    i = pl.program_id(1)
    keep = SC_CONV_W - 1

    @pl.when(i == 0)
    def _():
        for k, w_ref in enumerate((wb_ref, wc_ref, wh_ref)):
            w_scr[:, k * width:(k + 1) * width] = w_ref[...].T.astype(BF16)
        w_scr[:, 3 * width:] = (wz_ref[...] * 0.5).T.astype(BF16)
        rm = jnp.dot(hm_ref[...], w_scr[:, width:3 * width], preferred_element_type=F32)
        meta_u[0] = (rm[:, :width] * rm[:, width:])[META - keep:]

    r = jnp.dot(hs_ref[...], w_scr[...], preferred_element_type=F32)
    u = r[:, width:2 * width] * r[:, 2 * width:3 * width]

    def finish(uc):
        v_ref[...] = (r[:, :width] * uc * _silu_of_twice(r[:, 3 * width:])).astype(BF16)

    @pl.when(i < n_prompt_tiles)
    def _():
        finish(_conv_rows(u, halo_p, meta_u, cw_ref, first=(i % tiles_per_seq) == 0,
                          bs=1, q=tm, carry=True))
        newp_ref[0] = u[tm - keep:]

    @pl.when(i >= n_prompt_tiles)
    def _():
        finish(_conv_rows(u, halo_s, prev_ref, cw_ref, first=i >= n_prompt_tiles,
                          bs=bs_sample, q=q_sample, carry=False))
        news_ref[...] = u.reshape(bs_sample, q_sample, width)[:, q_sample - keep:, :]


def _sconv(hs, hm, w_t, cw, prev_s, *, n_prompt, seq, n_seq_p, n_seq_s, q_sample,
           tm=PROJ_ROWS, width=SCONV_CHANNELS):
    rows = hs.shape[0]
    keep = SC_CONV_W - 1
    assert seq % tm == 0 and (rows - n_prompt) == tm and tm == n_seq_s * q_sample
    tiles_per_seq = seq // tm
    n_pt = n_prompt // tm
    kern = functools.partial(_sconv_kernel, tm=tm, width=width, n_prompt_tiles=n_pt,
                             tiles_per_seq=tiles_per_seq, bs_sample=n_seq_s, q_sample=q_sample)

    def w_rows(k):
        base = (W_SC + k * D_MODEL) // N_HEADS
        return lambda cbk, i: ((base + cbk * (width // N_HEADS)) * N_HEADS, 0)

    w_specs = [pl.BlockSpec((pl.Element(width), pl.Element(D_MODEL)), w_rows(k)) for k in range(4)]
    return pl.pallas_call(
        kern,
        grid=(D_MODEL // width, rows // tm),
        in_specs=[
            pl.BlockSpec((tm, D_MODEL), lambda cbk, i: (i, 0)),
            pl.BlockSpec((META, D_MODEL), lambda cbk, i: (0, 0)),
            *w_specs,
            pl.BlockSpec((SC_CONV_W, width), lambda cbk, i: (0, cbk)),
            pl.BlockSpec((n_seq_s, keep, width), lambda cbk, i: (0, 0, cbk)),
        ],
        out_specs=[
            pl.BlockSpec((tm, width), lambda cbk, i: (i, cbk)),
            pl.BlockSpec((1, keep, width),
                         lambda cbk, i: (jnp.minimum(i, n_pt - 1) // tiles_per_seq, 0, cbk)),
            pl.BlockSpec((n_seq_s, keep, width), lambda cbk, i: (0, 0, cbk)),
        ],
        out_shape=[
            jax.ShapeDtypeStruct((rows, D_MODEL), BF16),
            jax.ShapeDtypeStruct((n_seq_p, keep, D_MODEL), F32),
            jax.ShapeDtypeStruct((n_seq_s, keep, D_MODEL), F32),
        ],
        scratch_shapes=[
            pltpu.VMEM((D_MODEL, 4 * width), BF16),
            pltpu.VMEM((1, SUBLANES, width), F32),
            pltpu.VMEM((n_seq_s, SUBLANES, width), F32),
            pltpu.VMEM((1, keep, width), F32),
        ],
        compiler_params=pltpu.CompilerParams(
            dimension_semantics=("arbitrary", "arbitrary"),
            vmem_limit_bytes=VMEM_LIMIT),
        name="sconv",
    )(hs, hm, w_t, w_t, w_t, w_t, cw, prev_s)


def _merge_kernel(yn_ref, v_ref, ga_ref, gb_ref, wa_ref, wb_ref, o_ref, wa_scr, wb_scr):
    @pl.when(pl.program_id(1) == 0)
    def _():
        wa_scr[...] = wa_ref[...].astype(BF16)
        wb_scr[...] = wb_ref[...].astype(BF16)

    ya = jnp.dot(yn_ref[...], wa_scr[...], preferred_element_type=F32)
    yb = jnp.dot(v_ref[...], wb_scr[...], preferred_element_type=F32)
    o_ref[...] = (_sigmoid(ga_ref[...]) * ya + _sigmoid(gb_ref[...]) * yb).astype(BF16)


def _merge(yn, v, gates, wa, wb, *, tm=MERGE_ROWS, tn=MERGE_COLS):
    rows = yn.shape[0]
    return pl.pallas_call(
        _merge_kernel,
        grid=(D_MODEL // tn, rows // tm),
        in_specs=[
            pl.BlockSpec((tm, D_INNER), lambda j, i: (i, 0)),
            pl.BlockSpec((tm, D_MODEL), lambda j, i: (i, 0)),
            pl.BlockSpec((tm, tn), lambda j, i: (i, j)),
            pl.BlockSpec((tm, tn), lambda j, i: (i, D_MODEL // tn + j)),
            pl.BlockSpec((D_INNER, tn), lambda j, i: (0, j)),
            pl.BlockSpec((D_MODEL, tn), lambda j, i: (0, j)),
        ],
        out_specs=pl.BlockSpec((tm, tn), lambda j, i: (i, j)),
        out_shape=jax.ShapeDtypeStruct((rows, D_MODEL), BF16),
        scratch_shapes=[pltpu.VMEM((D_INNER, tn), BF16), pltpu.VMEM((D_MODEL, tn), BF16)],
        compiler_params=pltpu.CompilerParams(
            dimension_semantics=("arbitrary", "arbitrary"),
            vmem_limit_bytes=VMEM_LIMIT),
        name="merge",
    )(yn, v, gates, gates, wa, wb)


def _outproj_kernel(m_ref, x_ref, wo_ref, fw_ref, o_ref):
    y = x_ref[...] + jnp.dot(m_ref[...], wo_ref[...], preferred_element_type=F32)
    ms = jnp.mean(y * y, axis=-1, keepdims=True)
    o_ref[...] = y * lax.rsqrt(ms + EPS) * fw_ref[...]


def _outproj(m, x, wo, fw, *, row0, tm=OUTPROJ_ROWS):
    rows = x.shape[0]
    assert row0 % tm == 0 and rows % tm == 0
    t0 = row0 // tm
    return pl.pallas_call(
        _outproj_kernel,
        grid=(rows // tm,),
        in_specs=[
            pl.BlockSpec((tm, D_MODEL), lambda i: (t0 + i, 0)),
            pl.BlockSpec((tm, D_MODEL), lambda i: (i, 0)),
            pl.BlockSpec((D_MODEL, D_MODEL), lambda i: (0, 0), pipeline_mode=pl.Buffered(1)),
            pl.BlockSpec((1, D_MODEL), lambda i: (0, 0)),
        ],
        out_specs=pl.BlockSpec((tm, D_MODEL), lambda i: (i, 0)),
        out_shape=jax.ShapeDtypeStruct((rows, D_MODEL), F32),
        compiler_params=pltpu.CompilerParams(
            dimension_semantics=("arbitrary",),
            vmem_limit_bytes=VMEM_LIMIT),
        name="outproj",
    )(m, x, wo, fw)


def kernel(x_prompt, x_sample, state_ssd_conv, state_ssm, state_sconv, meta_tokens, norm_w,
           w_in, ssd_conv_w, ssd_conv_b, dt_bias, a_log, d_skip, ssd_norm_w, w_ssd_out,
           sconv_w, w_sconv_out, w_o, final_norm_w):
    bp, seq = x_prompt.shape[0], x_prompt.shape[1]
    bd, dec_seq = x_sample.shape[0], x_sample.shape[1]

    w_t = jnp.transpose(w_in[0])
    nw = norm_w[0].reshape(1, D_MODEL)
    fw = final_norm_w.reshape(1, D_MODEL)
    conv_w = ssd_conv_w[0] * 0.5
    conv_b = ssd_conv_b[0].reshape(1, CONV_DIM) * 0.5
    dtb = jnp.pad(dt_bias[0], (0, LANES - N_HEADS)).reshape(1, LANES)
    alog = jnp.pad(a_log[0], (0, LANES - N_HEADS)).reshape(1, LANES)
    dsk = jnp.repeat(d_skip[0], HEAD_DIM).reshape(1, D_INNER)
    gnw = ssd_norm_w[0].reshape(1, D_INNER)
    scw = sconv_w[0]

    xp = x_prompt.reshape(bp * seq, D_MODEL)
    xs = x_sample.reshape(bd * dec_seq, D_MODEL)
    n_p, n_s = bp * seq, bd * dec_seq
    streams = dict(n_prompt=n_p, seq=seq, n_seq_p=bp, n_seq_s=bd, q_sample=dec_seq)

    hs, hm, dt_raw, dt_raw_m = _norm(xp, xs, meta_tokens, nw, w_t)
    sz = _proj(hs, w_t, w_row0=0, ncols=D_INNER, silu=True)
    gates = _proj(hs, w_t, w_row0=W_GATE, ncols=2 * D_MODEL, silu=False)
    xbc, xbc_m, conv_p, conv_s = _xbc(hs, hm, w_t, conv_w, conv_b, state_ssd_conv[0], **streams)
    v_all, sc_p, sc_s = _sconv(hs, hm, w_t, scw, state_sconv[0], **streams)
    headscal = functools.partial(_headscal, dt_bias=dtb, a_log=alog)
    ssd = functools.partial(_ssd, d_skip_x=dsk, norm_w=gnw)

    at_m, dtt_m = headscal(dt_raw_m, q=ROW_TILE, valid=META, tile0=0, ntiles=1)
    _, ssm_m = ssd(jnp.zeros((ROW_TILE, D_INNER), F32), xbc_m, at_m, dtt_m,
                   state0=jnp.zeros((1, N_GROUPS, GROUP_W, N_STATE), F32),
                   n_seq=1, bs=1, q=ROW_TILE, nc=1, gps=GROUPS_PER_STEP_META, shared_init=False)

    at_p, dtt_p = headscal(dt_raw, q=ROW_TILE, valid=ROW_TILE, tile0=0, ntiles=n_p // ROW_TILE)
    yn, ssm_p = ssd(sz, xbc, at_p, dtt_p, state0=ssm_m, n_seq=bp, bs=1, q=ROW_TILE,
                    nc=seq // ROW_TILE, gps=GROUPS_PER_STEP_PROMPT, shared_init=True,
                    out_rows=n_p + n_s,
                    zero_tail_tiles=n_s // ROW_TILE)

    sbs = ROW_TILE // dec_seq
    at_s, dtt_s = headscal(dt_raw, q=dec_seq, valid=ROW_TILE, tile0=n_p // ROW_TILE,
                           ntiles=n_s // ROW_TILE)
    yn, ssm_s = ssd(sz, xbc, at_s, dtt_s,
                    state0=state_ssm[0].reshape(bd, N_GROUPS, GROUP_W, N_STATE),
                    n_seq=bd, bs=sbs, q=dec_seq, nc=1, gps=GROUPS_PER_STEP_SAMPLE,
                    shared_init=False,
                    tile0=n_p // ROW_TILE, out_rows=n_p + n_s, yn_into=yn)

    merged = _merge(yn, v_all, gates, w_ssd_out[0], w_sconv_out[0])
    wo = w_o[0].astype(BF16)
    y_p = _outproj(merged, xp, wo, fw, row0=0)
    y_s = _outproj(merged, xs, wo, fw, row0=n_p)

    return (y_p.reshape(bp, seq, D_MODEL),
            y_s.reshape(bd, dec_seq, D_MODEL),
            conv_p[None],
            ssm_p.reshape(1, bp, N_HEADS, HEAD_DIM, N_STATE),
            sc_p[None],
            conv_s[None],
            ssm_s.reshape(1, bd, N_HEADS, HEAD_DIM, N_STATE),
            sc_s[None])
```

```python
import functools

import jax
import jax.numpy as jnp
from jax import lax
from jax.experimental import pallas as pl
from jax.experimental.pallas import tpu as pltpu

F32 = jnp.float32
BF16 = jnp.bfloat16

D_MODEL = 2048
D_INNER = 4096
N_HEADS = 64
HEAD_DIM = 64
N_STATE = 128
N_GROUPS = 8
GROUP_W = D_INNER // N_GROUPS
HEADS_PER_GROUP = N_HEADS // N_GROUPS
CONV_DIM = D_INNER + 2 * N_GROUPS * N_STATE
SSD_CONV_W = 4
SC_CONV_W = 3
META = 16
EPS = 1e-6
LOG2_E = 1.4426950408889634

LANES = 128
SUBLANES = 8
ROW_TILE = 128
W_DT = D_INNER + CONV_DIM
W_SC = W_DT + N_HEADS
W_GATE = W_SC + 4 * D_MODEL

VMEM_LIMIT = 52 * 1024 * 1024
NORM_ROWS = 512
PROJ_ROWS, PROJ_COLS = 1024, 1024
XBC_COLS = 512
SCONV_CHANNELS = 256
MERGE_ROWS, MERGE_COLS = 512, 512
OUTPROJ_ROWS = 512
HEADSCAL_TILES = 8
GROUPS_PER_STEP_PROMPT = N_GROUPS
GROUPS_PER_STEP_SAMPLE = 2
GROUPS_PER_STEP_META = 2


def _silu_of_twice(h):
    return h * (1.0 + jnp.tanh(h))


def _rms_bf16(x, w):
    ms = jnp.mean(x * x, axis=-1, keepdims=True)
    return (x * lax.rsqrt(ms + EPS) * w).astype(BF16)


def _norm_kernel(xp_ref, xs_ref, xm_ref, nw_ref, wdt_ref, hs_ref, hm_ref, dt_ref, dtm_ref,
                 wdt_scr, *, n_prompt):
    i = pl.program_id(0)

    def dt_proj(h):
        return lax.dot_general(h, wdt_scr[...], (((1,), (1,)), ((), ())),
                               preferred_element_type=F32)

    @pl.when(i == 0)
    def _():
        wdt_scr[...] = wdt_ref[...].astype(BF16)
        hm_ref[:META, :] = _rms_bf16(xm_ref[...], nw_ref[...])
        hm_ref[META:, :] = jnp.zeros((ROW_TILE - META, D_MODEL), BF16)
        dtm_ref[...] = dt_proj(hm_ref[...])

    @pl.when(i < n_prompt)
    def _():
        hs_ref[...] = _rms_bf16(xp_ref[...], nw_ref[...])

    @pl.when(i >= n_prompt)
    def _():
        hs_ref[...] = _rms_bf16(xs_ref[...], nw_ref[...])

    dt_ref[...] = dt_proj(hs_ref[...])


def _norm(xp, xs, xm, norm_w, w_t, *, tm=NORM_ROWS):
    n_p, n_s = xp.shape[0] // tm, xs.shape[0] // tm
    rows = xp.shape[0] + xs.shape[0]
    kern = functools.partial(_norm_kernel, n_prompt=n_p)
    return pl.pallas_call(
        kern,
        grid=(n_p + n_s,),
        in_specs=[
            pl.BlockSpec((tm, D_MODEL), lambda i: (jnp.minimum(i, n_p - 1), 0)),
            pl.BlockSpec((tm, D_MODEL), lambda i: (jnp.maximum(i - n_p, 0), 0)),
            pl.BlockSpec((META, D_MODEL), lambda i: (0, 0)),
            pl.BlockSpec((1, D_MODEL), lambda i: (0, 0)),
            pl.BlockSpec((LANES, D_MODEL), lambda i: (W_DT // LANES, 0)),
        ],
        out_specs=[
            pl.BlockSpec((tm, D_MODEL), lambda i: (i, 0)),
            pl.BlockSpec((ROW_TILE, D_MODEL), lambda i: (0, 0)),
            pl.BlockSpec((tm, LANES), lambda i: (i, 0)),
            pl.BlockSpec((ROW_TILE, LANES), lambda i: (0, 0)),
        ],
        out_shape=[
            jax.ShapeDtypeStruct((rows, D_MODEL), BF16),
            jax.ShapeDtypeStruct((ROW_TILE, D_MODEL), BF16),
            jax.ShapeDtypeStruct((rows, LANES), F32),
            jax.ShapeDtypeStruct((ROW_TILE, LANES), F32),
        ],
        scratch_shapes=[pltpu.VMEM((LANES, D_MODEL), BF16)],
        compiler_params=pltpu.CompilerParams(
            dimension_semantics=("arbitrary",), vmem_limit_bytes=VMEM_LIMIT),
        name="norm",
    )(xp, xs, xm, norm_w, w_t)


def _proj_kernel(hs_ref, wt_ref, o_ref, wb_ref, *, silu):
    @pl.when(pl.program_id(1) == 0)
    def _():
        wb_ref[...] = (wt_ref[...] * 0.5).T.astype(BF16)

    r = jnp.dot(hs_ref[...], wb_ref[...], preferred_element_type=F32)
    o_ref[...] = _silu_of_twice(r) if silu else r


def _proj(hs, w_t, *, w_row0, ncols, silu, tm=PROJ_ROWS, tn=PROJ_COLS):
    rows = hs.shape[0]
    assert w_row0 % N_HEADS == 0 and ncols % tn == 0 and tn % N_HEADS == 0

    def w_rows(j, i):
        return ((w_row0 // N_HEADS + j * (tn // N_HEADS)) * N_HEADS, 0)

    return pl.pallas_call(
        functools.partial(_proj_kernel, silu=silu),
        grid=(ncols // tn, rows // tm),
        in_specs=[
            pl.BlockSpec((tm, D_MODEL), lambda j, i: (i, 0)),
            pl.BlockSpec((pl.Element(tn), pl.Element(D_MODEL)), w_rows),
        ],
        out_specs=pl.BlockSpec((tm, tn), lambda j, i: (i, j)),
        out_shape=jax.ShapeDtypeStruct((rows, ncols), F32),
        scratch_shapes=[pltpu.VMEM((D_MODEL, tn), BF16)],
        compiler_params=pltpu.CompilerParams(
            dimension_semantics=("arbitrary", "arbitrary"),
            vmem_limit_bytes=VMEM_LIMIT),
        name="proj",
    )(hs, w_t)


def _conv_rows(x, halo_ref, prev_ref, w_ref, *, first, bs, q, carry):
    taps = w_ref.shape[0]
    rt, width = x.shape

    @pl.when(first)
    def _():
        halo_ref[:, SUBLANES - (taps - 1):, :] = prev_ref[...]

    prev = halo_ref[...]
    acc = None
    if bs == 1:
        row = lax.broadcasted_iota(jnp.int32, (SUBLANES, width), 0)
        for s in range(taps - 1, 0, -1):
            rolled = pltpu.roll(x, s, 0)
            head = jnp.where(row < s, pltpu.roll(prev[0], s, 0), rolled[:SUBLANES])
            term = jnp.concatenate([head, rolled[SUBLANES:]], axis=0) * w_ref[taps - 1 - s:taps - s, :]
            acc = term if acc is None else acc + term
        acc = acc + x * w_ref[taps - 1:taps, :]
        if carry:
            halo_ref[0] = x[rt - SUBLANES:, :]
        return acc
    assert q == SUBLANES and not carry
    x3 = x.reshape(bs, q, width)
    row = lax.broadcasted_iota(jnp.int32, x3.shape, 1)
    for s in range(taps - 1, 0, -1):
        shifted = jnp.where(row < s, pltpu.roll(prev, s, 1), pltpu.roll(x3, s, 1))
        term = shifted * w_ref[taps - 1 - s:taps - s, :]
        acc = term if acc is None else acc + term
    acc = acc + x3 * w_ref[taps - 1:taps, :]
    return acc.reshape(rt, width)


def _seg_cumsum(a, q):
    pos = lax.broadcasted_iota(jnp.int32, a.shape, 0) & (q - 1)
    s = 1
    while s < q:
        shifted = pltpu.roll(a, s, 0)
        a = a + jnp.where(pos >= s, shifted, 0.0)
        s *= 2
    return a


def _headscal_kernel(dtr_ref, dtb_ref, alog_ref, at_ref, bt_ref, *, q, valid, tps):
    dtv = jax.nn.softplus(dtr_ref[...] + dtb_ref[...])
    if valid < ROW_TILE:
        rows = lax.broadcasted_iota(jnp.int32, dtv.shape, 0)
        dtv = jnp.where(rows < valid, dtv, 0.0)
    acum = _seg_cumsum(dtv * (-jnp.exp(alog_ref[...])), q) * LOG2_E
    a_minus_logdt = acum - jnp.log(dtv) * LOG2_E
    for t in range(tps):
        at_ref[t] = acum[t * ROW_TILE:(t + 1) * ROW_TILE].T
        bt_ref[t] = a_minus_logdt[t * ROW_TILE:(t + 1) * ROW_TILE].T


def _headscal(dt_raw, dt_bias, a_log, *, q, valid, tile0, ntiles):
    tps = min(ntiles, HEADSCAL_TILES)
    assert ntiles % tps == 0 and tile0 % tps == 0 and q <= ROW_TILE
    kern = functools.partial(_headscal_kernel, q=q, valid=valid, tps=tps)
    shape = jax.ShapeDtypeStruct((ntiles, LANES, ROW_TILE), F32)
    return pl.pallas_call(
        kern,
        grid=(ntiles // tps,),
        in_specs=[
            pl.BlockSpec((tps * ROW_TILE, LANES), lambda t: (tile0 // tps + t, 0)),
            pl.BlockSpec((1, LANES), lambda t: (0, 0)),
            pl.BlockSpec((1, LANES), lambda t: (0, 0)),
        ],
        out_specs=[
            pl.BlockSpec((tps, LANES, ROW_TILE), lambda t: (t, 0, 0)),
            pl.BlockSpec((tps, LANES, ROW_TILE), lambda t: (t, 0, 0)),
        ],
        out_shape=[shape, shape],
        compiler_params=pltpu.CompilerParams(dimension_semantics=("arbitrary",)),
        name="headscal",
    )(dt_raw, dt_bias, a_log)


def _xbc_kernel(hs_ref, hm_ref, wt_ref, cw_ref, cb_ref, prev_ref,
                o_ref, om_ref, cnp_ref, cns_ref,
                wb_scr, halo_p, halo_s, halo_m, meta_prev, zero_prev,
                *, tm, n_prompt_tiles, tiles_per_seq, bs_sample, q_sample):
    i = pl.program_id(1)
    keep = SSD_CONV_W - 1
    tn = o_ref.shape[1]

    @pl.when(i == 0)
    def _():
        wb_scr[...] = wt_ref[...].T.astype(BF16)

    raw = jnp.dot(hs_ref[...], wb_scr[...], preferred_element_type=F32)

    def activate(conv):
        return _silu_of_twice(conv + cb_ref[...])

    @pl.when(i == 0)
    def _():
        raw_m = jnp.dot(hm_ref[...], wb_scr[...], preferred_element_type=F32)
        meta_prev[0] = raw_m[META - keep:]
        zero_prev[...] = jnp.zeros(zero_prev.shape, F32)
        conv_m = _conv_rows(raw_m, halo_m, zero_prev, cw_ref, first=i == 0, bs=1, q=META,
                            carry=False)
        om_ref[:META, :] = activate(conv_m)
        om_ref[META:, :] = jnp.zeros((ROW_TILE - META, tn), F32)

    @pl.when(i < n_prompt_tiles)
    def _():
        o_ref[...] = activate(_conv_rows(raw, halo_p, meta_prev, cw_ref,
                                         first=(i % tiles_per_seq) == 0, bs=1, q=tm, carry=True))
        cnp_ref[0] = raw[tm - keep:]

    @pl.when(i >= n_prompt_tiles)
    def _():
        o_ref[...] = activate(_conv_rows(raw, halo_s, prev_ref, cw_ref, first=i >= n_prompt_tiles,
                                         bs=bs_sample, q=q_sample, carry=False))
        cns_ref[...] = raw.reshape(bs_sample, q_sample, tn)[:, q_sample - keep:, :]


def _xbc(hs, hm, w_t, cw, cb, prev_s, *, n_prompt, seq, n_seq_p, n_seq_s, q_sample,
         tm=PROJ_ROWS, tn=XBC_COLS):
    rows = hs.shape[0]
    keep = SSD_CONV_W - 1
    assert seq % tm == 0 and (rows - n_prompt) == tm and tm == n_seq_s * q_sample
    assert D_INNER % tn == 0 and CONV_DIM % tn == 0
    tiles_per_seq = seq // tm
    n_pt = n_prompt // tm
    kern = functools.partial(_xbc_kernel, tm=tm, n_prompt_tiles=n_pt, tiles_per_seq=tiles_per_seq,
                             bs_sample=n_seq_s, q_sample=q_sample)
    return pl.pallas_call(
        kern,
        grid=(CONV_DIM // tn, rows // tm),
        in_specs=[
            pl.BlockSpec((tm, D_MODEL), lambda j, i: (i, 0)),
            pl.BlockSpec((META, D_MODEL), lambda j, i: (0, 0)),
            pl.BlockSpec((tn, D_MODEL), lambda j, i: (D_INNER // tn + j, 0)),
            pl.BlockSpec((SSD_CONV_W, tn), lambda j, i: (0, j)),
            pl.BlockSpec((1, tn), lambda j, i: (0, j)),
            pl.BlockSpec((n_seq_s, keep, tn), lambda j, i: (0, 0, j)),
        ],
        out_specs=[
            pl.BlockSpec((tm, tn), lambda j, i: (i, j)),
            pl.BlockSpec((ROW_TILE, tn), lambda j, i: (0, j)),
            pl.BlockSpec((1, keep, tn),
                         lambda j, i: (jnp.minimum(i, n_pt - 1) // tiles_per_seq, 0, j)),
            pl.BlockSpec((n_seq_s, keep, tn), lambda j, i: (0, 0, j)),
        ],
        out_shape=[
            jax.ShapeDtypeStruct((rows, CONV_DIM), F32),
            jax.ShapeDtypeStruct((ROW_TILE, CONV_DIM), F32),
            jax.ShapeDtypeStruct((n_seq_p, keep, CONV_DIM), F32),
            jax.ShapeDtypeStruct((n_seq_s, keep, CONV_DIM), F32),
        ],
        scratch_shapes=[
            pltpu.VMEM((D_MODEL, tn), BF16),
            pltpu.VMEM((1, SUBLANES, tn), F32),
            pltpu.VMEM((n_seq_s, SUBLANES, tn), F32),
            pltpu.VMEM((1, SUBLANES, tn), F32),
            pltpu.VMEM((1, keep, tn), F32),
            pltpu.VMEM((1, keep, tn), F32),
        ],
        compiler_params=pltpu.CompilerParams(
            dimension_semantics=("arbitrary", "arbitrary"),
            vmem_limit_bytes=VMEM_LIMIT),
        name="xbc",
    )(hs, hm, w_t, cw, cb, prev_s)


def _ssd_kernel(*refs, n_real, **static):
    yn_ref = refs[-3]
    s = pl.program_id(0)

    @pl.when(s < n_real)
    def _():
        _ssd_body(*refs, **static)

    @pl.when(s >= n_real)
    def _():
        yn_ref[...] = jnp.zeros(yn_ref.shape, yn_ref.dtype)


def _ssd_body(sz_ref, x_ref, b_ref, c_ref, at_ref, bt_ref, dsk_ref, nw_ref, s0_ref,
              *rest, bs, q, nc, gps):
    yn_ref, sout_ref, st_ref = rest[-3:]
    rt = bs * q
    nh = gps * HEADS_PER_GROUP
    c = pl.program_id(2)
    carry = nc > 1

    if carry:
        @pl.when(c == 0)
        def _():
            st_ref[...] = s0_ref[...]

    xc = x_ref[...]
    bcb = b_ref[...].astype(BF16)
    ccb = c_ref[...].astype(BF16)

    a_t = at_ref[0]
    b_t = bt_ref[0]
    cols = jnp.concatenate([a_t, jnp.zeros((LANES - nh, rt), F32)], axis=0).T

    if bs == 1:
        a_end = jnp.broadcast_to(a_t[:, rt - 1:rt], (nh, rt))
    else:
        pos = lax.broadcasted_iota(jnp.int32, (nh, rt), 1) & (q - 1)
        a_end = a_t
        s = 1
        while s < q:
            a_end = jnp.where(pos + s < q, pltpu.roll(a_end, rt - s, 1), a_end)
            s *= 2
    to_end = jnp.exp2(a_end - b_t)

    nblk = rt // SUBLANES
    ri = lax.broadcasted_iota(jnp.int32, (nblk, SUBLANES, rt), 0) * SUBLANES + \
        lax.broadcasted_iota(jnp.int32, (nblk, SUBLANES, rt), 1)
    ci = lax.broadcasted_iota(jnp.int32, (nblk, SUBLANES, rt), 2)
    mask = (ri >= ci) & ((ri // q) == (ci // q))
    low = lax.broadcasted_iota(jnp.int32, (rt, LANES), 1) < HEAD_DIM
    seq_of_row = lax.broadcasted_iota(jnp.int32, (rt, N_STATE), 0) // q

    for k in range(gps):
        xg = xc[:, k * GROUP_W:(k + 1) * GROUP_W]
        bg = bcb[:, k * N_STATE:(k + 1) * N_STATE]
        cg = ccb[:, k * N_STATE:(k + 1) * N_STATE]
        cb = lax.dot_general(cg, bg, (((1,), (1,)), ((), ())), preferred_element_type=F32)
        cb3 = cb.reshape(nblk, SUBLANES, rt)
        xt = xg.T

        ydiag, ea, xw, a_cols = [], [], [], []
        for pr in range(HEADS_PER_GROUP // 2):
            wts, ab = [], []
            for hh in range(2):
                h = k * HEADS_PER_GROUP + 2 * pr + hh
                a_col = jnp.broadcast_to(cols[:, h:h + 1], (rt, LANES))
                b_row = jnp.broadcast_to(b_t[h:h + 1, :], (SUBLANES, rt))
                seg = jnp.where(mask, a_col.reshape(nblk, SUBLANES, rt) - b_row[None], -jnp.inf)
                wts.append((cb3 * jnp.exp2(seg)).reshape(rt, rt).astype(BF16))
                ab.append(a_col)
                rows = slice((2 * pr + hh) * HEAD_DIM, (2 * pr + hh + 1) * HEAD_DIM)
                xw.append(xt[rows] * to_end[h:h + 1, :])
            a_cols += ab
            ea.append(jnp.exp2(jnp.where(low, ab[0], ab[1])))
            xb = xg[:, pr * LANES:(pr + 1) * LANES].astype(BF16)
            zero = jnp.zeros_like(xb)
            rhs = jnp.concatenate([jnp.where(low, xb, zero), jnp.where(low, zero, xb)], axis=0)
            ydiag.append(jnp.dot(jnp.concatenate(wts, axis=1), rhs, preferred_element_type=F32))
        ydiag = jnp.concatenate(ydiag, axis=1)
        ea = jnp.concatenate(ea, axis=1)
        xwt = jnp.concatenate(xw, axis=0).astype(BF16)

        yoff = []
        for s in range(bs):
            st = st_ref[s, k] if carry else s0_ref[s, k]
            yoff.append(lax.dot_general(cg[s * q:(s + 1) * q, :], st.astype(BF16),
                                        (((1,), (1,)), ((), ())), preferred_element_type=F32))
            bsel = bg if bs == 1 else jnp.where(seq_of_row == s, bg, jnp.zeros_like(bg))
            upd = jnp.dot(xwt, bsel, preferred_element_type=F32)
            last = (s + 1) * q - 1
            dec = jnp.concatenate(
                [jnp.broadcast_to(jnp.exp2(a_cols[h][last:last + 1, :]), (HEAD_DIM, N_STATE))
                 for h in range(HEADS_PER_GROUP)], axis=0)
            new = st * dec + upd
            if carry:
                st_ref[s, k] = new
            else:
                sout_ref[s, k] = new
        yoff = yoff[0] if bs == 1 else jnp.concatenate(yoff, axis=0)

        lanes = slice(k * GROUP_W, (k + 1) * GROUP_W)
        y = ydiag + yoff * ea + dsk_ref[:, lanes] * xg
        gz = y * sz_ref[:, lanes]
        ms = jnp.mean(gz * gz, axis=-1, keepdims=True)
        yn_ref[:, lanes] = (gz * lax.rsqrt(ms + EPS) * nw_ref[:, lanes]).astype(BF16)

    if carry:
        @pl.when(c == nc - 1)
        def _():
            sout_ref[...] = st_ref[...]


def _ssd(sz, xbc, a_t, dt_t, d_skip_x, norm_w, state0,
         *, n_seq, bs, q, nc, gps, shared_init, tile0=0, out_rows=None, zero_tail_tiles=0,
         yn_into=None):
    rt = bs * q
    assert rt == ROW_TILE and N_GROUPS % gps == 0
    rows = n_seq * q * nc if out_rows is None else out_rows
    otile0 = 0 if out_rows is None else tile0
    nsb = n_seq // bs
    n_pad = pl.cdiv(zero_tail_tiles, nc)
    gw, gn, nh = gps * GROUP_W, gps * N_STATE, gps * HEADS_PER_GROUP
    static = dict(bs=bs, q=q, nc=nc, gps=gps)
    kern = (functools.partial(_ssd_kernel, n_real=nsb, **static) if n_pad
            else functools.partial(_ssd_body, **static))
    bb, bc_ = D_INNER // gn, (D_INNER + N_GROUPS * N_STATE) // gn

    def real(s):
        return jnp.minimum(s, nsb - 1) if n_pad else s

    def tile(s, c):
        return jnp.where(s < nsb, s * nc + c, nsb * nc - 1) if n_pad else s * nc + c

    def otile(s, c):
        if not n_pad:
            return otile0 + s * nc + c
        tail = jnp.minimum((s - nsb) * nc + c, zero_tail_tiles - 1)
        return otile0 + jnp.where(s < nsb, s * nc + c, nsb * nc + tail)

    sidx = (lambda s: 0) if shared_init else real
    in_specs = [
        pl.BlockSpec((rt, gw), lambda s, g, c: (tile0 + tile(s, c), g)),
        pl.BlockSpec((rt, gw), lambda s, g, c: (tile0 + tile(s, c), g)),
        pl.BlockSpec((rt, gn), lambda s, g, c: (tile0 + tile(s, c), bb + g)),
        pl.BlockSpec((rt, gn), lambda s, g, c: (tile0 + tile(s, c), bc_ + g)),
        pl.BlockSpec((1, nh, rt), lambda s, g, c: (tile(s, c), g, 0)),
        pl.BlockSpec((1, nh, rt), lambda s, g, c: (tile(s, c), g, 0)),
        pl.BlockSpec((1, gw), lambda s, g, c: (0, g)),
        pl.BlockSpec((1, gw), lambda s, g, c: (0, g)),
        pl.BlockSpec((bs, gps, GROUP_W, N_STATE), lambda s, g, c: (sidx(s), g, 0, 0)),
    ]
    out_specs = [
        pl.BlockSpec((rt, gw), lambda s, g, c: (otile(s, c), g)),
        pl.BlockSpec((bs, gps, GROUP_W, N_STATE), lambda s, g, c: (real(s), g, 0, 0)),
    ]
    operands = [sz, xbc, xbc, xbc, a_t, dt_t, d_skip_x, norm_w, state0]
    aliases = {}
    if yn_into is not None:
        assert yn_into.shape == (rows, D_INNER)
        aliases = {len(operands): 0}
        in_specs.append(pl.BlockSpec(memory_space=pl.ANY))
        operands.append(yn_into)
    st_shape = (bs, gps, GROUP_W, N_STATE) if nc > 1 else (1, 1, SUBLANES, N_STATE)
    return pl.pallas_call(
        kern,
        grid=(nsb + n_pad, N_GROUPS // gps, nc),
        in_specs=in_specs,
        out_specs=out_specs,
        out_shape=[
            jax.ShapeDtypeStruct((rows, D_INNER), BF16),
            jax.ShapeDtypeStruct((n_seq, N_GROUPS, GROUP_W, N_STATE), F32),
        ],
        scratch_shapes=[pltpu.VMEM(st_shape, F32)],
        input_output_aliases=aliases,
        compiler_params=pltpu.CompilerParams(
            dimension_semantics=("arbitrary", "arbitrary", "arbitrary"),
            vmem_limit_bytes=VMEM_LIMIT),
        name="ssd",
    )(*operands)


def _sconv_kernel(hs_ref, hm_ref, wb_ref, wc_ref, wh_ref, wz_ref, cw_ref, prev_ref,
                  v_ref, newp_ref, news_ref,
                  w_scr, halo_p, halo_s, meta_u,
                  *, tm, width, n_prompt_tiles, tiles_per_seq, bs_sample, q_sample):
    i = pl.program_id(1)
    keep = SC_CONV_W - 1

    @pl.when(i == 0)
    def _():
        for k, w_ref in enumerate((wb_ref, wc_ref, wh_ref)):
            w_scr[:, k * width:(k + 1) * width] = w_ref[...].T.astype(BF16)
        w_scr[:, 3 * width:] = (wz_ref[...] * 0.5).T.astype(BF16)
        rm = jnp.dot(hm_ref[...], w_scr[:, width:3 * width], preferred_element_type=F32)
        meta_u[0] = (rm[:, :width] * rm[:, width:])[META - keep:]

    r = jnp.dot(hs_ref[...], w_scr[...], preferred_element_type=F32)
    u = r[:, width:2 * width] * r[:, 2 * width:3 * width]

    def finish(uc):
        v_ref[...] = (r[:, :width] * uc * _silu_of_twice(r[:, 3 * width:])).astype(BF16)

    @pl.when(i < n_prompt_tiles)
    def _():
        finish(_conv_rows(u, halo_p, meta_u, cw_ref, first=(i % tiles_per_seq) == 0,
                          bs=1, q=tm, carry=True))
        newp_ref[0] = u[tm - keep:]

    @pl.when(i >= n_prompt_tiles)
    def _():
        finish(_conv_rows(u, halo_s, prev_ref, cw_ref, first=i >= n_prompt_tiles,
                          bs=bs_sample, q=q_sample, carry=False))
        news_ref[...] = u.reshape(bs_sample, q_sample, width)[:, q_sample - keep:, :]


def _sconv(hs, hm, w_t, cw, prev_s, *, n_prompt, seq, n_seq_p, n_seq_s, q_sample,
           tm=PROJ_ROWS, width=SCONV_CHANNELS):
    rows = hs.shape[0]
    keep = SC_CONV_W - 1
    assert seq % tm == 0 and (rows - n_prompt) == tm and tm == n_seq_s * q_sample
    tiles_per_seq = seq // tm
    n_pt = n_prompt // tm
    kern = functools.partial(_sconv_kernel, tm=tm, width=width, n_prompt_tiles=n_pt,
                             tiles_per_seq=tiles_per_seq, bs_sample=n_seq_s, q_sample=q_sample)

    def w_rows(k):
        base = (W_SC + k * D_MODEL) // N_HEADS
        return lambda cbk, i: ((base + cbk * (width // N_HEADS)) * N_HEADS, 0)

    w_specs = [pl.BlockSpec((pl.Element(width), pl.Element(D_MODEL)), w_rows(k)) for k in range(4)]
    return pl.pallas_call(
        kern,
        grid=(D_MODEL // width, rows // tm),
        in_specs=[
            pl.BlockSpec((tm, D_MODEL), lambda cbk, i: (i, 0)),
            pl.BlockSpec((META, D_MODEL), lambda cbk, i: (0, 0)),
            *w_specs,
            pl.BlockSpec((SC_CONV_W, width), lambda cbk, i: (0, cbk)),
            pl.BlockSpec((n_seq_s, keep, width), lambda cbk, i: (0, 0, cbk)),
        ],
        out_specs=[
            pl.BlockSpec((tm, width), lambda cbk, i: (i, cbk)),
            pl.BlockSpec((1, keep, width),
                         lambda cbk, i: (jnp.minimum(i, n_pt - 1) // tiles_per_seq, 0, cbk)),
            pl.BlockSpec((n_seq_s, keep, width), lambda cbk, i: (0, 0, cbk)),
        ],
        out_shape=[
            jax.ShapeDtypeStruct((rows, D_MODEL), BF16),
            jax.ShapeDtypeStruct((n_seq_p, keep, D_MODEL), F32),
            jax.ShapeDtypeStruct((n_seq_s, keep, D_MODEL), F32),
        ],
        scratch_shapes=[
            pltpu.VMEM((D_MODEL, 4 * width), BF16),
            pltpu.VMEM((1, SUBLANES, width), F32),
            pltpu.VMEM((n_seq_s, SUBLANES, width), F32),
            pltpu.VMEM((1, keep, width), F32),
        ],
        compiler_params=pltpu.CompilerParams(
            dimension_semantics=("arbitrary", "arbitrary"),
            vmem_limit_bytes=VMEM_LIMIT),
        name="sconv",
    )(hs, hm, w_t, w_t, w_t, w_t, cw, prev_s)


def _merge_kernel(yn_ref, v_ref, ga_ref, gb_ref, wa_ref, wb_ref, o_ref, wa_scr, wb_scr):
    @pl.when(pl.program_id(1) == 0)
    def _():
        wa_scr[...] = (wa_ref[...] * 0.5).astype(BF16)
        wb_scr[...] = (wb_ref[...] * 0.5).astype(BF16)

    ya = jnp.dot(yn_ref[...], wa_scr[...], preferred_element_type=F32)
    yb = jnp.dot(v_ref[...], wb_scr[...], preferred_element_type=F32)
    o_ref[...] = ((1.0 + jnp.tanh(ga_ref[...])) * ya
                  + (1.0 + jnp.tanh(gb_ref[...])) * yb).astype(BF16)


def _merge(yn, v, gates, wa, wb, *, tm=MERGE_ROWS, tn=MERGE_COLS):
    rows = yn.shape[0]
    return pl.pallas_call(
        _merge_kernel,
        grid=(D_MODEL // tn, rows // tm),
        in_specs=[
            pl.BlockSpec((tm, D_INNER), lambda j, i: (i, 0)),
            pl.BlockSpec((tm, D_MODEL), lambda j, i: (i, 0)),
            pl.BlockSpec((tm, tn), lambda j, i: (i, j)),
            pl.BlockSpec((tm, tn), lambda j, i: (i, D_MODEL // tn + j)),
            pl.BlockSpec((D_INNER, tn), lambda j, i: (0, j)),
            pl.BlockSpec((D_MODEL, tn), lambda j, i: (0, j)),
        ],
        out_specs=pl.BlockSpec((tm, tn), lambda j, i: (i, j)),
        out_shape=jax.ShapeDtypeStruct((rows, D_MODEL), BF16),
        scratch_shapes=[pltpu.VMEM((D_INNER, tn), BF16), pltpu.VMEM((D_MODEL, tn), BF16)],
        compiler_params=pltpu.CompilerParams(
            dimension_semantics=("arbitrary", "arbitrary"),
            vmem_limit_bytes=VMEM_LIMIT),
        name="merge",
    )(yn, v, gates, gates, wa, wb)


def _outproj_kernel(m_ref, x_ref, wo_ref, fw_ref, o_ref):
    y = x_ref[...] + jnp.dot(m_ref[...], wo_ref[...], preferred_element_type=F32)
    ms = jnp.mean(y * y, axis=-1, keepdims=True)
    o_ref[...] = y * lax.rsqrt(ms + EPS) * fw_ref[...]


def _outproj(m, x, wo, fw, *, row0, tm=OUTPROJ_ROWS):
    rows = x.shape[0]
    assert row0 % tm == 0 and rows % tm == 0
    t0 = row0 // tm
    return pl.pallas_call(
        _outproj_kernel,
        grid=(rows // tm,),
        in_specs=[
            pl.BlockSpec((tm, D_MODEL), lambda i: (t0 + i, 0)),
            pl.BlockSpec((tm, D_MODEL), lambda i: (i, 0)),
            pl.BlockSpec((D_MODEL, D_MODEL), lambda i: (0, 0), pipeline_mode=pl.Buffered(1)),
            pl.BlockSpec((1, D_MODEL), lambda i: (0, 0)),
        ],
        out_specs=pl.BlockSpec((tm, D_MODEL), lambda i: (i, 0)),
        out_shape=jax.ShapeDtypeStruct((rows, D_MODEL), F32),
        compiler_params=pltpu.CompilerParams(
            dimension_semantics=("arbitrary",),
            vmem_limit_bytes=VMEM_LIMIT),
        name="outproj",
    )(m, x, wo, fw)


def kernel(x_prompt, x_sample, state_ssd_conv, state_ssm, state_sconv, meta_tokens, norm_w,
           w_in, ssd_conv_w, ssd_conv_b, dt_bias, a_log, d_skip, ssd_norm_w, w_ssd_out,
           sconv_w, w_sconv_out, w_o, final_norm_w):
    bp, seq = x_prompt.shape[0], x_prompt.shape[1]
    bd, dec_seq = x_sample.shape[0], x_sample.shape[1]

    w_t = jnp.transpose(w_in[0])
    nw = norm_w[0].reshape(1, D_MODEL)
    fw = final_norm_w.reshape(1, D_MODEL)
    conv_w = ssd_conv_w[0] * 0.5
    conv_b = ssd_conv_b[0].reshape(1, CONV_DIM) * 0.5
    dtb = jnp.pad(dt_bias[0], (0, LANES - N_HEADS)).reshape(1, LANES)
    alog = jnp.pad(a_log[0], (0, LANES - N_HEADS)).reshape(1, LANES)
    dsk = jnp.repeat(d_skip[0], HEAD_DIM).reshape(1, D_INNER)
    gnw = ssd_norm_w[0].reshape(1, D_INNER)
    scw = sconv_w[0]

    xp = x_prompt.reshape(bp * seq, D_MODEL)
    xs = x_sample.reshape(bd * dec_seq, D_MODEL)
    n_p, n_s = bp * seq, bd * dec_seq
    streams = dict(n_prompt=n_p, seq=seq, n_seq_p=bp, n_seq_s=bd, q_sample=dec_seq)

    hs, hm, dt_raw, dt_raw_m = _norm(xp, xs, meta_tokens, nw, w_t)
    sz = _proj(hs, w_t, w_row0=0, ncols=D_INNER, silu=True)
    gates = _proj(hs, w_t, w_row0=W_GATE, ncols=2 * D_MODEL, silu=False)
    xbc, xbc_m, conv_p, conv_s = _xbc(hs, hm, w_t, conv_w, conv_b, state_ssd_conv[0], **streams)
    v_all, sc_p, sc_s = _sconv(hs, hm, w_t, scw, state_sconv[0], **streams)
    headscal = functools.partial(_headscal, dt_bias=dtb, a_log=alog)
    ssd = functools.partial(_ssd, d_skip_x=dsk, norm_w=gnw)

    at_m, dtt_m = headscal(dt_raw_m, q=ROW_TILE, valid=META, tile0=0, ntiles=1)
    _, ssm_m = ssd(jnp.zeros((ROW_TILE, D_INNER), F32), xbc_m, at_m, dtt_m,
                   state0=jnp.zeros((1, N_GROUPS, GROUP_W, N_STATE), F32),
                   n_seq=1, bs=1, q=ROW_TILE, nc=1, gps=GROUPS_PER_STEP_META, shared_init=False)

    at_p, dtt_p = headscal(dt_raw, q=ROW_TILE, valid=ROW_TILE, tile0=0, ntiles=n_p // ROW_TILE)
    yn, ssm_p = ssd(sz, xbc, at_p, dtt_p, state0=ssm_m, n_seq=bp, bs=1, q=ROW_TILE,
                    nc=seq // ROW_TILE, gps=GROUPS_PER_STEP_PROMPT, shared_init=True,
                    out_rows=n_p + n_s,
                    zero_tail_tiles=n_s // ROW_TILE)

    sbs = ROW_TILE // dec_seq
    at_s, dtt_s = headscal(dt_raw, q=dec_seq, valid=ROW_TILE, tile0=n_p // ROW_TILE,
                           ntiles=n_s // ROW_TILE)
    yn, ssm_s = ssd(sz, xbc, at_s, dtt_s,
                    state0=state_ssm[0].reshape(bd, N_GROUPS, GROUP_W, N_STATE),
                    n_seq=bd, bs=sbs, q=dec_seq, nc=1, gps=GROUPS_PER_STEP_SAMPLE,
                    shared_init=False,
                    tile0=n_p // ROW_TILE, out_rows=n_p + n_s, yn_into=yn)

    merged = _merge(yn, v_all, gates, w_ssd_out[0], w_sconv_out[0])
    wo = w_o[0].astype(BF16)
    y_p = _outproj(merged, xp, wo, fw, row0=0)
    y_s = _outproj(merged, xs, wo, fw, row0=n_p)

    return (y_p.reshape(bp, seq, D_MODEL),
            y_s.reshape(bd, dec_seq, D_MODEL),
            conv_p[None],
            ssm_p.reshape(1, bp, N_HEADS, HEAD_DIM, N_STATE),
            sc_p[None],
            conv_s[None],
            ssm_s.reshape(1, bd, N_HEADS, HEAD_DIM, N_STATE),
            sc_s[None])
```

```python
import functools

import jax
import jax.numpy as jnp
from jax import lax
from jax.experimental import pallas as pl
from jax.experimental.pallas import tpu as pltpu

F32 = jnp.float32
BF16 = jnp.bfloat16

D_MODEL = 2048
D_INNER = 4096
N_HEADS = 64
HEAD_DIM = 64
N_STATE = 128
N_GROUPS = 8
GROUP_W = D_INNER // N_GROUPS
HEADS_PER_GROUP = N_HEADS // N_GROUPS
CONV_DIM = D_INNER + 2 * N_GROUPS * N_STATE
SSD_CONV_W = 4
SC_CONV_W = 3
META = 16
EPS = 1e-6
LOG2_E = 1.4426950408889634

LANES = 128
SUBLANES = 8
ROW_TILE = 128
W_DT = D_INNER + CONV_DIM
W_SC = W_DT + N_HEADS
W_GATE = W_SC + 4 * D_MODEL

VMEM_LIMIT = 52 * 1024 * 1024
NORM_ROWS = 512
PROJ_ROWS, PROJ_COLS = 1024, 1024
XBC_COLS = 512
SCONV_CHANNELS = 256
MERGE_ROWS, MERGE_COLS = 512, 512
OUTPROJ_ROWS = 512
HEADSCAL_TILES = 8
GROUPS_PER_STEP_PROMPT = N_GROUPS
GROUPS_PER_STEP_SAMPLE = 2
GROUPS_PER_STEP_META = 2


def _sigmoid(x):
    return 0.5 * (1.0 + jnp.tanh(0.5 * x))


def _silu_of_twice(h):
    return h * (1.0 + jnp.tanh(h))


def _rms_bf16(x, w):
    ms = jnp.mean(x * x, axis=-1, keepdims=True)
    return (x * lax.rsqrt(ms + EPS) * w).astype(BF16)


def _norm_kernel(xp_ref, xs_ref, xm_ref, nw_ref, wdt_ref, hs_ref, hm_ref, dt_ref, dtm_ref,
                 wdt_scr, *, n_prompt):
    i = pl.program_id(0)

    def dt_proj(h):
        return lax.dot_general(h, wdt_scr[...], (((1,), (1,)), ((), ())),
                               preferred_element_type=F32)

    @pl.when(i == 0)
    def _():
        wdt_scr[...] = wdt_ref[...].astype(BF16)
        hm_ref[:META, :] = _rms_bf16(xm_ref[...], nw_ref[...])
        hm_ref[META:, :] = jnp.zeros((ROW_TILE - META, D_MODEL), BF16)
        dtm_ref[...] = dt_proj(hm_ref[...])

    @pl.when(i < n_prompt)
    def _():
        hs_ref[...] = _rms_bf16(xp_ref[...], nw_ref[...])

    @pl.when(i >= n_prompt)
    def _():
        hs_ref[...] = _rms_bf16(xs_ref[...], nw_ref[...])

    dt_ref[...] = dt_proj(hs_ref[...])


def _norm(xp, xs, xm, norm_w, w_t, *, tm=NORM_ROWS):
    n_p, n_s = xp.shape[0] // tm, xs.shape[0] // tm
    rows = xp.shape[0] + xs.shape[0]
    kern = functools.partial(_norm_kernel, n_prompt=n_p)
    return pl.pallas_call(
        kern,
        grid=(n_p + n_s,),
        in_specs=[
            pl.BlockSpec((tm, D_MODEL), lambda i: (jnp.minimum(i, n_p - 1), 0)),
            pl.BlockSpec((tm, D_MODEL), lambda i: (jnp.maximum(i - n_p, 0), 0)),
            pl.BlockSpec((META, D_MODEL), lambda i: (0, 0)),
            pl.BlockSpec((1, D_MODEL), lambda i: (0, 0)),
            pl.BlockSpec((LANES, D_MODEL), lambda i: (W_DT // LANES, 0)),
        ],
        out_specs=[
            pl.BlockSpec((tm, D_MODEL), lambda i: (i, 0)),
            pl.BlockSpec((ROW_TILE, D_MODEL), lambda i: (0, 0)),
            pl.BlockSpec((tm, LANES), lambda i: (i, 0)),
            pl.BlockSpec((ROW_TILE, LANES), lambda i: (0, 0)),
        ],
        out_shape=[
            jax.ShapeDtypeStruct((rows, D_MODEL), BF16),
            jax.ShapeDtypeStruct((ROW_TILE, D_MODEL), BF16),
            jax.ShapeDtypeStruct((rows, LANES), F32),
            jax.ShapeDtypeStruct((ROW_TILE, LANES), F32),
        ],
        scratch_shapes=[pltpu.VMEM((LANES, D_MODEL), BF16)],
        compiler_params=pltpu.CompilerParams(
            dimension_semantics=("arbitrary",), vmem_limit_bytes=VMEM_LIMIT),
        name="norm",
    )(xp, xs, xm, norm_w, w_t)


def _proj_kernel(hs_ref, wt_ref, o_ref, wb_ref, *, silu):
    @pl.when(pl.program_id(1) == 0)
    def _():
        w = wt_ref[...] * 0.5 if silu else wt_ref[...]
        wb_ref[...] = w.T.astype(BF16)

    r = jnp.dot(hs_ref[...], wb_ref[...], preferred_element_type=F32)
    o_ref[...] = _silu_of_twice(r) if silu else r


def _proj(hs, w_t, *, w_row0, ncols, silu, tm=PROJ_ROWS, tn=PROJ_COLS):
    rows = hs.shape[0]
    assert w_row0 % N_HEADS == 0 and ncols % tn == 0 and tn % N_HEADS == 0

    def w_rows(j, i):
        return ((w_row0 // N_HEADS + j * (tn // N_HEADS)) * N_HEADS, 0)

    return pl.pallas_call(
        functools.partial(_proj_kernel, silu=silu),
        grid=(ncols // tn, rows // tm),
        in_specs=[
            pl.BlockSpec((tm, D_MODEL), lambda j, i: (i, 0)),
            pl.BlockSpec((pl.Element(tn), pl.Element(D_MODEL)), w_rows),
        ],
        out_specs=pl.BlockSpec((tm, tn), lambda j, i: (i, j)),
        out_shape=jax.ShapeDtypeStruct((rows, ncols), F32),
        scratch_shapes=[pltpu.VMEM((D_MODEL, tn), BF16)],
        compiler_params=pltpu.CompilerParams(
            dimension_semantics=("arbitrary", "arbitrary"),
            vmem_limit_bytes=VMEM_LIMIT),
        name="proj",
    )(hs, w_t)


def _conv_rows(x, halo_ref, prev_ref, w_ref, *, first, bs, q, carry):
    taps = w_ref.shape[0]
    rt, width = x.shape

    @pl.when(first)
    def _():
        halo_ref[:, SUBLANES - (taps - 1):, :] = prev_ref[...]

    prev = halo_ref[...]
    acc = None
    if bs == 1:
        row = lax.broadcasted_iota(jnp.int32, (SUBLANES, width), 0)
        for s in range(taps - 1, 0, -1):
            rolled = pltpu.roll(x, s, 0)
            head = jnp.where(row < s, pltpu.roll(prev[0], s, 0), rolled[:SUBLANES])
            term = jnp.concatenate([head, rolled[SUBLANES:]], axis=0) * w_ref[taps - 1 - s:taps - s, :]
            acc = term if acc is None else acc + term
        acc = acc + x * w_ref[taps - 1:taps, :]
        if carry:
            halo_ref[0] = x[rt - SUBLANES:, :]
        return acc
    assert q == SUBLANES and not carry
    x3 = x.reshape(bs, q, width)
    row = lax.broadcasted_iota(jnp.int32, x3.shape, 1)
    for s in range(taps - 1, 0, -1):
        shifted = jnp.where(row < s, pltpu.roll(prev, s, 1), pltpu.roll(x3, s, 1))
        term = shifted * w_ref[taps - 1 - s:taps - s, :]
        acc = term if acc is None else acc + term
    acc = acc + x3 * w_ref[taps - 1:taps, :]
    return acc.reshape(rt, width)


def _seg_cumsum(a, q):
    pos = lax.broadcasted_iota(jnp.int32, a.shape, 0) & (q - 1)
    s = 1
    while s < q:
        shifted = pltpu.roll(a, s, 0)
        a = a + jnp.where(pos >= s, shifted, 0.0)
        s *= 2
    return a


def _headscal_kernel(dtr_ref, dtb_ref, alog_ref, at_ref, bt_ref, ac_ref, *, q, valid, tps):
    dtv = jax.nn.softplus(dtr_ref[...] + dtb_ref[...])
    if valid < ROW_TILE:
        rows = lax.broadcasted_iota(jnp.int32, dtv.shape, 0)
        dtv = jnp.where(rows < valid, dtv, 0.0)
    acum = _seg_cumsum(dtv * (-jnp.exp(alog_ref[...])), q) * LOG2_E
    a_minus_logdt = acum - jnp.log(dtv) * LOG2_E
    for t in range(tps):
        at_ref[t] = acum[t * ROW_TILE:(t + 1) * ROW_TILE].T
        bt_ref[t] = a_minus_logdt[t * ROW_TILE:(t + 1) * ROW_TILE].T
        ac_ref[t] = acum[t * ROW_TILE:(t + 1) * ROW_TILE]


def _headscal(dt_raw, dt_bias, a_log, *, q, valid, tile0, ntiles):
    tps = min(ntiles, HEADSCAL_TILES)
    assert ntiles % tps == 0 and tile0 % tps == 0 and q <= ROW_TILE
    kern = functools.partial(_headscal_kernel, q=q, valid=valid, tps=tps)
    shape = jax.ShapeDtypeStruct((ntiles, LANES, ROW_TILE), F32)
    return pl.pallas_call(
        kern,
        grid=(ntiles // tps,),
        in_specs=[
            pl.BlockSpec((tps * ROW_TILE, LANES), lambda t: (tile0 // tps + t, 0)),
            pl.BlockSpec((1, LANES), lambda t: (0, 0)),
            pl.BlockSpec((1, LANES), lambda t: (0, 0)),
        ],
        out_specs=[
            pl.BlockSpec((tps, LANES, ROW_TILE), lambda t: (t, 0, 0)),
            pl.BlockSpec((tps, LANES, ROW_TILE), lambda t: (t, 0, 0)),
            pl.BlockSpec((tps, ROW_TILE, LANES), lambda t: (t, 0, 0)),
        ],
        out_shape=[shape, shape, jax.ShapeDtypeStruct((ntiles, ROW_TILE, LANES), F32)],
        compiler_params=pltpu.CompilerParams(dimension_semantics=("arbitrary",)),
        name="headscal",
    )(dt_raw, dt_bias, a_log)


def _xbc_kernel(hs_ref, hm_ref, wt_ref, cw_ref, cb_ref, prev_ref,
                o_ref, om_ref, cnp_ref, cns_ref,
                wb_scr, halo_p, halo_s, halo_m, meta_prev, zero_prev,
                *, tm, n_prompt_tiles, tiles_per_seq, bs_sample, q_sample):
    i = pl.program_id(1)
    keep = SSD_CONV_W - 1
    tn = o_ref.shape[1]

    @pl.when(i == 0)
    def _():
        wb_scr[...] = wt_ref[...].T.astype(BF16)

    raw = jnp.dot(hs_ref[...], wb_scr[...], preferred_element_type=F32)

    def activate(conv):
        return _silu_of_twice(conv + cb_ref[...])

    @pl.when(i == 0)
    def _():
        raw_m = jnp.dot(hm_ref[...], wb_scr[...], preferred_element_type=F32)
        meta_prev[0] = raw_m[META - keep:]
        zero_prev[...] = jnp.zeros(zero_prev.shape, F32)
        conv_m = _conv_rows(raw_m, halo_m, zero_prev, cw_ref, first=i == 0, bs=1, q=META,
                            carry=False)
        om_ref[:META, :] = activate(conv_m)
        om_ref[META:, :] = jnp.zeros((ROW_TILE - META, tn), F32)

    @pl.when(i < n_prompt_tiles)
    def _():
        o_ref[...] = activate(_conv_rows(raw, halo_p, meta_prev, cw_ref,
                                         first=(i % tiles_per_seq) == 0, bs=1, q=tm, carry=True))
        cnp_ref[0] = raw[tm - keep:]

    @pl.when(i >= n_prompt_tiles)
    def _():
        o_ref[...] = activate(_conv_rows(raw, halo_s, prev_ref, cw_ref, first=i >= n_prompt_tiles,
                                         bs=bs_sample, q=q_sample, carry=False))
        cns_ref[...] = raw.reshape(bs_sample, q_sample, tn)[:, q_sample - keep:, :]


def _xbc(hs, hm, w_t, cw, cb, prev_s, *, n_prompt, seq, n_seq_p, n_seq_s, q_sample,
         tm=PROJ_ROWS, tn=XBC_COLS):
    rows = hs.shape[0]
    keep = SSD_CONV_W - 1
    assert seq % tm == 0 and (rows - n_prompt) == tm and tm == n_seq_s * q_sample
    assert D_INNER % tn == 0 and CONV_DIM % tn == 0
    tiles_per_seq = seq // tm
    n_pt = n_prompt // tm
    kern = functools.partial(_xbc_kernel, tm=tm, n_prompt_tiles=n_pt, tiles_per_seq=tiles_per_seq,
                             bs_sample=n_seq_s, q_sample=q_sample)
    return pl.pallas_call(
        kern,
        grid=(CONV_DIM // tn, rows // tm),
        in_specs=[
            pl.BlockSpec((tm, D_MODEL), lambda j, i: (i, 0)),
            pl.BlockSpec((META, D_MODEL), lambda j, i: (0, 0)),
            pl.BlockSpec((tn, D_MODEL), lambda j, i: (D_INNER // tn + j, 0)),
            pl.BlockSpec((SSD_CONV_W, tn), lambda j, i: (0, j)),
            pl.BlockSpec((1, tn), lambda j, i: (0, j)),
            pl.BlockSpec((n_seq_s, keep, tn), lambda j, i: (0, 0, j)),
        ],
        out_specs=[
            pl.BlockSpec((tm, tn), lambda j, i: (i, j)),
            pl.BlockSpec((ROW_TILE, tn), lambda j, i: (0, j)),
            pl.BlockSpec((1, keep, tn),
                         lambda j, i: (jnp.minimum(i, n_pt - 1) // tiles_per_seq, 0, j)),
            pl.BlockSpec((n_seq_s, keep, tn), lambda j, i: (0, 0, j)),
        ],
        out_shape=[
            jax.ShapeDtypeStruct((rows, CONV_DIM), F32),
            jax.ShapeDtypeStruct((ROW_TILE, CONV_DIM), F32),
            jax.ShapeDtypeStruct((n_seq_p, keep, CONV_DIM), F32),
            jax.ShapeDtypeStruct((n_seq_s, keep, CONV_DIM), F32),
        ],
        scratch_shapes=[
            pltpu.VMEM((D_MODEL, tn), BF16),
            pltpu.VMEM((1, SUBLANES, tn), F32),
            pltpu.VMEM((n_seq_s, SUBLANES, tn), F32),
            pltpu.VMEM((1, SUBLANES, tn), F32),
            pltpu.VMEM((1, keep, tn), F32),
            pltpu.VMEM((1, keep, tn), F32),
        ],
        compiler_params=pltpu.CompilerParams(
            dimension_semantics=("arbitrary", "arbitrary"),
            vmem_limit_bytes=VMEM_LIMIT),
        name="xbc",
    )(hs, hm, w_t, cw, cb, prev_s)


def _ssd_kernel(*refs, n_real, **static):
    yn_ref = refs[-3]
    s = pl.program_id(0)

    @pl.when(s < n_real)
    def _():
        _ssd_body(*refs, **static)

    @pl.when(s >= n_real)
    def _():
        yn_ref[...] = jnp.zeros(yn_ref.shape, yn_ref.dtype)


def _ssd_body(sz_ref, x_ref, b_ref, c_ref, at_ref, bt_ref, ac_ref, dsk_ref, nw_ref, s0_ref,
              *rest, bs, q, nc, gps):
    yn_ref, sout_ref, st_ref = rest[-3:]
    rt = bs * q
    nh = gps * HEADS_PER_GROUP
    c = pl.program_id(2)
    carry = nc > 1

    if carry:
        @pl.when(c == 0)
        def _():
            st_ref[...] = s0_ref[...]

    xc = x_ref[...]
    bcb = b_ref[...].astype(BF16)
    ccb = c_ref[...].astype(BF16)

    a_t = at_ref[0]
    b_t = bt_ref[0]
    if gps == N_GROUPS:
        cols = ac_ref[0]
    else:
        cols = jnp.concatenate([a_t, jnp.zeros((LANES - nh, rt), F32)], axis=0).T

    if bs == 1:
        a_end = jnp.broadcast_to(a_t[:, rt - 1:rt], (nh, rt))
    else:
        pos = lax.broadcasted_iota(jnp.int32, (nh, rt), 1) & (q - 1)
        a_end = a_t
        s = 1
        while s < q:
            a_end = jnp.where(pos + s < q, pltpu.roll(a_end, rt - s, 1), a_end)
            s *= 2
    to_end = jnp.exp2(a_end - b_t)

    nblk = rt // SUBLANES
    ri = lax.broadcasted_iota(jnp.int32, (nblk, SUBLANES, rt), 0) * SUBLANES + \
        lax.broadcasted_iota(jnp.int32, (nblk, SUBLANES, rt), 1)
    ci = lax.broadcasted_iota(jnp.int32, (nblk, SUBLANES, rt), 2)
    mask = (ri >= ci) & ((ri // q) == (ci // q))
    low = lax.broadcasted_iota(jnp.int32, (rt, LANES), 1) < HEAD_DIM
    seq_of_row = lax.broadcasted_iota(jnp.int32, (rt, N_STATE), 0) // q

    for k in range(gps):
        xg = xc[:, k * GROUP_W:(k + 1) * GROUP_W]
        bg = bcb[:, k * N_STATE:(k + 1) * N_STATE]
        cg = ccb[:, k * N_STATE:(k + 1) * N_STATE]
        cb = lax.dot_general(cg, bg, (((1,), (1,)), ((), ())), preferred_element_type=F32)
        cb3 = cb.reshape(nblk, SUBLANES, rt)
        xt = xg.T

        ydiag, ea, xw, a_cols = [], [], [], []
        for pr in range(HEADS_PER_GROUP // 2):
            wts, ab = [], []
            for hh in range(2):
                h = k * HEADS_PER_GROUP + 2 * pr + hh
                a_col = jnp.broadcast_to(cols[:, h:h + 1], (rt, LANES))
                b_row = jnp.broadcast_to(b_t[h:h + 1, :], (SUBLANES, rt))
                seg = jnp.where(mask, a_col.reshape(nblk, SUBLANES, rt) - b_row[None], -jnp.inf)
                wts.append((cb3 * jnp.exp2(seg)).reshape(rt, rt).astype(BF16))
                ab.append(a_col)
                rows = slice((2 * pr + hh) * HEAD_DIM, (2 * pr + hh + 1) * HEAD_DIM)
                xw.append(xt[rows] * to_end[h:h + 1, :])
            a_cols += ab
            ea.append(jnp.exp2(jnp.where(low, ab[0], ab[1])))
            xb = xg[:, pr * LANES:(pr + 1) * LANES].astype(BF16)
            zero = jnp.zeros_like(xb)
            rhs = jnp.concatenate([jnp.where(low, xb, zero), jnp.where(low, zero, xb)], axis=0)
            ydiag.append(jnp.dot(jnp.concatenate(wts, axis=1), rhs, preferred_element_type=F32))
        ydiag = jnp.concatenate(ydiag, axis=1)
        ea = jnp.concatenate(ea, axis=1)
        xwt = jnp.concatenate(xw, axis=0).astype(BF16)

        yoff = []
        for s in range(bs):
            st = st_ref[s, k] if carry else s0_ref[s, k]
            yoff.append(lax.dot_general(cg[s * q:(s + 1) * q, :], st.astype(BF16),
                                        (((1,), (1,)), ((), ())), preferred_element_type=F32))
            bsel = bg if bs == 1 else jnp.where(seq_of_row == s, bg, jnp.zeros_like(bg))
            upd = jnp.dot(xwt, bsel, preferred_element_type=F32)
            last = (s + 1) * q - 1
            dec = jnp.concatenate(
                [jnp.broadcast_to(jnp.exp2(a_cols[h][last:last + 1, :]), (HEAD_DIM, N_STATE))
                 for h in range(HEADS_PER_GROUP)], axis=0)
            new = st * dec + upd
            if carry:
                st_ref[s, k] = new
            else:
                sout_ref[s, k] = new
        yoff = yoff[0] if bs == 1 else jnp.concatenate(yoff, axis=0)

        lanes = slice(k * GROUP_W, (k + 1) * GROUP_W)
        y = ydiag + yoff * ea + dsk_ref[:, lanes] * xg
        gz = y * sz_ref[:, lanes]
        ms = jnp.mean(gz * gz, axis=-1, keepdims=True)
        yn_ref[:, lanes] = (gz * lax.rsqrt(ms + EPS) * nw_ref[:, lanes]).astype(BF16)

    if carry:
        @pl.when(c == nc - 1)
        def _():
            sout_ref[...] = st_ref[...]


def _ssd(sz, xbc, a_t, dt_t, a_c, d_skip_x, norm_w, state0,
         *, n_seq, bs, q, nc, gps, shared_init, tile0=0, out_rows=None, zero_tail_tiles=0,
         yn_into=None):
    rt = bs * q
    assert rt == ROW_TILE and N_GROUPS % gps == 0
    rows = n_seq * q * nc if out_rows is None else out_rows
    otile0 = 0 if out_rows is None else tile0
    nsb = n_seq // bs
    n_pad = pl.cdiv(zero_tail_tiles, nc)
    gw, gn, nh = gps * GROUP_W, gps * N_STATE, gps * HEADS_PER_GROUP
    static = dict(bs=bs, q=q, nc=nc, gps=gps)
    kern = (functools.partial(_ssd_kernel, n_real=nsb, **static) if n_pad
            else functools.partial(_ssd_body, **static))
    bb, bc_ = D_INNER // gn, (D_INNER + N_GROUPS * N_STATE) // gn

    def real(s):
        return jnp.minimum(s, nsb - 1) if n_pad else s

    def tile(s, c):
        return jnp.where(s < nsb, s * nc + c, nsb * nc - 1) if n_pad else s * nc + c

    def otile(s, c):
        if not n_pad:
            return otile0 + s * nc + c
        tail = jnp.minimum((s - nsb) * nc + c, zero_tail_tiles - 1)
        return otile0 + jnp.where(s < nsb, s * nc + c, nsb * nc + tail)

    sidx = (lambda s: 0) if shared_init else real
    in_specs = [
        pl.BlockSpec((rt, gw), lambda s, g, c: (tile0 + tile(s, c), g)),
        pl.BlockSpec((rt, gw), lambda s, g, c: (tile0 + tile(s, c), g)),
        pl.BlockSpec((rt, gn), lambda s, g, c: (tile0 + tile(s, c), bb + g)),
        pl.BlockSpec((rt, gn), lambda s, g, c: (tile0 + tile(s, c), bc_ + g)),
        pl.BlockSpec((1, nh, rt), lambda s, g, c: (tile(s, c), g, 0)),
        pl.BlockSpec((1, nh, rt), lambda s, g, c: (tile(s, c), g, 0)),
        pl.BlockSpec((1, rt, LANES), lambda s, g, c: (tile(s, c), 0, 0)),
        pl.BlockSpec((1, gw), lambda s, g, c: (0, g)),
        pl.BlockSpec((1, gw), lambda s, g, c: (0, g)),
        pl.BlockSpec((bs, gps, GROUP_W, N_STATE), lambda s, g, c: (sidx(s), g, 0, 0)),
    ]
    out_specs = [
        pl.BlockSpec((rt, gw), lambda s, g, c: (otile(s, c), g)),
        pl.BlockSpec((bs, gps, GROUP_W, N_STATE), lambda s, g, c: (real(s), g, 0, 0)),
    ]
    operands = [sz, xbc, xbc, xbc, a_t, dt_t, a_c, d_skip_x, norm_w, state0]
    aliases = {}
    if yn_into is not None:
        assert yn_into.shape == (rows, D_INNER)
        aliases = {len(operands): 0}
        in_specs.append(pl.BlockSpec(memory_space=pl.ANY))
        operands.append(yn_into)
    st_shape = (bs, gps, GROUP_W, N_STATE) if nc > 1 else (1, 1, SUBLANES, N_STATE)
    return pl.pallas_call(
        kern,
        grid=(nsb + n_pad, N_GROUPS // gps, nc),
        in_specs=in_specs,
        out_specs=out_specs,
        out_shape=[
            jax.ShapeDtypeStruct((rows, D_INNER), BF16),
            jax.ShapeDtypeStruct((n_seq, N_GROUPS, GROUP_W, N_STATE), F32),
        ],
        scratch_shapes=[pltpu.VMEM(st_shape, F32)],
        input_output_aliases=aliases,
        compiler_params=pltpu.CompilerParams(
            dimension_semantics=("arbitrary", "arbitrary", "arbitrary"),
            vmem_limit_bytes=VMEM_LIMIT),
        name="ssd",
    )(*operands)


def _sconv_kernel(hs_ref, hm_ref, wb_ref, wc_ref, wh_ref, wz_ref, cw_ref, prev_ref,
                  v_ref, newp_ref, news_ref,
                  w_scr, halo_p, halo_s, meta_u,
                  *, tm, width, n_prompt_tiles, tiles_per_seq, bs_sample, q_sample):
    i = pl.program_id(1)
    keep = SC_CONV_W - 1

    @pl.when(i == 0)
    def _():
        for k, w_ref in enumerate((wb_ref, wc_ref, wh_ref)):
            w_scr[:, k * width:(k + 1) * width] = w_ref[...].T.astype(BF16)
        w_scr[:, 3 * width:] = (wz_ref[...] * 0.5).T.astype(BF16)
        rm = jnp.dot(hm_ref[...], w_scr[:, width:3 * width], preferred_element_type=F32)
        meta_u[0] = (rm[:, :width] * rm[:, width:])[META - keep:]

    r = jnp.dot(hs_ref[...], w_scr[...], preferred_element_type=F32)
    u = r[:, width:2 * width] * r[:, 2 * width:3 * width]

    def finish(uc):
        v_ref[...] = (r[:, :width] * uc * _silu_of_twice(r[:, 3 * width:])).astype(BF16)

    @pl.when(i < n_prompt_tiles)
    def _():
        finish(_conv_rows(u, halo_p, meta_u, cw_ref, first=(i % tiles_per_seq) == 0,
                          bs=1, q=tm, carry=True))
        newp_ref[0] = u[tm - keep:]

    @pl.when(i >= n_prompt_tiles)
    def _():
        finish(_conv_rows(u, halo_s, prev_ref, cw_ref, first=i >= n_prompt_tiles,
                          bs=bs_sample, q=q_sample, carry=False))
        news_ref[...] = u.reshape(bs_sample, q_sample, width)[:, q_sample - keep:, :]


def _sconv(hs, hm, w_t, cw, prev_s, *, n_prompt, seq, n_seq_p, n_seq_s, q_sample,
           tm=PROJ_ROWS, width=SCONV_CHANNELS):
    rows = hs.shape[0]
    keep = SC_CONV_W - 1
    assert seq % tm == 0 and (rows - n_prompt) == tm and tm == n_seq_s * q_sample
    tiles_per_seq = seq // tm
    n_pt = n_prompt // tm
    kern = functools.partial(_sconv_kernel, tm=tm, width=width, n_prompt_tiles=n_pt,
                             tiles_per_seq=tiles_per_seq, bs_sample=n_seq_s, q_sample=q_sample)

    def w_rows(k):
        base = (W_SC + k * D_MODEL) // N_HEADS
        return lambda cbk, i: ((base + cbk * (width // N_HEADS)) * N_HEADS, 0)

    w_specs = [pl.BlockSpec((pl.Element(width), pl.Element(D_MODEL)), w_rows(k)) for k in range(4)]
    return pl.pallas_call(
        kern,
        grid=(D_MODEL // width, rows // tm),
        in_specs=[
            pl.BlockSpec((tm, D_MODEL), lambda cbk, i: (i, 0)),
            pl.BlockSpec((META, D_MODEL), lambda cbk, i: (0, 0)),
            *w_specs,
            pl.BlockSpec((SC_CONV_W, width), lambda cbk, i: (0, cbk)),
            pl.BlockSpec((n_seq_s, keep, width), lambda cbk, i: (0, 0, cbk)),
        ],
        out_specs=[
            pl.BlockSpec((tm, width), lambda cbk, i: (i, cbk)),
            pl.BlockSpec((1, keep, width),
                         lambda cbk, i: (jnp.minimum(i, n_pt - 1) // tiles_per_seq, 0, cbk)),
            pl.BlockSpec((n_seq_s, keep, width), lambda cbk, i: (0, 0, cbk)),
        ],
        out_shape=[
            jax.ShapeDtypeStruct((rows, D_MODEL), BF16),
            jax.ShapeDtypeStruct((n_seq_p, keep, D_MODEL), F32),
            jax.ShapeDtypeStruct((n_seq_s, keep, D_MODEL), F32),
        ],
        scratch_shapes=[
            pltpu.VMEM((D_MODEL, 4 * width), BF16),
            pltpu.VMEM((1, SUBLANES, width), F32),
            pltpu.VMEM((n_seq_s, SUBLANES, width), F32),
            pltpu.VMEM((1, keep, width), F32),
        ],
        compiler_params=pltpu.CompilerParams(
            dimension_semantics=("arbitrary", "arbitrary"),
            vmem_limit_bytes=VMEM_LIMIT),
        name="sconv",
    )(hs, hm, w_t, w_t, w_t, w_t, cw, prev_s)


def _merge_kernel(yn_ref, v_ref, ga_ref, gb_ref, wa_ref, wb_ref, o_ref, wa_scr, wb_scr):
    @pl.when(pl.program_id(1) == 0)
    def _():
        wa_scr[...] = wa_ref[...].astype(BF16)
        wb_scr[...] = wb_ref[...].astype(BF16)

    ya = jnp.dot(yn_ref[...], wa_scr[...], preferred_element_type=F32)
    yb = jnp.dot(v_ref[...], wb_scr[...], preferred_element_type=F32)
    o_ref[...] = (_sigmoid(ga_ref[...]) * ya + _sigmoid(gb_ref[...]) * yb).astype(BF16)


def _merge(yn, v, gates, wa, wb, *, tm=MERGE_ROWS, tn=MERGE_COLS):
    rows = yn.shape[0]
    return pl.pallas_call(
        _merge_kernel,
        grid=(D_MODEL // tn, rows // tm),
        in_specs=[
            pl.BlockSpec((tm, D_INNER), lambda j, i: (i, 0)),
            pl.BlockSpec((tm, D_MODEL), lambda j, i: (i, 0)),
            pl.BlockSpec((tm, tn), lambda j, i: (i, j)),
            pl.BlockSpec((tm, tn), lambda j, i: (i, D_MODEL // tn + j)),
            pl.BlockSpec((D_INNER, tn), lambda j, i: (0, j)),
            pl.BlockSpec((D_MODEL, tn), lambda j, i: (0, j)),
        ],
        out_specs=pl.BlockSpec((tm, tn), lambda j, i: (i, j)),
        out_shape=jax.ShapeDtypeStruct((rows, D_MODEL), BF16),
        scratch_shapes=[pltpu.VMEM((D_INNER, tn), BF16), pltpu.VMEM((D_MODEL, tn), BF16)],
        compiler_params=pltpu.CompilerParams(
            dimension_semantics=("arbitrary", "arbitrary"),
            vmem_limit_bytes=VMEM_LIMIT),
        name="merge",
    )(yn, v, gates, gates, wa, wb)


def _outproj_kernel(m_ref, x_ref, wo_ref, fw_ref, o_ref):
    y = x_ref[...] + jnp.dot(m_ref[...], wo_ref[...], preferred_element_type=F32)
    ms = jnp.mean(y * y, axis=-1, keepdims=True)
    o_ref[...] = y * lax.rsqrt(ms + EPS) * fw_ref[...]


def _outproj(m, x, wo, fw, *, row0, tm=OUTPROJ_ROWS):
    rows = x.shape[0]
    assert row0 % tm == 0 and rows % tm == 0
    t0 = row0 // tm
    return pl.pallas_call(
        _outproj_kernel,
        grid=(rows // tm,),
        in_specs=[
            pl.BlockSpec((tm, D_MODEL), lambda i: (t0 + i, 0)),
            pl.BlockSpec((tm, D_MODEL), lambda i: (i, 0)),
            pl.BlockSpec((D_MODEL, D_MODEL), lambda i: (0, 0), pipeline_mode=pl.Buffered(1)),
            pl.BlockSpec((1, D_MODEL), lambda i: (0, 0)),
        ],
        out_specs=pl.BlockSpec((tm, D_MODEL), lambda i: (i, 0)),
        out_shape=jax.ShapeDtypeStruct((rows, D_MODEL), F32),
        compiler_params=pltpu.CompilerParams(
            dimension_semantics=("arbitrary",),
            vmem_limit_bytes=VMEM_LIMIT),
        name="outproj",
    )(m, x, wo, fw)


def kernel(x_prompt, x_sample, state_ssd_conv, state_ssm, state_sconv, meta_tokens, norm_w,
           w_in, ssd_conv_w, ssd_conv_b, dt_bias, a_log, d_skip, ssd_norm_w, w_ssd_out,
           sconv_w, w_sconv_out, w_o, final_norm_w):
    bp, seq = x_prompt.shape[0], x_prompt.shape[1]
    bd, dec_seq = x_sample.shape[0], x_sample.shape[1]

    w_t = jnp.transpose(w_in[0])
    nw = norm_w[0].reshape(1, D_MODEL)
    fw = final_norm_w.reshape(1, D_MODEL)
    conv_w = ssd_conv_w[0] * 0.5
    conv_b = ssd_conv_b[0].reshape(1, CONV_DIM) * 0.5
    dtb = jnp.pad(dt_bias[0], (0, LANES - N_HEADS)).reshape(1, LANES)
    alog = jnp.pad(a_log[0], (0, LANES - N_HEADS)).reshape(1, LANES)
    dsk = jnp.repeat(d_skip[0], HEAD_DIM).reshape(1, D_INNER)
    gnw = ssd_norm_w[0].reshape(1, D_INNER)
    scw = sconv_w[0]

    xp = x_prompt.reshape(bp * seq, D_MODEL)
    xs = x_sample.reshape(bd * dec_seq, D_MODEL)
    n_p, n_s = bp * seq, bd * dec_seq
    streams = dict(n_prompt=n_p, seq=seq, n_seq_p=bp, n_seq_s=bd, q_sample=dec_seq)

    hs, hm, dt_raw, dt_raw_m = _norm(xp, xs, meta_tokens, nw, w_t)
    sz = _proj(hs, w_t, w_row0=0, ncols=D_INNER, silu=True)
    gates = _proj(hs, w_t, w_row0=W_GATE, ncols=2 * D_MODEL, silu=False)
    xbc, xbc_m, conv_p, conv_s = _xbc(hs, hm, w_t, conv_w, conv_b, state_ssd_conv[0], **streams)
    v_all, sc_p, sc_s = _sconv(hs, hm, w_t, scw, state_sconv[0], **streams)
    headscal = functools.partial(_headscal, dt_bias=dtb, a_log=alog)
    ssd = functools.partial(_ssd, d_skip_x=dsk, norm_w=gnw)

    hsc_m = headscal(dt_raw_m, q=ROW_TILE, valid=META, tile0=0, ntiles=1)
    _, ssm_m = ssd(jnp.zeros((ROW_TILE, D_INNER), F32), xbc_m, *hsc_m,
                   state0=jnp.zeros((1, N_GROUPS, GROUP_W, N_STATE), F32),
                   n_seq=1, bs=1, q=ROW_TILE, nc=1, gps=GROUPS_PER_STEP_META, shared_init=False)

    hsc_p = headscal(dt_raw, q=ROW_TILE, valid=ROW_TILE, tile0=0, ntiles=n_p // ROW_TILE)
    yn, ssm_p = ssd(sz, xbc, *hsc_p, state0=ssm_m, n_seq=bp, bs=1, q=ROW_TILE,
                    nc=seq // ROW_TILE, gps=GROUPS_PER_STEP_PROMPT, shared_init=True,
                    out_rows=n_p + n_s,
                    zero_tail_tiles=n_s // ROW_TILE)

    sbs = ROW_TILE // dec_seq
    hsc_s = headscal(dt_raw, q=dec_seq, valid=ROW_TILE, tile0=n_p // ROW_TILE,
                     ntiles=n_s // ROW_TILE)
    yn, ssm_s = ssd(sz, xbc, *hsc_s,
                    state0=state_ssm[0].reshape(bd, N_GROUPS, GROUP_W, N_STATE),
                    n_seq=bd, bs=sbs, q=dec_seq, nc=1, gps=GROUPS_PER_STEP_SAMPLE,
                    shared_init=False,
                    tile0=n_p // ROW_TILE, out_rows=n_p + n_s, yn_into=yn)

    merged = _merge(yn, v_all, gates, w_ssd_out[0], w_sconv_out[0])
    wo = w_o[0].astype(BF16)
    y_p = _outproj(merged, xp, wo, fw, row0=0)
    y_s = _outproj(merged, xs, wo, fw, row0=n_p)

    return (y_p.reshape(bp, seq, D_MODEL),
            y_s.reshape(bd, dec_seq, D_MODEL),
            conv_p[None],
            ssm_p.reshape(1, bp, N_HEADS, HEAD_DIM, N_STATE),
            sc_p[None],
            conv_s[None],
            ssm_s.reshape(1, bd, N_HEADS, HEAD_DIM, N_STATE),
            sc_s[None])
```
